```python
import math
import jax
import jax.numpy as jnp
from jax import lax
import numpy as np

D_MODEL = 1024
BATCH = 16
SEQ = 256
DEPTH = 4
DEC_BATCH = 2
DEC_SEQ = 2048
PAST_LEN = 512

GRID_W = 64
N_EVEN = (DEPTH + 1) // 2
N_ODD = DEPTH // 2
EPS = 1e-6
Q_BLOCK = 128

RET_HEADS = 4
RET_W = D_MODEL // 2
RET_DIM = RET_W // RET_HEADS
RET_CHUNK = 64

MLA_HEADS = 8
MLA_NOPE = 64
MLA_ROPE = 32
MLA_V = (D_MODEL - RET_W) // MLA_HEADS
MLA_Q_RANK = 256
MLA_KV_RANK = 128
MLA_SCALE = (MLA_NOPE + MLA_ROPE) ** -0.5
ROPE_BASE = 10000.0

S5_W = D_MODEL // 2
S5_GROUP = 16
S5_GROUPS = S5_W // S5_GROUP
S5_P = 64

NA_HEADS = 8
NA_W = D_MODEL - S5_W
NA_DIM = NA_W // NA_HEADS
NA_WIN_R = 8
NA_WIN_C = 16
NA_SCALE = NA_DIM ** -0.5

D_FF = 2816
CONV_W = 3

EVEN_IN = 4 * RET_W + MLA_Q_RANK + MLA_KV_RANK + MLA_ROPE
ODD_IN = S5_W + 3 * NA_W
EVEN_SPLITS = [RET_W, 2 * RET_W, 3 * RET_W, 4 * RET_W, 4 * RET_W + MLA_Q_RANK, 4 * RET_W + MLA_Q_RANK + MLA_KV_RANK]
ODD_SPLITS = [S5_W, S5_W + NA_W, S5_W + 2 * NA_W]

kernel_name = 'hybrid_diffusion_prefix_trunk_step'


def rmsnorm(x, g):
    xf = x.astype(jnp.float32)
    y = xf * lax.rsqrt(jnp.mean(xf * xf, axis=-1, keepdims=True) + EPS)
    return (y * g.astype(jnp.float32)).astype(x.dtype)


def ada_modulation(cvec, w, b):
    m = jnp.dot(jax.nn.silu(cvec), w) + b
    return jnp.split(m[..., None, :], 6, axis=-1)


def axial_rope(n_tok, dim):
    n_freq = dim // 4
    inv = ROPE_BASE ** (-jnp.arange(n_freq, dtype=jnp.float32) / n_freq)
    t = jnp.arange(n_tok)
    row = (t // GRID_W).astype(jnp.float32)
    col = (t % GRID_W).astype(jnp.float32)
    ang = jnp.concatenate([row[:, None] * inv, col[:, None] * inv], axis=-1)
    return jnp.cos(ang), jnp.sin(ang)


def apply_rope(x, cos, sin):
    x1, x2 = jnp.split(x.astype(jnp.float32), 2, axis=-1)
    return jnp.concatenate([x1 * cos - x2 * sin, x1 * sin + x2 * cos], axis=-1).astype(x.dtype)


def retention_scan(q, k, v, log_gamma, s0):
    bsz, seq, heads, dim = q.shape
    nc = seq // RET_CHUNK
    shp = (bsz, nc, RET_CHUNK, heads, dim)
    qc, kc, vc = q.reshape(shp), k.reshape(shp), v.reshape(shp)
    pos = jnp.arange(RET_CHUNK, dtype=jnp.float32)
    diff = pos[:, None] - pos[None, :]
    decay = jnp.where(diff >= 0, jnp.exp(log_gamma[:, None, None] * jnp.maximum(diff, 0.0)), 0.0)
    scores = jnp.einsum('bnihd,bnjhd->bnhij', qc, kc) * decay
    o_inner = jnp.einsum('bnhij,bnjhe->bnihe', scores, vc)
    k_w = jnp.exp(log_gamma[None, :] * (RET_CHUNK - 1.0 - pos)[:, None])
    q_w = jnp.exp(log_gamma[None, :] * (pos + 1.0)[:, None])
    kv = jnp.einsum('bnjhd,bnjhe->nbhde', kc * k_w[:, :, None], vc)
    chunk_decay = jnp.exp(log_gamma * RET_CHUNK)[:, None, None]

    def step(s, kv_n):
        return chunk_decay * s + kv_n, s

    s_final, s_prev = lax.scan(step, s0, kv)
    o_cross = jnp.einsum('bnihd,nbhde->bnihe', qc * q_w[:, :, None], s_prev)
    return (o_inner + o_cross).reshape(bsz, seq, heads, dim), s_final


def retention_mixer(q, k, v, g, ret_logit, ret_gn, s0_fb):
    bsz, seq, _ = q.shape
    heads = lambda a: a.astype(jnp.float32).reshape(bsz, seq, RET_HEADS, RET_DIM)
    flip = lambda a: jnp.flip(a, axis=1)
    log_gamma = jax.nn.log_sigmoid(ret_logit.astype(jnp.float32))
    s0 = s0_fb.astype(jnp.float32)
    qh, kh, vh = heads(q) * RET_DIM ** -0.5, heads(k), heads(v)
    o_f, s_f = retention_scan(qh, kh, vh, log_gamma[0], s0[:, 0])
    o_b, s_b = retention_scan(flip(qh), flip(kh), flip(vh), log_gamma[1], s0[:, 1])
    o = o_f + flip(o_b)
    mu = jnp.mean(o, axis=-1, keepdims=True)
    var = jnp.mean(jnp.square(o - mu), axis=-1, keepdims=True)
    o = ((o - mu) * lax.rsqrt(var + EPS)).reshape(bsz, seq, RET_W) * ret_gn.astype(jnp.float32)
    y = jax.nn.silu(g.astype(jnp.float32)) * o
    return y.astype(q.dtype), jnp.stack([s_f, s_b], axis=1)


def mla_queries(cq, q_norm, w_uq):
    bsz, seq = cq.shape[:2]
    q = jnp.dot(rmsnorm(cq, q_norm), w_uq).reshape(bsz, seq, MLA_HEADS, MLA_NOPE + MLA_ROPE)
    return q[..., :MLA_NOPE], q[..., MLA_NOPE:]


def mla_keys_values(ckv, w_ukv):
    bsz, seq = ckv.shape[:2]
    kv = jnp.dot(ckv, w_ukv).reshape(bsz, seq, MLA_HEADS, MLA_NOPE + MLA_V)
    return kv[..., :MLA_NOPE], kv[..., MLA_NOPE:]


def mla_attend(q_nope, q_rope, k_nope, k_rope, v):
    s = (jnp.einsum('bqhd,bkhd->bhqk', q_nope, k_nope, preferred_element_type=jnp.float32)
         + jnp.einsum('bqhr,bkr->bhqk', q_rope, k_rope, preferred_element_type=jnp.float32))
    p = jax.nn.softmax(s * MLA_SCALE, axis=-1).astype(v.dtype)
    return jnp.einsum('bhqk,bkhd->bqhd', p, v)


def mla_blocked(q_nope, q_rope, k_nope, k_rope, v):
    bsz, seq = q_nope.shape[:2]
    nb = seq // Q_BLOCK
    blk = lambda a: jnp.swapaxes(a.reshape((bsz, nb, Q_BLOCK) + a.shape[2:]), 0, 1)
    out = lax.map(lambda qs: mla_attend(qs[0], qs[1], k_nope, k_rope, v), (blk(q_nope), blk(q_rope)))
    return jnp.swapaxes(out, 0, 1).reshape(bsz, seq, MLA_HEADS * MLA_V)


def cmul(ar, ai, br, bi):
    return ar * br - ai * bi, ar * bi + ai * br


def s5_scan(u, lam_re, lam_im, log_step, b_re, b_im, c_re, c_im, h0_re, h0_im):
    step = jnp.exp(log_step)[:, None]
    mag = jnp.exp(lam_re * step)
    a_re, a_im = mag * jnp.cos(lam_im * step), mag * jnp.sin(lam_im * step)
    den = lam_re * lam_re + lam_im * lam_im
    z_re, z_im = cmul(a_re - 1.0, a_im, lam_re / den, -lam_im / den)
    bb_re, bb_im = cmul(z_re[..., None], z_im[..., None], b_re, b_im)
    bu_re = jnp.einsum('blgs,gps->blgp', u, bb_re)
    bu_im = jnp.einsum('blgs,gps->blgp', u, bb_im)
    ar = jnp.broadcast_to(a_re, bu_re.shape)
    ai = jnp.broadcast_to(a_im, bu_re.shape)

    def combine(e1, e2):
        a1r, a1i, b1r, b1i = e1
        a2r, a2i, b2r, b2i = e2
        nar, nai = cmul(a2r, a2i, a1r, a1i)
        nbr, nbi = cmul(a2r, a2i, b1r, b1i)
        return nar, nai, nbr + b2r, nbi + b2i

    pr, pi, hr, hi = lax.associative_scan(combine, (ar, ai, bu_re, bu_im), axis=1)
    ir, ii = cmul(pr, pi, h0_re[:, None], h0_im[:, None])
    hr, hi = hr + ir, hi + ii
    y = jnp.einsum('gsp,blgp->blgs', c_re, hr) - jnp.einsum('gsp,blgp->blgs', c_im, hi)
    return y, hr[:, -1], hi[:, -1]


def s5_mixer(u, s5p, h0_re, h0_im):
    lam_re, lam_im, log_step, b_re, b_im, c_re, c_im, d_skip, glu_w, glu_b = [a.astype(jnp.float32) for a in s5p]
    bsz, seq, _ = u.shape
    uf = u.astype(jnp.float32)
    ug = uf.reshape(bsz, seq, S5_GROUPS, S5_GROUP)
    h0_re = h0_re.astype(jnp.float32)
    h0_im = h0_im.astype(jnp.float32)
    y_f, f_re, f_im = s5_scan(ug, lam_re[0], lam_im[0], log_step[0], b_re[0], b_im[0], c_re[0], c_im[0], h0_re[:, 0], h0_im[:, 0])
    y_b, r_re, r_im = s5_scan(jnp.flip(ug, axis=1), lam_re[1], lam_im[1], log_step[1], b_re[1], b_im[1], c_re[1], c_im[1], h0_re[:, 1], h0_im[:, 1])
    y = (y_f + jnp.flip(y_b, axis=1)).reshape(bsz, seq, S5_W) + d_skip * uf
    y = jax.nn.gelu(y)
    y = y * jax.nn.sigmoid(jnp.dot(y, glu_w) + glu_b)
    return y.astype(u.dtype), jnp.stack([f_re, r_re], axis=1), jnp.stack([f_im, r_im], axis=1)


def na_context(q, k, v):
    bsz, seq = q.shape[:2]
    nb = seq // Q_BLOCK

    def attend(qb):
        s = jnp.einsum('bqhd,bkhd->bhqk', qb, k, preferred_element_type=jnp.float32) * NA_SCALE
        p = jax.nn.softmax(s, axis=-1).astype(v.dtype)
        return jnp.einsum('bhqk,bkhd->bqhd', p, v)

    out = lax.map(attend, jnp.swapaxes(q.reshape(bsz, nb, Q_BLOCK, NA_HEADS, NA_DIM), 0, 1))
    return jnp.swapaxes(out, 0, 1).reshape(bsz, seq, NA_W)


def na_latent(q, k, v, k_ctx, v_ctx, rpb):
    bsz, seq = q.shape[:2]
    rows = seq // GRID_W
    wr = min(NA_WIN_R, rows)
    r = jnp.arange(rows)
    rs = jnp.clip(r - wr // 2, 0, rows - wr)
    row_idx = rs[:, None] + jnp.arange(wr)[None, :]
    col = jnp.arange(GRID_W)
    cs = jnp.clip(col - NA_WIN_C // 2, 0, GRID_W - NA_WIN_C)
    in_band = (col[None, :] >= cs[:, None]) & (col[None, :] < cs[:, None] + NA_WIN_C)
    dr = row_idx - r[:, None] + NA_WIN_R - 1
    dc = jnp.clip(col[None, :] - col[:, None] + NA_WIN_C - 1, 0, 2 * NA_WIN_C - 2)
    n_loc = wr * GRID_W
    bias = rpb.astype(jnp.float32)[:, dr[:, None, :, None], dc[None, :, None, :]]
    bias = bias.reshape(NA_HEADS, rows, GRID_W, n_loc)
    mask = jnp.broadcast_to(in_band[:, None, :], (GRID_W, wr, GRID_W)).reshape(GRID_W, n_loc)
    grid = lambda a: a.reshape(bsz, rows, GRID_W, NA_HEADS, NA_DIM)
    qg = grid(q)
    kb = grid(k)[:, row_idx].reshape(bsz, rows, n_loc, NA_HEADS, NA_DIM)
    vb = grid(v)[:, row_idx].reshape(bsz, rows, n_loc, NA_HEADS, NA_DIM)
    s_loc = jnp.einsum('brqhd,brkhd->bhrqk', qg, kb, preferred_element_type=jnp.float32) * NA_SCALE + bias
    s_loc = jnp.where(mask, s_loc, -jnp.inf)
    s_ctx = jnp.einsum('brqhd,bkhd->bhrqk', qg, k_ctx, preferred_element_type=jnp.float32) * NA_SCALE
    p = jax.nn.softmax(jnp.concatenate([s_loc, s_ctx], axis=-1), axis=-1).astype(v.dtype)
    o = (jnp.einsum('bhrqk,brkhd->brqhd', p[..., :n_loc], vb)
         + jnp.einsum('bhrqk,bkhd->brqhd', p[..., n_loc:], v_ctx))
    return o.reshape(bsz, seq, NA_W)


def even_layer_context(h, w_in, w_out, ret_logit, ret_gn, q_norm, w_uq, kv_norm, w_ukv):
    bsz = h.shape[0]
    q, k, v, g, cq, ckv_raw, k_rope = jnp.split(jnp.dot(h, w_in), EVEN_SPLITS, axis=-1)
    s0 = jnp.zeros((bsz, 2, RET_HEADS, RET_DIM, RET_DIM), jnp.float32)
    y_ret, s_ret = retention_mixer(q, k, v, g, ret_logit, ret_gn, s0)
    ckv = rmsnorm(ckv_raw, kv_norm)
    k_nope, v_m = mla_keys_values(ckv, w_ukv)
    q_nope, q_rope = mla_queries(cq, q_norm, w_uq)
    y_mla = mla_blocked(q_nope, q_rope, k_nope, k_rope, v_m)
    y = jnp.dot(jnp.concatenate([y_ret, y_mla], axis=-1), w_out)
    return y, s_ret, ckv, k_rope


def even_layer_latent(h, s_ret_ctx, ckv_ctx, kr_ctx, w_in, w_out, ret_logit, ret_gn, q_norm, w_uq, kv_norm, w_ukv):
    seq = h.shape[1]
    q, k, v, g, cq, ckv_raw, k_rope = jnp.split(jnp.dot(h, w_in), EVEN_SPLITS, axis=-1)
    y_ret, _ = retention_mixer(q, k, v, g, ret_logit, ret_gn, s_ret_ctx)
    cos, sin = axial_rope(seq, MLA_ROPE)
    ckv = jnp.concatenate([rmsnorm(ckv_raw, kv_norm), ckv_ctx.astype(h.dtype)], axis=1)
    k_rope = jnp.concatenate([apply_rope(k_rope, cos, sin), kr_ctx.astype(h.dtype)], axis=1)
    k_nope, v_m = mla_keys_values(ckv, w_ukv)
    q_nope, q_rope = mla_queries(cq, q_norm, w_uq)
    q_rope = apply_rope(q_rope, cos[:, None], sin[:, None])
    y_mla = mla_blocked(q_nope, q_rope, k_nope, k_rope, v_m)
    return jnp.dot(jnp.concatenate([y_ret, y_mla], axis=-1), w_out)


def odd_layer_context(h, w_in, w_out, s5p):
    bsz, seq = h.shape[:2]
    u, q, k, v = jnp.split(jnp.dot(h, w_in), ODD_SPLITS, axis=-1)
    heads = lambda a: a.reshape(bsz, seq, NA_HEADS, NA_DIM)
    h0 = jnp.zeros((bsz, 2, S5_GROUPS, S5_P), jnp.float32)
    y_s5, st_re, st_im = s5_mixer(u, s5p, h0, h0)
    k, v = heads(k), heads(v)
    y_na = na_context(heads(q), k, v)
    y = jnp.dot(jnp.concatenate([y_s5, y_na], axis=-1), w_out)
    return y, st_re, st_im, k, v


def odd_layer_latent(h, st_re_ctx, st_im_ctx, k_ctx, v_ctx, w_in, w_out, s5p, rpb):
    bsz, seq = h.shape[:2]
    u, q, k, v = jnp.split(jnp.dot(h, w_in), ODD_SPLITS, axis=-1)
    heads = lambda a: a.reshape(bsz, seq, NA_HEADS, NA_DIM)
    y_s5, _, _ = s5_mixer(u, s5p, st_re_ctx, st_im_ctx)
    y_na = na_latent(heads(q), heads(k), heads(v), k_ctx.astype(h.dtype), v_ctx.astype(h.dtype), rpb)
    return jnp.dot(jnp.concatenate([y_s5, y_na], axis=-1), w_out)


def conv_ffn(h, w_up, conv_w, conv_b, w_down):
    u = jnp.dot(h, w_up)
    ch = u.shape[-1]
    u = lax.conv_general_dilated(u, conv_w[:, None, :].astype(u.dtype), window_strides=(1,),
                                 padding=((CONV_W // 2, CONV_W // 2),),
                                 dimension_numbers=('NWC', 'WIO', 'NWC'), feature_group_count=ch) + conv_b
    a, g = jnp.split(u, 2, axis=-1)
    return jnp.dot(jax.nn.silu(g) * a, w_down)


def setup_inputs(seed: int = 0) -> dict:
    key = jax.random.key(seed)
    ks = iter(jax.random.split(key, 64))
    nrm = lambda shape, scale: jax.random.normal(next(ks), shape, jnp.float32) * scale
    gain = lambda shape: 1.0 + nrm(shape, 0.05)
    D = D_MODEL
    ret_base = jnp.log(2.0 ** (5.0 + jnp.arange(RET_HEADS, dtype=jnp.float32)) - 1.0)
    lam_im_base = math.pi * jnp.arange(S5_P, dtype=jnp.float32)
    return {
        'x_prompt': nrm((BATCH, SEQ, D), 1.0),
        'x_sample': nrm((DEC_BATCH, DEC_SEQ, D), 1.0),
        'c': nrm((DEC_BATCH, D), 1.0),
        'state_ret': nrm((DEC_BATCH, N_EVEN, 2, RET_HEADS, RET_DIM, RET_DIM), 1.0),
        'cache_mla_ckv': nrm((DEC_BATCH, N_EVEN, PAST_LEN, MLA_KV_RANK), 1.0),
        'cache_mla_krope': nrm((DEC_BATCH, N_EVEN, PAST_LEN, MLA_ROPE), 1.0),
        'state_s5_re': nrm((DEC_BATCH, N_ODD, 2, S5_GROUPS, S5_P), 0.1),
        'state_s5_im': nrm((DEC_BATCH, N_ODD, 2, S5_GROUPS, S5_P), 0.1),
        'cache_na_k': nrm((DEC_BATCH, N_ODD, PAST_LEN, NA_HEADS, NA_DIM), 1.0),
        'cache_na_v': nrm((DEC_BATCH, N_ODD, PAST_LEN, NA_HEADS, NA_DIM), 1.0),
        'c_ctx': nrm((D,), 1.0),
        'ada_w': nrm((DEPTH, D, 6 * D), 0.5 * D ** -0.5),
        'ada_b': nrm((DEPTH, 6 * D), 0.01),
        'mix_pre_g': gain((DEPTH, D)),
        'mix_post_g': gain((DEPTH, D)),
        'ffn_pre_g': gain((DEPTH, D)),
        'ffn_post_g': gain((DEPTH, D)),
        'ffn_w_up': nrm((DEPTH, D, 2 * D_FF), D ** -0.5),
        'ffn_conv_w': nrm((DEPTH, CONV_W, 2 * D_FF), CONV_W ** -0.5),
        'ffn_conv_b': nrm((DEPTH, 2 * D_FF), 0.01),
        'ffn_w_down': nrm((DEPTH, D_FF, D), D_FF ** -0.5),
        'even_w_in': nrm((N_EVEN, D, EVEN_IN), D ** -0.5),
        'even_w_out': nrm((N_EVEN, D, D), D ** -0.5),
        'ret_logit': ret_base + nrm((N_EVEN, 2, RET_HEADS), 0.1),
        'ret_gn': gain((N_EVEN, RET_W)),
        'mla_q_norm': gain((N_EVEN, MLA_Q_RANK)),
        'mla_w_uq': nrm((N_EVEN, MLA_Q_RANK, MLA_HEADS * (MLA_NOPE + MLA_ROPE)), MLA_Q_RANK ** -0.5),
        'mla_kv_norm': gain((N_EVEN, MLA_KV_RANK)),
        'mla_w_ukv': nrm((N_EVEN, MLA_KV_RANK, MLA_HEADS * (MLA_NOPE + MLA_V)), MLA_KV_RANK ** -0.5),
        'odd_w_in': nrm((N_ODD, D, ODD_IN), D ** -0.5),
        'odd_w_out': nrm((N_ODD, D, D), D ** -0.5),
        's5_lambda_re': -0.5 + nrm((N_ODD, 2, S5_GROUPS, S5_P), 0.01),
        's5_lambda_im': lam_im_base + nrm((N_ODD, 2, S5_GROUPS, S5_P), 0.01),
        's5_log_step': jax.random.uniform(next(ks), (N_ODD, 2, S5_GROUPS), jnp.float32, math.log(0.001), math.log(0.1)),
        's5_b_re': nrm((N_ODD, 2, S5_GROUPS, S5_P, S5_GROUP), (2 * S5_GROUP) ** -0.5),
        's5_b_im': nrm((N_ODD, 2, S5_GROUPS, S5_P, S5_GROUP), (2 * S5_GROUP) ** -0.5),
        's5_c_re': nrm((N_ODD, 2, S5_GROUPS, S5_GROUP, S5_P), 0.5),
        's5_c_im': nrm((N_ODD, 2, S5_GROUPS, S5_GROUP, S5_P), 0.5),
        's5_d': nrm((N_ODD, S5_W), 1.0),
        's5_glu_w': nrm((N_ODD, S5_W, S5_W), S5_W ** -0.5),
        's5_glu_b': nrm((N_ODD, S5_W), 0.01),
        'na_rpb': nrm((N_ODD, NA_HEADS, 2 * NA_WIN_R - 1, 2 * NA_WIN_C - 1), 0.02),
    }


def reference(x_prompt, x_sample, c, state_ret, cache_mla_ckv, cache_mla_krope, state_s5_re, state_s5_im,
              cache_na_k, cache_na_v, c_ctx, ada_w, ada_b, mix_pre_g, mix_post_g, ffn_pre_g, ffn_post_g,
              ffn_w_up, ffn_conv_w, ffn_conv_b, ffn_w_down, even_w_in, even_w_out, ret_logit, ret_gn,
              mla_q_norm, mla_w_uq, mla_kv_norm, mla_w_ukv, odd_w_in, odd_w_out, s5_lambda_re, s5_lambda_im,
              s5_log_step, s5_b_re, s5_b_im, s5_c_re, s5_c_im, s5_d, s5_glu_w, s5_glu_b, na_rpb):
    xp, xs = x_prompt, x_sample
    new_ret, new_ckv, new_kr, new_s5_re, new_s5_im, new_nak, new_nav = [], [], [], [], [], [], []
    for layer in range(DEPTH):
        j = layer // 2
        mp = ada_modulation(c_ctx, ada_w[layer], ada_b[layer])
        ms = ada_modulation(c, ada_w[layer], ada_b[layer])
        hp = rmsnorm(xp, mix_pre_g[layer]) * (1.0 + mp[1]) + mp[0]
        hs = rmsnorm(xs, mix_pre_g[layer]) * (1.0 + ms[1]) + ms[0]
        if layer % 2 == 0:
            ep = (even_w_in[j], even_w_out[j], ret_logit[j], ret_gn[j],
                  mla_q_norm[j], mla_w_uq[j], mla_kv_norm[j], mla_w_ukv[j])
            yp, s_ret, ckv, kr = even_layer_context(hp, *ep)
            ys = even_layer_latent(hs, state_ret[:, j], cache_mla_ckv[:, j], cache_mla_krope[:, j], *ep)
            new_ret.append(s_ret)
            new_ckv.append(ckv)
            new_kr.append(kr)
        else:
            s5p = (s5_lambda_re[j], s5_lambda_im[j], s5_log_step[j], s5_b_re[j], s5_b_im[j],
                   s5_c_re[j], s5_c_im[j], s5_d[j], s5_glu_w[j], s5_glu_b[j])
            yp, st_re, st_im, nk, nv = odd_layer_context(hp, odd_w_in[j], odd_w_out[j], s5p)
            ys = odd_layer_latent(hs, state_s5_re[:, j], state_s5_im[:, j], cache_na_k[:, j], cache_na_v[:, j],
                                  odd_w_in[j], odd_w_out[j], s5p, na_rpb[j])
            new_s5_re.append(st_re)
            new_s5_im.append(st_im)
            new_nak.append(nk)
            new_nav.append(nv)
        xp = xp + mp[2] * rmsnorm(yp, mix_post_g[layer])
        xs = xs + ms[2] * rmsnorm(ys, mix_post_g[layer])
        fp = (ffn_w_up[layer], ffn_conv_w[layer], ffn_conv_b[layer], ffn_w_down[layer])
        hp = rmsnorm(xp, ffn_pre_g[layer]) * (1.0 + mp[4]) + mp[3]
        hs = rmsnorm(xs, ffn_pre_g[layer]) * (1.0 + ms[4]) + ms[3]
        xp = xp + mp[5] * rmsnorm(conv_ffn(hp, *fp), ffn_post_g[layer])
        xs = xs + ms[5] * rmsnorm(conv_ffn(hs, *fp), ffn_post_g[layer])
    st_ret = jnp.stack(new_ret, axis=1)
    ck_ckv = jnp.stack(new_ckv, axis=1)
    ck_kr = jnp.stack(new_kr, axis=1)
    st_s5_re = jnp.stack(new_s5_re, axis=1)
    st_s5_im = jnp.stack(new_s5_im, axis=1)
    ck_na_k = jnp.stack(new_nak, axis=1)
    ck_na_v = jnp.stack(new_nav, axis=1)
    return (xp, xs, st_ret, ck_ckv, ck_kr, st_s5_re, st_s5_im, ck_na_k, ck_na_v)
```

```python
import functools

import numpy as np
import jax
import jax.numpy as jnp
from jax import lax
from jax.experimental import pallas as pl
from jax.experimental.pallas import tpu as pltpu

F32 = jnp.float32
BF16 = jnp.bfloat16

D = 1024
BATCH = 16
SEQ = 256
DEPTH = 4
DEC_BATCH = 2
DEC_SEQ = 2048
PAST_LEN = 512
GRID_W = 64
GRID_H = DEC_SEQ // GRID_W
EPS = 1e-6

RET_HEADS = 4
RET_W = 512
RET_DIM = 128
RET_CHUNK = 256

MLA_HEADS = 8
MLA_NOPE = 64
MLA_ROPE = 32
MLA_V = 64
MLA_Q_RANK = 256
MLA_KV_RANK = 128
MLA_SCALE = (MLA_NOPE + MLA_ROPE) ** -0.5
ROPE_BASE = 10000.0
HEAD_SLAB = 128

S5_W = 512
S5_GROUP = 16
S5_GROUPS = 32
S5_P = 64
S5_CHUNK = 16
S5_PAIRS = S5_GROUPS // 2

NA_HEADS = 8
NA_W = 512
NA_DIM = 64
NA_WIN_R = 8
NA_WIN_C = 16
NA_SCALE = NA_DIM ** -0.5
NA_QROWS = 4
NA_KROWS = 12
NA_CLASSES = 5

D_FF = 2816
FF_CHUNK = 256
FF_NCHUNK = D_FF // FF_CHUNK
HALO = 16

TM = 256
N_TOK_P = BATCH * SEQ
N_TOK_S = DEC_BATCH * DEC_SEQ
N_TOK = N_TOK_P + N_TOK_S
NT_P = N_TOK_P // TM
NT_S = N_TOK_S // TM
NT = NT_P + NT_S
TILES_PER_DEC = DEC_SEQ // TM

VMEM_LIMIT = 56 * 1024 * 1024


def _cp(*sem):
    return pltpu.CompilerParams(dimension_semantics=sem, vmem_limit_bytes=VMEM_LIMIT)


def _dot(a, b):
    return jnp.dot(a, b, preferred_element_type=F32)


def _dot_nt(a, b):
    return lax.dot_general(a, b, (((1,), (1,)), ((), ())), preferred_element_type=F32)


def _dot_tn(a, b):
    return lax.dot_general(a, b, (((0,), (0,)), ((), ())), preferred_element_type=F32)


def _rms(x, g):
    return x * lax.rsqrt(jnp.mean(x * x, axis=-1, keepdims=True) + EPS) * g


def _sigmoid(x):
    return 1.0 / (1.0 + jnp.exp(-x))


def _silu(x):
    return x * _sigmoid(x)


def _cmul(ar, ai, br, bi):
    return ar * br - ai * bi, ar * bi + ai * br


def _mrow(i):
    return jnp.where(i < NT_P, 0, 1 + (i - NT_P) // TILES_PER_DEC)


def _full(shape):
    n = len(shape)
    return pl.BlockSpec(shape, lambda *_: (0,) * n)


def _mod_spec(layer):
    return pl.BlockSpec((None, None, 1, 6 * D), lambda i: (layer, _mrow(i), 0, 0))


def _gain_spec(layer):
    return pl.BlockSpec((None, 1, D), lambda i: (layer, 0, 0))


def _ada_kernel(c_ref, w_ref, b_ref, o_ref):
    o_ref[...] = _dot(_silu(c_ref[...]).astype(BF16), w_ref[...].astype(BF16)) + b_ref[...]


def _ada_mods(cvec, ada_w, ada_b):
    nb = 4
    bn = 6 * D // nb
    out = pl.pallas_call(
        _ada_kernel,
        grid=(DEPTH, nb),
        in_specs=[
            pl.BlockSpec((8, D), lambda l, n: (0, 0)),
            pl.BlockSpec((None, D, bn), lambda l, n: (l, 0, n)),
            pl.BlockSpec((None, 1, bn), lambda l, n: (l, 0, n)),
        ],
        out_specs=pl.BlockSpec((None, 8, bn), lambda l, n: (l, 0, n)),
        out_shape=jax.ShapeDtypeStruct((DEPTH, 8, 6 * D), F32),
        compiler_params=_cp("arbitrary", "arbitrary"),
        name="ada_mods",
    )(cvec, ada_w, ada_b.reshape(DEPTH, 1, 6 * D))
    return out[:, :3].reshape(DEPTH, 3, 1, 6 * D)


EVEN_N = 4 * RET_W + MLA_Q_RANK + MLA_KV_RANK + 2 * HEAD_SLAB
MLA_QW = MLA_HEADS * HEAD_SLAB


def _in_even_kernel(x_ref, mod_ref, g_ref, w_ref, qn_ref, kvn_ref, wuq_ref, wk_ref, wv_ref, cos_ref, sin_ref,
                    qkvg_ref, qp_ref, kp_ref, v_ref, ckv_ref, kr_ref):
    mod = mod_ref[...]
    h = _rms(x_ref[...], g_ref[...]) * (1.0 + mod[:, D:2 * D]) + mod[:, :D]
    r = _dot(h.astype(BF16), w_ref[...])
    qkvg_ref[...] = r[:, :4 * RET_W]
    o = 4 * RET_W
    cq = r[:, o:o + MLA_Q_RANK]
    o += MLA_Q_RANK
    ckv_raw = r[:, o:o + MLA_KV_RANK]
    o += MLA_KV_RANK
    kr = r[:, o:o + HEAD_SLAB]
    krs = r[:, o + HEAD_SLAB:o + 2 * HEAD_SLAB]
    cosf = cos_ref[...]
    sinf = sin_ref[...]
    q2 = _dot(_rms(cq, qn_ref[...]).astype(BF16), wuq_ref[...])
    ckvn = _rms(ckv_raw, kvn_ref[...])
    ckv_ref[...] = ckvn
    kr_ref[...] = kr
    cb = ckvn.astype(BF16)
    kp = _dot(cb, wk_ref[...])
    v_ref[...] = _dot(cb, wv_ref[...]).astype(BF16)
    krr = kr * cosf + krs * sinf
    for hh in range(MLA_HEADS):
        sl = slice(hh * HEAD_SLAB, (hh + 1) * HEAD_SLAB)
        ss = slice(MLA_QW + hh * HEAD_SLAB, MLA_QW + (hh + 1) * HEAD_SLAB)
        qp_ref[:, sl] = (q2[:, sl] * cosf + q2[:, ss] * sinf).astype(BF16)
        kp_ref[:, sl] = (kp[:, sl] + krr).astype(BF16)


def _in_even(x, mods, gains, layer, w_e, qn, kvn, wuq2, wk, wv, cos_t, sin_t):
    row = lambda w: pl.BlockSpec((TM, w), lambda i: (i, 0))
    pos_spec = pl.BlockSpec((TM, HEAD_SLAB), lambda i: (jnp.where(i < NT_P, 0, 1 + (i - NT_P) % TILES_PER_DEC), 0))
    return pl.pallas_call(
        _in_even_kernel,
        grid=(NT,),
        in_specs=[row(D), _mod_spec(layer), _gain_spec(layer), _full(w_e.shape), _full(qn.shape), _full(kvn.shape),
                  _full(wuq2.shape), _full(wk.shape), _full(wv.shape), pos_spec, pos_spec],
        out_specs=[row(4 * RET_W), row(MLA_QW), row(MLA_QW), row(MLA_HEADS * MLA_V), row(MLA_KV_RANK), row(HEAD_SLAB)],
        out_shape=[
            jax.ShapeDtypeStruct((N_TOK, 4 * RET_W), F32),
            jax.ShapeDtypeStruct((N_TOK, MLA_QW), BF16),
            jax.ShapeDtypeStruct((N_TOK, MLA_QW), BF16),
            jax.ShapeDtypeStruct((N_TOK, MLA_HEADS * MLA_V), BF16),
            jax.ShapeDtypeStruct((N_TOK, MLA_KV_RANK), F32),
            jax.ShapeDtypeStruct((N_TOK, HEAD_SLAB), F32),
        ],
        compiler_params=_cp("parallel"),
        name="in_even",
    )(x, mods, gains, w_e, qn, kvn, wuq2, wk, wv, cos_t, sin_t)


def _mla_cache_kernel(ckv_ref, kr_ref, wk_ref, wv_ref, kp_ref, v_ref):
    cb = ckv_ref[...].astype(BF16)
    kp = _dot(cb, wk_ref[...])
    kr = kr_ref[...]
    for hh in range(MLA_HEADS):
        sl = slice(hh * HEAD_SLAB, (hh + 1) * HEAD_SLAB)
        kp_ref[:, sl] = (kp[:, sl] + kr).astype(BF16)
    v_ref[...] = _dot(cb, wv_ref[...]).astype(BF16)


def _mla_cache(ckv_c, kr_slab, wk, wv):
    row = lambda w: pl.BlockSpec((PAST_LEN, w), lambda b: (b, 0))
    return pl.pallas_call(
        _mla_cache_kernel,
        grid=(DEC_BATCH,),
        in_specs=[row(MLA_KV_RANK), row(HEAD_SLAB), _full(wk.shape), _full(wv.shape)],
        out_specs=[row(MLA_QW), row(MLA_HEADS * MLA_V)],
        out_shape=[jax.ShapeDtypeStruct((DEC_BATCH * PAST_LEN, MLA_QW), BF16),
                   jax.ShapeDtypeStruct((DEC_BATCH * PAST_LEN, MLA_HEADS * MLA_V), BF16)],
        compiler_params=_cp("parallel"),
        name="mla_cache",
    )(ckv_c, kr_slab, wk, wv)


def _ret_kernel(*refs, seq_len, emit_state, aliased):
    lg_ref, q_ref, k_ref, v_ref, g_ref, gn_ref, s0_ref = refs[:7]
    refs = refs[7 + (1 if aliased else 0):]
    y_ref = refs[0]
    if emit_state:
        st_ref, sf_scr, sb_scr = refs[1:]
    else:
        sf_scr, sb_scr = refs[1:]
    C = RET_CHUNK
    nc = seq_len // C
    lg = -jnp.log(1.0 + jnp.exp(-lg_ref[...]))
    lg_f = lg[0]
    lg_b = lg[1]
    ii = lax.broadcasted_iota(jnp.int32, (C, C), 0)
    jj = lax.broadcasted_iota(jnp.int32, (C, C), 1)
    diff = (ii - jj).astype(F32)
    dm = (jnp.where(diff >= 0, jnp.exp(lg_f * jnp.maximum(diff, 0.0)), 0.0)
          + jnp.where(diff <= 0, jnp.exp(lg_b * jnp.maximum(-diff, 0.0)), 0.0))
    pos = lax.broadcasted_iota(jnp.int32, (C, 1), 0).astype(F32)
    qw_f = jnp.exp(lg_f * (pos + 1.0))
    kw_f = jnp.exp(lg_f * (C - 1.0 - pos))
    qw_b = jnp.exp(lg_b * (C - pos))
    kw_b = jnp.exp(lg_b * pos)
    cd_f = jnp.exp(lg_f * C)
    cd_b = jnp.exp(lg_b * C)

    sf_scr[0] = s0_ref[0]
    for n in range(nc):
        rows = slice(n * C, (n + 1) * C)
        kv = _dot_tn((k_ref[rows, :] * kw_f).astype(BF16), v_ref[rows, :].astype(BF16))
        sf_scr[n + 1] = cd_f * sf_scr[n] + kv
    sb_scr[nc] = s0_ref[1]
    for n in reversed(range(nc)):
        rows = slice(n * C, (n + 1) * C)
        kv = _dot_tn((k_ref[rows, :] * kw_b).astype(BF16), v_ref[rows, :].astype(BF16))
        sb_scr[n] = cd_b * sb_scr[n + 1] + kv
    if emit_state:
        st_ref[0] = sf_scr[nc]
        st_ref[1] = sb_scr[0]

    gn = gn_ref[...]
    for n in range(nc):
        rows = slice(n * C, (n + 1) * C)
        q = q_ref[rows, :] * (RET_DIM ** -0.5)
        kb = k_ref[rows, :].astype(BF16)
        vb = v_ref[rows, :].astype(BF16)
        s = _dot_nt(q.astype(BF16), kb) * dm
        o = (_dot(s.astype(BF16), vb)
             + _dot((q * qw_f).astype(BF16), sf_scr[n].astype(BF16))
             + _dot((q * qw_b).astype(BF16), sb_scr[n + 1].astype(BF16)))
        mu = jnp.mean(o, axis=-1, keepdims=True)
        oc = o - mu
        var = jnp.mean(oc * oc, axis=-1, keepdims=True)
        on = oc * lax.rsqrt(var + EPS) * gn
        y_ref[rows, :] = (_silu(g_ref[rows, :]) * on).astype(BF16)


def _retention(qkvg, logit, gn, s0, y_prev, *, seq_len, n_seq, row_block0, emit_state):
    nc = seq_len // RET_CHUNK
    col = lambda part: pl.BlockSpec((seq_len, RET_DIM), lambda s, h: (row_block0 + s, part * RET_HEADS + h))
    in_specs = [
        pl.BlockSpec((None, 2, 1, 1), lambda s, h: (h, 0, 0, 0)),
        col(0), col(1), col(2), col(3),
        pl.BlockSpec((1, RET_DIM), lambda s, h: (0, h)),
        pl.BlockSpec((None, 2, None, RET_DIM, RET_DIM), lambda s, h: (s, 0, h, 0, 0)),
    ]
    args = [logit, qkvg, qkvg, qkvg, qkvg, gn, s0]
    aliases = {}
    if y_prev is not None:
        in_specs.append(pl.BlockSpec(memory_space=pl.ANY))
        args.append(y_prev)
        aliases = {7: 0}
    out_specs = [pl.BlockSpec((seq_len, RET_DIM), lambda s, h: (row_block0 + s, h))]
    out_shape = [jax.ShapeDtypeStruct((N_TOK, D), BF16)]
    if emit_state:
        out_specs.append(pl.BlockSpec((None, 2, None, RET_DIM, RET_DIM), lambda s, h: (s, 0, h, 0, 0)))
        out_shape.append(jax.ShapeDtypeStruct((n_seq, 2, RET_HEADS, RET_DIM, RET_DIM), F32))
    return pl.pallas_call(
        functools.partial(_ret_kernel, seq_len=seq_len, emit_state=emit_state, aliased=y_prev is not None),
        grid=(n_seq, RET_HEADS),
        in_specs=in_specs,
        out_specs=out_specs,
        out_shape=out_shape,
        scratch_shapes=[pltpu.VMEM((nc + 1, RET_DIM, RET_DIM), F32), pltpu.VMEM((nc + 1, RET_DIM, RET_DIM), F32)],
        input_output_aliases=aliases,
        compiler_params=_cp("parallel", "parallel"),
        name="retention_%d" % seq_len,
    )(*args)


def _softmax_pv(score_blocks, value_blocks):
    m = functools.reduce(jnp.maximum, [jnp.max(s, axis=-1, keepdims=True) for s in score_blocks])
    ps = [jnp.exp(s - m) for s in score_blocks]
    l = functools.reduce(lambda a, b: a + b, [jnp.sum(p, axis=-1, keepdims=True) for p in ps])
    o = functools.reduce(lambda a, b: a + b, [_dot(p.astype(BF16), v) for p, v in zip(ps, value_blocks)])
    return o / l


def _low_half(shape):
    return lax.broadcasted_iota(jnp.int32, shape, 1) < 64


def _mla_ctx_kernel(q_ref, k_ref, v_ref, _, y_ref):
    low = _low_half((SEQ, 128))
    for hp in range(MLA_HEADS // 2):
        vs = v_ref[:, hp * 128:(hp + 1) * 128]
        outs = []
        for e in range(2):
            sl = slice((2 * hp + e) * HEAD_SLAB, (2 * hp + e + 1) * HEAD_SLAB)
            s = _dot_nt(q_ref[:, sl], k_ref[:, sl]) * MLA_SCALE
            outs.append(_softmax_pv([s], [vs]))
        y_ref[:, hp * 128:(hp + 1) * 128] = jnp.where(low, outs[0], outs[1]).astype(BF16)


def _mla_ctx(qp, kp, v, y_prev):
    return pl.pallas_call(
        _mla_ctx_kernel,
        grid=(BATCH,),
        in_specs=[pl.BlockSpec((SEQ, MLA_QW), lambda b: (b, 0)), pl.BlockSpec((SEQ, MLA_QW), lambda b: (b, 0)),
                  pl.BlockSpec((SEQ, MLA_HEADS * MLA_V), lambda b: (b, 0)), pl.BlockSpec(memory_space=pl.ANY)],
        out_specs=pl.BlockSpec((SEQ, MLA_HEADS * MLA_V), lambda b: (b, 1)),
        out_shape=jax.ShapeDtypeStruct((N_TOK, D), BF16),
        input_output_aliases={3: 0},
        compiler_params=_cp("parallel"),
        name="mla_ctx",
    )(qp, kp, v, y_prev)


def _mla_lat_kernel(q_ref, k_ref, v_ref, kc_ref, vc_ref, _, y_ref):
    low = _low_half((TM, 128))
    for hp in range(MLA_HEADS // 2):
        vsl = slice(hp * 128, (hp + 1) * 128)
        outs = []
        for e in range(2):
            sl = slice((2 * hp + e) * HEAD_SLAB, (2 * hp + e + 1) * HEAD_SLAB)
            q = q_ref[:, sl]
            s1 = _dot_nt(q, k_ref[:, sl]) * MLA_SCALE
            s2 = _dot_nt(q, kc_ref[:, sl]) * MLA_SCALE
            outs.append(_softmax_pv([s1, s2], [v_ref[:, vsl], vc_ref[:, vsl]]))
        y_ref[:, vsl] = jnp.where(low, outs[0], outs[1]).astype(BF16)


def _mla_lat(qp, kp, v, kp_c, v_c, y_prev):
    seq_blk = N_TOK_P // DEC_SEQ
    return pl.pallas_call(
        _mla_lat_kernel,
        grid=(DEC_BATCH, TILES_PER_DEC),
        in_specs=[
            pl.BlockSpec((TM, MLA_QW), lambda b, t: (NT_P + b * TILES_PER_DEC + t, 0)),
            pl.BlockSpec((DEC_SEQ, MLA_QW), lambda b, t: (seq_blk + b, 0)),
            pl.BlockSpec((DEC_SEQ, MLA_HEADS * MLA_V), lambda b, t: (seq_blk + b, 0)),
            pl.BlockSpec((PAST_LEN, MLA_QW), lambda b, t: (b, 0)),
            pl.BlockSpec((PAST_LEN, MLA_HEADS * MLA_V), lambda b, t: (b, 0)),
            pl.BlockSpec(memory_space=pl.ANY),
        ],
        out_specs=pl.BlockSpec((TM, MLA_HEADS * MLA_V), lambda b, t: (NT_P + b * TILES_PER_DEC + t, 1)),
        out_shape=jax.ShapeDtypeStruct((N_TOK, D), BF16),
        input_output_aliases={5: 0},
        compiler_params=_cp("parallel", "parallel"),
        name="mla_lat",
    )(qp, kp, v, kp_c, v_c, y_prev)


def _out_even_kernel(x_ref, y_ref, mod_ref, g_ref, w_ref, o_ref):
    mod = mod_ref[...]
    r = _dot(y_ref[...], w_ref[...])
    o_ref[...] = x_ref[...] + mod[:, 2 * D:3 * D] * _rms(r, g_ref[...])


def _out_even(x, y, mods, gains, layer, w):
    row = pl.BlockSpec((TM, D), lambda i: (i, 0))
    return pl.pallas_call(
        _out_even_kernel,
        grid=(NT,),
        in_specs=[row, row, _mod_spec(layer), _gain_spec(layer), _full(w.shape)],
        out_specs=row,
        out_shape=jax.ShapeDtypeStruct((N_TOK, D), F32),
        compiler_params=_cp("parallel"),
        name="out_even",
    )(x, y, mods, gains, w)


def _gelu_tanh(x):
    return 0.5 * x * (1.0 + jnp.tanh(np.sqrt(2.0 / np.pi).astype(np.float32) * (x + 0.044715 * (x * x * x))))


def _out_odd_kernel(x_ref, yr_ref, u_ref, yna_ref, mod_ref, g_ref, d_ref, gw_ref, gb_ref, w_ref, o_ref):
    mod = mod_ref[...]
    y = _gelu_tanh(yr_ref[...] + d_ref[...] * u_ref[...])
    y = y * _sigmoid(_dot(y.astype(BF16), gw_ref[...]) + gb_ref[...])
    r = _dot(y.astype(BF16), w_ref[:S5_W, :]) + _dot(yna_ref[...], w_ref[S5_W:, :])
    o_ref[...] = x_ref[...] + mod[:, 2 * D:3 * D] * _rms(r, g_ref[...])


def _out_odd(x, y_raw, u, y, mods, gains, layer, d_skip, glu_w, glu_b, w):
    row = pl.BlockSpec((TM, D), lambda i: (i, 0))
    half = pl.BlockSpec((TM, S5_W), lambda i: (i, 0))
    return pl.pallas_call(
        _out_odd_kernel,
        grid=(NT,),
        in_specs=[row, half, half, pl.BlockSpec((TM, NA_W), lambda i: (i, 1)), _mod_spec(layer), _gain_spec(layer),
                  _full(d_skip.shape), _full(glu_w.shape), _full(glu_b.shape), _full(w.shape)],
        out_specs=row,
        out_shape=jax.ShapeDtypeStruct((N_TOK, D), F32),
        compiler_params=_cp("parallel"),
        name="out_odd",
    )(x, y_raw, u, y, mods, gains, d_skip, glu_w, glu_b, w)


def _ffn_kernel(xm_ref, xp_ref, xn_ref, mod_ref, gpre_ref, gpost_ref, wa_ref, wg_ref, cwa_ref, cwg_ref, cba_ref,
                cbg_ref, wd_ref, o_ref, h_scr, u_scr, acc_scr):
    i = pl.program_id(0)
    t = (i - NT_P) % TILES_PER_DEC
    has_prev = jnp.logical_and(i >= NT_P, t != 0)
    has_next = jnp.logical_and(i >= NT_P, t != TILES_PER_DEC - 1)
    mod = mod_ref[...]
    shift = mod[:, 3 * D:4 * D]
    scale = mod[:, 4 * D:5 * D]
    gate = mod[:, 5 * D:6 * D]
    gpre = gpre_ref[...]
    nm = lambda x: _rms(x, gpre) * (1.0 + scale) + shift
    x = xm_ref[...]
    h_scr[0:HALO, :] = jnp.where(has_prev, nm(xp_ref[...]), 0.0).astype(BF16)
    h_scr[HALO:HALO + TM, :] = nm(x).astype(BF16)
    h_scr[HALO + TM:, :] = jnp.where(has_next, nm(xn_ref[...]), 0.0).astype(BF16)
    acc_scr[...] = jnp.zeros_like(acc_scr)

    def chunk(j, carry):
        hb = h_scr[...]
        u_scr[0] = _dot(hb, wa_ref[j])
        u_scr[1] = _dot(hb, wg_ref[j])

        def conv(idx, cw, cb):
            return (cw[0:1, :] * u_scr[idx, HALO - 1:HALO - 1 + TM, :] + cw[1:2, :] * u_scr[idx, HALO:HALO + TM, :]
                    + cw[2:3, :] * u_scr[idx, HALO + 1:HALO + 1 + TM, :] + cb)

        a = conv(0, cwa_ref[j], cba_ref[j])
        g = conv(1, cwg_ref[j], cbg_ref[j])
        acc_scr[...] += _dot((_silu(g) * a).astype(BF16), wd_ref[j])
        return carry

    lax.fori_loop(0, FF_NCHUNK, chunk, 0)
    o_ref[...] = x + gate * _rms(acc_scr[...], gpost_ref[...])


def _ffn(x, mods, gpre, gpost, layer, wa, wg, cwa, cwg, cba, cbg, wd):
    hb = TM // HALO
    nblk = N_TOK // HALO
    row = pl.BlockSpec((TM, D), lambda i: (i, 0))
    prev = pl.BlockSpec((HALO, D), lambda i: (jnp.maximum(i * hb - 1, 0), 0))
    nxt = pl.BlockSpec((HALO, D), lambda i: (jnp.minimum((i + 1) * hb, nblk - 1), 0))
    return pl.pallas_call(
        _ffn_kernel,
        grid=(NT,),
        in_specs=[row, prev, nxt, _mod_spec(layer), _gain_spec(layer), _gain_spec(layer), _full(wa.shape),
                  _full(wg.shape), _full(cwa.shape), _full(cwg.shape), _full(cba.shape), _full(cbg.shape),
                  _full(wd.shape)],
        out_specs=row,
        out_shape=jax.ShapeDtypeStruct((N_TOK, D), F32),
        scratch_shapes=[pltpu.VMEM((TM + 2 * HALO, D), BF16), pltpu.VMEM((2, TM + 2 * HALO, FF_CHUNK), F32),
                        pltpu.VMEM((TM, D), F32)],
        compiler_params=_cp("parallel"),
        name="ffn",
    )(x, x, x, mods, gpre, gpost, wa, wg, cwa, cwg, cba, cbg, wd)


def _in_odd_kernel(x_ref, mod_ref, g_ref, w_ref, u_ref, u16_ref, qkv_ref, kv_ref):
    mod = mod_ref[...]
    h = _rms(x_ref[...], g_ref[...]) * (1.0 + mod[:, D:2 * D]) + mod[:, :D]
    r = _dot(h.astype(BF16), w_ref[...])
    u_ref[...] = r[:, :S5_W]
    u16_ref[...] = r[:, :S5_W].astype(BF16)
    qkv_ref[...] = r[:, S5_W:].astype(BF16)
    kv_ref[...] = r[:, S5_W + NA_W:]


def _in_odd(x, mods, gains, layer, w):
    row = lambda w_: pl.BlockSpec((TM, w_), lambda i: (i, 0))
    return pl.pallas_call(
        _in_odd_kernel,
        grid=(NT,),
        in_specs=[row(D), _mod_spec(layer), _gain_spec(layer), _full(w.shape)],
        out_specs=[row(S5_W), row(S5_W), row(3 * NA_W), row(2 * NA_W)],
        out_shape=[jax.ShapeDtypeStruct((N_TOK, S5_W), F32), jax.ShapeDtypeStruct((N_TOK, S5_W), BF16),
                   jax.ShapeDtypeStruct((N_TOK, 3 * NA_W), BF16), jax.ShapeDtypeStruct((N_TOK, 2 * NA_W), F32)],
        compiler_params=_cp("parallel"),
        name="in_odd",
    )(x, mods, gains, w)


def _s5_prep_kernel(lre_ref, lim_ref, ls_ref, btr_ref, bti_ref, cr_ref, ci_ref, wt_ref, pt_ref, kd_ref, a16_ref):
    T = S5_CHUNK
    hi = lax.Precision.HIGHEST
    for d in range(2):
        lre = lre_ref[d]
        lim = lim_ref[d]
        step = jnp.exp(ls_ref[d])
        mag = jnp.exp(lre * step)
        are = mag * jnp.cos(lim * step)
        aim = mag * jnp.sin(lim * step)
        den = lre * lre + lim * lim
        zr, zi = _cmul(are - 1.0, aim, lre / den, -lim / den)
        bbr, bbi = _cmul(zr, zi, btr_ref[d], bti_ref[d])
        cr = cr_ref[d]
        ci = ci_ref[d]
        pr = jnp.ones_like(are)
        pi = jnp.zeros_like(are)
        for k in range(T + 1):
            if k < T:
                j = T - 1 - k if d == 0 else k
                wr, wi = _cmul(pr, pi, bbr, bbi)
                wt_ref[d, 0, :, j * S5_GROUP:(j + 1) * S5_GROUP, :] = wr
                wt_ref[d, 1, :, j * S5_GROUP:(j + 1) * S5_GROUP, :] = wi
            if k >= 1:
                t = k - 1 if d == 0 else T - k
                er, ei = _cmul(cr, ci, pr, pi)
                pt_ref[d, 0, :, t * S5_GROUP:(t + 1) * S5_GROUP, :] = er
                pt_ref[d, 1, :, t * S5_GROUP:(t + 1) * S5_GROUP, :] = -ei
            if k == T:
                a16_ref[d, 0] = pr
                a16_ref[d, 1] = pi
            pr, pi = _cmul(pr, pi, are, aim)
        kd_ref[d] = (jnp.einsum("gsp,gjp->gsj", cr, wt_ref[d, 0], precision=hi, preferred_element_type=F32)
                     - jnp.einsum("gsp,gjp->gsj", ci, wt_ref[d, 1], precision=hi, preferred_element_type=F32))


def _s5_prep(lre, lim, ls, btr, bti, cr, ci):
    G, S, P, T = S5_GROUPS, S5_GROUP, S5_P, S5_CHUNK
    args = (lre.reshape(2, G, 1, P), lim.reshape(2, G, 1, P), ls.reshape(2, G, 1, 1), btr, bti, cr, ci)
    return pl.pallas_call(
        _s5_prep_kernel,
        grid=(1,),
        in_specs=[_full(a.shape) for a in args],
        out_specs=[_full((2, 2, G, T * S, P)), _full((2, 2, G, T * S, P)), _full((2, G, S, T * S)),
                   _full((2, 2, G, 1, P))],
        out_shape=[jax.ShapeDtypeStruct((2, 2, G, T * S, P), F32), jax.ShapeDtypeStruct((2, 2, G, T * S, P), F32),
                   jax.ShapeDtypeStruct((2, G, S, T * S), F32), jax.ShapeDtypeStruct((2, 2, G, 1, P), F32)],
        compiler_params=_cp("arbitrary"),
        name="s5_prep",
    )(*args)


def _s5_operators(wt, pt, kd, a16):
    G, S, P, T, NP = S5_GROUPS, S5_GROUP, S5_P, S5_CHUNK, S5_PAIRS
    TS = T * S
    pad = (T - 1) * S
    kext = [jnp.pad(kd[0], ((0, 0), (0, 0), (0, pad))), jnp.pad(kd[1], ((0, 0), (0, 0), (pad, 0)))]
    ms = []
    for d in range(2):
        mt = jnp.stack([kext[d][:, :, (T - 1 - t) * S:(T - 1 - t) * S + TS] for t in range(T)], axis=1)
        ms.append(jnp.swapaxes(mt.reshape(G, TS, TS), 1, 2).reshape(NP, 2, TS, TS))
    w = wt.reshape(2, 2, NP, 2, TS, P)
    z = jnp.zeros_like(w[:, :, :, 0])
    top = jnp.concatenate([w[:, :, :, 0], z], axis=-1)
    bot = jnp.concatenate([z, w[:, :, :, 1]], axis=-1)
    lay = lambda a: jnp.transpose(a, (2, 0, 3, 1, 4)).reshape(NP, 2, TS, 4 * P)
    n2 = jnp.concatenate([lay(top), lay(bot)], axis=2).astype(BF16)
    q = jnp.transpose(pt.reshape(2, 2, NP, 2, TS, P), (2, 0, 1, 3, 5, 4))
    zq = jnp.zeros_like(q[:, :, :, 0])
    p2 = jnp.stack([jnp.concatenate([q[:, :, :, 0], zq], axis=-1), jnp.concatenate([zq, q[:, :, :, 1]], axis=-1)],
                   axis=3).reshape(NP, 8 * P, 2 * TS).astype(BF16)
    a = jnp.transpose(a16.reshape(2, 2, NP, 2, 1, P), (2, 0, 1, 4, 3, 5)).reshape(NP, 2, 2, 1, 2 * P)
    return ms[0], ms[1], n2, p2, a


S5_ROWS_P = N_TOK_P // S5_CHUNK
S5_ROWS_S = N_TOK_S // S5_CHUNK
S5_NC_P = SEQ // S5_CHUNK
S5_NC_S = DEC_SEQ // S5_CHUNK


def _s5_kernel(u_ref, mf_ref, mb_ref, n_ref, p_ref, a_ref, h0c_ref, h0l_ref, y_ref, fin_ref, e_scr, hin_scr):
    W = 2 * S5_P
    u = u_ref[...]
    for d in range(2):
        e_scr[d] = _dot(u, n_ref[d])

    def scan(d, h0_ref, n_chunks, n_seq, row0):
        are = a_ref[d, 0]
        aim = a_ref[d, 1]
        hr = h0_ref[d, 0]
        hi = h0_ref[d, 1]
        order = range(n_chunks) if d == 0 else reversed(range(n_chunks))
        for c in order:
            rows = slice(row0 + c * n_seq, row0 + (c + 1) * n_seq)
            hin_scr[rows, 2 * d * W:(2 * d + 1) * W] = hr
            hin_scr[rows, (2 * d + 1) * W:(2 * d + 2) * W] = hi
            er = e_scr[d, rows, 0:W]
            ei = e_scr[d, rows, W:2 * W]
            hr, hi = are * hr - aim * hi + er, are * hi + aim * hr + ei
        return hr, hi

    for d in range(2):
        hr, hi = scan(d, h0c_ref, S5_NC_P, BATCH, 0)
        fin_ref[d, 0] = hr
        fin_ref[d, 1] = hi
        scan(d, h0l_ref, S5_NC_S, DEC_BATCH, S5_ROWS_P)

    hin = hin_scr[...].astype(BF16)
    TS = S5_CHUNK * S5_GROUP
    for e in range(2):
        cols = slice(e * TS, (e + 1) * TS)
        m = (mf_ref[e] + mb_ref[e]).astype(BF16)
        y_ref[:, cols] = _dot(u[:, cols], m) + _dot(hin, p_ref[:, cols])


def _s5_scan(u2, mf, mb, n2, p2, a, h0c, h0l):
    rows = S5_ROWS_P + S5_ROWS_S
    TS = S5_CHUNK * S5_GROUP
    lead = lambda shape: pl.BlockSpec((None,) + shape, lambda g: (g,) + (0,) * len(shape))
    return pl.pallas_call(
        _s5_kernel,
        grid=(S5_PAIRS,),
        in_specs=[lead((rows, 2 * TS)), lead((2, TS, TS)), lead((2, TS, TS)), lead((2, 2 * TS, 4 * S5_P)),
                  lead((8 * S5_P, 2 * TS)), lead((2, 2, 1, 2 * S5_P)), lead((2, 2, BATCH, 2 * S5_P)),
                  lead((2, 2, DEC_BATCH, 2 * S5_P))],
        out_specs=[lead((rows, 2 * TS)), lead((2, 2, BATCH, 2 * S5_P))],
        out_shape=[jax.ShapeDtypeStruct((S5_PAIRS, rows, 2 * TS), F32),
                   jax.ShapeDtypeStruct((S5_PAIRS, 2, 2, BATCH, 2 * S5_P), F32)],
        scratch_shapes=[pltpu.VMEM((2, rows, 4 * S5_P), F32), pltpu.VMEM((rows, 8 * S5_P), F32)],
        compiler_params=_cp("parallel"),
        name="s5_scan",
    )(u2, mf, mb, n2, p2, a, h0c, h0l)


def _s5_to_chunks(u16):
    def lay(part, n_seq, n_chunk):
        a = part.reshape(n_seq, n_chunk, S5_CHUNK, S5_PAIRS, 2, S5_GROUP)
        return jnp.transpose(a, (3, 1, 0, 4, 2, 5)).reshape(S5_PAIRS, n_chunk * n_seq, 2 * S5_CHUNK * S5_GROUP)
    return jnp.concatenate([lay(u16[:N_TOK_P], BATCH, S5_NC_P), lay(u16[N_TOK_P:], DEC_BATCH, S5_NC_S)], axis=1)


def _s5_from_chunks(y2):
    def lay(part, n_seq, n_chunk):
        a = part.reshape(S5_PAIRS, n_chunk, n_seq, 2, S5_CHUNK, S5_GROUP)
        return jnp.transpose(a, (2, 1, 4, 0, 3, 5)).reshape(n_seq * n_chunk * S5_CHUNK, S5_W)
    return jnp.concatenate([lay(y2[:, :S5_ROWS_P], BATCH, S5_NC_P), lay(y2[:, S5_ROWS_P:], DEC_BATCH, S5_NC_S)],
                           axis=0)


def _na_ctx_kernel(q_ref, k_ref, v_ref, y_ref):
    low = _low_half((SEQ, 128))
    for hp in range(NA_HEADS // 2):
        sl = slice(hp * 128, (hp + 1) * 128)
        q = q_ref[:, sl]
        k = k_ref[:, sl]
        v = v_ref[:, sl]
        outs = []
        for e in range(2):
            qm = jnp.where(low == (e == 0), q, jnp.zeros_like(q))
            outs.append(_softmax_pv([_dot_nt(qm, k) * NA_SCALE], [v]))
        y_ref[:, sl] = jnp.where(low, outs[0], outs[1]).astype(BF16)


def _na_ctx(qkv):
    col = lambda part: pl.BlockSpec((SEQ, NA_W), lambda b: (b, part))
    return pl.pallas_call(
        _na_ctx_kernel,
        grid=(BATCH,),
        in_specs=[col(0), col(1), col(2)],
        out_specs=pl.BlockSpec((SEQ, NA_W), lambda b: (b, 1)),
        out_shape=jax.ShapeDtypeStruct((N_TOK, D), BF16),
        compiler_params=_cp("parallel"),
        name="na_ctx",
    )(qkv, qkv, qkv)


def _na_key_row0(rb):
    return jnp.clip(NA_QROWS * rb - NA_WIN_R // 2, 0, GRID_H - NA_KROWS)


def _na_lat_kernel(q_ref, ks_ref, vs_ref, kc_ref, vc_ref, bias_ref, _, y_ref):
    rb = pl.program_id(1)
    start = pl.multiple_of(_na_key_row0(rb) * GRID_W, GRID_W)
    nk = NA_KROWS * GRID_W
    low = _low_half((TM, 128))
    for hp in range(NA_HEADS // 2):
        sl = slice(hp * 128, (hp + 1) * 128)
        q = q_ref[:, sl]
        kl = ks_ref[pl.ds(start, nk), sl]
        vl = vs_ref[pl.ds(start, nk), sl]
        kc = kc_ref[:, sl].astype(BF16)
        vc = vc_ref[:, sl].astype(BF16)
        outs = []
        for e in range(2):
            qm = jnp.where(low == (e == 0), q, jnp.zeros_like(q))
            s1 = _dot_nt(qm, kl) * NA_SCALE + bias_ref[2 * hp + e]
            s2 = _dot_nt(qm, kc) * NA_SCALE
            outs.append(_softmax_pv([s1, s2], [vl, vc]))
        y_ref[:, sl] = jnp.where(low, outs[0], outs[1]).astype(BF16)


def _na_class(rb):
    nb = GRID_H // NA_QROWS
    return jnp.where(rb < 2, rb, jnp.where(rb < nb - 2, 2, rb - (nb - NA_CLASSES)))


def _na_lat(qkv, k_c, v_c, bias, y_prev):
    seq_blk = N_TOK_P // DEC_SEQ
    nb = GRID_H // NA_QROWS
    nk = NA_KROWS * GRID_W
    return pl.pallas_call(
        _na_lat_kernel,
        grid=(DEC_BATCH, nb),
        in_specs=[
            pl.BlockSpec((TM, NA_W), lambda b, r: (NT_P + b * nb + r, 0)),
            pl.BlockSpec((DEC_SEQ, NA_W), lambda b, r: (seq_blk + b, 1)),
            pl.BlockSpec((DEC_SEQ, NA_W), lambda b, r: (seq_blk + b, 2)),
            pl.BlockSpec((PAST_LEN, NA_W), lambda b, r: (b, 0)),
            pl.BlockSpec((PAST_LEN, NA_W), lambda b, r: (b, 0)),
            pl.BlockSpec((None, NA_HEADS, TM, nk), lambda b, r: (_na_class(r), 0, 0, 0)),
            pl.BlockSpec(memory_space=pl.ANY),
        ],
        out_specs=pl.BlockSpec((TM, NA_W), lambda b, r: (NT_P + b * nb + r, 1)),
        out_shape=jax.ShapeDtypeStruct((N_TOK, D), BF16),
        input_output_aliases={6: 0},
        compiler_params=_cp("parallel", "arbitrary"),
        name="na_lat",
    )(qkv, qkv, qkv, k_c, v_c, bias, y_prev)


def _na_bias_table(rpb):
    col = np.arange(GRID_W)
    cs = np.clip(col - NA_WIN_C // 2, 0, GRID_W - NA_WIN_C)
    in_band = (col[None, :] >= cs[:, None]) & (col[None, :] < cs[:, None] + NA_WIN_C)
    dc = np.clip(col[None, :] - col[:, None] + NA_WIN_C - 1, 0, 2 * NA_WIN_C - 2)
    n_dr = 2 * NA_WIN_R - 1
    t = jnp.where(in_band, rpb.astype(F32)[:, :, dc], -jnp.inf)
    t = jnp.concatenate([t, jnp.full((NA_HEADS, 1, GRID_W, GRID_W), -jnp.inf, F32)], axis=1)
    nb = GRID_H // NA_QROWS
    dr_idx = np.full((NA_CLASSES, NA_QROWS, NA_KROWS), n_dr, np.int32)
    for cls, rb in enumerate([0, 1, 2, nb - 2, nb - 1]):
        r0 = rb * NA_QROWS
        u0 = int(np.clip(r0 - NA_WIN_R // 2, 0, GRID_H - NA_KROWS))
        for i in range(NA_QROWS):
            qr = r0 + i
            rs = int(np.clip(qr - NA_WIN_R // 2, 0, GRID_H - NA_WIN_R))
            for w in range(NA_KROWS):
                kr = u0 + w
                if rs <= kr < rs + NA_WIN_R:
                    dr_idx[cls, i, w] = kr - qr + NA_WIN_R - 1
    b = t[:, dr_idx]
    return jnp.transpose(b, (1, 0, 2, 4, 3, 5)).reshape(NA_CLASSES, NA_HEADS, NA_QROWS * GRID_W, NA_KROWS * GRID_W)


def _rope_tables():
    n_freq = MLA_ROPE // 4
    inv = ROPE_BASE ** (-jnp.arange(n_freq, dtype=F32) / n_freq)
    t = jnp.arange(DEC_SEQ)
    row = (t // GRID_W).astype(F32)
    colp = (t % GRID_W).astype(F32)
    ang = jnp.concatenate([row[:, None] * inv, colp[:, None] * inv], axis=-1)
    cos, sin = jnp.cos(ang), jnp.sin(ang)
    one = jnp.ones((DEC_SEQ, MLA_NOPE), F32)
    zero = jnp.zeros((DEC_SEQ, MLA_NOPE), F32)
    cos_s = jnp.concatenate([one, cos, cos, one[:, :32]], axis=-1)
    sin_s = jnp.concatenate([zero, -sin, sin, zero[:, :32]], axis=-1)
    cos_t = jnp.concatenate([jnp.ones((TM, HEAD_SLAB), F32), cos_s], axis=0)
    sin_t = jnp.concatenate([jnp.zeros((TM, HEAD_SLAB), F32), sin_s], axis=0)
    return cos_t, sin_t


def _even_weights(w_in, w_uq, w_ukv):
    half = MLA_ROPE // 2
    base = 4 * RET_W + MLA_Q_RANK + MLA_KV_RANK
    wkr = w_in[:, base:]
    z64 = jnp.zeros((D, MLA_NOPE), F32)
    z32 = jnp.zeros((D, HEAD_SLAB - MLA_NOPE - MLA_ROPE), F32)
    slab = jnp.concatenate([z64, wkr, z32], axis=1)
    slab_sw = jnp.concatenate([z64, wkr[:, half:], wkr[:, :half], z32], axis=1)
    w_e = jnp.concatenate([w_in[:, :base], slab, slab_sw], axis=1).astype(BF16)
    wq = w_uq.reshape(MLA_Q_RANK, MLA_HEADS, MLA_NOPE + MLA_ROPE)
    nope, rope = wq[..., :MLA_NOPE], wq[..., MLA_NOPE:]
    zq64 = jnp.zeros_like(nope)
    zq32 = jnp.zeros_like(rope)
    q_slab = jnp.concatenate([nope, rope, zq32], axis=-1).reshape(MLA_Q_RANK, MLA_QW)
    q_sw = jnp.concatenate([zq64, rope[..., half:], rope[..., :half], zq32], axis=-1).reshape(MLA_Q_RANK, MLA_QW)
    wuq2 = jnp.concatenate([q_slab, q_sw], axis=1).astype(BF16)
    wkv = w_ukv.reshape(MLA_KV_RANK, MLA_HEADS, MLA_NOPE + MLA_V)
    wk = jnp.concatenate([wkv[..., :MLA_NOPE], jnp.zeros_like(wkv[..., :MLA_NOPE])], axis=-1)
    wk = wk.reshape(MLA_KV_RANK, MLA_QW).astype(BF16)
    wv = wkv[..., MLA_NOPE:].reshape(MLA_KV_RANK, MLA_HEADS * MLA_V).astype(BF16)
    return w_e, wuq2, wk, wv


def _ffn_weights(w_up, conv_w, conv_b, w_down):
    split = lambda a, lead: jnp.moveaxis(a.reshape(lead, FF_NCHUNK, FF_CHUNK), 1, 0)
    wa = split(w_up[:, :D_FF], D).astype(BF16)
    wg = split(w_up[:, D_FF:], D).astype(BF16)
    cwa = split(conv_w[:, :D_FF], 3)
    cwg = split(conv_w[:, D_FF:], 3)
    cba = split(conv_b[None, :D_FF], 1)
    cbg = split(conv_b[None, D_FF:], 1)
    wd = w_down.reshape(FF_NCHUNK, FF_CHUNK, D).astype(BF16)
    return wa, wg, cwa, cwg, cba, cbg, wd


def kernel(x_prompt, x_sample, c, state_ret, cache_mla_ckv, cache_mla_krope, state_s5_re, state_s5_im, cache_na_k, cache_na_v, c_ctx, ada_w, ada_b, mix_pre_g, mix_post_g, ffn_pre_g, ffn_post_g, ffn_w_up, ffn_conv_w, ffn_conv_b, ffn_w_down, even_w_in, even_w_out, ret_logit, ret_gn, mla_q_norm, mla_w_uq, mla_kv_norm, mla_w_ukv, odd_w_in, odd_w_out, s5_lambda_re, s5_lambda_im, s5_log_step, s5_b_re, s5_b_im, s5_c_re, s5_c_im, s5_d, s5_glu_w, s5_glu_b, na_rpb):
    x = jnp.concatenate([x_prompt.reshape(N_TOK_P, D), x_sample.reshape(N_TOK_S, D)], axis=0)
    cvec = jnp.concatenate([c_ctx[None, :], c, jnp.zeros((8 - 1 - DEC_BATCH, D), F32)], axis=0)
    mods = _ada_mods(cvec, ada_w, ada_b)
    g3 = lambda a: a.reshape(DEPTH, 1, D)
    mix_pre, mix_post, ffn_pre, ffn_post = g3(mix_pre_g), g3(mix_post_g), g3(ffn_pre_g), g3(ffn_post_g)
    cos_t, sin_t = _rope_tables()
    new_ret, new_ckv, new_kr, new_s5_re, new_s5_im, new_nak, new_nav = [], [], [], [], [], [], []
    for layer in range(DEPTH):
        j = layer // 2
        if layer % 2 == 0:
            w_e, wuq2, wk, wv = _even_weights(even_w_in[j], mla_w_uq[j], mla_w_ukv[j])
            qkvg, qp, kp, v, ckvn, kr = _in_even(
                x, mods, mix_pre, layer, w_e, mla_q_norm[j][None, :], mla_kv_norm[j][None, :], wuq2, wk, wv,
                cos_t, sin_t)
            logit = jnp.transpose(ret_logit[j]).reshape(RET_HEADS, 2, 1, 1)
            gn = ret_gn[j][None, :]
            s0 = jnp.zeros((BATCH, 2, RET_HEADS, RET_DIM, RET_DIM), F32)
            y, st = _retention(qkvg, logit, gn, s0, None, seq_len=SEQ, n_seq=BATCH, row_block0=0, emit_state=True)
            (y,) = _retention(qkvg, logit, gn, state_ret[:, j], y, seq_len=DEC_SEQ, n_seq=DEC_BATCH,
                              row_block0=N_TOK_P // DEC_SEQ, emit_state=False)
            y = _mla_ctx(qp, kp, v, y)
            kr_c = jnp.pad(cache_mla_krope[:, j].reshape(DEC_BATCH * PAST_LEN, MLA_ROPE),
                           ((0, 0), (MLA_NOPE, HEAD_SLAB - MLA_NOPE - MLA_ROPE)))
            kp_c, v_c = _mla_cache(cache_mla_ckv[:, j].reshape(DEC_BATCH * PAST_LEN, MLA_KV_RANK), kr_c, wk, wv)
            y = _mla_lat(qp, kp, v, kp_c, v_c, y)
            x = _out_even(x, y, mods, mix_post, layer, even_w_out[j].astype(BF16))
            new_ret.append(st)
            new_ckv.append(ckvn[:N_TOK_P].reshape(BATCH, SEQ, MLA_KV_RANK))
            new_kr.append(kr[:N_TOK_P, MLA_NOPE:MLA_NOPE + MLA_ROPE].reshape(BATCH, SEQ, MLA_ROPE))
        else:
            u, u16, qkv, kv = _in_odd(x, mods, mix_pre, layer, odd_w_in[j].astype(BF16))
            tr = lambda a: jnp.swapaxes(a, -1, -2)
            wt, pt, kd, a16 = _s5_prep(s5_lambda_re[j], s5_lambda_im[j], s5_log_step[j], tr(s5_b_re[j]),
                                       tr(s5_b_im[j]), s5_c_re[j], s5_c_im[j])
            mf, mb, n2, p2, a = _s5_operators(wt, pt, kd, a16)
            h0c = jnp.zeros((S5_PAIRS, 2, 2, BATCH, 2 * S5_P), F32)
            h0 = jnp.stack([state_s5_re[:, j], state_s5_im[:, j]], axis=0)
            h0l = jnp.transpose(h0.reshape(2, DEC_BATCH, 2, S5_PAIRS, 2 * S5_P), (3, 2, 0, 1, 4))
            y2, fin = _s5_scan(_s5_to_chunks(u16), mf, mb, n2, p2, a, h0c, h0l)
            y_raw = _s5_from_chunks(y2)
            y = _na_ctx(qkv)
            bias = _na_bias_table(na_rpb[j])
            y = _na_lat(qkv, cache_na_k[:, j].reshape(DEC_BATCH * PAST_LEN, NA_W),
                        cache_na_v[:, j].reshape(DEC_BATCH * PAST_LEN, NA_W), bias, y)
            x = _out_odd(x, y_raw, u, y, mods, mix_post, layer, s5_d[j][None, :], s5_glu_w[j].astype(BF16),
                         s5_glu_b[j][None, :], odd_w_out[j].astype(BF16))
            st = jnp.transpose(fin.reshape(S5_PAIRS, 2, 2, BATCH, 2, S5_P), (2, 3, 1, 0, 4, 5))
            st = st.reshape(2, BATCH, 2, S5_GROUPS, S5_P)
            new_s5_re.append(st[0])
            new_s5_im.append(st[1])
            new_nak.append(kv[:N_TOK_P, :NA_W].reshape(BATCH, SEQ, NA_HEADS, NA_DIM))
            new_nav.append(kv[:N_TOK_P, NA_W:].reshape(BATCH, SEQ, NA_HEADS, NA_DIM))
        x = _ffn(x, mods, ffn_pre, ffn_post, layer,
                 *_ffn_weights(ffn_w_up[layer], ffn_conv_w[layer], ffn_conv_b[layer], ffn_w_down[layer]))
    stack = lambda a: jnp.stack(a, axis=1)
    return (x[:N_TOK_P].reshape(BATCH, SEQ, D), x[N_TOK_P:].reshape(DEC_BATCH, DEC_SEQ, D), stack(new_ret),
            stack(new_ckv), stack(new_kr), stack(new_s5_re), stack(new_s5_im), stack(new_nak), stack(new_nav))
```

```python
import functools

import numpy as np
import jax
import jax.numpy as jnp
from jax import lax
from jax.experimental import pallas as pl
from jax.experimental.pallas import tpu as pltpu

F32 = jnp.float32
BF16 = jnp.bfloat16

D = 1024
BATCH = 16
SEQ = 256
DEPTH = 4
DEC_BATCH = 2
DEC_SEQ = 2048
PAST_LEN = 512
GRID_W = 64
GRID_H = DEC_SEQ // GRID_W
EPS = 1e-6

RET_HEADS = 4
RET_W = 512
RET_DIM = 128
RET_CHUNK = 256

MLA_HEADS = 8
MLA_NOPE = 64
MLA_ROPE = 32
MLA_V = 64
MLA_Q_RANK = 256
MLA_KV_RANK = 128
MLA_SCALE = (MLA_NOPE + MLA_ROPE) ** -0.5
ROPE_BASE = 10000.0
HEAD_SLAB = 128

S5_W = 512
S5_GROUP = 16
S5_GROUPS = 32
S5_P = 64
S5_CHUNK = 16
S5_PAIRS = S5_GROUPS // 2

NA_HEADS = 8
NA_W = 512
NA_DIM = 64
NA_WIN_R = 8
NA_WIN_C = 16
NA_SCALE = NA_DIM ** -0.5
NA_QROWS = 4
NA_KROWS = 12

D_FF = 2816
FF_CHUNK = 256
FF_NCHUNK = D_FF // FF_CHUNK

TM = 256
N_TOK_P = BATCH * SEQ
N_TOK_S = DEC_BATCH * DEC_SEQ
N_TOK = N_TOK_P + N_TOK_S
NT_P = N_TOK_P // TM
NT_S = N_TOK_S // TM
NT = NT_P + NT_S
TILES_PER_DEC = DEC_SEQ // TM

VMEM_LIMIT = 56 * 1024 * 1024


def _cp(*sem):
    return pltpu.CompilerParams(dimension_semantics=sem, vmem_limit_bytes=VMEM_LIMIT)


def _dot(a, b):
    return jnp.dot(a, b, preferred_element_type=F32)


def _dot_nt(a, b):
    return lax.dot_general(a, b, (((1,), (1,)), ((), ())), preferred_element_type=F32)


def _dot_tn(a, b):
    return lax.dot_general(a, b, (((0,), (0,)), ((), ())), preferred_element_type=F32)


def _rms(x, g):
    return x * lax.rsqrt(jnp.mean(x * x, axis=-1, keepdims=True) + EPS) * g


def _sigmoid(x):
    return 1.0 / (1.0 + jnp.exp(-x))


def _silu(x):
    return x * _sigmoid(x)


def _cmul(ar, ai, br, bi):
    return ar * br - ai * bi, ar * bi + ai * br


def _mrow(i):
    return jnp.where(i < NT_P, 0, 1 + (i - NT_P) // TILES_PER_DEC)


def _full(shape):
    n = len(shape)
    return pl.BlockSpec(shape, lambda *_: (0,) * n)


def _mod_spec(layer):
    return pl.BlockSpec((None, None, 1, 6 * D), lambda i: (layer, _mrow(i), 0, 0))


def _gain_spec(layer):
    return pl.BlockSpec((None, 1, D), lambda i: (layer, 0, 0))


def _ada_kernel(c_ref, w_ref, b_ref, o_ref):
    o_ref[...] = _dot(_silu(c_ref[...]).astype(BF16), w_ref[...].astype(BF16)) + b_ref[...]


def _ada_mods(cvec, ada_w, ada_b):
    nb = 4
    bn = 6 * D // nb
    out = pl.pallas_call(
        _ada_kernel,
        grid=(DEPTH, nb),
        in_specs=[
            pl.BlockSpec((8, D), lambda l, n: (0, 0)),
            pl.BlockSpec((None, D, bn), lambda l, n: (l, 0, n)),
            pl.BlockSpec((None, 1, bn), lambda l, n: (l, 0, n)),
        ],
        out_specs=pl.BlockSpec((None, 8, bn), lambda l, n: (l, 0, n)),
        out_shape=jax.ShapeDtypeStruct((DEPTH, 8, 6 * D), F32),
        compiler_params=_cp("arbitrary", "arbitrary"),
        name="ada_mods",
    )(cvec, ada_w, ada_b.reshape(DEPTH, 1, 6 * D))
    return out[:, :3].reshape(DEPTH, 3, 1, 6 * D)


EVEN_N = 4 * RET_W + MLA_Q_RANK + MLA_KV_RANK + 2 * HEAD_SLAB
MLA_QW = MLA_HEADS * HEAD_SLAB


def _in_even_kernel(x_ref, mod_ref, g_ref, w_ref, wkr_ref, qn_ref, kvn_ref, wuq_ref, wk_ref, wv_ref, cos_ref, sin_ref,
                    qkvg_ref, qp_ref, kp_ref, v_ref, ckv_ref, kr_ref):
    mod = mod_ref[...]
    h = _rms(x_ref[...], g_ref[...]) * (1.0 + mod[:, D:2 * D]) + mod[:, :D]
    hb = h.astype(BF16)
    o = 4 * RET_W
    r = _dot(hb, w_ref[:, :o + MLA_Q_RANK + MLA_KV_RANK])
    qkvg_ref[...] = r[:, :o]
    cq = r[:, o:o + MLA_Q_RANK]
    o += MLA_Q_RANK
    ckv_raw = r[:, o:o + MLA_KV_RANK]
    r2 = _dot(hb, wkr_ref[...])
    kr = r2[:, :HEAD_SLAB]
    krs = r2[:, HEAD_SLAB:]
    cosf = cos_ref[...]
    sinf = sin_ref[...]
    q2 = _dot(_rms(cq, qn_ref[...]).astype(BF16), wuq_ref[...])
    ckvn = _rms(ckv_raw, kvn_ref[...])
    ckv_ref[...] = ckvn
    kr_ref[...] = kr
    cb = ckvn.astype(BF16)
    kp = _dot(cb, wk_ref[...])
    v_ref[...] = _dot(cb, wv_ref[...]).astype(BF16)
    krr = kr * cosf + krs * sinf
    for hh in range(MLA_HEADS):
        sl = slice(hh * HEAD_SLAB, (hh + 1) * HEAD_SLAB)
        ss = slice(MLA_QW + hh * HEAD_SLAB, MLA_QW + (hh + 1) * HEAD_SLAB)
        qp_ref[:, sl] = (q2[:, sl] * cosf + q2[:, ss] * sinf).astype(BF16)
        kp_ref[:, sl] = (kp[:, sl] + krr).astype(BF16)


def _in_even(x, mods, gains, layer, w_e, wkr2, qn, kvn, wuq2, wk, wv, cos_t, sin_t):
    row = lambda w: pl.BlockSpec((TM, w), lambda i: (i, 0))
    pos_spec = pl.BlockSpec((TM, HEAD_SLAB), lambda i: (jnp.where(i < NT_P, 0, 1 + (i - NT_P) % TILES_PER_DEC), 0))
    return pl.pallas_call(
        _in_even_kernel,
        grid=(NT,),
        in_specs=[row(D), _mod_spec(layer), _gain_spec(layer), _full(w_e.shape), _full(wkr2.shape), _full(qn.shape),
                  _full(kvn.shape), _full(wuq2.shape), _full(wk.shape), _full(wv.shape), pos_spec, pos_spec],
        out_specs=[row(4 * RET_W), row(MLA_QW), row(MLA_QW), row(MLA_HEADS * MLA_V), row(MLA_KV_RANK), row(HEAD_SLAB)],
        out_shape=[
            jax.ShapeDtypeStruct((N_TOK, 4 * RET_W), F32),
            jax.ShapeDtypeStruct((N_TOK, MLA_QW), BF16),
            jax.ShapeDtypeStruct((N_TOK, MLA_QW), BF16),
            jax.ShapeDtypeStruct((N_TOK, MLA_HEADS * MLA_V), BF16),
            jax.ShapeDtypeStruct((N_TOK, MLA_KV_RANK), F32),
            jax.ShapeDtypeStruct((N_TOK, HEAD_SLAB), F32),
        ],
        compiler_params=_cp("parallel"),
        name="in_even",
    )(x, mods, gains, w_e, wkr2, qn, kvn, wuq2, wk, wv, cos_t, sin_t)


def _mla_cache_kernel(ckv_ref, kr_ref, wk_ref, wv_ref, kp_ref, v_ref):
    cb = ckv_ref[...].astype(BF16)
    kp = _dot(cb, wk_ref[...])
    kr = kr_ref[...]
    for hh in range(MLA_HEADS):
        sl = slice(hh * HEAD_SLAB, (hh + 1) * HEAD_SLAB)
        kp_ref[:, sl] = (kp[:, sl] + kr).astype(BF16)
    v_ref[...] = _dot(cb, wv_ref[...]).astype(BF16)


def _mla_cache(ckv_c, kr_slab, wk, wv):
    row = lambda w: pl.BlockSpec((PAST_LEN, w), lambda b: (b, 0))
    return pl.pallas_call(
        _mla_cache_kernel,
        grid=(DEC_BATCH,),
        in_specs=[row(MLA_KV_RANK), row(HEAD_SLAB), _full(wk.shape), _full(wv.shape)],
        out_specs=[row(MLA_QW), row(MLA_HEADS * MLA_V)],
        out_shape=[jax.ShapeDtypeStruct((DEC_BATCH * PAST_LEN, MLA_QW), BF16),
                   jax.ShapeDtypeStruct((DEC_BATCH * PAST_LEN, MLA_HEADS * MLA_V), BF16)],
        compiler_params=_cp("parallel"),
        name="mla_cache",
    )(ckv_c, kr_slab, wk, wv)


def _ret_kernel(*refs, seq_len, emit_state, aliased):
    lg_ref, q_ref, k_ref, v_ref, g_ref, gn_ref, s0_ref = refs[:7]
    refs = refs[7 + (1 if aliased else 0):]
    y_ref = refs[0]
    if emit_state:
        st_ref, sf_scr, sb_scr = refs[1:]
    else:
        sf_scr, sb_scr = refs[1:]
    C = RET_CHUNK
    nc = seq_len // C
    lg = -jnp.log(1.0 + jnp.exp(-lg_ref[...]))
    lg_f = lg[0]
    lg_b = lg[1]
    ii = lax.broadcasted_iota(jnp.int32, (C, C), 0)
    jj = lax.broadcasted_iota(jnp.int32, (C, C), 1)
    diff = (ii - jj).astype(F32)
    dm = (jnp.where(diff >= 0, jnp.exp(lg_f * jnp.maximum(diff, 0.0)), 0.0)
          + jnp.where(diff <= 0, jnp.exp(lg_b * jnp.maximum(-diff, 0.0)), 0.0))
    pos = lax.broadcasted_iota(jnp.int32, (C, 1), 0).astype(F32)
    qw_f = jnp.exp(lg_f * (pos + 1.0))
    kw_f = jnp.exp(lg_f * (C - 1.0 - pos))
    qw_b = jnp.exp(lg_b * (C - pos))
    kw_b = jnp.exp(lg_b * pos)
    cd_f = jnp.exp(lg_f * C)
    cd_b = jnp.exp(lg_b * C)

    sf_scr[0] = s0_ref[0]
    for n in range(nc):
        rows = slice(n * C, (n + 1) * C)
        kv = _dot_tn((k_ref[rows, :] * kw_f).astype(BF16), v_ref[rows, :].astype(BF16))
        sf_scr[n + 1] = cd_f * sf_scr[n] + kv
    sb_scr[nc] = s0_ref[1]
    for n in reversed(range(nc)):
        rows = slice(n * C, (n + 1) * C)
        kv = _dot_tn((k_ref[rows, :] * kw_b).astype(BF16), v_ref[rows, :].astype(BF16))
        sb_scr[n] = cd_b * sb_scr[n + 1] + kv
    if emit_state:
        st_ref[0] = sf_scr[nc]
        st_ref[1] = sb_scr[0]

    gn = gn_ref[...]
    for n in range(nc):
        rows = slice(n * C, (n + 1) * C)
        q = q_ref[rows, :] * (RET_DIM ** -0.5)
        kb = k_ref[rows, :].astype(BF16)
        vb = v_ref[rows, :].astype(BF16)
        s = _dot_nt(q.astype(BF16), kb) * dm
        o = (_dot(s.astype(BF16), vb)
             + _dot((q * qw_f).astype(BF16), sf_scr[n].astype(BF16))
             + _dot((q * qw_b).astype(BF16), sb_scr[n + 1].astype(BF16)))
        mu = jnp.mean(o, axis=-1, keepdims=True)
        oc = o - mu
        var = jnp.mean(oc * oc, axis=-1, keepdims=True)
        on = oc * lax.rsqrt(var + EPS) * gn
        y_ref[rows, :] = (_silu(g_ref[rows, :]) * on).astype(BF16)


def _retention(qkvg, logit, gn, s0, y_prev, *, seq_len, n_seq, row_block0, emit_state):
    nc = seq_len // RET_CHUNK
    col = lambda part: pl.BlockSpec((seq_len, RET_DIM), lambda s, h: (row_block0 + s, part * RET_HEADS + h))
    in_specs = [
        pl.BlockSpec((None, 2, 1, 1), lambda s, h: (h, 0, 0, 0)),
        col(0), col(1), col(2), col(3),
        pl.BlockSpec((1, RET_DIM), lambda s, h: (0, h)),
        pl.BlockSpec((None, 2, None, RET_DIM, RET_DIM), lambda s, h: (s, 0, h, 0, 0)),
    ]
    args = [logit, qkvg, qkvg, qkvg, qkvg, gn, s0]
    aliases = {}
    if y_prev is not None:
        in_specs.append(pl.BlockSpec(memory_space=pl.ANY))
        args.append(y_prev)
        aliases = {7: 0}
    out_specs = [pl.BlockSpec((seq_len, RET_DIM), lambda s, h: (row_block0 + s, h))]
    out_shape = [jax.ShapeDtypeStruct((N_TOK, D), BF16)]
    if emit_state:
        out_specs.append(pl.BlockSpec((None, 2, None, RET_DIM, RET_DIM), lambda s, h: (s, 0, h, 0, 0)))
        out_shape.append(jax.ShapeDtypeStruct((n_seq, 2, RET_HEADS, RET_DIM, RET_DIM), F32))
    return pl.pallas_call(
        functools.partial(_ret_kernel, seq_len=seq_len, emit_state=emit_state, aliased=y_prev is not None),
        grid=(n_seq, RET_HEADS),
        in_specs=in_specs,
        out_specs=out_specs,
        out_shape=out_shape,
        scratch_shapes=[pltpu.VMEM((nc + 1, RET_DIM, RET_DIM), F32), pltpu.VMEM((nc + 1, RET_DIM, RET_DIM), F32)],
        input_output_aliases=aliases,
        compiler_params=_cp("parallel", "parallel"),
        name="retention_%d" % seq_len,
    )(*args)


def _softmax_pv(score_blocks, value_blocks):
    m = functools.reduce(jnp.maximum, [jnp.max(s, axis=-1, keepdims=True) for s in score_blocks])
    ps = [jnp.exp(s - m) for s in score_blocks]
    l = functools.reduce(lambda a, b: a + b, [jnp.sum(p, axis=-1, keepdims=True) for p in ps])
    o = functools.reduce(lambda a, b: a + b, [_dot(p.astype(BF16), v) for p, v in zip(ps, value_blocks)])
    return o / l


def _low_half(shape):
    return lax.broadcasted_iota(jnp.int32, shape, 1) < 64


def _mla_ctx_kernel(q_ref, k_ref, v_ref, _, y_ref):
    low = _low_half((SEQ, 128))
    for hp in range(MLA_HEADS // 2):
        vs = v_ref[:, hp * 128:(hp + 1) * 128]
        outs = []
        for e in range(2):
            sl = slice((2 * hp + e) * HEAD_SLAB, (2 * hp + e + 1) * HEAD_SLAB)
            s = _dot_nt(q_ref[:, sl], k_ref[:, sl]) * MLA_SCALE
            outs.append(_softmax_pv([s], [vs]))
        y_ref[:, hp * 128:(hp + 1) * 128] = jnp.where(low, outs[0], outs[1]).astype(BF16)


def _mla_ctx(qp, kp, v, y_prev):
    return pl.pallas_call(
        _mla_ctx_kernel,
        grid=(BATCH,),
        in_specs=[pl.BlockSpec((SEQ, MLA_QW), lambda b: (b, 0)), pl.BlockSpec((SEQ, MLA_QW), lambda b: (b, 0)),
                  pl.BlockSpec((SEQ, MLA_HEADS * MLA_V), lambda b: (b, 0)), pl.BlockSpec(memory_space=pl.ANY)],
        out_specs=pl.BlockSpec((SEQ, MLA_HEADS * MLA_V), lambda b: (b, 1)),
        out_shape=jax.ShapeDtypeStruct((N_TOK, D), BF16),
        input_output_aliases={3: 0},
        compiler_params=_cp("parallel"),
        name="mla_ctx",
    )(qp, kp, v, y_prev)


def _mla_lat_kernel(q_ref, k_ref, v_ref, kc_ref, vc_ref, _, y_ref):
    low = _low_half((TM, 128))
    for hp in range(MLA_HEADS // 2):
        vsl = slice(hp * 128, (hp + 1) * 128)
        outs = []
        for e in range(2):
            sl = slice((2 * hp + e) * HEAD_SLAB, (2 * hp + e + 1) * HEAD_SLAB)
            q = q_ref[:, sl]
            s1 = _dot_nt(q, k_ref[:, sl]) * MLA_SCALE
            s2 = _dot_nt(q, kc_ref[:, sl]) * MLA_SCALE
            outs.append(_softmax_pv([s1, s2], [v_ref[:, vsl], vc_ref[:, vsl]]))
        y_ref[:, vsl] = jnp.where(low, outs[0], outs[1]).astype(BF16)


def _mla_lat(qp, kp, v, kp_c, v_c, y_prev):
    seq_blk = N_TOK_P // DEC_SEQ
    return pl.pallas_call(
        _mla_lat_kernel,
        grid=(DEC_BATCH, TILES_PER_DEC),
        in_specs=[
            pl.BlockSpec((TM, MLA_QW), lambda b, t: (NT_P + b * TILES_PER_DEC + t, 0)),
            pl.BlockSpec((DEC_SEQ, MLA_QW), lambda b, t: (seq_blk + b, 0)),
            pl.BlockSpec((DEC_SEQ, MLA_HEADS * MLA_V), lambda b, t: (seq_blk + b, 0)),
            pl.BlockSpec((PAST_LEN, MLA_QW), lambda b, t: (b, 0)),
            pl.BlockSpec((PAST_LEN, MLA_HEADS * MLA_V), lambda b, t: (b, 0)),
            pl.BlockSpec(memory_space=pl.ANY),
        ],
        out_specs=pl.BlockSpec((TM, MLA_HEADS * MLA_V), lambda b, t: (NT_P + b * TILES_PER_DEC + t, 1)),
        out_shape=jax.ShapeDtypeStruct((N_TOK, D), BF16),
        input_output_aliases={5: 0},
        compiler_params=_cp("parallel", "parallel"),
        name="mla_lat",
    )(qp, kp, v, kp_c, v_c, y_prev)


def _out_even_kernel(x_ref, y_ref, mod_ref, g_ref, w_ref, o_ref):
    mod = mod_ref[...]
    r = _dot(y_ref[...], w_ref[...])
    o_ref[...] = x_ref[...] + mod[:, 2 * D:3 * D] * _rms(r, g_ref[...])


def _out_even(x, y, mods, gains, layer, w):
    row = pl.BlockSpec((TM, D), lambda i: (i, 0))
    return pl.pallas_call(
        _out_even_kernel,
        grid=(NT,),
        in_specs=[row, row, _mod_spec(layer), _gain_spec(layer), _full(w.shape)],
        out_specs=row,
        out_shape=jax.ShapeDtypeStruct((N_TOK, D), F32),
        compiler_params=_cp("parallel"),
        name="out_even",
    )(x, y, mods, gains, w)


def _gelu_tanh(x):
    return 0.5 * x * (1.0 + jnp.tanh(np.sqrt(2.0 / np.pi).astype(np.float32) * (x + 0.044715 * (x * x * x))))


def _out_odd_kernel(x_ref, yr_ref, u_ref, yna_ref, mod_ref, g_ref, d_ref, gw_ref, gb_ref, w_ref, o_ref):
    mod = mod_ref[...]
    y = _gelu_tanh(yr_ref[...] + d_ref[...] * u_ref[...])
    y = y * _sigmoid(_dot(y.astype(BF16), gw_ref[...]) + gb_ref[...])
    r = _dot(y.astype(BF16), w_ref[:S5_W, :]) + _dot(yna_ref[...], w_ref[S5_W:, :])
    o_ref[...] = x_ref[...] + mod[:, 2 * D:3 * D] * _rms(r, g_ref[...])


def _out_odd(x, y_raw, u, y, mods, gains, layer, d_skip, glu_w, glu_b, w):
    row = pl.BlockSpec((TM, D), lambda i: (i, 0))
    half = pl.BlockSpec((TM, S5_W), lambda i: (i, 0))
    return pl.pallas_call(
        _out_odd_kernel,
        grid=(NT,),
        in_specs=[row, half, half, pl.BlockSpec((TM, NA_W), lambda i: (i, 1)), _mod_spec(layer), _gain_spec(layer),
                  _full(d_skip.shape), _full(glu_w.shape), _full(glu_b.shape), _full(w.shape)],
        out_specs=row,
        out_shape=jax.ShapeDtypeStruct((N_TOK, D), F32),
        compiler_params=_cp("parallel"),
        name="out_odd",
    )(x, y_raw, u, y, mods, gains, d_skip, glu_w, glu_b, w)


FF_EXT = 16


def _ffn_kernel(xm_ref, xp_ref, xn_ref, mod_ref, gpre_ref, gpost_ref, wu_ref, cw_ref, cb_ref, wd_ref, o_ref, h_scr,
                act_scr):
    i = pl.program_id(0)
    t = (i - NT_P) % TILES_PER_DEC
    has_prev = jnp.logical_and(i >= NT_P, t != 0)
    has_next = jnp.logical_and(i >= NT_P, t != TILES_PER_DEC - 1)
    mod = mod_ref[...]
    shift = mod[:, 3 * D:4 * D]
    scale = mod[:, 4 * D:5 * D]
    gate = mod[:, 5 * D:6 * D]
    gpre = gpre_ref[...]
    nm = lambda x: _rms(x, gpre) * (1.0 + scale) + shift
    x = xm_ref[...]
    h_scr[0:TM, :] = nm(x).astype(BF16)
    ext = jnp.concatenate([jnp.where(has_next, nm(xn_ref[...]), 0.0), jnp.where(has_prev, nm(xp_ref[...]), 0.0)],
                          axis=0)
    h_scr[TM:, :] = ext.astype(BF16)
    hb = h_scr[...]
    rows = TM + FF_EXT

    def up(j):
        ca = slice(j * FF_CHUNK, (j + 1) * FF_CHUNK)
        cg = slice(D_FF + j * FF_CHUNK, D_FF + (j + 1) * FF_CHUNK)
        return (_dot(hb, wu_ref[:, ca]), ca), (_dot(hb, wu_ref[:, cg]), cg)

    def conv(part):
        u, cols = part
        cw = cw_ref[:, cols]
        return (cw[0:1, :] * pltpu.roll(u, 1, axis=0)[:TM] + cw[1:2, :] * u[:TM]
                + cw[2:3, :] * pltpu.roll(u, rows - 1, axis=0)[:TM] + cb_ref[:, cols])

    nxt = up(0)
    for j in range(FF_NCHUNK):
        cur = nxt
        if j + 1 < FF_NCHUNK:
            nxt = up(j + 1)
        a = conv(cur[0])
        g = conv(cur[1])
        act_scr[:, j * FF_CHUNK:(j + 1) * FF_CHUNK] = (_silu(g) * a).astype(BF16)
    y = _dot(act_scr[...], wd_ref[...])
    o_ref[...] = x + gate * _rms(y, gpost_ref[...])


def _ffn(x, mods, gpre, gpost, layer, wu, cw, cb, wd):
    hb = TM // 8
    nblk = N_TOK // 8
    row = pl.BlockSpec((TM, D), lambda i: (i, 0))
    prev = pl.BlockSpec((8, D), lambda i: (jnp.maximum(i * hb - 1, 0), 0))
    nxt = pl.BlockSpec((8, D), lambda i: (jnp.minimum((i + 1) * hb, nblk - 1), 0))
    return pl.pallas_call(
        _ffn_kernel,
        grid=(NT,),
        in_specs=[row, prev, nxt, _mod_spec(layer), _gain_spec(layer), _gain_spec(layer), _full(wu.shape),
                  _full(cw.shape), _full(cb.shape), _full(wd.shape)],
        out_specs=row,
        out_shape=jax.ShapeDtypeStruct((N_TOK, D), F32),
        scratch_shapes=[pltpu.VMEM((TM + FF_EXT, D), BF16), pltpu.VMEM((TM, D_FF), BF16)],
        compiler_params=_cp("parallel"),
        name="ffn",
    )(x, x, x, mods, gpre, gpost, wu, cw, cb, wd)


def _in_odd_kernel(x_ref, mod_ref, g_ref, w_ref, u_ref, qkv_ref, kv_ref):
    mod = mod_ref[...]
    h = _rms(x_ref[...], g_ref[...]) * (1.0 + mod[:, D:2 * D]) + mod[:, :D]
    r = _dot(h.astype(BF16), w_ref[...])
    u_ref[...] = r[:, :S5_W]
    qkv_ref[...] = r[:, S5_W:].astype(BF16)
    kv_ref[...] = r[:, S5_W + NA_W:]


def _in_odd(x, mods, gains, layer, w):
    row = lambda w_: pl.BlockSpec((TM, w_), lambda i: (i, 0))
    return pl.pallas_call(
        _in_odd_kernel,
        grid=(NT,),
        in_specs=[row(D), _mod_spec(layer), _gain_spec(layer), _full(w.shape)],
        out_specs=[row(S5_W), row(3 * NA_W), row(2 * NA_W)],
        out_shape=[jax.ShapeDtypeStruct((N_TOK, S5_W), F32), jax.ShapeDtypeStruct((N_TOK, 3 * NA_W), BF16),
                   jax.ShapeDtypeStruct((N_TOK, 2 * NA_W), F32)],
        compiler_params=_cp("parallel"),
        name="in_odd",
    )(x, mods, gains, w)


S5_TS = S5_CHUNK * S5_GROUP
S5_PL = 2 * S5_P


def _s5_prep_kernel(lre_ref, lim_ref, ls_ref, btr_ref, bti_ref, cr_ref, ci_ref, m_ref, n_ref, p_ref, a_ref, ct_scr):
    T = S5_CHUNK
    S = S5_GROUP
    hi = lax.Precision.HIGHEST
    low = lax.broadcasted_iota(jnp.int32, (S, S5_PL), 1) < S5_P
    half = [low, jnp.logical_not(low)]
    pick = lambda e, v: jnp.where(half[e], v, 0.0)
    nt = (((1,), (1,)), ((), ()))
    kps = [[None, None], [None, None]]
    for d in range(2):
        lre = lre_ref[d]
        lim = lim_ref[d]
        step = jnp.exp(ls_ref[d])
        mag = jnp.exp(lre * step)
        are = mag * jnp.cos(lim * step)
        aim = mag * jnp.sin(lim * step)
        den = lre * lre + lim * lim
        zr, zi = _cmul(are - 1.0, aim, lre / den, -lim / den)
        bbr, bbi = _cmul(zr, zi, btr_ref[d], bti_ref[d])
        cr = cr_ref[d]
        ci = ci_ref[d]
        pr = jnp.ones_like(are)
        pi = jnp.zeros_like(are)
        for k in range(T + 1):
            er, ei = _cmul(cr, ci, pr, pi)
            if k < T:
                jn = T - 1 - k if d == 0 else k
                wr, wi = _cmul(pr, pi, bbr, bbi)
                for e in range(2):
                    rows = slice(e * S5_TS + jn * S, e * S5_TS + (jn + 1) * S)
                    n_ref[d, rows, 0:S5_PL] = pick(e, wr).astype(BF16)
                    n_ref[d, rows, S5_PL:2 * S5_PL] = pick(e, wi).astype(BF16)
                jc = k if d == 0 else T - 1 - k
                ct_scr[0, jc * S:(jc + 1) * S, :] = er
                ct_scr[1, jc * S:(jc + 1) * S, :] = ei
            if k >= 1:
                t = k - 1 if d == 0 else T - k
                for e in range(2):
                    rows = slice(e * S5_TS + t * S, e * S5_TS + (t + 1) * S)
                    p_ref[rows, 2 * d * S5_PL:(2 * d + 1) * S5_PL] = pick(e, er).astype(BF16)
                    p_ref[rows, (2 * d + 1) * S5_PL:(2 * d + 2) * S5_PL] = pick(e, -ei).astype(BF16)
            if k == T:
                a_ref[d, 0] = pr
                a_ref[d, 1] = pi
            pr, pi = _cmul(pr, pi, are, aim)
        for e in range(2):
            kd = (lax.dot_general(pick(e, bbr), ct_scr[0], nt, precision=hi, preferred_element_type=F32)
                  - lax.dot_general(pick(e, bbi), ct_scr[1], nt, precision=hi, preferred_element_type=F32))
            kps[d][e] = jnp.concatenate([kd, jnp.zeros_like(kd)], axis=1)
    for e in range(2):
        for t in range(T):
            fwd = pltpu.roll(kps[0][e], t * S, axis=1)[:, :S5_TS]
            bwd = pltpu.roll(kps[1][e], (2 * S5_TS - (T - 1 - t) * S) % (2 * S5_TS), axis=1)[:, :S5_TS]
            m_ref[e, t * S:(t + 1) * S, :] = (fwd + bwd).astype(BF16)


def _s5_prep(lre, lim, ls, b_re, b_im, c_re, c_im):
    NP, S, P = S5_PAIRS, S5_GROUP, S5_P
    vec = lambda a: a.reshape(2, NP, 1, S5_PL)
    b_lay = lambda a: jnp.transpose(a.reshape(2, NP, 2, P, S), (0, 1, 4, 2, 3)).reshape(2, NP, S, S5_PL)
    c_lay = lambda a: jnp.transpose(a.reshape(2, NP, 2, S, P), (0, 1, 3, 2, 4)).reshape(2, NP, S, S5_PL)
    args = (vec(lre), vec(lim), vec(jnp.repeat(ls, P, axis=-1)), b_lay(b_re), b_lay(b_im), c_lay(c_re), c_lay(c_im))
    vspec = pl.BlockSpec((2, None, 1, S5_PL), lambda g: (0, g, 0, 0))
    mspec = pl.BlockSpec((2, None, S, S5_PL), lambda g: (0, g, 0, 0))
    lead = lambda shape: pl.BlockSpec((None,) + shape, lambda g: (g,) + (0,) * len(shape))
    return pl.pallas_call(
        _s5_prep_kernel,
        grid=(NP,),
        in_specs=[vspec, vspec, vspec, mspec, mspec, mspec, mspec],
        out_specs=[lead((2, S5_TS, S5_TS)), lead((2, 2 * S5_TS, 2 * S5_PL)), lead((2 * S5_TS, 4 * S5_PL)),
                   lead((2, 2, 1, S5_PL))],
        out_shape=[jax.ShapeDtypeStruct((NP, 2, S5_TS, S5_TS), BF16),
                   jax.ShapeDtypeStruct((NP, 2, 2 * S5_TS, 2 * S5_PL), BF16),
                   jax.ShapeDtypeStruct((NP, 2 * S5_TS, 4 * S5_PL), BF16),
                   jax.ShapeDtypeStruct((NP, 2, 2, 1, S5_PL), F32)],
        scratch_shapes=[pltpu.VMEM((2, S5_TS, S5_PL), F32)],
        compiler_params=_cp("parallel"),
        name="s5_prep",
    )(*args)


S5_NCH = N_TOK // S5_CHUNK
S5_ROWS_P = N_TOK_P // S5_CHUNK
S5_NC_P = SEQ // S5_CHUNK
S5_NC_S = DEC_SEQ // S5_CHUNK
S5_GPB = 8
S5_PPB = S5_GPB // 2
S5_XL = 8 * 128


def _s5_perm():
    r = np.arange(S5_XL)
    dst = (r // S5_GROUP % S5_GPB) * 128 + (r // 128) * S5_GROUP + r % S5_GROUP
    perm = np.zeros((S5_XL, S5_XL), np.float32)
    perm[r, dst] = 1.0
    return jnp.asarray(perm, BF16)


def _s5_kernel(u_ref, perm_ref, m_ref, n_ref, p_ref, a_ref, h0c_ref, h0l_ref, y_ref, fin_ref, z_scr, up_scr, e_scr,
               hin_scr, yc_scr):
    T = S5_CHUNK
    W = S5_PL
    for t in range(T):
        z_scr[t // 8, :, (t % 8) * 128:(t % 8 + 1) * 128] = u_ref[pl.ds(t, S5_NCH, stride=T), :].astype(BF16)
    perm = perm_ref[...]
    for j in range(2):
        up_scr[j] = _dot(z_scr[j], perm).astype(BF16)

    def scan(pp, d, h0_ref, n_chunks, n_seq, row0):
        are = a_ref[pp, d, 0]
        aim = a_ref[pp, d, 1]
        hr = h0_ref[pp, d, 0]
        hi = h0_ref[pp, d, 1]
        order = range(n_chunks) if d == 0 else reversed(range(n_chunks))
        for c in order:
            rows = pl.ds(row0 + c, n_seq, stride=n_chunks)
            hin_scr[2 * d, rows, :] = hr
            hin_scr[2 * d + 1, rows, :] = hi
            er = e_scr[2 * d, rows, :]
            ei = e_scr[2 * d + 1, rows, :]
            hr, hi = are * hr - aim * hi + er, are * hi + aim * hr + ei
        return hr, hi

    for pp in range(S5_PPB):
        us = []
        for e in range(2):
            sl = slice((2 * pp + e) * 128, (2 * pp + e + 1) * 128)
            us.append(jnp.concatenate([up_scr[0, :, sl], up_scr[1, :, sl]], axis=1))
        u2 = jnp.concatenate(us, axis=1)
        for d in range(2):
            ed = _dot(u2, n_ref[pp, d])
            e_scr[2 * d] = ed[:, :W]
            e_scr[2 * d + 1] = ed[:, W:]
        for d in range(2):
            hr, hi = scan(pp, d, h0c_ref, S5_NC_P, BATCH, 0)
            fin_ref[pp, d, 0] = hr
            fin_ref[pp, d, 1] = hi
            scan(pp, d, h0l_ref, S5_NC_S, DEC_BATCH, S5_ROWS_P)
        hin = jnp.concatenate([hin_scr[k] for k in range(4)], axis=1).astype(BF16)
        for e in range(2):
            y = _dot(us[e], m_ref[pp, e]) + _dot_nt(hin, p_ref[pp, e * S5_TS:(e + 1) * S5_TS, :])
            sl = slice((2 * pp + e) * 128, (2 * pp + e + 1) * 128)
            for j in range(2):
                yc_scr[j, :, sl] = y[:, j * 128:(j + 1) * 128]

    for j in range(2):
        yp = yc_scr[j]
        y_hi = yp.astype(BF16)
        y_lo = (yp - y_hi.astype(F32)).astype(BF16)
        r = _dot_nt(y_hi, perm) + _dot_nt(y_lo, perm)
        for k in range(8):
            y_ref[pl.ds(8 * j + k, S5_NCH, stride=T), :] = r[:, k * 128:(k + 1) * 128]


def _s5_scan(u, perm, m, n2, p2, a, h0c, h0l):
    nb = S5_GROUPS // S5_GPB
    lead = lambda shape: pl.BlockSpec((S5_PPB,) + shape, lambda w: (w,) + (0,) * len(shape))
    col = pl.BlockSpec((N_TOK, 128), lambda w: (0, w))
    return pl.pallas_call(
        _s5_kernel,
        grid=(nb,),
        in_specs=[col, _full(perm.shape), lead((2, S5_TS, S5_TS)), lead((2, 2 * S5_TS, 2 * S5_PL)),
                  lead((2 * S5_TS, 4 * S5_PL)), lead((2, 2, 1, S5_PL)), lead((2, 2, BATCH, S5_PL)),
                  lead((2, 2, DEC_BATCH, S5_PL))],
        out_specs=[col, lead((2, 2, BATCH, S5_PL))],
        out_shape=[jax.ShapeDtypeStruct((N_TOK, S5_W), F32),
                   jax.ShapeDtypeStruct((S5_PAIRS, 2, 2, BATCH, S5_PL), F32)],
        scratch_shapes=[pltpu.VMEM((2, S5_NCH, S5_XL), BF16), pltpu.VMEM((2, S5_NCH, S5_XL), BF16),
                        pltpu.VMEM((4, S5_NCH, S5_PL), F32), pltpu.VMEM((4, S5_NCH, S5_PL), F32),
                        pltpu.VMEM((2, S5_NCH, S5_XL), F32)],
        compiler_params=_cp("parallel"),
        name="s5_scan",
    )(u, perm, m, n2, p2, a, h0c, h0l)


def _na_ctx_kernel(q_ref, k_ref, v_ref, y_ref):
    low = _low_half((SEQ, 128))
    for hp in range(NA_HEADS // 2):
        sl = slice(hp * 128, (hp + 1) * 128)
        q = q_ref[:, sl]
        k = k_ref[:, sl]
        v = v_ref[:, sl]
        outs = []
        for e in range(2):
            qm = jnp.where(low == (e == 0), q, jnp.zeros_like(q))
            outs.append(_softmax_pv([_dot_nt(qm, k) * NA_SCALE], [v]))
        y_ref[:, sl] = jnp.where(low, outs[0], outs[1]).astype(BF16)


def _na_ctx(qkv):
    col = lambda part: pl.BlockSpec((SEQ, NA_W), lambda b: (b, part))
    return pl.pallas_call(
        _na_ctx_kernel,
        grid=(BATCH,),
        in_specs=[col(0), col(1), col(2)],
        out_specs=pl.BlockSpec((SEQ, NA_W), lambda b: (b, 1)),
        out_shape=jax.ShapeDtypeStruct((N_TOK, D), BF16),
        compiler_params=_cp("parallel"),
        name="na_ctx",
    )(qkv, qkv, qkv)


def _na_key_row0(rb):
    return jnp.clip(NA_QROWS * rb - NA_WIN_R // 2, 0, GRID_H - NA_KROWS)


NA_NDR = 2 * NA_WIN_R - 1


def _na_lat_kernel(q_ref, ks_ref, vs_ref, kc_ref, vc_ref, tab_ref, _, y_ref):
    rb = pl.program_id(1)
    u0 = _na_key_row0(rb)
    start = pl.multiple_of(u0 * GRID_W, GRID_W)
    nk = NA_KROWS * GRID_W
    low = _low_half((TM, 128))
    low_t = _low_half((GRID_W, 128))

    def table_row(i, w):
        qr = NA_QROWS * rb + i
        kr = u0 + w
        rs = jnp.clip(qr - NA_WIN_R // 2, 0, GRID_H - NA_WIN_R)
        inside = jnp.logical_and(kr >= rs, kr < rs + NA_WIN_R)
        return jnp.where(inside, kr - qr + NA_WIN_R - 1, NA_NDR)

    idx = [[table_row(i, w) for w in range(NA_KROWS)] for i in range(NA_QROWS)]

    def bias(h):
        rows = [jnp.concatenate([jnp.where(low_t, tab_ref[h, idx[i][w]], tab_ref[h, idx[i][w + 1]])
                                 for w in range(0, NA_KROWS, 2)], axis=1) for i in range(NA_QROWS)]
        return jnp.concatenate(rows, axis=0)

    for hp in range(NA_HEADS // 2):
        sl = slice(hp * 128, (hp + 1) * 128)
        q = q_ref[:, sl]
        kl = ks_ref[pl.ds(start, nk), sl]
        vl = vs_ref[pl.ds(start, nk), sl]
        kc = kc_ref[:, sl].astype(BF16)
        vc = vc_ref[:, sl].astype(BF16)
        outs = []
        for e in range(2):
            qm = jnp.where(low == (e == 0), q, jnp.zeros_like(q))
            s1 = _dot_nt(qm, kl) * NA_SCALE + bias(2 * hp + e)
            s2 = _dot_nt(qm, kc) * NA_SCALE
            outs.append(_softmax_pv([s1, s2], [vl, vc]))
        y_ref[:, sl] = jnp.where(low, outs[0], outs[1]).astype(BF16)


def _na_lat(qkv, k_c, v_c, table, y_prev):
    seq_blk = N_TOK_P // DEC_SEQ
    nb = GRID_H // NA_QROWS
    return pl.pallas_call(
        _na_lat_kernel,
        grid=(DEC_BATCH, nb),
        in_specs=[
            pl.BlockSpec((TM, NA_W), lambda b, r: (NT_P + b * nb + r, 0)),
            pl.BlockSpec((DEC_SEQ, NA_W), lambda b, r: (seq_blk + b, 1)),
            pl.BlockSpec((DEC_SEQ, NA_W), lambda b, r: (seq_blk + b, 2)),
            pl.BlockSpec((PAST_LEN, NA_W), lambda b, r: (b, 0)),
            pl.BlockSpec((PAST_LEN, NA_W), lambda b, r: (b, 0)),
            pl.BlockSpec(table.shape, lambda b, r: (0, 0, 0, 0)),
            pl.BlockSpec(memory_space=pl.ANY),
        ],
        out_specs=pl.BlockSpec((TM, NA_W), lambda b, r: (NT_P + b * nb + r, 1)),
        out_shape=jax.ShapeDtypeStruct((N_TOK, D), BF16),
        input_output_aliases={6: 0},
        compiler_params=_cp("parallel", "arbitrary"),
        name="na_lat",
    )(qkv, qkv, qkv, k_c, v_c, table, y_prev)


def _na_bias_table(rpb):
    col = np.arange(GRID_W)
    cs = np.clip(col - NA_WIN_C // 2, 0, GRID_W - NA_WIN_C)
    in_band = (col[None, :] >= cs[:, None]) & (col[None, :] < cs[:, None] + NA_WIN_C)
    dc = np.clip(col[None, :] - col[:, None] + NA_WIN_C - 1, 0, 2 * NA_WIN_C - 2)
    t = jnp.where(in_band, rpb.astype(F32)[:, :, dc], -jnp.inf)
    t = jnp.concatenate([t, jnp.full((NA_HEADS, 1, GRID_W, GRID_W), -jnp.inf, F32)], axis=1)
    return jnp.concatenate([t, t], axis=-1)


def _rope_tables():
    n_freq = MLA_ROPE // 4
    inv = ROPE_BASE ** (-jnp.arange(n_freq, dtype=F32) / n_freq)
    t = jnp.arange(DEC_SEQ)
    row = (t // GRID_W).astype(F32)
    colp = (t % GRID_W).astype(F32)
    ang = jnp.concatenate([row[:, None] * inv, colp[:, None] * inv], axis=-1)
    cos, sin = jnp.cos(ang), jnp.sin(ang)
    one = jnp.ones((DEC_SEQ, MLA_NOPE), F32)
    zero = jnp.zeros((DEC_SEQ, MLA_NOPE), F32)
    cos_s = jnp.concatenate([one, cos, cos, one[:, :32]], axis=-1)
    sin_s = jnp.concatenate([zero, -sin, sin, zero[:, :32]], axis=-1)
    cos_t = jnp.concatenate([jnp.ones((TM, HEAD_SLAB), F32), cos_s], axis=0)
    sin_t = jnp.concatenate([jnp.zeros((TM, HEAD_SLAB), F32), sin_s], axis=0)
    return cos_t, sin_t


def _even_weights(w_in, w_uq, w_ukv):
    half = MLA_ROPE // 2
    base = 4 * RET_W + MLA_Q_RANK + MLA_KV_RANK
    wkr = w_in[:, base:]
    z64 = jnp.zeros((D, MLA_NOPE), F32)
    z32 = jnp.zeros((D, HEAD_SLAB - MLA_NOPE - MLA_ROPE), F32)
    wkr2 = jnp.concatenate([z64, wkr, z32, z64, wkr[:, half:], wkr[:, :half], z32], axis=1).astype(BF16)
    w_e = w_in.astype(BF16)
    wq = w_uq.reshape(MLA_Q_RANK, MLA_HEADS, MLA_NOPE + MLA_ROPE)
    nope, rope = wq[..., :MLA_NOPE], wq[..., MLA_NOPE:]
    zq64 = jnp.zeros_like(nope)
    zq32 = jnp.zeros_like(rope)
    q_slab = jnp.concatenate([nope, rope, zq32], axis=-1).reshape(MLA_Q_RANK, MLA_QW)
    q_sw = jnp.concatenate([zq64, rope[..., half:], rope[..., :half], zq32], axis=-1).reshape(MLA_Q_RANK, MLA_QW)
    wuq2 = jnp.concatenate([q_slab, q_sw], axis=1).astype(BF16)
    wkv = w_ukv.reshape(MLA_KV_RANK, MLA_HEADS, MLA_NOPE + MLA_V)
    wk = jnp.concatenate([wkv[..., :MLA_NOPE], jnp.zeros_like(wkv[..., :MLA_NOPE])], axis=-1)
    wk = wk.reshape(MLA_KV_RANK, MLA_QW).astype(BF16)
    wv = wkv[..., MLA_NOPE:].reshape(MLA_KV_RANK, MLA_HEADS * MLA_V).astype(BF16)
    return w_e, wkr2, wuq2, wk, wv


def kernel(x_prompt, x_sample, c, state_ret, cache_mla_ckv, cache_mla_krope, state_s5_re, state_s5_im, cache_na_k, cache_na_v, c_ctx, ada_w, ada_b, mix_pre_g, mix_post_g, ffn_pre_g, ffn_post_g, ffn_w_up, ffn_conv_w, ffn_conv_b, ffn_w_down, even_w_in, even_w_out, ret_logit, ret_gn, mla_q_norm, mla_w_uq, mla_kv_norm, mla_w_ukv, odd_w_in, odd_w_out, s5_lambda_re, s5_lambda_im, s5_log_step, s5_b_re, s5_b_im, s5_c_re, s5_c_im, s5_d, s5_glu_w, s5_glu_b, na_rpb):
    x = jnp.concatenate([x_prompt.reshape(N_TOK_P, D), x_sample.reshape(N_TOK_S, D)], axis=0)
    cvec = jnp.concatenate([c_ctx[None, :], c, jnp.zeros((8 - 1 - DEC_BATCH, D), F32)], axis=0)
    mods = _ada_mods(cvec, ada_w, ada_b)
    g3 = lambda a: a.reshape(DEPTH, 1, D)
    mix_pre, mix_post, ffn_pre, ffn_post = g3(mix_pre_g), g3(mix_post_g), g3(ffn_pre_g), g3(ffn_post_g)
    cos_t, sin_t = _rope_tables()
    perm = _s5_perm()
    new_ret, new_ckv, new_kr, new_s5_re, new_s5_im, new_nak, new_nav = [], [], [], [], [], [], []
    for layer in range(DEPTH):
        j = layer // 2
        if layer % 2 == 0:
            w_e, wkr2, wuq2, wk, wv = _even_weights(even_w_in[j], mla_w_uq[j], mla_w_ukv[j])
            qkvg, qp, kp, v, ckvn, kr = _in_even(
                x, mods, mix_pre, layer, w_e, wkr2, mla_q_norm[j][None, :], mla_kv_norm[j][None, :], wuq2, wk, wv,
                cos_t, sin_t)
            logit = jnp.transpose(ret_logit[j]).reshape(RET_HEADS, 2, 1, 1)
            gn = ret_gn[j][None, :]
            s0 = jnp.zeros((BATCH, 2, RET_HEADS, RET_DIM, RET_DIM), F32)
            y, st = _retention(qkvg, logit, gn, s0, None, seq_len=SEQ, n_seq=BATCH, row_block0=0, emit_state=True)
            (y,) = _retention(qkvg, logit, gn, state_ret[:, j], y, seq_len=DEC_SEQ, n_seq=DEC_BATCH,
                              row_block0=N_TOK_P // DEC_SEQ, emit_state=False)
            y = _mla_ctx(qp, kp, v, y)
            kr_c = jnp.pad(cache_mla_krope[:, j].reshape(DEC_BATCH * PAST_LEN, MLA_ROPE),
                           ((0, 0), (MLA_NOPE, HEAD_SLAB - MLA_NOPE - MLA_ROPE)))
            kp_c, v_c = _mla_cache(cache_mla_ckv[:, j].reshape(DEC_BATCH * PAST_LEN, MLA_KV_RANK), kr_c, wk, wv)
            y = _mla_lat(qp, kp, v, kp_c, v_c, y)
            x = _out_even(x, y, mods, mix_post, layer, even_w_out[j].astype(BF16))
            new_ret.append(st)
            new_ckv.append(ckvn[:N_TOK_P].reshape(BATCH, SEQ, MLA_KV_RANK))
            new_kr.append(kr[:N_TOK_P, MLA_NOPE:MLA_NOPE + MLA_ROPE].reshape(BATCH, SEQ, MLA_ROPE))
        else:
            u, qkv, kv = _in_odd(x, mods, mix_pre, layer, odd_w_in[j].astype(BF16))
            m, n2, p2, a = _s5_prep(s5_lambda_re[j], s5_lambda_im[j], s5_log_step[j], s5_b_re[j], s5_b_im[j],
                                    s5_c_re[j], s5_c_im[j])
            h0c = jnp.zeros((S5_PAIRS, 2, 2, BATCH, S5_PL), F32)
            h0 = jnp.stack([state_s5_re[:, j], state_s5_im[:, j]], axis=0)
            h0l = jnp.transpose(h0.reshape(2, DEC_BATCH, 2, S5_PAIRS, S5_PL), (3, 2, 0, 1, 4))
            y_raw, fin = _s5_scan(u, perm, m, n2, p2, a, h0c, h0l)
            y = _na_ctx(qkv)
            bias = _na_bias_table(na_rpb[j])
            y = _na_lat(qkv, cache_na_k[:, j].reshape(DEC_BATCH * PAST_LEN, NA_W),
                        cache_na_v[:, j].reshape(DEC_BATCH * PAST_LEN, NA_W), bias, y)
            x = _out_odd(x, y_raw, u, y, mods, mix_post, layer, s5_d[j][None, :], s5_glu_w[j].astype(BF16),
                         s5_glu_b[j][None, :], odd_w_out[j].astype(BF16))
            st = jnp.transpose(fin.reshape(S5_PAIRS, 2, 2, BATCH, 2, S5_P), (2, 3, 1, 0, 4, 5))
            st = st.reshape(2, BATCH, 2, S5_GROUPS, S5_P)
            new_s5_re.append(st[0])
            new_s5_im.append(st[1])
            new_nak.append(kv[:N_TOK_P, :NA_W].reshape(BATCH, SEQ, NA_HEADS, NA_DIM))
            new_nav.append(kv[:N_TOK_P, NA_W:].reshape(BATCH, SEQ, NA_HEADS, NA_DIM))
        x = _ffn(x, mods, ffn_pre, ffn_post, layer, ffn_w_up[layer].astype(BF16), ffn_conv_w[layer],
                 ffn_conv_b[layer][None, :], ffn_w_down[layer].astype(BF16))
    stack = lambda a: jnp.stack(a, axis=1)
    return (x[:N_TOK_P].reshape(BATCH, SEQ, D), x[N_TOK_P:].reshape(DEC_BATCH, DEC_SEQ, D), stack(new_ret),
            stack(new_ckv), stack(new_kr), stack(new_s5_re), stack(new_s5_im), stack(new_nak), stack(new_nav))
```

```python
import functools

import numpy as np
import jax
import jax.numpy as jnp
from jax import lax
from jax.experimental import pallas as pl
from jax.experimental.pallas import tpu as pltpu

F32 = jnp.float32
BF16 = jnp.bfloat16

D = 1024
BATCH = 16
SEQ = 256
DEPTH = 4
DEC_BATCH = 2
DEC_SEQ = 2048
PAST_LEN = 512
GRID_W = 64
GRID_H = DEC_SEQ // GRID_W
EPS = 1e-6

RET_HEADS = 4
RET_W = 512
RET_DIM = 128
RET_CHUNK = 256

MLA_HEADS = 8
MLA_NOPE = 64
MLA_ROPE = 32
MLA_V = 64
MLA_Q_RANK = 256
MLA_KV_RANK = 128
MLA_SCALE = (MLA_NOPE + MLA_ROPE) ** -0.5
ROPE_BASE = 10000.0
HEAD_SLAB = 128
MLA_QW = MLA_HEADS * HEAD_SLAB
MLA_VW = MLA_HEADS * MLA_V

S5_W = 512
S5_GROUP = 16
S5_GROUPS = 32
S5_P = 64
S5_CHUNK = 16
S5_PAIRS = S5_GROUPS // 2

NA_HEADS = 8
NA_W = 512
NA_DIM = 64
NA_WIN_R = 8
NA_WIN_C = 16
NA_SCALE = NA_DIM ** -0.5
NA_QROWS = 4
NA_KROWS = 12
NA_NDR = 2 * NA_WIN_R - 1
NA_NDC = 2 * NA_WIN_C - 1

D_FF = 2816
FF_CHUNK = 256
FF_NCHUNK = D_FF // FF_CHUNK
FF_EXT = 16

TM = 256
N_TOK_P = BATCH * SEQ
N_TOK_S = DEC_BATCH * DEC_SEQ
N_TOK = N_TOK_P + N_TOK_S
NT_P = N_TOK_P // TM
NT_S = N_TOK_S // TM
NT = NT_P + NT_S
TILES_PER_DEC = DEC_SEQ // TM

VMEM_LIMIT = 56 * 1024 * 1024


def _cp(*sem):
    return pltpu.CompilerParams(dimension_semantics=sem, vmem_limit_bytes=VMEM_LIMIT)


def _dot(a, b):
    return jnp.dot(a, b, preferred_element_type=F32)


def _dot_nt(a, b):
    return lax.dot_general(a, b, (((1,), (1,)), ((), ())), preferred_element_type=F32)


def _dot_tn(a, b):
    return lax.dot_general(a, b, (((0,), (0,)), ((), ())), preferred_element_type=F32)


def _rms(x, g):
    return x * lax.rsqrt(jnp.mean(x * x, axis=-1, keepdims=True) + EPS) * g


def _sigmoid(x):
    return 1.0 / (1.0 + jnp.exp(-x))


def _silu(x):
    return x * _sigmoid(x)


def _cmul(ar, ai, br, bi):
    return ar * br - ai * bi, ar * bi + ai * br


def _mrow(i):
    return jnp.where(i < NT_P, 0, 1 + (i - NT_P) // TILES_PER_DEC)


def _full(shape):
    n = len(shape)
    return pl.BlockSpec(shape, lambda *_: (0,) * n)


def _layer(layer, shape):
    n = len(shape)
    return pl.BlockSpec((None,) + shape, lambda *_: (layer,) + (0,) * n)


def _mod_spec(layer):
    return pl.BlockSpec((None, None, 1, 6 * D), lambda i: (layer, _mrow(i), 0, 0))


def _row(width):
    return pl.BlockSpec((TM, width), lambda i: (i, 0))


def _ctx_row(width):
    return pl.BlockSpec((TM, width), lambda i: (jnp.minimum(i, NT_P - 1), 0))


def _lat_row(width):
    return pl.BlockSpec((TM, width), lambda i: (jnp.maximum(i - NT_P, 0), 0))


def _pick(a_ref, b_ref):
    return jnp.where(pl.program_id(0) < NT_P, a_ref[...], b_ref[...])


def _ada_kernel(c_ref, w_ref, b_ref, o_ref):
    o_ref[...] = _dot(_silu(c_ref[...]).astype(BF16), w_ref[...].astype(BF16)) + b_ref[...]


def _ada_mods(cvec, ada_w, ada_b):
    nb = 4
    bn = 6 * D // nb
    out = pl.pallas_call(
        _ada_kernel,
        grid=(DEPTH, nb),
        in_specs=[
            pl.BlockSpec((8, D), lambda l, n: (0, 0)),
            pl.BlockSpec((None, D, bn), lambda l, n: (l, 0, n)),
            pl.BlockSpec((None, 1, bn), lambda l, n: (l, 0, n)),
        ],
        out_specs=pl.BlockSpec((None, 8, bn), lambda l, n: (l, 0, n)),
        out_shape=jax.ShapeDtypeStruct((DEPTH, 8, 6 * D), F32),
        compiler_params=_cp("arbitrary", "arbitrary"),
        name="ada_mods",
    )(cvec, ada_w, ada_b.reshape(DEPTH, 1, 6 * D))
    return out[:, :3].reshape(DEPTH, 3, 1, 6 * D)


def _in_even_kernel(*refs, split_x):
    nx = 2 if split_x else 1
    x = _pick(*refs[:2]) if split_x else refs[0][...]
    (mod_ref, g_ref, w_ref, wkr_ref, qn_ref, kvn_ref, wuq_ref, wk_ref, wv_ref, cos_ref, sin_ref,
     qkvg_ref, qp_ref, kp_ref, v_ref, ckv_ref, kr_ref) = refs[nx:]
    mod = mod_ref[...]
    h = _rms(x, g_ref[...]) * (1.0 + mod[:, D:2 * D]) + mod[:, :D]
    hb = h.astype(BF16)
    o = 4 * RET_W
    r = _dot(hb, w_ref[:, :o + MLA_Q_RANK + MLA_KV_RANK])
    qkvg_ref[...] = r[:, :o].astype(BF16)
    cq = r[:, o:o + MLA_Q_RANK]
    o += MLA_Q_RANK
    ckv_raw = r[:, o:o + MLA_KV_RANK]
    r2 = _dot(hb, wkr_ref[...])
    kr = r2[:, :HEAD_SLAB]
    krs = r2[:, HEAD_SLAB:]
    cosf = cos_ref[...]
    sinf = sin_ref[...]
    q2 = _dot(_rms(cq, qn_ref[...]).astype(BF16), wuq_ref[...])
    ckvn = _rms(ckv_raw, kvn_ref[...])

    @pl.when(pl.program_id(0) < NT_P)
    def _():
        ckv_ref[...] = ckvn
        kr_ref[...] = kr

    cb = ckvn.astype(BF16)
    kp = _dot(cb, wk_ref[...])
    v_ref[...] = _dot(cb, wv_ref[...]).astype(BF16)
    krr = kr * cosf + krs * sinf
    for hh in range(MLA_HEADS):
        sl = slice(hh * HEAD_SLAB, (hh + 1) * HEAD_SLAB)
        ss = slice(MLA_QW + hh * HEAD_SLAB, MLA_QW + (hh + 1) * HEAD_SLAB)
        qp_ref[:, sl] = ((q2[:, sl] * cosf + q2[:, ss] * sinf) * MLA_SCALE).astype(BF16)
        kp_ref[:, sl] = (kp[:, sl] + krr).astype(BF16)


def _in_even(xs, mods, gains, layer, j, w_in, wkr2, qn, kvn, wuq2, wk, wv, cos_t, sin_t):
    split_x = len(xs) == 2
    x_specs = [_ctx_row(D), _lat_row(D)] if split_x else [_row(D)]
    pos_spec = pl.BlockSpec((TM, HEAD_SLAB), lambda i: (jnp.where(i < NT_P, 0, 1 + (i - NT_P) % TILES_PER_DEC), 0))
    return pl.pallas_call(
        functools.partial(_in_even_kernel, split_x=split_x),
        grid=(NT,),
        in_specs=x_specs + [_mod_spec(layer), _layer(layer, (1, D)), _layer(j, w_in.shape[1:]), _full(wkr2.shape),
                            _layer(j, (1, MLA_Q_RANK)), _layer(j, (1, MLA_KV_RANK)), _full(wuq2.shape),
                            _full(wk.shape), _full(wv.shape), pos_spec, pos_spec],
        out_specs=[_row(4 * RET_W), _row(MLA_QW), _row(MLA_QW), _row(MLA_VW), _ctx_row(MLA_KV_RANK),
                   _ctx_row(HEAD_SLAB)],
        out_shape=[
            jax.ShapeDtypeStruct((N_TOK, 4 * RET_W), BF16),
            jax.ShapeDtypeStruct((N_TOK, MLA_QW), BF16),
            jax.ShapeDtypeStruct((N_TOK, MLA_QW), BF16),
            jax.ShapeDtypeStruct((N_TOK, MLA_VW), BF16),
            jax.ShapeDtypeStruct((N_TOK_P, MLA_KV_RANK), F32),
            jax.ShapeDtypeStruct((N_TOK_P, HEAD_SLAB), F32),
        ],
        compiler_params=_cp("arbitrary"),
        name="in_even",
    )(*xs, mods, gains, w_in, wkr2, qn, kvn, wuq2, wk, wv, cos_t, sin_t)


def _mla_cache_kernel(ckv_ref, kr_ref, wk_ref, wv_ref, kp_ref, v_ref):
    cb = ckv_ref[...].astype(BF16)
    kp = _dot(cb, wk_ref[...])
    kr = kr_ref[...]
    for hh in range(MLA_HEADS):
        sl = slice(hh * HEAD_SLAB, (hh + 1) * HEAD_SLAB)
        kp_ref[:, sl] = (kp[:, sl] + kr).astype(BF16)
    v_ref[...] = _dot(cb, wv_ref[...]).astype(BF16)


def _mla_cache(ckv_c, kr_slab, wk, wv):
    row = lambda w: pl.BlockSpec((PAST_LEN, w), lambda b: (b, 0))
    return pl.pallas_call(
        _mla_cache_kernel,
        grid=(DEC_BATCH,),
        in_specs=[row(MLA_KV_RANK), row(HEAD_SLAB), _full(wk.shape), _full(wv.shape)],
        out_specs=[row(MLA_QW), row(MLA_VW)],
        out_shape=[jax.ShapeDtypeStruct((DEC_BATCH * PAST_LEN, MLA_QW), BF16),
                   jax.ShapeDtypeStruct((DEC_BATCH * PAST_LEN, MLA_VW), BF16)],
        compiler_params=_cp("parallel"),
        name="mla_cache",
    )(ckv_c, kr_slab, wk, wv)


def _ret_kernel(lg_ref, q_ref, k_ref, v_ref, g_ref, gn_ref, s0_ref, *rest, seq_len, n_blk, emit_state):
    if emit_state:
        y_ref, st_ref, sf_scr, sb_scr, dm_scr, w_scr = rest
    else:
        y_ref, sf_scr, sb_scr, dm_scr, w_scr = rest
    C = RET_CHUNK
    nc = seq_len // C

    @pl.when(pl.program_id(1) == 0)
    def _():
        lg = -jnp.log(1.0 + jnp.exp(-lg_ref[...]))
        lg_f = lg[0]
        lg_b = lg[1]
        ii = lax.broadcasted_iota(jnp.int32, (C, C), 0)
        jj = lax.broadcasted_iota(jnp.int32, (C, C), 1)
        diff = (ii - jj).astype(F32)
        dm_scr[...] = (jnp.where(diff >= 0, jnp.exp(lg_f * jnp.maximum(diff, 0.0)), 0.0)
                       + jnp.where(diff <= 0, jnp.exp(lg_b * jnp.maximum(-diff, 0.0)), 0.0))
        pos = lax.broadcasted_iota(jnp.int32, (C, RET_DIM), 0).astype(F32)
        w_scr[0] = jnp.exp(lg_f * (pos + 1.0))
        w_scr[1] = jnp.exp(lg_f * (C - 1.0 - pos))
        w_scr[2] = jnp.exp(lg_b * (C - pos))
        w_scr[3] = jnp.exp(lg_b * pos)
        w_scr[4] = jnp.exp(lg_f * C) + jnp.zeros((C, RET_DIM), F32)
        w_scr[5] = jnp.exp(lg_b * C) + jnp.zeros((C, RET_DIM), F32)

    qw_f, kw_f, qw_b, kw_b = w_scr[0], w_scr[1], w_scr[2], w_scr[3]
    cd_f = w_scr[4, :RET_DIM, :]
    cd_b = w_scr[5, :RET_DIM, :]
    ld = lambda ref, rows: ref[rows, :].astype(F32)
    gn = gn_ref[...]
    dm = dm_scr[...]

    for b in range(n_blk):
        rows_of = lambda n: slice(b * seq_len + n * C, b * seq_len + (n + 1) * C)
        sf = lambda n: sf_scr.at[b * (nc + 1) + n]
        sb = lambda n: sb_scr.at[b * (nc + 1) + n]
        sf(0)[...] = s0_ref[b, 0]
        for n in range(nc):
            kv = _dot_tn((ld(k_ref, rows_of(n)) * kw_f).astype(BF16), v_ref[rows_of(n), :])
            sf(n + 1)[...] = cd_f * sf(n)[...] + kv
        sb(nc)[...] = s0_ref[b, 1]
        for n in reversed(range(nc)):
            kv = _dot_tn((ld(k_ref, rows_of(n)) * kw_b).astype(BF16), v_ref[rows_of(n), :])
            sb(n)[...] = cd_b * sb(n + 1)[...] + kv
        if emit_state:
            st_ref[b, 0] = sf(nc)[...]
            st_ref[b, 1] = sb(0)[...]
        for n in range(nc):
            rows = rows_of(n)
            q = ld(q_ref, rows) * (RET_DIM ** -0.5)
            s = _dot_nt(q.astype(BF16), k_ref[rows, :]) * dm
            o = (_dot(s.astype(BF16), v_ref[rows, :])
                 + _dot((q * qw_f).astype(BF16), sf(n)[...].astype(BF16))
                 + _dot((q * qw_b).astype(BF16), sb(n + 1)[...].astype(BF16)))
            mu = jnp.mean(o, axis=-1, keepdims=True)
            oc = o - mu
            var = jnp.mean(oc * oc, axis=-1, keepdims=True)
            on = oc * lax.rsqrt(var + EPS) * gn
            y_ref[rows, :] = (_silu(ld(g_ref, rows)) * on).astype(BF16)


def _retention(qkvg, logit, gn, j, s0, *, seq_len, n_seq, n_blk, row0, emit_state):
    nc = seq_len // RET_CHUNK
    rows = n_blk * seq_len
    blk0 = row0 // rows
    col = lambda part: pl.BlockSpec((rows, RET_DIM), lambda h, s: (blk0 + s, part * RET_HEADS + h))
    state = pl.BlockSpec((n_blk, 2, None, RET_DIM, RET_DIM), lambda h, s: (s, 0, h, 0, 0))
    in_specs = [
        pl.BlockSpec((None, 2, 1, 1), lambda h, s: (h, 0, 0, 0)),
        col(0), col(1), col(2), col(3),
        pl.BlockSpec((None, 1, RET_DIM), lambda h, s: (j, 0, h)),
        state,
    ]
    out_specs = [pl.BlockSpec((rows, RET_DIM), lambda h, s: (s, h))]
    out_shape = [jax.ShapeDtypeStruct((n_seq * seq_len, RET_W), BF16)]
    if emit_state:
        out_specs.append(state)
        out_shape.append(jax.ShapeDtypeStruct((n_seq, 2, RET_HEADS, RET_DIM, RET_DIM), F32))
    n_st = n_blk * (nc + 1)
    return pl.pallas_call(
        functools.partial(_ret_kernel, seq_len=seq_len, n_blk=n_blk, emit_state=emit_state),
        grid=(RET_HEADS, n_seq // n_blk),
        in_specs=in_specs,
        out_specs=out_specs,
        out_shape=out_shape,
        scratch_shapes=[pltpu.VMEM((n_st, RET_DIM, RET_DIM), F32), pltpu.VMEM((n_st, RET_DIM, RET_DIM), F32),
                        pltpu.VMEM((RET_CHUNK, RET_CHUNK), F32), pltpu.VMEM((6, RET_CHUNK, RET_DIM), F32)],
        compiler_params=_cp("parallel", "arbitrary"),
        name="retention_%d" % seq_len,
    )(logit, qkvg, qkvg, qkvg, qkvg, gn, s0)


def _softmax_pv(score_blocks, value_blocks):
    m = functools.reduce(jnp.maximum, [jnp.max(s, axis=-1, keepdims=True) for s in score_blocks])
    ps = [jnp.exp(s - m) for s in score_blocks]
    l = functools.reduce(lambda a, b: a + b, [jnp.sum(p, axis=-1, keepdims=True) for p in ps])
    o = functools.reduce(lambda a, b: a + b, [_dot(p.astype(BF16), v) for p, v in zip(ps, value_blocks)])
    return o / l


def _low_half(shape):
    return lax.broadcasted_iota(jnp.int32, shape, 1) < 64


def _heads_pipelined(n_heads, scores, attend, y_ref, rows):
    low = _low_half((rows, 128))
    nxt = scores(0)
    outs = []
    for hh in range(n_heads):
        cur = nxt
        if hh + 1 < n_heads:
            nxt = scores(hh + 1)
        outs.append(attend(hh, cur))
        if hh % 2 == 1:
            hp = hh // 2
            y_ref[:, hp * 128:(hp + 1) * 128] = jnp.where(low, outs[hh - 1], outs[hh]).astype(BF16)


def _mla_ctx_kernel(q_ref, k_ref, v_ref, y_ref):
    slab = lambda hh: slice(hh * HEAD_SLAB, (hh + 1) * HEAD_SLAB)
    pair = lambda hh: slice(hh // 2 * 128, (hh // 2 + 1) * 128)
    scores = lambda hh: [_dot_nt(q_ref[:, slab(hh)], k_ref[:, slab(hh)])]
    attend = lambda hh, s: _softmax_pv(s, [v_ref[:, pair(hh)]])
    _heads_pipelined(MLA_HEADS, scores, attend, y_ref, SEQ)


def _mla_ctx(qp, kp, v):
    return pl.pallas_call(
        _mla_ctx_kernel,
        grid=(BATCH,),
        in_specs=[pl.BlockSpec((SEQ, MLA_QW), lambda b: (b, 0)), pl.BlockSpec((SEQ, MLA_QW), lambda b: (b, 0)),
                  pl.BlockSpec((SEQ, MLA_VW), lambda b: (b, 0))],
        out_specs=pl.BlockSpec((SEQ, MLA_VW), lambda b: (b, 0)),
        out_shape=jax.ShapeDtypeStruct((N_TOK_P, MLA_VW), BF16),
        compiler_params=_cp("parallel"),
        name="mla_ctx",
    )(qp, kp, v)


def _mla_lat_kernel(q_ref, k_ref, v_ref, kc_ref, vc_ref, y_ref):
    slab = lambda hh: slice(hh * HEAD_SLAB, (hh + 1) * HEAD_SLAB)
    pair = lambda hh: slice(hh // 2 * 128, (hh // 2 + 1) * 128)
    scores = lambda hh: [_dot_nt(q_ref[:, slab(hh)], k_ref[:, slab(hh)]),
                         _dot_nt(q_ref[:, slab(hh)], kc_ref[:, slab(hh)])]
    attend = lambda hh, s: _softmax_pv(s, [v_ref[:, pair(hh)], vc_ref[:, pair(hh)]])
    _heads_pipelined(MLA_HEADS, scores, attend, y_ref, TM)


def _mla_lat(qp, kp, v, kp_c, v_c):
    seq_blk = N_TOK_P // DEC_SEQ
    return pl.pallas_call(
        _mla_lat_kernel,
        grid=(DEC_BATCH, TILES_PER_DEC),
        in_specs=[
            pl.BlockSpec((TM, MLA_QW), lambda b, t: (NT_P + b * TILES_PER_DEC + t, 0)),
            pl.BlockSpec((DEC_SEQ, MLA_QW), lambda b, t: (seq_blk + b, 0)),
            pl.BlockSpec((DEC_SEQ, MLA_VW), lambda b, t: (seq_blk + b, 0)),
            pl.BlockSpec((PAST_LEN, MLA_QW), lambda b, t: (b, 0)),
            pl.BlockSpec((PAST_LEN, MLA_VW), lambda b, t: (b, 0)),
        ],
        out_specs=pl.BlockSpec((TM, MLA_VW), lambda b, t: (b * TILES_PER_DEC + t, 0)),
        out_shape=jax.ShapeDtypeStruct((N_TOK_S, MLA_VW), BF16),
        compiler_params=_cp("parallel", "parallel"),
        name="mla_lat",
    )(qp, kp, v, kp_c, v_c)


def _out_even_kernel(*refs, split_x):
    nx = 2 if split_x else 1
    x = _pick(*refs[:2]) if split_x else refs[0][...]
    yrc_ref, yrl_ref, ymc_ref, yml_ref, mod_ref, g_ref, w_ref, o_ref = refs[nx:]
    mod = mod_ref[...]
    r = _dot(_pick(yrc_ref, yrl_ref), w_ref[:RET_W, :]) + _dot(_pick(ymc_ref, yml_ref), w_ref[RET_W:, :])
    o_ref[...] = x + mod[:, 2 * D:3 * D] * _rms(r, g_ref[...])


def _out_even(xs, yr_c, yr_l, ym_c, ym_l, mods, gains, layer, j, w):
    split_x = len(xs) == 2
    x_specs = [_ctx_row(D), _lat_row(D)] if split_x else [_row(D)]
    return pl.pallas_call(
        functools.partial(_out_even_kernel, split_x=split_x),
        grid=(NT,),
        in_specs=x_specs + [_ctx_row(RET_W), _lat_row(RET_W), _ctx_row(MLA_VW), _lat_row(MLA_VW), _mod_spec(layer),
                            _layer(layer, (1, D)), _layer(j, (D, D))],
        out_specs=_row(D),
        out_shape=jax.ShapeDtypeStruct((N_TOK, D), F32),
        compiler_params=_cp("parallel"),
        name="out_even",
    )(*xs, yr_c, yr_l, ym_c, ym_l, mods, gains, w)


def _gelu_tanh(x):
    return 0.5 * x * (1.0 + jnp.tanh(np.sqrt(2.0 / np.pi).astype(np.float32) * (x + 0.044715 * (x * x * x))))


def _out_odd_kernel(x_ref, yr_ref, u_ref, ync_ref, ynl_ref, mod_ref, g_ref, d_ref, gw_ref, gb_ref, w_ref, o_ref):
    mod = mod_ref[...]
    y = _gelu_tanh(yr_ref[...] + d_ref[...] * u_ref[...])
    y = y * _sigmoid(_dot(y.astype(BF16), gw_ref[...]) + gb_ref[...])
    r = _dot(y.astype(BF16), w_ref[:S5_W, :]) + _dot(_pick(ync_ref, ynl_ref), w_ref[S5_W:, :])
    o_ref[...] = x_ref[...] + mod[:, 2 * D:3 * D] * _rms(r, g_ref[...])


def _out_odd(x, y_raw, u, yn_c, yn_l, mods, gains, layer, j, d_skip, glu_w, glu_b, w):
    return pl.pallas_call(
        _out_odd_kernel,
        grid=(NT,),
        in_specs=[_row(D), _row(S5_W), _row(S5_W), _ctx_row(NA_W), _lat_row(NA_W), _mod_spec(layer),
                  _layer(layer, (1, D)), _layer(j, (1, S5_W)), _layer(j, (S5_W, S5_W)), _layer(j, (1, S5_W)),
                  _layer(j, (D, D))],
        out_specs=_row(D),
        out_shape=jax.ShapeDtypeStruct((N_TOK, D), F32),
        compiler_params=_cp("parallel"),
        name="out_odd",
    )(x, y_raw, u, yn_c, yn_l, mods, gains, d_skip, glu_w, glu_b, w)


def _ffn_kernel(xm_ref, xp_ref, xn_ref, mod_ref, gpre_ref, gpost_ref, wu_ref, cw_ref, cb_ref, wd_ref, *rest,
                split_out):
    if split_out:
        oc_ref, ol_ref, h_scr, act_scr = rest
    else:
        o_ref, h_scr, act_scr = rest
    i = pl.program_id(0)
    t = (i - NT_P) % TILES_PER_DEC
    has_prev = jnp.logical_and(i >= NT_P, t != 0)
    has_next = jnp.logical_and(i >= NT_P, t != TILES_PER_DEC - 1)
    mod = mod_ref[...]
    shift = mod[:, 3 * D:4 * D]
    scale = mod[:, 4 * D:5 * D]
    gate = mod[:, 5 * D:6 * D]
    gpre = gpre_ref[...]
    nm = lambda x: _rms(x, gpre) * (1.0 + scale) + shift
    x = xm_ref[...]
    h_scr[0:TM, :] = nm(x).astype(BF16)
    ext = jnp.concatenate([jnp.where(has_next, nm(xn_ref[...]), 0.0), jnp.where(has_prev, nm(xp_ref[...]), 0.0)],
                          axis=0)
    h_scr[TM:, :] = ext.astype(BF16)
    hb = h_scr[...]
    rows = TM + FF_EXT

    def up(j):
        ca = slice(j * FF_CHUNK, (j + 1) * FF_CHUNK)
        cg = slice(D_FF + j * FF_CHUNK, D_FF + (j + 1) * FF_CHUNK)
        return (_dot(hb, wu_ref[:, ca]), ca), (_dot(hb, wu_ref[:, cg]), cg)

    def conv(part):
        u, cols = part
        cw = cw_ref[:, cols]
        return (cw[0:1, :] * pltpu.roll(u, 1, axis=0)[:TM] + cw[1:2, :] * u[:TM]
                + cw[2:3, :] * pltpu.roll(u, rows - 1, axis=0)[:TM] + cb_ref[:, cols])

    nxt = up(0)
    for j in range(FF_NCHUNK):
        cur = nxt
        if j + 1 < FF_NCHUNK:
            nxt = up(j + 1)
        a = conv(cur[0])
        g = conv(cur[1])
        act_scr[:, j * FF_CHUNK:(j + 1) * FF_CHUNK] = (_silu(g) * a).astype(BF16)
    y = _dot(act_scr[...], wd_ref[...])
    out = x + gate * _rms(y, gpost_ref[...])
    if split_out:
        @pl.when(i < NT_P)
        def _():
            oc_ref[...] = out

        @pl.when(i >= NT_P)
        def _():
            ol_ref[...] = out
    else:
        o_ref[...] = out


def _ffn(x, mods, gpre, gpost, layer, wu, cw, cb, wd, split_out):
    hb = TM // 8
    nblk = N_TOK // 8
    prev = pl.BlockSpec((8, D), lambda i: (jnp.maximum(i * hb - 1, 0), 0))
    nxt = pl.BlockSpec((8, D), lambda i: (jnp.minimum((i + 1) * hb, nblk - 1), 0))
    if split_out:
        out_specs = [_ctx_row(D), _lat_row(D)]
        out_shape = [jax.ShapeDtypeStruct((N_TOK_P, D), F32), jax.ShapeDtypeStruct((N_TOK_S, D), F32)]
    else:
        out_specs = _row(D)
        out_shape = jax.ShapeDtypeStruct((N_TOK, D), F32)
    return pl.pallas_call(
        functools.partial(_ffn_kernel, split_out=split_out),
        grid=(NT,),
        in_specs=[_row(D), prev, nxt, _mod_spec(layer), _layer(layer, (1, D)), _layer(layer, (1, D)),
                  _layer(layer, (D, 2 * D_FF)), _layer(layer, (3, 2 * D_FF)), _layer(layer, (1, 2 * D_FF)),
                  _layer(layer, (D_FF, D))],
        out_specs=out_specs,
        out_shape=out_shape,
        scratch_shapes=[pltpu.VMEM((TM + FF_EXT, D), BF16), pltpu.VMEM((TM, D_FF), BF16)],
        compiler_params=_cp("arbitrary"),
        name="ffn",
    )(x, x, x, mods, gpre, gpost, wu, cw, cb, wd)


def _in_odd_kernel(x_ref, mod_ref, g_ref, w_ref, u_ref, qkv_ref, kv_ref):
    mod = mod_ref[...]
    h = _rms(x_ref[...], g_ref[...]) * (1.0 + mod[:, D:2 * D]) + mod[:, :D]
    r = _dot(h.astype(BF16), w_ref[...])
    u_ref[...] = r[:, :S5_W]
    qkv_ref[:, :NA_W] = (r[:, S5_W:S5_W + NA_W] * NA_SCALE).astype(BF16)
    qkv_ref[:, NA_W:] = r[:, S5_W + NA_W:].astype(BF16)

    @pl.when(pl.program_id(0) < NT_P)
    def _():
        kv_ref[...] = r[:, S5_W + NA_W:]


def _in_odd(x, mods, gains, layer, j, w):
    return pl.pallas_call(
        _in_odd_kernel,
        grid=(NT,),
        in_specs=[_row(D), _mod_spec(layer), _layer(layer, (1, D)), _layer(j, w.shape[1:])],
        out_specs=[_row(S5_W), _row(3 * NA_W), _ctx_row(2 * NA_W)],
        out_shape=[jax.ShapeDtypeStruct((N_TOK, S5_W), F32), jax.ShapeDtypeStruct((N_TOK, 3 * NA_W), BF16),
                   jax.ShapeDtypeStruct((N_TOK_P, 2 * NA_W), F32)],
        compiler_params=_cp("arbitrary"),
        name="in_odd",
    )(x, mods, gains, w)


S5_TS = S5_CHUNK * S5_GROUP
S5_PL = 2 * S5_P


def _s5_prep_kernel(lre_ref, lim_ref, ls_ref, btr_ref, bti_ref, cr_ref, ci_ref, m_ref, n_ref, p_ref, a_ref, ct_scr):
    T = S5_CHUNK
    S = S5_GROUP
    hi = lax.Precision.HIGHEST
    low = lax.broadcasted_iota(jnp.int32, (S, S5_PL), 1) < S5_P
    half = [low, jnp.logical_not(low)]
    pick = lambda e, v: jnp.where(half[e], v, 0.0)
    nt = (((1,), (1,)), ((), ()))
    kps = [[None, None], [None, None]]
    for d in range(2):
        lre = lre_ref[d]
        lim = lim_ref[d]
        step = jnp.exp(ls_ref[d])
        mag = jnp.exp(lre * step)
        are = mag * jnp.cos(lim * step)
        aim = mag * jnp.sin(lim * step)
        den = lre * lre + lim * lim
        zr, zi = _cmul(are - 1.0, aim, lre / den, -lim / den)
        bbr, bbi = _cmul(zr, zi, btr_ref[d], bti_ref[d])
        cr = cr_ref[d]
        ci = ci_ref[d]
        pr = jnp.ones_like(are)
        pi = jnp.zeros_like(are)
        for k in range(T + 1):
            er, ei = _cmul(cr, ci, pr, pi)
            if k < T:
                jn = T - 1 - k if d == 0 else k
                wr, wi = _cmul(pr, pi, bbr, bbi)
                for e in range(2):
                    rows = slice(e * S5_TS + jn * S, e * S5_TS + (jn + 1) * S)
                    n_ref[d, rows, 0:S5_PL] = pick(e, wr).astype(BF16)
                    n_ref[d, rows, S5_PL:2 * S5_PL] = pick(e, wi).astype(BF16)
                jc = k if d == 0 else T - 1 - k
                ct_scr[0, jc * S:(jc + 1) * S, :] = er
                ct_scr[1, jc * S:(jc + 1) * S, :] = ei
            if k >= 1:
                t = k - 1 if d == 0 else T - k
                for e in range(2):
                    rows = slice(e * S5_TS + t * S, e * S5_TS + (t + 1) * S)
                    p_ref[rows, 2 * d * S5_PL:(2 * d + 1) * S5_PL] = pick(e, er).astype(BF16)
                    p_ref[rows, (2 * d + 1) * S5_PL:(2 * d + 2) * S5_PL] = pick(e, -ei).astype(BF16)
            if k == T:
                a_ref[d, 0] = pr
                a_ref[d, 1] = pi
            pr, pi = _cmul(pr, pi, are, aim)
        for e in range(2):
            kd = (lax.dot_general(pick(e, bbr), ct_scr[0], nt, precision=hi, preferred_element_type=F32)
                  - lax.dot_general(pick(e, bbi), ct_scr[1], nt, precision=hi, preferred_element_type=F32))
            kps[d][e] = jnp.concatenate([kd, jnp.zeros_like(kd)], axis=1)
    for e in range(2):
        for t in range(T):
            fwd = pltpu.roll(kps[0][e], t * S, axis=1)[:, :S5_TS]
            bwd = pltpu.roll(kps[1][e], (2 * S5_TS - (T - 1 - t) * S) % (2 * S5_TS), axis=1)[:, :S5_TS]
            m_ref[e, t * S:(t + 1) * S, :] = (fwd + bwd).astype(BF16)


def _s5_prep(lre, lim, ls, b_re, b_im, c_re, c_im):
    NP, S, P = S5_PAIRS, S5_GROUP, S5_P
    vec = lambda a: a.reshape(2, NP, 1, S5_PL)
    b_lay = lambda a: jnp.transpose(a.reshape(2, NP, 2, P, S), (0, 1, 4, 2, 3)).reshape(2, NP, S, S5_PL)
    c_lay = lambda a: jnp.transpose(a.reshape(2, NP, 2, S, P), (0, 1, 3, 2, 4)).reshape(2, NP, S, S5_PL)
    args = (vec(lre), vec(lim), vec(jnp.repeat(ls, P, axis=-1)), b_lay(b_re), b_lay(b_im), c_lay(c_re), c_lay(c_im))
    vspec = pl.BlockSpec((2, None, 1, S5_PL), lambda g: (0, g, 0, 0))
    mspec = pl.BlockSpec((2, None, S, S5_PL), lambda g: (0, g, 0, 0))
    lead = lambda shape: pl.BlockSpec((None,) + shape, lambda g: (g,) + (0,) * len(shape))
    return pl.pallas_call(
        _s5_prep_kernel,
        grid=(NP,),
        in_specs=[vspec, vspec, vspec, mspec, mspec, mspec, mspec],
        out_specs=[lead((2, S5_TS, S5_TS)), lead((2, 2 * S5_TS, 2 * S5_PL)), lead((2 * S5_TS, 4 * S5_PL)),
                   lead((2, 2, 1, S5_PL))],
        out_shape=[jax.ShapeDtypeStruct((NP, 2, S5_TS, S5_TS), BF16),
                   jax.ShapeDtypeStruct((NP, 2, 2 * S5_TS, 2 * S5_PL), BF16),
                   jax.ShapeDtypeStruct((NP, 2 * S5_TS, 4 * S5_PL), BF16),
                   jax.ShapeDtypeStruct((NP, 2, 2, 1, S5_PL), F32)],
        scratch_shapes=[pltpu.VMEM((2, S5_TS, S5_PL), F32)],
        compiler_params=_cp("parallel"),
        name="s5_prep",
    )(*args)


S5_NCH = N_TOK // S5_CHUNK
S5_ROWS_P = N_TOK_P // S5_CHUNK
S5_NC_P = SEQ // S5_CHUNK
S5_NC_S = DEC_SEQ // S5_CHUNK
S5_GPB = 8
S5_PPB = S5_GPB // 2
S5_XL = 8 * 128


def _s5_perm():
    r = np.arange(S5_XL)
    dst = (r // S5_GROUP % S5_GPB) * 128 + (r // 128) * S5_GROUP + r % S5_GROUP
    perm = np.zeros((S5_XL, S5_XL), np.float32)
    perm[r, dst] = 1.0
    return jnp.asarray(perm, BF16)


def _s5_kernel(u_ref, perm_ref, m_ref, n_ref, p_ref, a_ref, h0c_ref, h0l_ref, y_ref, fin_ref, z_scr, up_scr, e_scr,
               hin_scr, yc_scr):
    T = S5_CHUNK
    W = S5_PL
    for t in range(T):
        z_scr[t // 8, :, (t % 8) * 128:(t % 8 + 1) * 128] = u_ref[pl.ds(t, S5_NCH, stride=T), :].astype(BF16)
    perm = perm_ref[...]
    for j in range(2):
        up_scr[j] = _dot(z_scr[j], perm).astype(BF16)

    def scan(pp, d, h0_ref, n_chunks, n_seq, row0):
        are = a_ref[pp, d, 0]
        aim = a_ref[pp, d, 1]
        hr = h0_ref[pp, d, 0]
        hi = h0_ref[pp, d, 1]
        order = range(n_chunks) if d == 0 else reversed(range(n_chunks))
        for c in order:
            rows = pl.ds(row0 + c, n_seq, stride=n_chunks)
            hin_scr[2 * d, rows, :] = hr
            hin_scr[2 * d + 1, rows, :] = hi
            er = e_scr[2 * d, rows, :]
            ei = e_scr[2 * d + 1, rows, :]
            hr, hi = are * hr - aim * hi + er, are * hi + aim * hr + ei
        return hr, hi

    for pp in range(S5_PPB):
        us = []
        for e in range(2):
            sl = slice((2 * pp + e) * 128, (2 * pp + e + 1) * 128)
            us.append(jnp.concatenate([up_scr[0, :, sl], up_scr[1, :, sl]], axis=1))
        u2 = jnp.concatenate(us, axis=1)
        for d in range(2):
            ed = _dot(u2, n_ref[pp, d])
            e_scr[2 * d] = ed[:, :W]
            e_scr[2 * d + 1] = ed[:, W:]
        for d in range(2):
            hr, hi = scan(pp, d, h0c_ref, S5_NC_P, BATCH, 0)
            fin_ref[pp, d, 0] = hr
            fin_ref[pp, d, 1] = hi
            scan(pp, d, h0l_ref, S5_NC_S, DEC_BATCH, S5_ROWS_P)
        hin = jnp.concatenate([hin_scr[k] for k in range(4)], axis=1).astype(BF16)
        for e in range(2):
            y = _dot(us[e], m_ref[pp, e]) + _dot_nt(hin, p_ref[pp, e * S5_TS:(e + 1) * S5_TS, :])
            sl = slice((2 * pp + e) * 128, (2 * pp + e + 1) * 128)
            for j in range(2):
                yc_scr[j, :, sl] = y[:, j * 128:(j + 1) * 128]

    for j in range(2):
        yp = yc_scr[j]
        y_hi = yp.astype(BF16)
        y_lo = (yp - y_hi.astype(F32)).astype(BF16)
        r = _dot_nt(y_hi, perm) + _dot_nt(y_lo, perm)
        for k in range(8):
            y_ref[pl.ds(8 * j + k, S5_NCH, stride=T), :] = r[:, k * 128:(k + 1) * 128]


def _s5_scan(u, perm, m, n2, p2, a, h0c, h0l):
    nb = S5_GROUPS // S5_GPB
    lead = lambda shape: pl.BlockSpec((S5_PPB,) + shape, lambda w: (w,) + (0,) * len(shape))
    col = pl.BlockSpec((N_TOK, 128), lambda w: (0, w))
    return pl.pallas_call(
        _s5_kernel,
        grid=(nb,),
        in_specs=[col, _full(perm.shape), lead((2, S5_TS, S5_TS)), lead((2, 2 * S5_TS, 2 * S5_PL)),
                  lead((2 * S5_TS, 4 * S5_PL)), lead((2, 2, 1, S5_PL)), lead((2, 2, BATCH, S5_PL)),
                  lead((2, 2, DEC_BATCH, S5_PL))],
        out_specs=[col, lead((2, 2, BATCH, S5_PL))],
        out_shape=[jax.ShapeDtypeStruct((N_TOK, S5_W), F32),
                   jax.ShapeDtypeStruct((S5_PAIRS, 2, 2, BATCH, S5_PL), F32)],
        scratch_shapes=[pltpu.VMEM((2, S5_NCH, S5_XL), BF16), pltpu.VMEM((2, S5_NCH, S5_XL), BF16),
                        pltpu.VMEM((4, S5_NCH, S5_PL), F32), pltpu.VMEM((4, S5_NCH, S5_PL), F32),
                        pltpu.VMEM((2, S5_NCH, S5_XL), F32)],
        compiler_params=_cp("parallel"),
        name="s5_scan",
    )(u, perm, m, n2, p2, a, h0c, h0l)


def _na_heads(q_ref, keys, values, y_ref, rows, bias=None):
    low = _low_half((rows, 128))
    pair = lambda hh: slice(hh // 2 * 128, (hh // 2 + 1) * 128)

    def scores(hh):
        q = q_ref[:, pair(hh)]
        qm = jnp.where(low == (hh % 2 == 0), q, jnp.zeros_like(q))
        s = [_dot_nt(qm, k(pair(hh))) for k in keys]
        if bias is not None:
            s[0] = s[0] + bias(hh)
        return s

    attend = lambda hh, s: _softmax_pv(s, [v(pair(hh)) for v in values])
    _heads_pipelined(NA_HEADS, scores, attend, y_ref, rows)


def _na_ctx_kernel(q_ref, k_ref, v_ref, y_ref):
    _na_heads(q_ref, [lambda sl: k_ref[:, sl]], [lambda sl: v_ref[:, sl]], y_ref, SEQ)


def _na_ctx(qkv):
    col = lambda part: pl.BlockSpec((SEQ, NA_W), lambda b: (b, part))
    return pl.pallas_call(
        _na_ctx_kernel,
        grid=(BATCH,),
        in_specs=[col(0), col(1), col(2)],
        out_specs=pl.BlockSpec((SEQ, NA_W), lambda b: (b, 0)),
        out_shape=jax.ShapeDtypeStruct((N_TOK_P, NA_W), BF16),
        compiler_params=_cp("parallel"),
        name="na_ctx",
    )(qkv, qkv, qkv)


def _na_key_row0(rb):
    return jnp.clip(NA_QROWS * rb - NA_WIN_R // 2, 0, GRID_H - NA_KROWS)


def _na_lat_kernel(q_ref, ks_ref, vs_ref, kc_ref, vc_ref, tab_ref, y_ref):
    rb = pl.program_id(1)
    u0 = _na_key_row0(rb)
    start = pl.multiple_of(u0 * GRID_W, GRID_W)
    nk = NA_KROWS * GRID_W
    low_t = _low_half((GRID_W, 128))

    def table_row(i, w):
        qr = NA_QROWS * rb + i
        kr = u0 + w
        rs = jnp.clip(qr - NA_WIN_R // 2, 0, GRID_H - NA_WIN_R)
        inside = jnp.logical_and(kr >= rs, kr < rs + NA_WIN_R)
        return jnp.where(inside, kr - qr + NA_WIN_R - 1, NA_NDR)

    idx = [[table_row(i, w) for w in range(NA_KROWS)] for i in range(NA_QROWS)]

    def bias(h):
        rows = [jnp.concatenate([jnp.where(low_t, tab_ref[h, idx[i][w]], tab_ref[h, idx[i][w + 1]])
                                 for w in range(0, NA_KROWS, 2)], axis=1) for i in range(NA_QROWS)]
        return jnp.concatenate(rows, axis=0)

    keys = [lambda sl: ks_ref[pl.ds(start, nk), sl], lambda sl: kc_ref[:, sl].astype(BF16)]
    values = [lambda sl: vs_ref[pl.ds(start, nk), sl], lambda sl: vc_ref[:, sl].astype(BF16)]
    _na_heads(q_ref, keys, values, y_ref, TM, bias)


def _na_lat(qkv, k_c, v_c, table):
    seq_blk = N_TOK_P // DEC_SEQ
    nb = GRID_H // NA_QROWS
    return pl.pallas_call(
        _na_lat_kernel,
        grid=(DEC_BATCH, nb),
        in_specs=[
            pl.BlockSpec((TM, NA_W), lambda b, r: (NT_P + b * nb + r, 0)),
            pl.BlockSpec((DEC_SEQ, NA_W), lambda b, r: (seq_blk + b, 1)),
            pl.BlockSpec((DEC_SEQ, NA_W), lambda b, r: (seq_blk + b, 2)),
            pl.BlockSpec((PAST_LEN, NA_W), lambda b, r: (b, 0)),
            pl.BlockSpec((PAST_LEN, NA_W), lambda b, r: (b, 0)),
            pl.BlockSpec(table.shape, lambda b, r: (0, 0, 0, 0)),
        ],
        out_specs=pl.BlockSpec((TM, NA_W), lambda b, r: (b * nb + r, 0)),
        out_shape=jax.ShapeDtypeStruct((N_TOK_S, NA_W), BF16),
        compiler_params=_cp("parallel", "arbitrary"),
        name="na_lat",
    )(qkv, qkv, qkv, k_c, v_c, table)


def _na_table_kernel(rpb_ref, t_ref):
    qc = lax.broadcasted_iota(jnp.int32, (GRID_W, 128), 0)
    kc = lax.broadcasted_iota(jnp.int32, (GRID_W, 128), 1) % GRID_W
    cs = jnp.clip(qc - NA_WIN_C // 2, 0, GRID_W - NA_WIN_C)
    in_band = jnp.logical_and(kc >= cs, kc < cs + NA_WIN_C)
    dc = jnp.clip(kc - qc + NA_WIN_C - 1, 0, NA_NDC - 1)
    neg = jnp.full((GRID_W, 128), -jnp.inf, F32)

    def body(n, carry):
        t = neg
        for c in range(NA_NDC):
            t = jnp.where(dc == c, rpb_ref[n * NA_NDC + c], t)
        t_ref[n // NA_NDR, n % NA_NDR] = jnp.where(in_band, t, neg)
        return carry

    lax.fori_loop(0, NA_HEADS * NA_NDR, body, 0)
    for h in range(NA_HEADS):
        t_ref[h, NA_NDR] = neg


def _na_table(rpb):
    return pl.pallas_call(
        _na_table_kernel,
        in_specs=[pl.BlockSpec(memory_space=pltpu.SMEM)],
        out_shape=jax.ShapeDtypeStruct((NA_HEADS, NA_NDR + 1, GRID_W, 128), F32),
        name="na_table",
    )(rpb.reshape(-1))


def _rope_tables():
    n_freq = MLA_ROPE // 4
    inv = ROPE_BASE ** (-jnp.arange(n_freq, dtype=F32) / n_freq)
    t = jnp.arange(DEC_SEQ)
    row = (t // GRID_W).astype(F32)
    colp = (t % GRID_W).astype(F32)
    ang = jnp.concatenate([row[:, None] * inv, colp[:, None] * inv], axis=-1)
    cos, sin = jnp.cos(ang), jnp.sin(ang)
    one = jnp.ones((DEC_SEQ, MLA_NOPE), F32)
    zero = jnp.zeros((DEC_SEQ, MLA_NOPE), F32)
    cos_s = jnp.concatenate([one, cos, cos, one[:, :32]], axis=-1)
    sin_s = jnp.concatenate([zero, -sin, sin, zero[:, :32]], axis=-1)
    cos_t = jnp.concatenate([jnp.ones((TM, HEAD_SLAB), F32), cos_s], axis=0)
    sin_t = jnp.concatenate([jnp.zeros((TM, HEAD_SLAB), F32), sin_s], axis=0)
    return cos_t, sin_t


def _mla_weights(w_in, w_uq, w_ukv):
    half = MLA_ROPE // 2
    wkr = w_in[:, 4 * RET_W + MLA_Q_RANK + MLA_KV_RANK:]
    z64 = jnp.zeros((D, MLA_NOPE), F32)
    z32 = jnp.zeros((D, HEAD_SLAB - MLA_NOPE - MLA_ROPE), F32)
    wkr2 = jnp.concatenate([z64, wkr, z32, z64, wkr[:, half:], wkr[:, :half], z32], axis=1).astype(BF16)
    wq = w_uq.reshape(MLA_Q_RANK, MLA_HEADS, MLA_NOPE + MLA_ROPE)
    nope, rope = wq[..., :MLA_NOPE], wq[..., MLA_NOPE:]
    zq64 = jnp.zeros_like(nope)
    zq32 = jnp.zeros_like(rope)
    q_slab = jnp.concatenate([nope, rope, zq32], axis=-1).reshape(MLA_Q_RANK, MLA_QW)
    q_sw = jnp.concatenate([zq64, rope[..., half:], rope[..., :half], zq32], axis=-1).reshape(MLA_Q_RANK, MLA_QW)
    wuq2 = jnp.concatenate([q_slab, q_sw], axis=1).astype(BF16)
    wkv = w_ukv.reshape(MLA_KV_RANK, MLA_HEADS, MLA_NOPE + MLA_V)
    wk = jnp.concatenate([wkv[..., :MLA_NOPE], jnp.zeros_like(wkv[..., :MLA_NOPE])], axis=-1)
    wk = wk.reshape(MLA_KV_RANK, MLA_QW).astype(BF16)
    wv = wkv[..., MLA_NOPE:].reshape(MLA_KV_RANK, MLA_VW).astype(BF16)
    return wkr2, wuq2, wk, wv


def kernel(x_prompt, x_sample, c, state_ret, cache_mla_ckv, cache_mla_krope, state_s5_re, state_s5_im, cache_na_k, cache_na_v, c_ctx, ada_w, ada_b, mix_pre_g, mix_post_g, ffn_pre_g, ffn_post_g, ffn_w_up, ffn_conv_w, ffn_conv_b, ffn_w_down, even_w_in, even_w_out, ret_logit, ret_gn, mla_q_norm, mla_w_uq, mla_kv_norm, mla_w_ukv, odd_w_in, odd_w_out, s5_lambda_re, s5_lambda_im, s5_log_step, s5_b_re, s5_b_im, s5_c_re, s5_c_im, s5_d, s5_glu_w, s5_glu_b, na_rpb):
    cvec = jnp.concatenate([c_ctx[None, :], c, jnp.zeros((8 - 1 - DEC_BATCH, D), F32)], axis=0)
    mods = _ada_mods(cvec, ada_w, ada_b)
    row3 = lambda a: a.reshape(a.shape[0], 1, a.shape[1])
    mix_pre, mix_post, ffn_pre, ffn_post = row3(mix_pre_g), row3(mix_post_g), row3(ffn_pre_g), row3(ffn_post_g)
    w_up, w_down = ffn_w_up.astype(BF16), ffn_w_down.astype(BF16)
    conv_b = row3(ffn_conv_b)
    e_in, e_out = even_w_in.astype(BF16), even_w_out.astype(BF16)
    o_in, o_out, glu_w = odd_w_in.astype(BF16), odd_w_out.astype(BF16), s5_glu_w.astype(BF16)
    cos_t, sin_t = _rope_tables()
    perm = _s5_perm()
    xs = (x_prompt.reshape(N_TOK_P, D), x_sample.reshape(N_TOK_S, D))
    new_ret, new_ckv, new_kr, new_s5_re, new_s5_im, new_nak, new_nav = [], [], [], [], [], [], []
    for layer in range(DEPTH):
        j = layer // 2
        if layer % 2 == 0:
            wkr2, wuq2, wk, wv = _mla_weights(even_w_in[j], mla_w_uq[j], mla_w_ukv[j])
            qkvg, qp, kp, v, ckvn, kr = _in_even(xs, mods, mix_pre, layer, j, e_in, wkr2, row3(mla_q_norm),
                                                 row3(mla_kv_norm), wuq2, wk, wv, cos_t, sin_t)
            logit = jnp.transpose(ret_logit[j]).reshape(RET_HEADS, 2, 1, 1)
            gn = row3(ret_gn)
            s0 = jnp.zeros((BATCH, 2, RET_HEADS, RET_DIM, RET_DIM), F32)
            yr_c, st = _retention(qkvg, logit, gn, j, s0, seq_len=SEQ, n_seq=BATCH, n_blk=4, row0=0,
                                  emit_state=True)
            (yr_l,) = _retention(qkvg, logit, gn, j, state_ret[:, j], seq_len=DEC_SEQ, n_seq=DEC_BATCH, n_blk=1,
                                 row0=N_TOK_P, emit_state=False)
            ym_c = _mla_ctx(qp, kp, v)
            kr_c = jnp.pad(cache_mla_krope[:, j].reshape(DEC_BATCH * PAST_LEN, MLA_ROPE),
                           ((0, 0), (MLA_NOPE, HEAD_SLAB - MLA_NOPE - MLA_ROPE)))
            kp_c, v_c = _mla_cache(cache_mla_ckv[:, j].reshape(DEC_BATCH * PAST_LEN, MLA_KV_RANK), kr_c, wk, wv)
            ym_l = _mla_lat(qp, kp, v, kp_c, v_c)
            x = _out_even(xs, yr_c, yr_l, ym_c, ym_l, mods, mix_post, layer, j, e_out)
            new_ret.append(st)
            new_ckv.append(ckvn.reshape(BATCH, SEQ, MLA_KV_RANK))
            new_kr.append(kr[:, MLA_NOPE:MLA_NOPE + MLA_ROPE].reshape(BATCH, SEQ, MLA_ROPE))
        else:
            u, qkv, kv = _in_odd(xs[0], mods, mix_pre, layer, j, o_in)
            m, n2, p2, a = _s5_prep(s5_lambda_re[j], s5_lambda_im[j], s5_log_step[j], s5_b_re[j], s5_b_im[j],
                                    s5_c_re[j], s5_c_im[j])
            h0c = jnp.zeros((S5_PAIRS, 2, 2, BATCH, S5_PL), F32)
            h0 = jnp.stack([state_s5_re[:, j], state_s5_im[:, j]], axis=0)
            h0l = jnp.transpose(h0.reshape(2, DEC_BATCH, 2, S5_PAIRS, S5_PL), (3, 2, 0, 1, 4))
            y_raw, fin = _s5_scan(u, perm, m, n2, p2, a, h0c, h0l)
            yn_c = _na_ctx(qkv)
            yn_l = _na_lat(qkv, cache_na_k[:, j].reshape(DEC_BATCH * PAST_LEN, NA_W),
                           cache_na_v[:, j].reshape(DEC_BATCH * PAST_LEN, NA_W), _na_table(na_rpb[j]))
            x = _out_odd(xs[0], y_raw, u, yn_c, yn_l, mods, mix_post, layer, j, row3(s5_d), glu_w, row3(s5_glu_b),
                         o_out)
            st = jnp.transpose(fin.reshape(S5_PAIRS, 2, 2, BATCH, 2, S5_P), (2, 3, 1, 0, 4, 5))
            st = st.reshape(2, BATCH, 2, S5_GROUPS, S5_P)
            new_s5_re.append(st[0])
            new_s5_im.append(st[1])
            new_nak.append(kv[:, :NA_W].reshape(BATCH, SEQ, NA_HEADS, NA_DIM))
            new_nav.append(kv[:, NA_W:].reshape(BATCH, SEQ, NA_HEADS, NA_DIM))
        last = layer == DEPTH - 1
        out = _ffn(x, mods, ffn_pre, ffn_post, layer, w_up, ffn_conv_w, conv_b, w_down, split_out=last)
        xs = tuple(out) if last else (out,)
    stack = lambda a: jnp.stack(a, axis=1)
    return (xs[0].reshape(BATCH, SEQ, D), xs[1].reshape(DEC_BATCH, DEC_SEQ, D), stack(new_ret), stack(new_ckv),
            stack(new_kr), stack(new_s5_re), stack(new_s5_im), stack(new_nak), stack(new_nav))
```

```python
import functools

import numpy as np
import jax
import jax.numpy as jnp
from jax import lax
from jax.experimental import pallas as pl
from jax.experimental.pallas import tpu as pltpu

F32 = jnp.float32
BF16 = jnp.bfloat16

D = 1024
BATCH = 16
SEQ = 256
DEPTH = 4
DEC_BATCH = 2
DEC_SEQ = 2048
PAST_LEN = 512
GRID_W = 64
GRID_H = DEC_SEQ // GRID_W
EPS = 1e-6

RET_HEADS = 4
RET_W = 512
RET_DIM = 128
RET_CHUNK = 256

MLA_HEADS = 8
MLA_NOPE = 64
MLA_ROPE = 32
MLA_V = 64
MLA_Q_RANK = 256
MLA_KV_RANK = 128
MLA_SCALE = (MLA_NOPE + MLA_ROPE) ** -0.5
ROPE_BASE = 10000.0
HEAD_SLAB = 128
MLA_QW = MLA_HEADS * HEAD_SLAB
MLA_VW = MLA_HEADS * MLA_V

S5_W = 512
S5_GROUP = 16
S5_GROUPS = 32
S5_P = 64
S5_CHUNK = 16
S5_PAIRS = S5_GROUPS // 2

NA_HEADS = 8
NA_W = 512
NA_DIM = 64
NA_WIN_R = 8
NA_WIN_C = 16
NA_SCALE = NA_DIM ** -0.5
NA_QROWS = 4
NA_KROWS = 12
NA_NDR = 2 * NA_WIN_R - 1
NA_NDC = 2 * NA_WIN_C - 1

D_FF = 2816
FF_CHUNK = 256
FF_NCHUNK = D_FF // FF_CHUNK
FF_EXT = 16

TM = 512
TQ = 256
FF_SUB = SEQ
FF_NSUB = TM // FF_SUB
N_TOK_P = BATCH * SEQ
N_TOK_S = DEC_BATCH * DEC_SEQ
N_TOK = N_TOK_P + N_TOK_S
NT_P = N_TOK_P // TM
NT_S = N_TOK_S // TM
NT = NT_P + NT_S
TILES_PER_DEC = DEC_SEQ // TM

VMEM_LIMIT = 56 * 1024 * 1024


def _cp(*sem):
    return pltpu.CompilerParams(dimension_semantics=sem, vmem_limit_bytes=VMEM_LIMIT)


def _dot(a, b):
    return jnp.dot(a, b, preferred_element_type=F32)


def _dot_nt(a, b):
    return lax.dot_general(a, b, (((1,), (1,)), ((), ())), preferred_element_type=F32)


def _dot_tn(a, b):
    return lax.dot_general(a, b, (((0,), (0,)), ((), ())), preferred_element_type=F32)


def _rms(x, g):
    return x * lax.rsqrt(jnp.mean(x * x, axis=-1, keepdims=True) + EPS) * g


def _sigmoid(x):
    return 1.0 / (1.0 + jnp.exp(-x))


def _silu(x):
    return x * _sigmoid(x)


def _cmul(ar, ai, br, bi):
    return ar * br - ai * bi, ar * bi + ai * br


def _mrow(i):
    return jnp.where(i < NT_P, 0, 1 + (i - NT_P) // TILES_PER_DEC)


def _full(shape):
    n = len(shape)
    return pl.BlockSpec(shape, lambda *_: (0,) * n)


def _layer(layer, shape, single_buffer=False):
    n = len(shape)
    kw = dict(pipeline_mode=pl.Buffered(1)) if single_buffer else {}
    return pl.BlockSpec((None,) + shape, lambda *_: (layer,) + (0,) * n, **kw)


def _mod_spec(layer):
    return pl.BlockSpec((None, None, 1, 6 * D), lambda i: (layer, _mrow(i), 0, 0))


def _row(width):
    return pl.BlockSpec((TM, width), lambda i: (i, 0))


def _ctx_row(width):
    return pl.BlockSpec((TM, width), lambda i: (jnp.minimum(i, NT_P - 1), 0))


def _lat_row(width):
    return pl.BlockSpec((TM, width), lambda i: (jnp.maximum(i - NT_P, 0), 0))


def _pick(a_ref, b_ref):
    return jnp.where(pl.program_id(0) < NT_P, a_ref[...], b_ref[...])


def _ada_kernel(c_ref, w_ref, b_ref, o_ref):
    o_ref[...] = _dot(_silu(c_ref[...]).astype(BF16), w_ref[...].astype(BF16)) + b_ref[...]


def _ada_mods(cvec, ada_w, ada_b):
    nb = 4
    bn = 6 * D // nb
    out = pl.pallas_call(
        _ada_kernel,
        grid=(DEPTH, nb),
        in_specs=[
            pl.BlockSpec((8, D), lambda l, n: (0, 0)),
            pl.BlockSpec((None, D, bn), lambda l, n: (l, 0, n)),
            pl.BlockSpec((None, 1, bn), lambda l, n: (l, 0, n)),
        ],
        out_specs=pl.BlockSpec((None, 8, bn), lambda l, n: (l, 0, n)),
        out_shape=jax.ShapeDtypeStruct((DEPTH, 8, 6 * D), F32),
        compiler_params=_cp("arbitrary", "arbitrary"),
        name="ada_mods",
    )(cvec, ada_w, ada_b.reshape(DEPTH, 1, 6 * D))
    return out[:, :3].reshape(DEPTH, 3, 1, 6 * D)


def _in_even_kernel(*refs, split_x):
    nx = 2 if split_x else 1
    x = _pick(*refs[:2]) if split_x else refs[0][...]
    (mod_ref, g_ref, w_ref, wkr_ref, qn_ref, kvn_ref, wuq_ref, wk_ref, wv_ref, cos_ref, sin_ref,
     qkvg_ref, qp_ref, kp_ref, v_ref, ckv_ref, kr_ref) = refs[nx:]
    mod = mod_ref[...]
    h = _rms(x, g_ref[...]) * (1.0 + mod[:, D:2 * D]) + mod[:, :D]
    hb = h.astype(BF16)
    o = 4 * RET_W
    for part in range(4):
        cols = slice(part * RET_W, (part + 1) * RET_W)
        qkvg_ref[:, cols] = _dot(hb, w_ref[:, cols]).astype(BF16)
    r = _dot(hb, w_ref[:, o:o + MLA_Q_RANK + MLA_KV_RANK])
    cq = r[:, :MLA_Q_RANK]
    ckv_raw = r[:, MLA_Q_RANK:]
    r2 = _dot(hb, wkr_ref[...])
    kr = r2[:, :HEAD_SLAB]
    krs = r2[:, HEAD_SLAB:]
    cosf = cos_ref[...]
    sinf = sin_ref[...]
    q2 = _dot(_rms(cq, qn_ref[...]).astype(BF16), wuq_ref[...])
    ckvn = _rms(ckv_raw, kvn_ref[...])

    @pl.when(pl.program_id(0) < NT_P)
    def _():
        ckv_ref[...] = ckvn
        kr_ref[...] = kr

    cb = ckvn.astype(BF16)
    kp = _dot(cb, wk_ref[...])
    v_ref[...] = _dot(cb, wv_ref[...]).astype(BF16)
    krr = kr * cosf + krs * sinf
    for hh in range(MLA_HEADS):
        sl = slice(hh * HEAD_SLAB, (hh + 1) * HEAD_SLAB)
        ss = slice(MLA_QW + hh * HEAD_SLAB, MLA_QW + (hh + 1) * HEAD_SLAB)
        qp_ref[:, sl] = ((q2[:, sl] * cosf + q2[:, ss] * sinf) * MLA_SCALE).astype(BF16)
        kp_ref[:, sl] = (kp[:, sl] + krr).astype(BF16)


def _in_even(xs, mods, gains, layer, j, w_in, wkr2, qn, kvn, wuq2, wk, wv, cos_t, sin_t):
    split_x = len(xs) == 2
    x_specs = [_ctx_row(D), _lat_row(D)] if split_x else [_row(D)]
    pos_spec = pl.BlockSpec((TM, HEAD_SLAB), lambda i: (jnp.where(i < NT_P, 0, 1 + (i - NT_P) % TILES_PER_DEC), 0))
    return pl.pallas_call(
        functools.partial(_in_even_kernel, split_x=split_x),
        grid=(NT,),
        in_specs=x_specs + [_mod_spec(layer), _layer(layer, (1, D)), _layer(j, w_in.shape[1:]), _full(wkr2.shape),
                            _layer(j, (1, MLA_Q_RANK)), _layer(j, (1, MLA_KV_RANK)), _full(wuq2.shape),
                            _full(wk.shape), _full(wv.shape), pos_spec, pos_spec],
        out_specs=[_row(4 * RET_W), _row(MLA_QW), _row(MLA_QW), _row(MLA_VW), _ctx_row(MLA_KV_RANK),
                   _ctx_row(HEAD_SLAB)],
        out_shape=[
            jax.ShapeDtypeStruct((N_TOK, 4 * RET_W), BF16),
            jax.ShapeDtypeStruct((N_TOK, MLA_QW), BF16),
            jax.ShapeDtypeStruct((N_TOK, MLA_QW), BF16),
            jax.ShapeDtypeStruct((N_TOK, MLA_VW), BF16),
            jax.ShapeDtypeStruct((N_TOK_P, MLA_KV_RANK), F32),
            jax.ShapeDtypeStruct((N_TOK_P, HEAD_SLAB), F32),
        ],
        compiler_params=_cp("arbitrary"),
        name="in_even",
    )(*xs, mods, gains, w_in, wkr2, qn, kvn, wuq2, wk, wv, cos_t, sin_t)


def _mla_cache_kernel(ckv_ref, kr_ref, wk_ref, wv_ref, kp_ref, v_ref):
    cb = ckv_ref[...].astype(BF16)
    kp = _dot(cb, wk_ref[...])
    kr = kr_ref[...]
    for hh in range(MLA_HEADS):
        sl = slice(hh * HEAD_SLAB, (hh + 1) * HEAD_SLAB)
        kp_ref[:, sl] = (kp[:, sl] + kr).astype(BF16)
    v_ref[...] = _dot(cb, wv_ref[...]).astype(BF16)


def _mla_cache(ckv_c, kr_slab, wk, wv):
    row = lambda w: pl.BlockSpec((PAST_LEN, w), lambda b: (b, 0))
    return pl.pallas_call(
        _mla_cache_kernel,
        grid=(DEC_BATCH,),
        in_specs=[row(MLA_KV_RANK), row(HEAD_SLAB), _full(wk.shape), _full(wv.shape)],
        out_specs=[row(MLA_QW), row(MLA_VW)],
        out_shape=[jax.ShapeDtypeStruct((DEC_BATCH * PAST_LEN, MLA_QW), BF16),
                   jax.ShapeDtypeStruct((DEC_BATCH * PAST_LEN, MLA_VW), BF16)],
        compiler_params=_cp("parallel"),
        name="mla_cache",
    )(ckv_c, kr_slab, wk, wv)


def _ret_kernel(lg_ref, q_ref, k_ref, v_ref, g_ref, gn_ref, s0_ref, *rest, seq_len, n_blk, emit_state):
    if emit_state:
        y_ref, st_ref, sf_scr, sb_scr, dm_scr, w_scr = rest
    else:
        y_ref, sf_scr, sb_scr, dm_scr, w_scr = rest
    C = RET_CHUNK
    nc = seq_len // C

    @pl.when(pl.program_id(1) == 0)
    def _():
        lg = -jnp.log(1.0 + jnp.exp(-lg_ref[...]))
        lg_f = lg[0]
        lg_b = lg[1]
        ii = lax.broadcasted_iota(jnp.int32, (C, C), 0)
        jj = lax.broadcasted_iota(jnp.int32, (C, C), 1)
        diff = (ii - jj).astype(F32)
        dm_scr[...] = (jnp.where(diff >= 0, jnp.exp(lg_f * jnp.maximum(diff, 0.0)), 0.0)
                       + jnp.where(diff <= 0, jnp.exp(lg_b * jnp.maximum(-diff, 0.0)), 0.0))
        pos = lax.broadcasted_iota(jnp.int32, (C, RET_DIM), 0).astype(F32)
        w_scr[0] = jnp.exp(lg_f * (pos + 1.0))
        w_scr[1] = jnp.exp(lg_f * (C - 1.0 - pos))
        w_scr[2] = jnp.exp(lg_b * (C - pos))
        w_scr[3] = jnp.exp(lg_b * pos)
        w_scr[4] = jnp.exp(lg_f * C) + jnp.zeros((C, RET_DIM), F32)
        w_scr[5] = jnp.exp(lg_b * C) + jnp.zeros((C, RET_DIM), F32)

    qw_f, kw_f, qw_b, kw_b = w_scr[0], w_scr[1], w_scr[2], w_scr[3]
    cd_f = w_scr[4, :RET_DIM, :]
    cd_b = w_scr[5, :RET_DIM, :]
    ld = lambda ref, rows: ref[rows, :].astype(F32)
    gn = gn_ref[...]
    dm = dm_scr[...]

    for b in range(n_blk):
        rows_of = lambda n: slice(b * seq_len + n * C, b * seq_len + (n + 1) * C)
        sf = lambda n: sf_scr.at[b * (nc + 1) + n]
        sb = lambda n: sb_scr.at[b * (nc + 1) + n]
        sf(0)[...] = s0_ref[b, 0]
        for n in range(nc):
            kv = _dot_tn((ld(k_ref, rows_of(n)) * kw_f).astype(BF16), v_ref[rows_of(n), :])
            sf(n + 1)[...] = cd_f * sf(n)[...] + kv
        sb(nc)[...] = s0_ref[b, 1]
        for n in reversed(range(nc)):
            kv = _dot_tn((ld(k_ref, rows_of(n)) * kw_b).astype(BF16), v_ref[rows_of(n), :])
            sb(n)[...] = cd_b * sb(n + 1)[...] + kv
        if emit_state:
            st_ref[b, 0] = sf(nc)[...]
            st_ref[b, 1] = sb(0)[...]
        for n in range(nc):
            rows = rows_of(n)
            q = ld(q_ref, rows) * (RET_DIM ** -0.5)
            s = _dot_nt(q.astype(BF16), k_ref[rows, :]) * dm
            o = (_dot(s.astype(BF16), v_ref[rows, :])
                 + _dot((q * qw_f).astype(BF16), sf(n)[...].astype(BF16))
                 + _dot((q * qw_b).astype(BF16), sb(n + 1)[...].astype(BF16)))
            mu = jnp.mean(o, axis=-1, keepdims=True)
            oc = o - mu
            var = jnp.mean(oc * oc, axis=-1, keepdims=True)
            on = oc * lax.rsqrt(var + EPS) * gn
            y_ref[rows, :] = (_silu(ld(g_ref, rows)) * on).astype(BF16)


def _retention(qkvg, logit, gn, j, s0, *, seq_len, n_seq, n_blk, row0, emit_state):
    nc = seq_len // RET_CHUNK
    rows = n_blk * seq_len
    blk0 = row0 // rows
    col = lambda part: pl.BlockSpec((rows, RET_DIM), lambda h, s: (blk0 + s, part * RET_HEADS + h))
    state = pl.BlockSpec((n_blk, 2, None, RET_DIM, RET_DIM), lambda h, s: (s, 0, h, 0, 0))
    in_specs = [
        pl.BlockSpec((None, 2, 1, 1), lambda h, s: (h, 0, 0, 0)),
        col(0), col(1), col(2), col(3),
        pl.BlockSpec((None, 1, RET_DIM), lambda h, s: (j, 0, h)),
        state,
    ]
    out_specs = [pl.BlockSpec((rows, RET_DIM), lambda h, s: (s, h))]
    out_shape = [jax.ShapeDtypeStruct((n_seq * seq_len, RET_W), BF16)]
    if emit_state:
        out_specs.append(state)
        out_shape.append(jax.ShapeDtypeStruct((n_seq, 2, RET_HEADS, RET_DIM, RET_DIM), F32))
    n_st = n_blk * (nc + 1)
    return pl.pallas_call(
        functools.partial(_ret_kernel, seq_len=seq_len, n_blk=n_blk, emit_state=emit_state),
        grid=(RET_HEADS, n_seq // n_blk),
        in_specs=in_specs,
        out_specs=out_specs,
        out_shape=out_shape,
        scratch_shapes=[pltpu.VMEM((n_st, RET_DIM, RET_DIM), F32), pltpu.VMEM((n_st, RET_DIM, RET_DIM), F32),
                        pltpu.VMEM((RET_CHUNK, RET_CHUNK), F32), pltpu.VMEM((6, RET_CHUNK, RET_DIM), F32)],
        compiler_params=_cp("parallel", "arbitrary"),
        name="retention_%d" % seq_len,
    )(logit, qkvg, qkvg, qkvg, qkvg, gn, s0)


def _softmax_pv(score_blocks, value_blocks):
    m = functools.reduce(jnp.maximum, [jnp.max(s, axis=-1, keepdims=True) for s in score_blocks])
    ps = [jnp.exp(s - m) for s in score_blocks]
    l = functools.reduce(lambda a, b: a + b, [jnp.sum(p, axis=-1, keepdims=True) for p in ps])
    o = functools.reduce(lambda a, b: a + b, [_dot(p.astype(BF16), v) for p, v in zip(ps, value_blocks)])
    return o / l


def _low_half(shape):
    return lax.broadcasted_iota(jnp.int32, shape, 1) < 64


def _heads_pipelined(n_heads, scores, attend, y_ref, rows):
    low = _low_half((rows, 128))
    nxt = scores(0)
    outs = []
    for hh in range(n_heads):
        cur = nxt
        if hh + 1 < n_heads:
            nxt = scores(hh + 1)
        outs.append(attend(hh, cur))
        if hh % 2 == 1:
            hp = hh // 2
            y_ref[:, hp * 128:(hp + 1) * 128] = jnp.where(low, outs[hh - 1], outs[hh]).astype(BF16)


def _mla_ctx_kernel(q_ref, k_ref, v_ref, y_ref):
    slab = lambda hh: slice(hh * HEAD_SLAB, (hh + 1) * HEAD_SLAB)
    pair = lambda hh: slice(hh // 2 * 128, (hh // 2 + 1) * 128)
    scores = lambda hh: [_dot_nt(q_ref[:, slab(hh)], k_ref[:, slab(hh)])]
    attend = lambda hh, s: _softmax_pv(s, [v_ref[:, pair(hh)]])
    _heads_pipelined(MLA_HEADS, scores, attend, y_ref, SEQ)


def _mla_ctx(qp, kp, v):
    return pl.pallas_call(
        _mla_ctx_kernel,
        grid=(BATCH,),
        in_specs=[pl.BlockSpec((SEQ, MLA_QW), lambda b: (b, 0)), pl.BlockSpec((SEQ, MLA_QW), lambda b: (b, 0)),
                  pl.BlockSpec((SEQ, MLA_VW), lambda b: (b, 0))],
        out_specs=pl.BlockSpec((SEQ, MLA_VW), lambda b: (b, 0)),
        out_shape=jax.ShapeDtypeStruct((N_TOK_P, MLA_VW), BF16),
        compiler_params=_cp("parallel"),
        name="mla_ctx",
    )(qp, kp, v)


def _mla_lat_kernel(q_ref, k_ref, v_ref, kc_ref, vc_ref, y_ref):
    slab = lambda hh: slice(hh * HEAD_SLAB, (hh + 1) * HEAD_SLAB)
    pair = lambda hh: slice(hh // 2 * 128, (hh // 2 + 1) * 128)
    scores = lambda hh: [_dot_nt(q_ref[:, slab(hh)], k_ref[:, slab(hh)]),
                         _dot_nt(q_ref[:, slab(hh)], kc_ref[:, slab(hh)])]
    attend = lambda hh, s: _softmax_pv(s, [v_ref[:, pair(hh)], vc_ref[:, pair(hh)]])
    _heads_pipelined(MLA_HEADS, scores, attend, y_ref, TQ)


def _mla_lat(qp, kp, v, kp_c, v_c):
    seq_blk = N_TOK_P // DEC_SEQ
    nq = DEC_SEQ // TQ
    return pl.pallas_call(
        _mla_lat_kernel,
        grid=(DEC_BATCH, nq),
        in_specs=[
            pl.BlockSpec((TQ, MLA_QW), lambda b, t: (N_TOK_P // TQ + b * nq + t, 0)),
            pl.BlockSpec((DEC_SEQ, MLA_QW), lambda b, t: (seq_blk + b, 0)),
            pl.BlockSpec((DEC_SEQ, MLA_VW), lambda b, t: (seq_blk + b, 0)),
            pl.BlockSpec((PAST_LEN, MLA_QW), lambda b, t: (b, 0)),
            pl.BlockSpec((PAST_LEN, MLA_VW), lambda b, t: (b, 0)),
        ],
        out_specs=pl.BlockSpec((TQ, MLA_VW), lambda b, t: (b * nq + t, 0)),
        out_shape=jax.ShapeDtypeStruct((N_TOK_S, MLA_VW), BF16),
        compiler_params=_cp("parallel", "parallel"),
        name="mla_lat",
    )(qp, kp, v, kp_c, v_c)


def _out_even_kernel(*refs, split_x):
    nx = 2 if split_x else 1
    x = _pick(*refs[:2]) if split_x else refs[0][...]
    yrc_ref, yrl_ref, ymc_ref, yml_ref, mod_ref, g_ref, w_ref, o_ref = refs[nx:]
    mod = mod_ref[...]
    r = _dot(_pick(yrc_ref, yrl_ref), w_ref[:RET_W, :]) + _dot(_pick(ymc_ref, yml_ref), w_ref[RET_W:, :])
    o_ref[...] = x + mod[:, 2 * D:3 * D] * _rms(r, g_ref[...])


def _out_even(xs, yr_c, yr_l, ym_c, ym_l, mods, gains, layer, j, w):
    split_x = len(xs) == 2
    x_specs = [_ctx_row(D), _lat_row(D)] if split_x else [_row(D)]
    return pl.pallas_call(
        functools.partial(_out_even_kernel, split_x=split_x),
        grid=(NT,),
        in_specs=x_specs + [_ctx_row(RET_W), _lat_row(RET_W), _ctx_row(MLA_VW), _lat_row(MLA_VW), _mod_spec(layer),
                            _layer(layer, (1, D)), _layer(j, (D, D))],
        out_specs=_row(D),
        out_shape=jax.ShapeDtypeStruct((N_TOK, D), F32),
        compiler_params=_cp("parallel"),
        name="out_even",
    )(*xs, yr_c, yr_l, ym_c, ym_l, mods, gains, w)


def _gelu_tanh(x):
    return 0.5 * x * (1.0 + jnp.tanh(np.sqrt(2.0 / np.pi).astype(np.float32) * (x + 0.044715 * (x * x * x))))


def _out_odd_kernel(x_ref, yr_ref, u_ref, ync_ref, ynl_ref, mod_ref, g_ref, d_ref, gw_ref, gb_ref, w_ref, o_ref):
    mod = mod_ref[...]
    y = _gelu_tanh(yr_ref[...] + d_ref[...] * u_ref[...])
    y = y * _sigmoid(_dot(y.astype(BF16), gw_ref[...]) + gb_ref[...])
    r = _dot(y.astype(BF16), w_ref[:S5_W, :]) + _dot(_pick(ync_ref, ynl_ref), w_ref[S5_W:, :])
    o_ref[...] = x_ref[...] + mod[:, 2 * D:3 * D] * _rms(r, g_ref[...])


def _out_odd(x, y_raw, u, yn_c, yn_l, mods, gains, layer, j, d_skip, glu_w, glu_b, w):
    return pl.pallas_call(
        _out_odd_kernel,
        grid=(NT,),
        in_specs=[_row(D), _row(S5_W), _row(S5_W), _ctx_row(NA_W), _lat_row(NA_W), _mod_spec(layer),
                  _layer(layer, (1, D)), _layer(j, (1, S5_W)), _layer(j, (S5_W, S5_W)), _layer(j, (1, S5_W)),
                  _layer(j, (D, D))],
        out_specs=_row(D),
        out_shape=jax.ShapeDtypeStruct((N_TOK, D), F32),
        compiler_params=_cp("parallel"),
        name="out_odd",
    )(x, y_raw, u, yn_c, yn_l, mods, gains, d_skip, glu_w, glu_b, w)


def _ffn_kernel(xm_ref, xp_ref, xn_ref, mod_ref, gpre_ref, gpost_ref, wu_ref, cw_ref, cb_ref, wd_ref, *rest,
                split_out):
    if split_out:
        oc_ref, ol_ref, h_scr, act_scr = rest
    else:
        o_ref, h_scr, act_scr = rest
    i = pl.program_id(0)
    is_lat = i >= NT_P
    t = (i - NT_P) % TILES_PER_DEC
    has_prev = jnp.logical_and(is_lat, t != 0)
    has_next = jnp.logical_and(is_lat, t != TILES_PER_DEC - 1)
    mod = mod_ref[...]
    shift = mod[:, 3 * D:4 * D]
    scale = mod[:, 4 * D:5 * D]
    gate = mod[:, 5 * D:6 * D]
    gpre = gpre_ref[...]
    nm = lambda x: _rms(x, gpre) * (1.0 + scale) + shift
    x = xm_ref[...]
    h = nm(x)
    blk = FF_SUB + FF_EXT
    rows = FF_NSUB * blk
    for k in range(FF_NSUB):
        h_scr[k * blk:k * blk + FF_SUB, :] = h[k * FF_SUB:(k + 1) * FF_SUB].astype(BF16)
        if k + 1 < FF_NSUB:
            after = jnp.where(is_lat, h[(k + 1) * FF_SUB:(k + 1) * FF_SUB + 8], 0.0)
            before = jnp.where(is_lat, h[(k + 1) * FF_SUB - 8:(k + 1) * FF_SUB], 0.0)
        else:
            after = jnp.where(has_next, nm(xn_ref[...]), 0.0)
            before = jnp.where(has_prev, nm(xp_ref[...]), 0.0)
        h_scr[k * blk + FF_SUB:(k + 1) * blk, :] = jnp.concatenate([after, before], axis=0).astype(BF16)
    hb = h_scr[...]

    def up(j):
        ca = slice(j * FF_CHUNK, (j + 1) * FF_CHUNK)
        cg = slice(D_FF + j * FF_CHUNK, D_FF + (j + 1) * FF_CHUNK)
        return (_dot(hb, wu_ref[:, ca]), ca), (_dot(hb, wu_ref[:, cg]), cg)

    def conv(part):
        u, cols = part
        cw = cw_ref[:, cols]
        return (cw[0:1, :] * pltpu.roll(u, 1, axis=0) + cw[1:2, :] * u + cw[2:3, :] * pltpu.roll(u, rows - 1, axis=0)
                + cb_ref[:, cols])

    nxt = up(0)
    for j in range(FF_NCHUNK):
        cur = nxt
        if j + 1 < FF_NCHUNK:
            nxt = up(j + 1)
        act = (_silu(conv(cur[1])) * conv(cur[0])).astype(BF16)
        for k in range(FF_NSUB):
            act_scr[k * FF_SUB:(k + 1) * FF_SUB, j * FF_CHUNK:(j + 1) * FF_CHUNK] = act[k * blk:k * blk + FF_SUB]
    y = _dot(act_scr[...], wd_ref[...])
    out = x + gate * _rms(y, gpost_ref[...])
    if split_out:
        @pl.when(i < NT_P)
        def _():
            oc_ref[...] = out

        @pl.when(i >= NT_P)
        def _():
            ol_ref[...] = out
    else:
        o_ref[...] = out


def _ffn(x, mods, gpre, gpost, layer, wu, cw, cb, wd, split_out):
    hb = TM // 8
    nblk = N_TOK // 8
    prev = pl.BlockSpec((8, D), lambda i: (jnp.maximum(i * hb - 1, 0), 0))
    nxt = pl.BlockSpec((8, D), lambda i: (jnp.minimum((i + 1) * hb, nblk - 1), 0))
    if split_out:
        out_specs = [_ctx_row(D), _lat_row(D)]
        out_shape = [jax.ShapeDtypeStruct((N_TOK_P, D), F32), jax.ShapeDtypeStruct((N_TOK_S, D), F32)]
    else:
        out_specs = _row(D)
        out_shape = jax.ShapeDtypeStruct((N_TOK, D), F32)
    return pl.pallas_call(
        functools.partial(_ffn_kernel, split_out=split_out),
        grid=(NT,),
        in_specs=[_row(D), prev, nxt, _mod_spec(layer), _layer(layer, (1, D)), _layer(layer, (1, D)),
                  _layer(layer, (D, 2 * D_FF), True), _layer(layer, (3, 2 * D_FF)), _layer(layer, (1, 2 * D_FF)),
                  _layer(layer, (D_FF, D), True)],
        out_specs=out_specs,
        out_shape=out_shape,
        scratch_shapes=[pltpu.VMEM((FF_NSUB * (FF_SUB + FF_EXT), D), BF16), pltpu.VMEM((TM, D_FF), BF16)],
        compiler_params=_cp("arbitrary"),
        name="ffn",
    )(x, x, x, mods, gpre, gpost, wu, cw, cb, wd)


def _in_odd_kernel(x_ref, mod_ref, g_ref, w_ref, u_ref, qkv_ref, kv_ref):
    mod = mod_ref[...]
    h = _rms(x_ref[...], g_ref[...]) * (1.0 + mod[:, D:2 * D]) + mod[:, :D]
    r = _dot(h.astype(BF16), w_ref[...])
    u_ref[...] = r[:, :S5_W]
    qkv_ref[:, :NA_W] = (r[:, S5_W:S5_W + NA_W] * NA_SCALE).astype(BF16)
    qkv_ref[:, NA_W:] = r[:, S5_W + NA_W:].astype(BF16)

    @pl.when(pl.program_id(0) < NT_P)
    def _():
        kv_ref[...] = r[:, S5_W + NA_W:]


def _in_odd(x, mods, gains, layer, j, w):
    return pl.pallas_call(
        _in_odd_kernel,
        grid=(NT,),
        in_specs=[_row(D), _mod_spec(layer), _layer(layer, (1, D)), _layer(j, w.shape[1:])],
        out_specs=[_row(S5_W), _row(3 * NA_W), _ctx_row(2 * NA_W)],
        out_shape=[jax.ShapeDtypeStruct((N_TOK, S5_W), F32), jax.ShapeDtypeStruct((N_TOK, 3 * NA_W), BF16),
                   jax.ShapeDtypeStruct((N_TOK_P, 2 * NA_W), F32)],
        compiler_params=_cp("arbitrary"),
        name="in_odd",
    )(x, mods, gains, w)


S5_TS = S5_CHUNK * S5_GROUP
S5_PL = 2 * S5_P


def _s5_prep_kernel(lre_ref, lim_ref, ls_ref, btr_ref, bti_ref, cr_ref, ci_ref, m_ref, n_ref, p_ref, a_ref, ct_scr):
    T = S5_CHUNK
    S = S5_GROUP
    hi = lax.Precision.HIGHEST
    low = lax.broadcasted_iota(jnp.int32, (S, S5_PL), 1) < S5_P
    half = [low, jnp.logical_not(low)]
    pick = lambda e, v: jnp.where(half[e], v, 0.0)
    nt = (((1,), (1,)), ((), ()))
    kps = [[None, None], [None, None]]
    for d in range(2):
        lre = lre_ref[d]
        lim = lim_ref[d]
        step = jnp.exp(ls_ref[d])
        mag = jnp.exp(lre * step)
        are = mag * jnp.cos(lim * step)
        aim = mag * jnp.sin(lim * step)
        den = lre * lre + lim * lim
        zr, zi = _cmul(are - 1.0, aim, lre / den, -lim / den)
        bbr, bbi = _cmul(zr, zi, btr_ref[d], bti_ref[d])
        cr = cr_ref[d]
        ci = ci_ref[d]
        pr = jnp.ones_like(are)
        pi = jnp.zeros_like(are)
        for k in range(T + 1):
            er, ei = _cmul(cr, ci, pr, pi)
            if k < T:
                jn = T - 1 - k if d == 0 else k
                wr, wi = _cmul(pr, pi, bbr, bbi)
                for e in range(2):
                    rows = slice(e * S5_TS + jn * S, e * S5_TS + (jn + 1) * S)
                    n_ref[d, rows, 0:S5_PL] = pick(e, wr).astype(BF16)
                    n_ref[d, rows, S5_PL:2 * S5_PL] = pick(e, wi).astype(BF16)
                jc = k if d == 0 else T - 1 - k
                ct_scr[0, jc * S:(jc + 1) * S, :] = er
                ct_scr[1, jc * S:(jc + 1) * S, :] = ei
            if k >= 1:
                t = k - 1 if d == 0 else T - k
                for e in range(2):
                    rows = slice(e * S5_TS + t * S, e * S5_TS + (t + 1) * S)
                    p_ref[rows, 2 * d * S5_PL:(2 * d + 1) * S5_PL] = pick(e, er).astype(BF16)
                    p_ref[rows, (2 * d + 1) * S5_PL:(2 * d + 2) * S5_PL] = pick(e, -ei).astype(BF16)
            if k == T:
                a_ref[d, 0] = pr
                a_ref[d, 1] = pi
            pr, pi = _cmul(pr, pi, are, aim)
        for e in range(2):
            kd = (lax.dot_general(pick(e, bbr), ct_scr[0], nt, precision=hi, preferred_element_type=F32)
                  - lax.dot_general(pick(e, bbi), ct_scr[1], nt, precision=hi, preferred_element_type=F32))
            kps[d][e] = jnp.concatenate([kd, jnp.zeros_like(kd)], axis=1)
    for e in range(2):
        for t in range(T):
            fwd = pltpu.roll(kps[0][e], t * S, axis=1)[:, :S5_TS]
            bwd = pltpu.roll(kps[1][e], (2 * S5_TS - (T - 1 - t) * S) % (2 * S5_TS), axis=1)[:, :S5_TS]
            m_ref[e, t * S:(t + 1) * S, :] = (fwd + bwd).astype(BF16)


def _s5_prep(lre, lim, ls, b_re, b_im, c_re, c_im):
    NP, S, P = S5_PAIRS, S5_GROUP, S5_P
    vec = lambda a: a.reshape(2, NP, 1, S5_PL)
    b_lay = lambda a: jnp.transpose(a.reshape(2, NP, 2, P, S), (0, 1, 4, 2, 3)).reshape(2, NP, S, S5_PL)
    c_lay = lambda a: jnp.transpose(a.reshape(2, NP, 2, S, P), (0, 1, 3, 2, 4)).reshape(2, NP, S, S5_PL)
    args = (vec(lre), vec(lim), vec(jnp.repeat(ls, P, axis=-1)), b_lay(b_re), b_lay(b_im), c_lay(c_re), c_lay(c_im))
    vspec = pl.BlockSpec((2, None, 1, S5_PL), lambda g: (0, g, 0, 0))
    mspec = pl.BlockSpec((2, None, S, S5_PL), lambda g: (0, g, 0, 0))
    lead = lambda shape: pl.BlockSpec((None,) + shape, lambda g: (g,) + (0,) * len(shape))
    return pl.pallas_call(
        _s5_prep_kernel,
        grid=(NP,),
        in_specs=[vspec, vspec, vspec, mspec, mspec, mspec, mspec],
        out_specs=[lead((2, S5_TS, S5_TS)), lead((2, 2 * S5_TS, 2 * S5_PL)), lead((2 * S5_TS, 4 * S5_PL)),
                   lead((2, 2, 1, S5_PL))],
        out_shape=[jax.ShapeDtypeStruct((NP, 2, S5_TS, S5_TS), BF16),
                   jax.ShapeDtypeStruct((NP, 2, 2 * S5_TS, 2 * S5_PL), BF16),
                   jax.ShapeDtypeStruct((NP, 2 * S5_TS, 4 * S5_PL), BF16),
                   jax.ShapeDtypeStruct((NP, 2, 2, 1, S5_PL), F32)],
        scratch_shapes=[pltpu.VMEM((2, S5_TS, S5_PL), F32)],
        compiler_params=_cp("parallel"),
        name="s5_prep",
    )(*args)


S5_NCH = N_TOK // S5_CHUNK
S5_ROWS_P = N_TOK_P // S5_CHUNK
S5_NC_P = SEQ // S5_CHUNK
S5_NC_S = DEC_SEQ // S5_CHUNK
S5_GPB = 8
S5_PPB = S5_GPB // 2
S5_XL = 8 * 128


def _s5_perm():
    r = np.arange(S5_XL)
    dst = (r // S5_GROUP % S5_GPB) * 128 + (r // 128) * S5_GROUP + r % S5_GROUP
    perm = np.zeros((S5_XL, S5_XL), np.float32)
    perm[r, dst] = 1.0
    return jnp.asarray(perm, BF16)


def _s5_kernel(u_ref, perm_ref, m_ref, n_ref, p_ref, a_ref, h0c_ref, h0l_ref, y_ref, fin_ref, z_scr, up_scr, e_scr,
               hin_scr, yc_scr):
    T = S5_CHUNK
    W = S5_PL
    for t in range(T):
        z_scr[t // 8, :, (t % 8) * 128:(t % 8 + 1) * 128] = u_ref[pl.ds(t, S5_NCH, stride=T), :].astype(BF16)
    perm = perm_ref[...]
    for j in range(2):
        up_scr[j] = _dot(z_scr[j], perm).astype(BF16)

    def scan(pp, d, h0_ref, n_chunks, n_seq, row0):
        are = a_ref[pp, d, 0]
        aim = a_ref[pp, d, 1]
        hr = h0_ref[pp, d, 0]
        hi = h0_ref[pp, d, 1]
        order = range(n_chunks) if d == 0 else reversed(range(n_chunks))
        for c in order:
            rows = pl.ds(row0 + c, n_seq, stride=n_chunks)
            hin_scr[2 * d, rows, :] = hr
            hin_scr[2 * d + 1, rows, :] = hi
            er = e_scr[2 * d, rows, :]
            ei = e_scr[2 * d + 1, rows, :]
            hr, hi = are * hr - aim * hi + er, are * hi + aim * hr + ei
        return hr, hi

    for pp in range(S5_PPB):
        us = []
        for e in range(2):
            sl = slice((2 * pp + e) * 128, (2 * pp + e + 1) * 128)
            us.append(jnp.concatenate([up_scr[0, :, sl], up_scr[1, :, sl]], axis=1))
        u2 = jnp.concatenate(us, axis=1)
        for d in range(2):
            ed = _dot(u2, n_ref[pp, d])
            e_scr[2 * d] = ed[:, :W]
            e_scr[2 * d + 1] = ed[:, W:]
        for d in range(2):
            hr, hi = scan(pp, d, h0c_ref, S5_NC_P, BATCH, 0)
            fin_ref[pp, d, 0] = hr
            fin_ref[pp, d, 1] = hi
            scan(pp, d, h0l_ref, S5_NC_S, DEC_BATCH, S5_ROWS_P)
        hin = jnp.concatenate([hin_scr[k] for k in range(4)], axis=1).astype(BF16)
        for e in range(2):
            y = _dot(us[e], m_ref[pp, e]) + _dot_nt(hin, p_ref[pp, e * S5_TS:(e + 1) * S5_TS, :])
            sl = slice((2 * pp + e) * 128, (2 * pp + e + 1) * 128)
            for j in range(2):
                yc_scr[j, :, sl] = y[:, j * 128:(j + 1) * 128]

    for j in range(2):
        yp = yc_scr[j]
        y_hi = yp.astype(BF16)
        y_lo = (yp - y_hi.astype(F32)).astype(BF16)
        r = _dot_nt(y_hi, perm) + _dot_nt(y_lo, perm)
        for k in range(8):
            y_ref[pl.ds(8 * j + k, S5_NCH, stride=T), :] = r[:, k * 128:(k + 1) * 128]


def _s5_scan(u, perm, m, n2, p2, a, h0c, h0l):
    nb = S5_GROUPS // S5_GPB
    lead = lambda shape: pl.BlockSpec((S5_PPB,) + shape, lambda w: (w,) + (0,) * len(shape))
    col = pl.BlockSpec((N_TOK, 128), lambda w: (0, w))
    return pl.pallas_call(
        _s5_kernel,
        grid=(nb,),
        in_specs=[col, _full(perm.shape), lead((2, S5_TS, S5_TS)), lead((2, 2 * S5_TS, 2 * S5_PL)),
                  lead((2 * S5_TS, 4 * S5_PL)), lead((2, 2, 1, S5_PL)), lead((2, 2, BATCH, S5_PL)),
                  lead((2, 2, DEC_BATCH, S5_PL))],
        out_specs=[col, lead((2, 2, BATCH, S5_PL))],
        out_shape=[jax.ShapeDtypeStruct((N_TOK, S5_W), F32),
                   jax.ShapeDtypeStruct((S5_PAIRS, 2, 2, BATCH, S5_PL), F32)],
        scratch_shapes=[pltpu.VMEM((2, S5_NCH, S5_XL), BF16), pltpu.VMEM((2, S5_NCH, S5_XL), BF16),
                        pltpu.VMEM((4, S5_NCH, S5_PL), F32), pltpu.VMEM((4, S5_NCH, S5_PL), F32),
                        pltpu.VMEM((2, S5_NCH, S5_XL), F32)],
        compiler_params=_cp("parallel"),
        name="s5_scan",
    )(u, perm, m, n2, p2, a, h0c, h0l)


def _na_heads(q_ref, keys, values, y_ref, rows, bias=None):
    low = _low_half((rows, 128))
    pair = lambda hh: slice(hh // 2 * 128, (hh // 2 + 1) * 128)

    def scores(hh):
        q = q_ref[:, pair(hh)]
        qm = jnp.where(low == (hh % 2 == 0), q, jnp.zeros_like(q))
        s = [_dot_nt(qm, k(pair(hh))) for k in keys]
        if bias is not None:
            s[0] = s[0] + bias(hh)
        return s

    attend = lambda hh, s: _softmax_pv(s, [v(pair(hh)) for v in values])
    _heads_pipelined(NA_HEADS, scores, attend, y_ref, rows)


def _na_ctx_kernel(q_ref, k_ref, v_ref, y_ref):
    _na_heads(q_ref, [lambda sl: k_ref[:, sl]], [lambda sl: v_ref[:, sl]], y_ref, SEQ)


def _na_ctx(qkv):
    col = lambda part: pl.BlockSpec((SEQ, NA_W), lambda b: (b, part))
    return pl.pallas_call(
        _na_ctx_kernel,
        grid=(BATCH,),
        in_specs=[col(0), col(1), col(2)],
        out_specs=pl.BlockSpec((SEQ, NA_W), lambda b: (b, 0)),
        out_shape=jax.ShapeDtypeStruct((N_TOK_P, NA_W), BF16),
        compiler_params=_cp("parallel"),
        name="na_ctx",
    )(qkv, qkv, qkv)


def _na_key_row0(rb):
    return jnp.clip(NA_QROWS * rb - NA_WIN_R // 2, 0, GRID_H - NA_KROWS)


def _na_lat_kernel(q_ref, ks_ref, vs_ref, kc_ref, vc_ref, tab_ref, y_ref):
    rb = pl.program_id(1)
    u0 = _na_key_row0(rb)
    start = pl.multiple_of(u0 * GRID_W, GRID_W)
    nk = NA_KROWS * GRID_W
    low_t = _low_half((GRID_W, 128))

    def table_row(i, w):
        qr = NA_QROWS * rb + i
        kr = u0 + w
        rs = jnp.clip(qr - NA_WIN_R // 2, 0, GRID_H - NA_WIN_R)
        inside = jnp.logical_and(kr >= rs, kr < rs + NA_WIN_R)
        return jnp.where(inside, kr - qr + NA_WIN_R - 1, NA_NDR)

    idx = [[table_row(i, w) for w in range(NA_KROWS)] for i in range(NA_QROWS)]

    def bias(h):
        rows = [jnp.concatenate([jnp.where(low_t, tab_ref[h, idx[i][w]], tab_ref[h, idx[i][w + 1]])
                                 for w in range(0, NA_KROWS, 2)], axis=1) for i in range(NA_QROWS)]
        return jnp.concatenate(rows, axis=0)

    keys = [lambda sl: ks_ref[pl.ds(start, nk), sl], lambda sl: kc_ref[:, sl].astype(BF16)]
    values = [lambda sl: vs_ref[pl.ds(start, nk), sl], lambda sl: vc_ref[:, sl].astype(BF16)]
    _na_heads(q_ref, keys, values, y_ref, TQ, bias)


def _na_lat(qkv, k_c, v_c, table):
    seq_blk = N_TOK_P // DEC_SEQ
    nb = GRID_H // NA_QROWS
    return pl.pallas_call(
        _na_lat_kernel,
        grid=(DEC_BATCH, nb),
        in_specs=[
            pl.BlockSpec((TQ, NA_W), lambda b, r: (N_TOK_P // TQ + b * nb + r, 0)),
            pl.BlockSpec((DEC_SEQ, NA_W), lambda b, r: (seq_blk + b, 1)),
            pl.BlockSpec((DEC_SEQ, NA_W), lambda b, r: (seq_blk + b, 2)),
            pl.BlockSpec((PAST_LEN, NA_W), lambda b, r: (b, 0)),
            pl.BlockSpec((PAST_LEN, NA_W), lambda b, r: (b, 0)),
            pl.BlockSpec(table.shape, lambda b, r: (0, 0, 0, 0)),
        ],
        out_specs=pl.BlockSpec((TQ, NA_W), lambda b, r: (b * nb + r, 0)),
        out_shape=jax.ShapeDtypeStruct((N_TOK_S, NA_W), BF16),
        compiler_params=_cp("parallel", "arbitrary"),
        name="na_lat",
    )(qkv, qkv, qkv, k_c, v_c, table)


def _na_table_kernel(rpb_ref, t_ref):
    qc = lax.broadcasted_iota(jnp.int32, (GRID_W, 128), 0)
    kc = lax.broadcasted_iota(jnp.int32, (GRID_W, 128), 1) % GRID_W
    cs = jnp.clip(qc - NA_WIN_C // 2, 0, GRID_W - NA_WIN_C)
    in_band = jnp.logical_and(kc >= cs, kc < cs + NA_WIN_C)
    dc = jnp.clip(kc - qc + NA_WIN_C - 1, 0, NA_NDC - 1)
    neg = jnp.full((GRID_W, 128), -jnp.inf, F32)

    def body(n, carry):
        t = neg
        for c in range(NA_NDC):
            t = jnp.where(dc == c, rpb_ref[n * NA_NDC + c], t)
        t_ref[n // NA_NDR, n % NA_NDR] = jnp.where(in_band, t, neg)
        return carry

    lax.fori_loop(0, NA_HEADS * NA_NDR, body, 0)
    for h in range(NA_HEADS):
        t_ref[h, NA_NDR] = neg


def _na_table(rpb):
    return pl.pallas_call(
        _na_table_kernel,
        in_specs=[pl.BlockSpec(memory_space=pltpu.SMEM)],
        out_shape=jax.ShapeDtypeStruct((NA_HEADS, NA_NDR + 1, GRID_W, 128), F32),
        name="na_table",
    )(rpb.reshape(-1))


def _rope_tables():
    n_freq = MLA_ROPE // 4
    inv = ROPE_BASE ** (-jnp.arange(n_freq, dtype=F32) / n_freq)
    t = jnp.arange(DEC_SEQ)
    row = (t // GRID_W).astype(F32)
    colp = (t % GRID_W).astype(F32)
    ang = jnp.concatenate([row[:, None] * inv, colp[:, None] * inv], axis=-1)
    cos, sin = jnp.cos(ang), jnp.sin(ang)
    one = jnp.ones((DEC_SEQ, MLA_NOPE), F32)
    zero = jnp.zeros((DEC_SEQ, MLA_NOPE), F32)
    cos_s = jnp.concatenate([one, cos, cos, one[:, :32]], axis=-1)
    sin_s = jnp.concatenate([zero, -sin, sin, zero[:, :32]], axis=-1)
    cos_t = jnp.concatenate([jnp.ones((TM, HEAD_SLAB), F32), cos_s], axis=0)
    sin_t = jnp.concatenate([jnp.zeros((TM, HEAD_SLAB), F32), sin_s], axis=0)
    return cos_t, sin_t


def _mla_weights(w_in, w_uq, w_ukv):
    half = MLA_ROPE // 2
    wkr = w_in[:, 4 * RET_W + MLA_Q_RANK + MLA_KV_RANK:]
    z64 = jnp.zeros((D, MLA_NOPE), F32)
    z32 = jnp.zeros((D, HEAD_SLAB - MLA_NOPE - MLA_ROPE), F32)
    wkr2 = jnp.concatenate([z64, wkr, z32, z64, wkr[:, half:], wkr[:, :half], z32], axis=1).astype(BF16)
    wq = w_uq.reshape(MLA_Q_RANK, MLA_HEADS, MLA_NOPE + MLA_ROPE)
    nope, rope = wq[..., :MLA_NOPE], wq[..., MLA_NOPE:]
    zq64 = jnp.zeros_like(nope)
    zq32 = jnp.zeros_like(rope)
    q_slab = jnp.concatenate([nope, rope, zq32], axis=-1).reshape(MLA_Q_RANK, MLA_QW)
    q_sw = jnp.concatenate([zq64, rope[..., half:], rope[..., :half], zq32], axis=-1).reshape(MLA_Q_RANK, MLA_QW)
    wuq2 = jnp.concatenate([q_slab, q_sw], axis=1).astype(BF16)
    wkv = w_ukv.reshape(MLA_KV_RANK, MLA_HEADS, MLA_NOPE + MLA_V)
    wk = jnp.concatenate([wkv[..., :MLA_NOPE], jnp.zeros_like(wkv[..., :MLA_NOPE])], axis=-1)
    wk = wk.reshape(MLA_KV_RANK, MLA_QW).astype(BF16)
    wv = wkv[..., MLA_NOPE:].reshape(MLA_KV_RANK, MLA_VW).astype(BF16)
    return wkr2, wuq2, wk, wv


def kernel(x_prompt, x_sample, c, state_ret, cache_mla_ckv, cache_mla_krope, state_s5_re, state_s5_im, cache_na_k, cache_na_v, c_ctx, ada_w, ada_b, mix_pre_g, mix_post_g, ffn_pre_g, ffn_post_g, ffn_w_up, ffn_conv_w, ffn_conv_b, ffn_w_down, even_w_in, even_w_out, ret_logit, ret_gn, mla_q_norm, mla_w_uq, mla_kv_norm, mla_w_ukv, odd_w_in, odd_w_out, s5_lambda_re, s5_lambda_im, s5_log_step, s5_b_re, s5_b_im, s5_c_re, s5_c_im, s5_d, s5_glu_w, s5_glu_b, na_rpb):
    cvec = jnp.concatenate([c_ctx[None, :], c, jnp.zeros((8 - 1 - DEC_BATCH, D), F32)], axis=0)
    mods = _ada_mods(cvec, ada_w, ada_b)
    row3 = lambda a: a.reshape(a.shape[0], 1, a.shape[1])
    mix_pre, mix_post, ffn_pre, ffn_post = row3(mix_pre_g), row3(mix_post_g), row3(ffn_pre_g), row3(ffn_post_g)
    w_up, w_down = ffn_w_up.astype(BF16), ffn_w_down.astype(BF16)
    conv_b = row3(ffn_conv_b)
    e_in, e_out = even_w_in.astype(BF16), even_w_out.astype(BF16)
    o_in, o_out, glu_w = odd_w_in.astype(BF16), odd_w_out.astype(BF16), s5_glu_w.astype(BF16)
    cos_t, sin_t = _rope_tables()
    perm = _s5_perm()
    xs = (x_prompt.reshape(N_TOK_P, D), x_sample.reshape(N_TOK_S, D))
    new_ret, new_ckv, new_kr, new_s5_re, new_s5_im, new_nak, new_nav = [], [], [], [], [], [], []
    for layer in range(DEPTH):
        j = layer // 2
        if layer % 2 == 0:
            wkr2, wuq2, wk, wv = _mla_weights(even_w_in[j], mla_w_uq[j], mla_w_ukv[j])
            qkvg, qp, kp, v, ckvn, kr = _in_even(xs, mods, mix_pre, layer, j, e_in, wkr2, row3(mla_q_norm),
                                                 row3(mla_kv_norm), wuq2, wk, wv, cos_t, sin_t)
            logit = jnp.transpose(ret_logit[j]).reshape(RET_HEADS, 2, 1, 1)
            gn = row3(ret_gn)
            s0 = jnp.zeros((BATCH, 2, RET_HEADS, RET_DIM, RET_DIM), F32)
            yr_c, st = _retention(qkvg, logit, gn, j, s0, seq_len=SEQ, n_seq=BATCH, n_blk=4, row0=0,
                                  emit_state=True)
            (yr_l,) = _retention(qkvg, logit, gn, j, state_ret[:, j], seq_len=DEC_SEQ, n_seq=DEC_BATCH, n_blk=1,
                                 row0=N_TOK_P, emit_state=False)
            ym_c = _mla_ctx(qp, kp, v)
            kr_c = jnp.pad(cache_mla_krope[:, j].reshape(DEC_BATCH * PAST_LEN, MLA_ROPE),
                           ((0, 0), (MLA_NOPE, HEAD_SLAB - MLA_NOPE - MLA_ROPE)))
            kp_c, v_c = _mla_cache(cache_mla_ckv[:, j].reshape(DEC_BATCH * PAST_LEN, MLA_KV_RANK), kr_c, wk, wv)
            ym_l = _mla_lat(qp, kp, v, kp_c, v_c)
            x = _out_even(xs, yr_c, yr_l, ym_c, ym_l, mods, mix_post, layer, j, e_out)
            new_ret.append(st)
            new_ckv.append(ckvn.reshape(BATCH, SEQ, MLA_KV_RANK))
            new_kr.append(kr[:, MLA_NOPE:MLA_NOPE + MLA_ROPE].reshape(BATCH, SEQ, MLA_ROPE))
        else:
            u, qkv, kv = _in_odd(xs[0], mods, mix_pre, layer, j, o_in)
            m, n2, p2, a = _s5_prep(s5_lambda_re[j], s5_lambda_im[j], s5_log_step[j], s5_b_re[j], s5_b_im[j],
                                    s5_c_re[j], s5_c_im[j])
            h0c = jnp.zeros((S5_PAIRS, 2, 2, BATCH, S5_PL), F32)
            h0 = jnp.stack([state_s5_re[:, j], state_s5_im[:, j]], axis=0)
            h0l = jnp.transpose(h0.reshape(2, DEC_BATCH, 2, S5_PAIRS, S5_PL), (3, 2, 0, 1, 4))
            y_raw, fin = _s5_scan(u, perm, m, n2, p2, a, h0c, h0l)
            yn_c = _na_ctx(qkv)
            yn_l = _na_lat(qkv, cache_na_k[:, j].reshape(DEC_BATCH * PAST_LEN, NA_W),
                           cache_na_v[:, j].reshape(DEC_BATCH * PAST_LEN, NA_W), _na_table(na_rpb[j]))
            x = _out_odd(xs[0], y_raw, u, yn_c, yn_l, mods, mix_post, layer, j, row3(s5_d), glu_w, row3(s5_glu_b),
                         o_out)
            st = jnp.transpose(fin.reshape(S5_PAIRS, 2, 2, BATCH, 2, S5_P), (2, 3, 1, 0, 4, 5))
            st = st.reshape(2, BATCH, 2, S5_GROUPS, S5_P)
            new_s5_re.append(st[0])
            new_s5_im.append(st[1])
            new_nak.append(kv[:, :NA_W].reshape(BATCH, SEQ, NA_HEADS, NA_DIM))
            new_nav.append(kv[:, NA_W:].reshape(BATCH, SEQ, NA_HEADS, NA_DIM))
        last = layer == DEPTH - 1
        out = _ffn(x, mods, ffn_pre, ffn_post, layer, w_up, ffn_conv_w, conv_b, w_down, split_out=last)
        xs = tuple(out) if last else (out,)
    stack = lambda a: jnp.stack(a, axis=1)
    return (xs[0].reshape(BATCH, SEQ, D), xs[1].reshape(DEC_BATCH, DEC_SEQ, D), stack(new_ret), stack(new_ckv),
            stack(new_kr), stack(new_s5_re), stack(new_s5_im), stack(new_nak), stack(new_nav))
```

```python
import functools

import numpy as np
import jax
import jax.numpy as jnp
from jax import lax
from jax.experimental import pallas as pl
from jax.experimental.pallas import tpu as pltpu

F32 = jnp.float32
BF16 = jnp.bfloat16

D = 1024
BATCH = 16
SEQ = 256
DEPTH = 4
DEC_BATCH = 2
DEC_SEQ = 2048
PAST_LEN = 512
GRID_W = 64
GRID_H = DEC_SEQ // GRID_W
EPS = 1e-6

RET_HEADS = 4
RET_W = 512
RET_DIM = 128
RET_CHUNK = 256

MLA_HEADS = 8
MLA_NOPE = 64
MLA_ROPE = 32
MLA_V = 64
MLA_Q_RANK = 256
MLA_KV_RANK = 128
MLA_SCALE = (MLA_NOPE + MLA_ROPE) ** -0.5
ROPE_BASE = 10000.0
HEAD_SLAB = 128
MLA_QW = MLA_HEADS * HEAD_SLAB
MLA_VW = MLA_HEADS * MLA_V

S5_W = 512
S5_GROUP = 16
S5_GROUPS = 32
S5_P = 64
S5_CHUNK = 16
S5_PAIRS = S5_GROUPS // 2

NA_HEADS = 8
NA_W = 512
NA_DIM = 64
NA_WIN_R = 8
NA_WIN_C = 16
NA_SCALE = NA_DIM ** -0.5
NA_QROWS = 4
NA_KROWS = 12
NA_NDR = 2 * NA_WIN_R - 1
NA_NDC = 2 * NA_WIN_C - 1

D_FF = 2816
FF_CHUNK = 256
FF_NCHUNK = D_FF // FF_CHUNK
FF_EXT = 16

TM = 512
TQ = 256
FF_SUB = SEQ
FF_NSUB = TM // FF_SUB
N_TOK_P = BATCH * SEQ
N_TOK_S = DEC_BATCH * DEC_SEQ
N_TOK = N_TOK_P + N_TOK_S
NT_P = N_TOK_P // TM
NT_S = N_TOK_S // TM
NT = NT_P + NT_S
TILES_PER_DEC = DEC_SEQ // TM

VMEM_LIMIT = 56 * 1024 * 1024


def _cp(*sem):
    return pltpu.CompilerParams(dimension_semantics=sem, vmem_limit_bytes=VMEM_LIMIT)


def _dot(a, b):
    return jnp.dot(a, b, preferred_element_type=F32)


def _dot_nt(a, b):
    return lax.dot_general(a, b, (((1,), (1,)), ((), ())), preferred_element_type=F32)


def _dot_tn(a, b):
    return lax.dot_general(a, b, (((0,), (0,)), ((), ())), preferred_element_type=F32)


def _rms(x, g):
    return x * lax.rsqrt(jnp.mean(x * x, axis=-1, keepdims=True) + EPS) * g


def _sigmoid(x):
    return 1.0 / (1.0 + jnp.exp(-x))


def _silu(x):
    return x * _sigmoid(x)


def _cmul(ar, ai, br, bi):
    return ar * br - ai * bi, ar * bi + ai * br


def _mrow(i):
    return jnp.where(i < NT_P, 0, 1 + (i - NT_P) // TILES_PER_DEC)


def _full(shape):
    n = len(shape)
    return pl.BlockSpec(shape, lambda *_: (0,) * n)


def _layer(layer, shape):
    n = len(shape)
    return pl.BlockSpec((None,) + shape, lambda *_: (layer,) + (0,) * n)


def _mod_spec(layer):
    return pl.BlockSpec((None, None, 1, 6 * D), lambda i: (layer, _mrow(i), 0, 0))


def _row(width):
    return pl.BlockSpec((TM, width), lambda i: (i, 0))


def _ctx_row(width):
    return pl.BlockSpec((TM, width), lambda i: (jnp.minimum(i, NT_P - 1), 0))


def _lat_row(width):
    return pl.BlockSpec((TM, width), lambda i: (jnp.maximum(i - NT_P, 0), 0))


def _pick(a_ref, b_ref):
    return jnp.where(pl.program_id(0) < NT_P, a_ref[...], b_ref[...])


def _ada_kernel(c_ref, w_ref, b_ref, o_ref):
    o_ref[...] = _dot(_silu(c_ref[...]).astype(BF16), w_ref[...].astype(BF16)) + b_ref[...]


def _ada_mods(cvec, ada_w, ada_b):
    nb = 4
    bn = 6 * D // nb
    out = pl.pallas_call(
        _ada_kernel,
        grid=(DEPTH, nb),
        in_specs=[
            pl.BlockSpec((8, D), lambda l, n: (0, 0)),
            pl.BlockSpec((None, D, bn), lambda l, n: (l, 0, n)),
            pl.BlockSpec((None, 1, bn), lambda l, n: (l, 0, n)),
        ],
        out_specs=pl.BlockSpec((None, 8, bn), lambda l, n: (l, 0, n)),
        out_shape=jax.ShapeDtypeStruct((DEPTH, 8, 6 * D), F32),
        compiler_params=_cp("arbitrary", "arbitrary"),
        name="ada_mods",
    )(cvec, ada_w, ada_b.reshape(DEPTH, 1, 6 * D))
    return out[:, :3].reshape(DEPTH, 3, 1, 6 * D)


def _in_even_kernel(*refs, split_x):
    nx = 2 if split_x else 1
    x = _pick(*refs[:2]) if split_x else refs[0][...]
    (mod_ref, g_ref, w_ref, wkr_ref, qn_ref, kvn_ref, wuq_ref, wk_ref, wv_ref, cos_ref, sin_ref,
     qkvg_ref, qp_ref, kp_ref, v_ref, ckv_ref, kr_ref) = refs[nx:]
    mod = mod_ref[...]
    h = _rms(x, g_ref[...]) * (1.0 + mod[:, D:2 * D]) + mod[:, :D]
    hb = h.astype(BF16)
    o = 4 * RET_W
    for part in range(4):
        cols = slice(part * RET_W, (part + 1) * RET_W)
        qkvg_ref[:, cols] = _dot(hb, w_ref[:, cols]).astype(BF16)
    r = _dot(hb, w_ref[:, o:o + MLA_Q_RANK + MLA_KV_RANK])
    cq = r[:, :MLA_Q_RANK]
    ckv_raw = r[:, MLA_Q_RANK:]
    r2 = _dot(hb, wkr_ref[...])
    kr = r2[:, :HEAD_SLAB]
    krs = r2[:, HEAD_SLAB:]
    cosf = cos_ref[...]
    sinf = sin_ref[...]
    q2 = _dot(_rms(cq, qn_ref[...]).astype(BF16), wuq_ref[...])
    ckvn = _rms(ckv_raw, kvn_ref[...])

    @pl.when(pl.program_id(0) < NT_P)
    def _():
        ckv_ref[...] = ckvn
        kr_ref[...] = kr

    cb = ckvn.astype(BF16)
    kp = _dot(cb, wk_ref[...])
    v_ref[...] = _dot(cb, wv_ref[...]).astype(BF16)
    krr = kr * cosf + krs * sinf
    for hh in range(MLA_HEADS):
        sl = slice(hh * HEAD_SLAB, (hh + 1) * HEAD_SLAB)
        ss = slice(MLA_QW + hh * HEAD_SLAB, MLA_QW + (hh + 1) * HEAD_SLAB)
        qp_ref[:, sl] = ((q2[:, sl] * cosf + q2[:, ss] * sinf) * MLA_SCALE).astype(BF16)
        kp_ref[:, sl] = (kp[:, sl] + krr).astype(BF16)


def _in_even(xs, mods, gains, layer, j, w_in, wkr2, qn, kvn, wuq2, wk, wv, cos_t, sin_t):
    split_x = len(xs) == 2
    x_specs = [_ctx_row(D), _lat_row(D)] if split_x else [_row(D)]
    pos_spec = pl.BlockSpec((TM, HEAD_SLAB), lambda i: (jnp.where(i < NT_P, 0, 1 + (i - NT_P) % TILES_PER_DEC), 0))
    return pl.pallas_call(
        functools.partial(_in_even_kernel, split_x=split_x),
        grid=(NT,),
        in_specs=x_specs + [_mod_spec(layer), _layer(layer, (1, D)), _layer(j, w_in.shape[1:]), _full(wkr2.shape),
                            _layer(j, (1, MLA_Q_RANK)), _layer(j, (1, MLA_KV_RANK)), _full(wuq2.shape),
                            _full(wk.shape), _full(wv.shape), pos_spec, pos_spec],
        out_specs=[_row(4 * RET_W), _row(MLA_QW), _row(MLA_QW), _row(MLA_VW), _ctx_row(MLA_KV_RANK),
                   _ctx_row(HEAD_SLAB)],
        out_shape=[
            jax.ShapeDtypeStruct((N_TOK, 4 * RET_W), BF16),
            jax.ShapeDtypeStruct((N_TOK, MLA_QW), BF16),
            jax.ShapeDtypeStruct((N_TOK, MLA_QW), BF16),
            jax.ShapeDtypeStruct((N_TOK, MLA_VW), BF16),
            jax.ShapeDtypeStruct((N_TOK_P, MLA_KV_RANK), F32),
            jax.ShapeDtypeStruct((N_TOK_P, HEAD_SLAB), F32),
        ],
        compiler_params=_cp("arbitrary"),
        name="in_even",
    )(*xs, mods, gains, w_in, wkr2, qn, kvn, wuq2, wk, wv, cos_t, sin_t)


def _mla_cache_kernel(ckv_ref, kr_ref, wk_ref, wv_ref, kp_ref, v_ref):
    cb = ckv_ref[...].astype(BF16)
    kp = _dot(cb, wk_ref[...])
    kr = kr_ref[...]
    for hh in range(MLA_HEADS):
        sl = slice(hh * HEAD_SLAB, (hh + 1) * HEAD_SLAB)
        kp_ref[:, sl] = (kp[:, sl] + kr).astype(BF16)
    v_ref[...] = _dot(cb, wv_ref[...]).astype(BF16)


def _mla_cache(ckv_c, kr_slab, wk, wv):
    row = lambda w: pl.BlockSpec((PAST_LEN, w), lambda b: (b, 0))
    return pl.pallas_call(
        _mla_cache_kernel,
        grid=(DEC_BATCH,),
        in_specs=[row(MLA_KV_RANK), row(HEAD_SLAB), _full(wk.shape), _full(wv.shape)],
        out_specs=[row(MLA_QW), row(MLA_VW)],
        out_shape=[jax.ShapeDtypeStruct((DEC_BATCH * PAST_LEN, MLA_QW), BF16),
                   jax.ShapeDtypeStruct((DEC_BATCH * PAST_LEN, MLA_VW), BF16)],
        compiler_params=_cp("parallel"),
        name="mla_cache",
    )(ckv_c, kr_slab, wk, wv)


def _ret_kernel(lg_ref, q_ref, k_ref, v_ref, g_ref, gn_ref, s0_ref, *rest, seq_len, n_blk, emit_state, n_prev):
    prev_refs, rest = rest[:min(n_prev, 1)], rest[min(n_prev, 1):]
    if emit_state:
        y_ref, st_ref, sf_scr, sb_scr, dm_scr, w_scr = rest
    else:
        y_ref, sf_scr, sb_scr, dm_scr, w_scr = rest
    C = RET_CHUNK
    nc = seq_len // C

    @pl.when(pl.program_id(1) == 0)
    def _():
        lg = -jnp.log(1.0 + jnp.exp(-lg_ref[...]))
        lg_f = lg[0]
        lg_b = lg[1]
        ii = lax.broadcasted_iota(jnp.int32, (C, C), 0)
        jj = lax.broadcasted_iota(jnp.int32, (C, C), 1)
        diff = (ii - jj).astype(F32)
        dm_scr[...] = (jnp.where(diff >= 0, jnp.exp(lg_f * jnp.maximum(diff, 0.0)), 0.0)
                       + jnp.where(diff <= 0, jnp.exp(lg_b * jnp.maximum(-diff, 0.0)), 0.0))
        pos = lax.broadcasted_iota(jnp.int32, (C, RET_DIM), 0).astype(F32)
        w_scr[0] = jnp.exp(lg_f * (pos + 1.0))
        w_scr[1] = jnp.exp(lg_f * (C - 1.0 - pos))
        w_scr[2] = jnp.exp(lg_b * (C - pos))
        w_scr[3] = jnp.exp(lg_b * pos)
        w_scr[4] = jnp.exp(lg_f * C) + jnp.zeros((C, RET_DIM), F32)
        w_scr[5] = jnp.exp(lg_b * C) + jnp.zeros((C, RET_DIM), F32)

    qw_f, kw_f, qw_b, kw_b = w_scr[0], w_scr[1], w_scr[2], w_scr[3]
    cd_f = w_scr[4, :RET_DIM, :]
    cd_b = w_scr[5, :RET_DIM, :]
    ld = lambda ref, rows: ref[rows, :].astype(F32)
    gn = gn_ref[...]
    dm = dm_scr[...]

    for b in range(n_blk):
        rows_of = lambda n: slice(b * seq_len + n * C, b * seq_len + (n + 1) * C)
        sf = lambda n: sf_scr.at[b * (nc + 1) + n]
        sb = lambda n: sb_scr.at[b * (nc + 1) + n]
        sf(0)[...] = s0_ref[b, 0]
        for n in range(nc):
            kv = _dot_tn((ld(k_ref, rows_of(n)) * kw_f).astype(BF16), v_ref[rows_of(n), :])
            sf(n + 1)[...] = cd_f * sf(n)[...] + kv
        sb(nc)[...] = s0_ref[b, 1]
        for n in reversed(range(nc)):
            kv = _dot_tn((ld(k_ref, rows_of(n)) * kw_b).astype(BF16), v_ref[rows_of(n), :])
            sb(n)[...] = cd_b * sb(n + 1)[...] + kv
        if emit_state:
            for jp in range(n_prev):
                st_ref[b, jp] = prev_refs[0][b, jp]
            st_ref[b, n_prev, 0] = sf(nc)[...]
            st_ref[b, n_prev, 1] = sb(0)[...]
        for n in range(nc):
            rows = rows_of(n)
            q = ld(q_ref, rows) * (RET_DIM ** -0.5)
            s = _dot_nt(q.astype(BF16), k_ref[rows, :]) * dm
            o = (_dot(s.astype(BF16), v_ref[rows, :])
                 + _dot((q * qw_f).astype(BF16), sf(n)[...].astype(BF16))
                 + _dot((q * qw_b).astype(BF16), sb(n + 1)[...].astype(BF16)))
            mu = jnp.mean(o, axis=-1, keepdims=True)
            oc = o - mu
            var = jnp.mean(oc * oc, axis=-1, keepdims=True)
            on = oc * lax.rsqrt(var + EPS) * gn
            y_ref[rows, :] = (_silu(ld(g_ref, rows)) * on).astype(BF16)


def _retention(qkvg, logit, gn, j, s0, *, seq_len, n_seq, n_blk, row0, emit_state, prev_states=None):
    n_prev = 0 if prev_states is None else prev_states.shape[1]
    nc = seq_len // RET_CHUNK
    rows = n_blk * seq_len
    blk0 = row0 // rows
    col = lambda part: pl.BlockSpec((rows, RET_DIM), lambda h, s: (blk0 + s, part * RET_HEADS + h))
    state = pl.BlockSpec((n_blk, 2, None, RET_DIM, RET_DIM), lambda h, s: (s, 0, h, 0, 0))
    in_specs = [
        pl.BlockSpec((None, 2, 1, 1), lambda h, s: (h, 0, 0, 0)),
        col(0), col(1), col(2), col(3),
        pl.BlockSpec((None, 1, RET_DIM), lambda h, s: (j, 0, h)),
        state,
    ]
    out_specs = [pl.BlockSpec((rows, RET_DIM), lambda h, s: (s, h))]
    out_shape = [jax.ShapeDtypeStruct((n_seq * seq_len, RET_W), BF16)]
    args = [logit, qkvg, qkvg, qkvg, qkvg, gn, s0]
    stacked = lambda n: pl.BlockSpec((n_blk, n, 2, None, RET_DIM, RET_DIM), lambda h, s: (s, 0, 0, h, 0, 0))
    if n_prev:
        in_specs.append(stacked(n_prev))
        args.append(prev_states)
    if emit_state:
        out_specs.append(stacked(n_prev + 1))
        out_shape.append(jax.ShapeDtypeStruct((n_seq, n_prev + 1, 2, RET_HEADS, RET_DIM, RET_DIM), F32))
    n_st = n_blk * (nc + 1)
    return pl.pallas_call(
        functools.partial(_ret_kernel, seq_len=seq_len, n_blk=n_blk, emit_state=emit_state, n_prev=n_prev),
        grid=(RET_HEADS, n_seq // n_blk),
        in_specs=in_specs,
        out_specs=out_specs,
        out_shape=out_shape,
        scratch_shapes=[pltpu.VMEM((n_st, RET_DIM, RET_DIM), F32), pltpu.VMEM((n_st, RET_DIM, RET_DIM), F32),
                        pltpu.VMEM((RET_CHUNK, RET_CHUNK), F32), pltpu.VMEM((6, RET_CHUNK, RET_DIM), F32)],
        compiler_params=_cp("parallel", "arbitrary"),
        name="retention_%d" % seq_len,
    )(*args)


def _softmax_pv(score_blocks, value_blocks):
    m = functools.reduce(jnp.maximum, [jnp.max(s, axis=-1, keepdims=True) for s in score_blocks])
    ps = [jnp.exp(s - m) for s in score_blocks]
    l = functools.reduce(lambda a, b: a + b, [jnp.sum(p, axis=-1, keepdims=True) for p in ps])
    o = functools.reduce(lambda a, b: a + b, [_dot(p.astype(BF16), v) for p, v in zip(ps, value_blocks)])
    return o / l


def _low_half(shape):
    return lax.broadcasted_iota(jnp.int32, shape, 1) < 64


def _heads_pipelined(n_heads, scores, attend, y_ref, rows):
    low = _low_half((rows, 128))
    nxt = scores(0)
    outs = []
    for hh in range(n_heads):
        cur = nxt
        if hh + 1 < n_heads:
            nxt = scores(hh + 1)
        outs.append(attend(hh, cur))
        if hh % 2 == 1:
            hp = hh // 2
            y_ref[:, hp * 128:(hp + 1) * 128] = jnp.where(low, outs[hh - 1], outs[hh]).astype(BF16)


def _mla_ctx_kernel(q_ref, k_ref, v_ref, y_ref):
    slab = lambda hh: slice(hh * HEAD_SLAB, (hh + 1) * HEAD_SLAB)
    pair = lambda hh: slice(hh // 2 * 128, (hh // 2 + 1) * 128)
    scores = lambda hh: [_dot_nt(q_ref[:, slab(hh)], k_ref[:, slab(hh)])]
    attend = lambda hh, s: _softmax_pv(s, [v_ref[:, pair(hh)]])
    _heads_pipelined(MLA_HEADS, scores, attend, y_ref, SEQ)


def _mla_ctx(qp, kp, v):
    return pl.pallas_call(
        _mla_ctx_kernel,
        grid=(BATCH,),
        in_specs=[pl.BlockSpec((SEQ, MLA_QW), lambda b: (b, 0)), pl.BlockSpec((SEQ, MLA_QW), lambda b: (b, 0)),
                  pl.BlockSpec((SEQ, MLA_VW), lambda b: (b, 0))],
        out_specs=pl.BlockSpec((SEQ, MLA_VW), lambda b: (b, 0)),
        out_shape=jax.ShapeDtypeStruct((N_TOK_P, MLA_VW), BF16),
        compiler_params=_cp("parallel"),
        name="mla_ctx",
    )(qp, kp, v)


def _mla_lat_kernel(q_ref, k_ref, v_ref, kc_ref, vc_ref, y_ref):
    slab = lambda hh: slice(hh * HEAD_SLAB, (hh + 1) * HEAD_SLAB)
    pair = lambda hh: slice(hh // 2 * 128, (hh // 2 + 1) * 128)
    scores = lambda hh: [_dot_nt(q_ref[:, slab(hh)], k_ref[:, slab(hh)]),
                         _dot_nt(q_ref[:, slab(hh)], kc_ref[:, slab(hh)])]
    attend = lambda hh, s: _softmax_pv(s, [v_ref[:, pair(hh)], vc_ref[:, pair(hh)]])
    _heads_pipelined(MLA_HEADS, scores, attend, y_ref, TQ)


def _mla_lat(qp, kp, v, kp_c, v_c):
    seq_blk = N_TOK_P // DEC_SEQ
    nq = DEC_SEQ // TQ
    return pl.pallas_call(
        _mla_lat_kernel,
        grid=(DEC_BATCH, nq),
        in_specs=[
            pl.BlockSpec((TQ, MLA_QW), lambda b, t: (N_TOK_P // TQ + b * nq + t, 0)),
            pl.BlockSpec((DEC_SEQ, MLA_QW), lambda b, t: (seq_blk + b, 0)),
            pl.BlockSpec((DEC_SEQ, MLA_VW), lambda b, t: (seq_blk + b, 0)),
            pl.BlockSpec((PAST_LEN, MLA_QW), lambda b, t: (b, 0)),
            pl.BlockSpec((PAST_LEN, MLA_VW), lambda b, t: (b, 0)),
        ],
        out_specs=pl.BlockSpec((TQ, MLA_VW), lambda b, t: (b * nq + t, 0)),
        out_shape=jax.ShapeDtypeStruct((N_TOK_S, MLA_VW), BF16),
        compiler_params=_cp("parallel", "parallel"),
        name="mla_lat",
    )(qp, kp, v, kp_c, v_c)


def _cast_ffn_slab(wu32_ref, wd32_ref, wu_ref, wd_ref):
    wu_ref[...] = wu32_ref[...].astype(BF16)
    wd_ref[...] = wd32_ref[...].astype(BF16)


def _cast_ffn_specs(layer):
    ru, rd = D // NT, D_FF // NT
    cast_in = [pl.BlockSpec((None, ru, 2 * D_FF), lambda i: (layer, i, 0)),
               pl.BlockSpec((None, rd, D), lambda i: (layer, i, 0))]
    cast_out = [pl.BlockSpec((ru, 2 * D_FF), lambda i: (i, 0)), pl.BlockSpec((rd, D), lambda i: (i, 0))]
    cast_shape = [jax.ShapeDtypeStruct((D, 2 * D_FF), BF16), jax.ShapeDtypeStruct((D_FF, D), BF16)]
    return cast_in, cast_out, cast_shape


def _out_even_kernel(*refs, split_x):
    nx = 2 if split_x else 1
    x = _pick(*refs[:2]) if split_x else refs[0][...]
    yrc_ref, yrl_ref, ymc_ref, yml_ref, mod_ref, g_ref, w_ref, wu32_ref, wd32_ref, o_ref, wu_ref, wd_ref = refs[nx:]
    mod = mod_ref[...]
    r = _dot(_pick(yrc_ref, yrl_ref), w_ref[:RET_W, :]) + _dot(_pick(ymc_ref, yml_ref), w_ref[RET_W:, :])
    o_ref[...] = x + mod[:, 2 * D:3 * D] * _rms(r, g_ref[...])
    _cast_ffn_slab(wu32_ref, wd32_ref, wu_ref, wd_ref)


def _out_even(xs, yr_c, yr_l, ym_c, ym_l, mods, gains, layer, j, w, w_up, w_down):
    split_x = len(xs) == 2
    x_specs = [_ctx_row(D), _lat_row(D)] if split_x else [_row(D)]
    cast_in, cast_out, cast_shape = _cast_ffn_specs(layer)
    return pl.pallas_call(
        functools.partial(_out_even_kernel, split_x=split_x),
        grid=(NT,),
        in_specs=x_specs + [_ctx_row(RET_W), _lat_row(RET_W), _ctx_row(MLA_VW), _lat_row(MLA_VW), _mod_spec(layer),
                            _layer(layer, (1, D)), _layer(j, (D, D))] + cast_in,
        out_specs=[_row(D)] + cast_out,
        out_shape=[jax.ShapeDtypeStruct((N_TOK, D), F32)] + cast_shape,
        compiler_params=_cp("parallel"),
        name="out_even",
    )(*xs, yr_c, yr_l, ym_c, ym_l, mods, gains, w, w_up, w_down)


def _gelu_tanh(x):
    return 0.5 * x * (1.0 + jnp.tanh(np.sqrt(2.0 / np.pi).astype(np.float32) * (x + 0.044715 * (x * x * x))))


def _out_odd_kernel(x_ref, yr_ref, u_ref, ync_ref, ynl_ref, mod_ref, g_ref, d_ref, gw_ref, gb_ref, w_ref, wu32_ref,
                    wd32_ref, o_ref, wu_ref, wd_ref):
    mod = mod_ref[...]
    y = _gelu_tanh(yr_ref[...] + d_ref[...] * u_ref[...])
    y = y * _sigmoid(_dot(y.astype(BF16), gw_ref[...]) + gb_ref[...])
    r = _dot(y.astype(BF16), w_ref[:S5_W, :]) + _dot(_pick(ync_ref, ynl_ref), w_ref[S5_W:, :])
    o_ref[...] = x_ref[...] + mod[:, 2 * D:3 * D] * _rms(r, g_ref[...])
    _cast_ffn_slab(wu32_ref, wd32_ref, wu_ref, wd_ref)


def _out_odd(x, y_raw, u, yn_c, yn_l, mods, gains, layer, j, d_skip, glu_w, glu_b, w, w_up, w_down):
    cast_in, cast_out, cast_shape = _cast_ffn_specs(layer)
    return pl.pallas_call(
        _out_odd_kernel,
        grid=(NT,),
        in_specs=[_row(D), _row(S5_W), _row(S5_W), _ctx_row(NA_W), _lat_row(NA_W), _mod_spec(layer),
                  _layer(layer, (1, D)), _layer(j, (1, S5_W)), _layer(j, (S5_W, S5_W)), _layer(j, (1, S5_W)),
                  _layer(j, (D, D))] + cast_in,
        out_specs=[_row(D)] + cast_out,
        out_shape=[jax.ShapeDtypeStruct((N_TOK, D), F32)] + cast_shape,
        compiler_params=_cp("parallel"),
        name="out_odd",
    )(x, y_raw, u, yn_c, yn_l, mods, gains, d_skip, glu_w, glu_b, w, w_up, w_down)


def _ffn_kernel(xm_ref, xp_ref, xn_ref, mod_ref, gpre_ref, gpost_ref, wu_ref, cw_ref, cb_ref, wd_ref, *rest,
                split_out):
    if split_out:
        oc_ref, ol_ref, h_scr, act_scr = rest
    else:
        o_ref, h_scr, act_scr = rest
    i = pl.program_id(0)
    is_lat = i >= NT_P
    t = (i - NT_P) % TILES_PER_DEC
    has_prev = jnp.logical_and(is_lat, t != 0)
    has_next = jnp.logical_and(is_lat, t != TILES_PER_DEC - 1)
    mod = mod_ref[...]
    shift = mod[:, 3 * D:4 * D]
    scale = mod[:, 4 * D:5 * D]
    gate = mod[:, 5 * D:6 * D]
    gpre = gpre_ref[...]
    nm = lambda x: _rms(x, gpre) * (1.0 + scale) + shift
    x = xm_ref[...]
    h = nm(x)
    blk = FF_SUB + FF_EXT
    rows = FF_NSUB * blk
    for k in range(FF_NSUB):
        h_scr[k * blk:k * blk + FF_SUB, :] = h[k * FF_SUB:(k + 1) * FF_SUB].astype(BF16)
        if k + 1 < FF_NSUB:
            after = jnp.where(is_lat, h[(k + 1) * FF_SUB:(k + 1) * FF_SUB + 8], 0.0)
            before = jnp.where(is_lat, h[(k + 1) * FF_SUB - 8:(k + 1) * FF_SUB], 0.0)
        else:
            after = jnp.where(has_next, nm(xn_ref[...]), 0.0)
            before = jnp.where(has_prev, nm(xp_ref[...]), 0.0)
        h_scr[k * blk + FF_SUB:(k + 1) * blk, :] = jnp.concatenate([after, before], axis=0).astype(BF16)
    hb = h_scr[...]

    def up(j):
        ca = slice(j * FF_CHUNK, (j + 1) * FF_CHUNK)
        cg = slice(D_FF + j * FF_CHUNK, D_FF + (j + 1) * FF_CHUNK)
        return (_dot(hb, wu_ref[:, ca]), ca), (_dot(hb, wu_ref[:, cg]), cg)

    def conv(part):
        u, cols = part
        cw = cw_ref[:, cols]
        return (cw[0:1, :] * pltpu.roll(u, 1, axis=0) + cw[1:2, :] * u + cw[2:3, :] * pltpu.roll(u, rows - 1, axis=0)
                + cb_ref[:, cols])

    nxt = up(0)
    for j in range(FF_NCHUNK):
        cur = nxt
        if j + 1 < FF_NCHUNK:
            nxt = up(j + 1)
        act = (_silu(conv(cur[1])) * conv(cur[0])).astype(BF16)
        for k in range(FF_NSUB):
            act_scr[k * FF_SUB:(k + 1) * FF_SUB, j * FF_CHUNK:(j + 1) * FF_CHUNK] = act[k * blk:k * blk + FF_SUB]
    y = _dot(act_scr[...], wd_ref[...])
    out = x + gate * _rms(y, gpost_ref[...])
    if split_out:
        @pl.when(i < NT_P)
        def _():
            oc_ref[...] = out

        @pl.when(i >= NT_P)
        def _():
            ol_ref[...] = out
    else:
        o_ref[...] = out


def _ffn(x, mods, gpre, gpost, layer, wu, cw, cb, wd, split_out):
    hb = TM // 8
    nblk = N_TOK // 8
    prev = pl.BlockSpec((8, D), lambda i: (jnp.maximum(i * hb - 1, 0), 0))
    nxt = pl.BlockSpec((8, D), lambda i: (jnp.minimum((i + 1) * hb, nblk - 1), 0))
    resident = lambda shape: pl.BlockSpec(shape, lambda i: (0, 0), pipeline_mode=pl.Buffered(1))
    if split_out:
        out_specs = [_ctx_row(D), _lat_row(D)]
        out_shape = [jax.ShapeDtypeStruct((N_TOK_P, D), F32), jax.ShapeDtypeStruct((N_TOK_S, D), F32)]
    else:
        out_specs = _row(D)
        out_shape = jax.ShapeDtypeStruct((N_TOK, D), F32)
    return pl.pallas_call(
        functools.partial(_ffn_kernel, split_out=split_out),
        grid=(NT,),
        in_specs=[_row(D), prev, nxt, _mod_spec(layer), _layer(layer, (1, D)), _layer(layer, (1, D)),
                  resident((D, 2 * D_FF)), _layer(layer, (3, 2 * D_FF)), _layer(layer, (1, 2 * D_FF)),
                  resident((D_FF, D))],
        out_specs=out_specs,
        out_shape=out_shape,
        scratch_shapes=[pltpu.VMEM((FF_NSUB * (FF_SUB + FF_EXT), D), BF16), pltpu.VMEM((TM, D_FF), BF16)],
        compiler_params=_cp("arbitrary"),
        name="ffn",
    )(x, x, x, mods, gpre, gpost, wu, cw, cb, wd)


def _in_odd_kernel(x_ref, mod_ref, g_ref, w_ref, *rest, n_prev):
    prev_refs, (u_ref, qkv_ref, kc_ref, vc_ref) = rest[:2 * min(n_prev, 1)], rest[2 * min(n_prev, 1):]
    mod = mod_ref[...]
    h = _rms(x_ref[...], g_ref[...]) * (1.0 + mod[:, D:2 * D]) + mod[:, :D]
    r = _dot(h.astype(BF16), w_ref[...])
    u_ref[...] = r[:, :S5_W]
    qkv_ref[:, :NA_W] = (r[:, S5_W:S5_W + NA_W] * NA_SCALE).astype(BF16)
    qkv_ref[:, NA_W:] = r[:, S5_W + NA_W:].astype(BF16)

    @pl.when(pl.program_id(0) < NT_P)
    def _():
        for part, (out_ref, col0) in enumerate([(kc_ref, S5_W + NA_W), (vc_ref, S5_W + 2 * NA_W)]):
            for b in range(TM // SEQ):
                for jp in range(n_prev):
                    out_ref[b, jp] = prev_refs[part][b, jp]
                out_ref[b, n_prev] = r[b * SEQ:(b + 1) * SEQ, col0:col0 + NA_W]


def _in_odd(x, mods, gains, layer, j, w, prev_kv=None):
    n_prev = 0 if prev_kv is None else prev_kv[0].shape[1]
    nb = TM // SEQ
    stacked = lambda n: pl.BlockSpec((nb, n, SEQ, NA_W), lambda i: (jnp.minimum(i, NT_P - 1), 0, 0, 0))
    leaf = jax.ShapeDtypeStruct((BATCH, n_prev + 1, SEQ, NA_W), F32)
    prev_specs = [stacked(n_prev)] * 2 if n_prev else []
    return pl.pallas_call(
        functools.partial(_in_odd_kernel, n_prev=n_prev),
        grid=(NT,),
        in_specs=[_row(D), _mod_spec(layer), _layer(layer, (1, D)), _layer(j, w.shape[1:])] + prev_specs,
        out_specs=[_row(S5_W), _row(3 * NA_W), stacked(n_prev + 1), stacked(n_prev + 1)],
        out_shape=[jax.ShapeDtypeStruct((N_TOK, S5_W), F32), jax.ShapeDtypeStruct((N_TOK, 3 * NA_W), BF16), leaf, leaf],
        compiler_params=_cp("arbitrary"),
        name="in_odd",
    )(x, mods, gains, w, *(prev_kv or ()))


S5_TS = S5_CHUNK * S5_GROUP
S5_PL = 2 * S5_P


def _s5_prep_kernel(lre_ref, lim_ref, ls_ref, btr_ref, bti_ref, cr_ref, ci_ref, m_ref, n_ref, p_ref, a_ref, ct_scr):
    T = S5_CHUNK
    S = S5_GROUP
    hi = lax.Precision.HIGHEST
    low = lax.broadcasted_iota(jnp.int32, (S, S5_PL), 1) < S5_P
    half = [low, jnp.logical_not(low)]
    pick = lambda e, v: jnp.where(half[e], v, 0.0)
    nt = (((1,), (1,)), ((), ()))
    kps = [[None, None], [None, None]]
    for d in range(2):
        lre = lre_ref[d]
        lim = lim_ref[d]
        step = jnp.exp(ls_ref[d])
        mag = jnp.exp(lre * step)
        are = mag * jnp.cos(lim * step)
        aim = mag * jnp.sin(lim * step)
        den = lre * lre + lim * lim
        zr, zi = _cmul(are - 1.0, aim, lre / den, -lim / den)
        bbr, bbi = _cmul(zr, zi, btr_ref[d], bti_ref[d])
        cr = cr_ref[d]
        ci = ci_ref[d]
        pr = jnp.ones_like(are)
        pi = jnp.zeros_like(are)
        for k in range(T + 1):
            er, ei = _cmul(cr, ci, pr, pi)
            if k < T:
                jn = T - 1 - k if d == 0 else k
                wr, wi = _cmul(pr, pi, bbr, bbi)
                for e in range(2):
                    rows = slice(e * S5_TS + jn * S, e * S5_TS + (jn + 1) * S)
                    n_ref[d, rows, 0:S5_PL] = pick(e, wr).astype(BF16)
                    n_ref[d, rows, S5_PL:2 * S5_PL] = pick(e, wi).astype(BF16)
                jc = k if d == 0 else T - 1 - k
                ct_scr[0, jc * S:(jc + 1) * S, :] = er
                ct_scr[1, jc * S:(jc + 1) * S, :] = ei
            if k >= 1:
                t = k - 1 if d == 0 else T - k
                for e in range(2):
                    rows = slice(e * S5_TS + t * S, e * S5_TS + (t + 1) * S)
                    p_ref[rows, 2 * d * S5_PL:(2 * d + 1) * S5_PL] = pick(e, er).astype(BF16)
                    p_ref[rows, (2 * d + 1) * S5_PL:(2 * d + 2) * S5_PL] = pick(e, -ei).astype(BF16)
            if k == T:
                a_ref[d, 0] = pr
                a_ref[d, 1] = pi
            pr, pi = _cmul(pr, pi, are, aim)
        for e in range(2):
            kd = (lax.dot_general(pick(e, bbr), ct_scr[0], nt, precision=hi, preferred_element_type=F32)
                  - lax.dot_general(pick(e, bbi), ct_scr[1], nt, precision=hi, preferred_element_type=F32))
            kps[d][e] = jnp.concatenate([kd, jnp.zeros_like(kd)], axis=1)
    for e in range(2):
        for t in range(T):
            fwd = pltpu.roll(kps[0][e], t * S, axis=1)[:, :S5_TS]
            bwd = pltpu.roll(kps[1][e], (2 * S5_TS - (T - 1 - t) * S) % (2 * S5_TS), axis=1)[:, :S5_TS]
            m_ref[e, t * S:(t + 1) * S, :] = (fwd + bwd).astype(BF16)


def _s5_prep(lre, lim, ls, b_re, b_im, c_re, c_im):
    NP, S, P = S5_PAIRS, S5_GROUP, S5_P
    vec = lambda a: a.reshape(2, NP, 1, S5_PL)
    b_lay = lambda a: jnp.transpose(a.reshape(2, NP, 2, P, S), (0, 1, 4, 2, 3)).reshape(2, NP, S, S5_PL)
    c_lay = lambda a: jnp.transpose(a.reshape(2, NP, 2, S, P), (0, 1, 3, 2, 4)).reshape(2, NP, S, S5_PL)
    args = (vec(lre), vec(lim), vec(jnp.repeat(ls, P, axis=-1)), b_lay(b_re), b_lay(b_im), c_lay(c_re), c_lay(c_im))
    vspec = pl.BlockSpec((2, None, 1, S5_PL), lambda g: (0, g, 0, 0))
    mspec = pl.BlockSpec((2, None, S, S5_PL), lambda g: (0, g, 0, 0))
    lead = lambda shape: pl.BlockSpec((None,) + shape, lambda g: (g,) + (0,) * len(shape))
    return pl.pallas_call(
        _s5_prep_kernel,
        grid=(NP,),
        in_specs=[vspec, vspec, vspec, mspec, mspec, mspec, mspec],
        out_specs=[lead((2, S5_TS, S5_TS)), lead((2, 2 * S5_TS, 2 * S5_PL)), lead((2 * S5_TS, 4 * S5_PL)),
                   lead((2, 2, 1, S5_PL))],
        out_shape=[jax.ShapeDtypeStruct((NP, 2, S5_TS, S5_TS), BF16),
                   jax.ShapeDtypeStruct((NP, 2, 2 * S5_TS, 2 * S5_PL), BF16),
                   jax.ShapeDtypeStruct((NP, 2 * S5_TS, 4 * S5_PL), BF16),
                   jax.ShapeDtypeStruct((NP, 2, 2, 1, S5_PL), F32)],
        scratch_shapes=[pltpu.VMEM((2, S5_TS, S5_PL), F32)],
        compiler_params=_cp("parallel"),
        name="s5_prep",
    )(*args)


S5_NCH = N_TOK // S5_CHUNK
S5_ROWS_P = N_TOK_P // S5_CHUNK
S5_NC_P = SEQ // S5_CHUNK
S5_NC_S = DEC_SEQ // S5_CHUNK
S5_GPB = 8
S5_PPB = S5_GPB // 2
S5_XL = 8 * 128


def _s5_perm():
    r = np.arange(S5_XL)
    dst = (r // S5_GROUP % S5_GPB) * 128 + (r // 128) * S5_GROUP + r % S5_GROUP
    perm = np.zeros((S5_XL, S5_XL), np.float32)
    perm[r, dst] = 1.0
    return jnp.asarray(perm, BF16)


def _s5_kernel(u_ref, perm_ref, m_ref, n_ref, p_ref, a_ref, h0c_ref, h0l_ref, y_ref, fin_ref, z_scr, up_scr, e_scr,
               hin_scr, yc_scr):
    T = S5_CHUNK
    W = S5_PL
    for t in range(T):
        z_scr[t // 8, :, (t % 8) * 128:(t % 8 + 1) * 128] = u_ref[pl.ds(t, S5_NCH, stride=T), :].astype(BF16)
    perm = perm_ref[...]
    for j in range(2):
        up_scr[j] = _dot(z_scr[j], perm).astype(BF16)

    def scan(pp, d, h0_ref, n_chunks, n_seq, row0):
        are = a_ref[pp, d, 0]
        aim = a_ref[pp, d, 1]
        hr = h0_ref[pp, d, 0]
        hi = h0_ref[pp, d, 1]
        order = range(n_chunks) if d == 0 else reversed(range(n_chunks))
        for c in order:
            rows = pl.ds(row0 + c, n_seq, stride=n_chunks)
            hin_scr[2 * d, rows, :] = hr
            hin_scr[2 * d + 1, rows, :] = hi
            er = e_scr[2 * d, rows, :]
            ei = e_scr[2 * d + 1, rows, :]
            hr, hi = are * hr - aim * hi + er, are * hi + aim * hr + ei
        return hr, hi

    for pp in range(S5_PPB):
        us = []
        for e in range(2):
            sl = slice((2 * pp + e) * 128, (2 * pp + e + 1) * 128)
            us.append(jnp.concatenate([up_scr[0, :, sl], up_scr[1, :, sl]], axis=1))
        u2 = jnp.concatenate(us, axis=1)
        for d in range(2):
            ed = _dot(u2, n_ref[pp, d])
            e_scr[2 * d] = ed[:, :W]
            e_scr[2 * d + 1] = ed[:, W:]
        for d in range(2):
            hr, hi = scan(pp, d, h0c_ref, S5_NC_P, BATCH, 0)
            fin_ref[pp, d, 0] = hr
            fin_ref[pp, d, 1] = hi
            scan(pp, d, h0l_ref, S5_NC_S, DEC_BATCH, S5_ROWS_P)
        hin = jnp.concatenate([hin_scr[k] for k in range(4)], axis=1).astype(BF16)
        for e in range(2):
            y = _dot(us[e], m_ref[pp, e]) + _dot_nt(hin, p_ref[pp, e * S5_TS:(e + 1) * S5_TS, :])
            sl = slice((2 * pp + e) * 128, (2 * pp + e + 1) * 128)
            for j in range(2):
                yc_scr[j, :, sl] = y[:, j * 128:(j + 1) * 128]

    for j in range(2):
        yp = yc_scr[j]
        y_hi = yp.astype(BF16)
        y_lo = (yp - y_hi.astype(F32)).astype(BF16)
        r = _dot_nt(y_hi, perm) + _dot_nt(y_lo, perm)
        for k in range(8):
            y_ref[pl.ds(8 * j + k, S5_NCH, stride=T), :] = r[:, k * 128:(k + 1) * 128]


def _s5_scan(u, perm, m, n2, p2, a, h0c, h0l):
    nb = S5_GROUPS // S5_GPB
    lead = lambda shape: pl.BlockSpec((S5_PPB,) + shape, lambda w: (w,) + (0,) * len(shape))
    col = pl.BlockSpec((N_TOK, 128), lambda w: (0, w))
    return pl.pallas_call(
        _s5_kernel,
        grid=(nb,),
        in_specs=[col, _full(perm.shape), lead((2, S5_TS, S5_TS)), lead((2, 2 * S5_TS, 2 * S5_PL)),
                  lead((2 * S5_TS, 4 * S5_PL)), lead((2, 2, 1, S5_PL)), lead((2, 2, BATCH, S5_PL)),
                  lead((2, 2, DEC_BATCH, S5_PL))],
        out_specs=[col, lead((2, 2, BATCH, S5_PL))],
        out_shape=[jax.ShapeDtypeStruct((N_TOK, S5_W), F32),
                   jax.ShapeDtypeStruct((S5_PAIRS, 2, 2, BATCH, S5_PL), F32)],
        scratch_shapes=[pltpu.VMEM((2, S5_NCH, S5_XL), BF16), pltpu.VMEM((2, S5_NCH, S5_XL), BF16),
                        pltpu.VMEM((4, S5_NCH, S5_PL), F32), pltpu.VMEM((4, S5_NCH, S5_PL), F32),
                        pltpu.VMEM((2, S5_NCH, S5_XL), F32)],
        compiler_params=_cp("parallel"),
        name="s5_scan",
    )(u, perm, m, n2, p2, a, h0c, h0l)


def _na_heads(q_ref, keys, values, y_ref, rows, bias=None):
    low = _low_half((rows, 128))
    pair = lambda hh: slice(hh // 2 * 128, (hh // 2 + 1) * 128)

    def scores(hh):
        q = q_ref[:, pair(hh)]
        qm = jnp.where(low == (hh % 2 == 0), q, jnp.zeros_like(q))
        s = [_dot_nt(qm, k(pair(hh))) for k in keys]
        if bias is not None:
            s[0] = s[0] + bias(hh)
        return s

    attend = lambda hh, s: _softmax_pv(s, [v(pair(hh)) for v in values])
    _heads_pipelined(NA_HEADS, scores, attend, y_ref, rows)


def _na_ctx_kernel(q_ref, k_ref, v_ref, y_ref):
    _na_heads(q_ref, [lambda sl: k_ref[:, sl]], [lambda sl: v_ref[:, sl]], y_ref, SEQ)


def _na_ctx(qkv):
    col = lambda part: pl.BlockSpec((SEQ, NA_W), lambda b: (b, part))
    return pl.pallas_call(
        _na_ctx_kernel,
        grid=(BATCH,),
        in_specs=[col(0), col(1), col(2)],
        out_specs=pl.BlockSpec((SEQ, NA_W), lambda b: (b, 0)),
        out_shape=jax.ShapeDtypeStruct((N_TOK_P, NA_W), BF16),
        compiler_params=_cp("parallel"),
        name="na_ctx",
    )(qkv, qkv, qkv)


def _na_key_row0(rb):
    return jnp.clip(NA_QROWS * rb - NA_WIN_R // 2, 0, GRID_H - NA_KROWS)


def _na_lat_kernel(q_ref, ks_ref, vs_ref, kc_ref, vc_ref, tab_ref, y_ref):
    rb = pl.program_id(1)
    u0 = _na_key_row0(rb)
    start = pl.multiple_of(u0 * GRID_W, GRID_W)
    nk = NA_KROWS * GRID_W
    low_t = _low_half((GRID_W, 128))

    def table_row(i, w):
        qr = NA_QROWS * rb + i
        kr = u0 + w
        rs = jnp.clip(qr - NA_WIN_R // 2, 0, GRID_H - NA_WIN_R)
        inside = jnp.logical_and(kr >= rs, kr < rs + NA_WIN_R)
        return jnp.where(inside, kr - qr + NA_WIN_R - 1, NA_NDR)

    idx = [[table_row(i, w) for w in range(NA_KROWS)] for i in range(NA_QROWS)]

    def bias(h):
        rows = [jnp.concatenate([jnp.where(low_t, tab_ref[h, idx[i][w]], tab_ref[h, idx[i][w + 1]])
                                 for w in range(0, NA_KROWS, 2)], axis=1) for i in range(NA_QROWS)]
        return jnp.concatenate(rows, axis=0)

    keys = [lambda sl: ks_ref[pl.ds(start, nk), sl], lambda sl: kc_ref[:, sl].astype(BF16)]
    values = [lambda sl: vs_ref[pl.ds(start, nk), sl], lambda sl: vc_ref[:, sl].astype(BF16)]
    _na_heads(q_ref, keys, values, y_ref, TQ, bias)


def _na_lat(qkv, k_c, v_c, table):
    seq_blk = N_TOK_P // DEC_SEQ
    nb = GRID_H // NA_QROWS
    return pl.pallas_call(
        _na_lat_kernel,
        grid=(DEC_BATCH, nb),
        in_specs=[
            pl.BlockSpec((TQ, NA_W), lambda b, r: (N_TOK_P // TQ + b * nb + r, 0)),
            pl.BlockSpec((DEC_SEQ, NA_W), lambda b, r: (seq_blk + b, 1)),
            pl.BlockSpec((DEC_SEQ, NA_W), lambda b, r: (seq_blk + b, 2)),
            pl.BlockSpec((PAST_LEN, NA_W), lambda b, r: (b, 0)),
            pl.BlockSpec((PAST_LEN, NA_W), lambda b, r: (b, 0)),
            pl.BlockSpec(table.shape, lambda b, r: (0, 0, 0, 0)),
        ],
        out_specs=pl.BlockSpec((TQ, NA_W), lambda b, r: (b * nb + r, 0)),
        out_shape=jax.ShapeDtypeStruct((N_TOK_S, NA_W), BF16),
        compiler_params=_cp("parallel", "arbitrary"),
        name="na_lat",
    )(qkv, qkv, qkv, k_c, v_c, table)


def _na_table_kernel(rpb_ref, t_ref):
    qc = lax.broadcasted_iota(jnp.int32, (GRID_W, 128), 0)
    kc = lax.broadcasted_iota(jnp.int32, (GRID_W, 128), 1) % GRID_W
    cs = jnp.clip(qc - NA_WIN_C // 2, 0, GRID_W - NA_WIN_C)
    in_band = jnp.logical_and(kc >= cs, kc < cs + NA_WIN_C)
    dc = jnp.clip(kc - qc + NA_WIN_C - 1, 0, NA_NDC - 1)
    neg = jnp.full((GRID_W, 128), -jnp.inf, F32)

    def body(n, carry):
        t = neg
        for c in range(NA_NDC):
            t = jnp.where(dc == c, rpb_ref[n * NA_NDC + c], t)
        t_ref[n // NA_NDR, n % NA_NDR] = jnp.where(in_band, t, neg)
        return carry

    lax.fori_loop(0, NA_HEADS * NA_NDR, body, 0)
    for h in range(NA_HEADS):
        t_ref[h, NA_NDR] = neg


def _na_table(rpb):
    return pl.pallas_call(
        _na_table_kernel,
        in_specs=[pl.BlockSpec(memory_space=pltpu.SMEM)],
        out_shape=jax.ShapeDtypeStruct((NA_HEADS, NA_NDR + 1, GRID_W, 128), F32),
        name="na_table",
    )(rpb.reshape(-1))


def _rope_tables():
    n_freq = MLA_ROPE // 4
    inv = ROPE_BASE ** (-jnp.arange(n_freq, dtype=F32) / n_freq)
    t = jnp.arange(DEC_SEQ)
    row = (t // GRID_W).astype(F32)
    colp = (t % GRID_W).astype(F32)
    ang = jnp.concatenate([row[:, None] * inv, colp[:, None] * inv], axis=-1)
    cos, sin = jnp.cos(ang), jnp.sin(ang)
    one = jnp.ones((DEC_SEQ, MLA_NOPE), F32)
    zero = jnp.zeros((DEC_SEQ, MLA_NOPE), F32)
    cos_s = jnp.concatenate([one, cos, cos, one[:, :32]], axis=-1)
    sin_s = jnp.concatenate([zero, -sin, sin, zero[:, :32]], axis=-1)
    cos_t = jnp.concatenate([jnp.ones((TM, HEAD_SLAB), F32), cos_s], axis=0)
    sin_t = jnp.concatenate([jnp.zeros((TM, HEAD_SLAB), F32), sin_s], axis=0)
    return cos_t, sin_t


def _mla_weights(w_in, w_uq, w_ukv):
    half = MLA_ROPE // 2
    wkr = w_in[:, 4 * RET_W + MLA_Q_RANK + MLA_KV_RANK:]
    z64 = jnp.zeros((D, MLA_NOPE), F32)
    z32 = jnp.zeros((D, HEAD_SLAB - MLA_NOPE - MLA_ROPE), F32)
    wkr2 = jnp.concatenate([z64, wkr, z32, z64, wkr[:, half:], wkr[:, :half], z32], axis=1).astype(BF16)
    wq = w_uq.reshape(MLA_Q_RANK, MLA_HEADS, MLA_NOPE + MLA_ROPE)
    nope, rope = wq[..., :MLA_NOPE], wq[..., MLA_NOPE:]
    zq64 = jnp.zeros_like(nope)
    zq32 = jnp.zeros_like(rope)
    q_slab = jnp.concatenate([nope, rope, zq32], axis=-1).reshape(MLA_Q_RANK, MLA_QW)
    q_sw = jnp.concatenate([zq64, rope[..., half:], rope[..., :half], zq32], axis=-1).reshape(MLA_Q_RANK, MLA_QW)
    wuq2 = jnp.concatenate([q_slab, q_sw], axis=1).astype(BF16)
    wkv = w_ukv.reshape(MLA_KV_RANK, MLA_HEADS, MLA_NOPE + MLA_V)
    wk = jnp.concatenate([wkv[..., :MLA_NOPE], jnp.zeros_like(wkv[..., :MLA_NOPE])], axis=-1)
    wk = wk.reshape(MLA_KV_RANK, MLA_QW).astype(BF16)
    wv = wkv[..., MLA_NOPE:].reshape(MLA_KV_RANK, MLA_VW).astype(BF16)
    return wkr2, wuq2, wk, wv


def kernel(x_prompt, x_sample, c, state_ret, cache_mla_ckv, cache_mla_krope, state_s5_re, state_s5_im, cache_na_k, cache_na_v, c_ctx, ada_w, ada_b, mix_pre_g, mix_post_g, ffn_pre_g, ffn_post_g, ffn_w_up, ffn_conv_w, ffn_conv_b, ffn_w_down, even_w_in, even_w_out, ret_logit, ret_gn, mla_q_norm, mla_w_uq, mla_kv_norm, mla_w_ukv, odd_w_in, odd_w_out, s5_lambda_re, s5_lambda_im, s5_log_step, s5_b_re, s5_b_im, s5_c_re, s5_c_im, s5_d, s5_glu_w, s5_glu_b, na_rpb):
    cvec = jnp.concatenate([c_ctx[None, :], c, jnp.zeros((8 - 1 - DEC_BATCH, D), F32)], axis=0)
    mods = _ada_mods(cvec, ada_w, ada_b)
    row3 = lambda a: a.reshape(a.shape[0], 1, a.shape[1])
    mix_pre, mix_post, ffn_pre, ffn_post = row3(mix_pre_g), row3(mix_post_g), row3(ffn_pre_g), row3(ffn_post_g)
    conv_b = row3(ffn_conv_b)
    e_in, e_out = even_w_in.astype(BF16), even_w_out.astype(BF16)
    o_in, o_out, glu_w = odd_w_in.astype(BF16), odd_w_out.astype(BF16), s5_glu_w.astype(BF16)
    cos_t, sin_t = _rope_tables()
    perm = _s5_perm()
    xs = (x_prompt.reshape(N_TOK_P, D), x_sample.reshape(N_TOK_S, D))
    ret_states, na_kv = None, None
    new_ckv, new_kr, new_s5_re, new_s5_im = [], [], [], []
    for layer in range(DEPTH):
        j = layer // 2
        if layer % 2 == 0:
            wkr2, wuq2, wk, wv = _mla_weights(even_w_in[j], mla_w_uq[j], mla_w_ukv[j])
            qkvg, qp, kp, v, ckvn, kr = _in_even(xs, mods, mix_pre, layer, j, e_in, wkr2, row3(mla_q_norm),
                                                 row3(mla_kv_norm), wuq2, wk, wv, cos_t, sin_t)
            logit = jnp.transpose(ret_logit[j]).reshape(RET_HEADS, 2, 1, 1)
            gn = row3(ret_gn)
            s0 = jnp.zeros((BATCH, 2, RET_HEADS, RET_DIM, RET_DIM), F32)
            yr_c, ret_states = _retention(qkvg, logit, gn, j, s0, seq_len=SEQ, n_seq=BATCH, n_blk=4, row0=0,
                                          emit_state=True, prev_states=ret_states)
            (yr_l,) = _retention(qkvg, logit, gn, j, state_ret[:, j], seq_len=DEC_SEQ, n_seq=DEC_BATCH, n_blk=1,
                                 row0=N_TOK_P, emit_state=False)
            ym_c = _mla_ctx(qp, kp, v)
            kr_c = jnp.pad(cache_mla_krope[:, j].reshape(DEC_BATCH * PAST_LEN, MLA_ROPE),
                           ((0, 0), (MLA_NOPE, HEAD_SLAB - MLA_NOPE - MLA_ROPE)))
            kp_c, v_c = _mla_cache(cache_mla_ckv[:, j].reshape(DEC_BATCH * PAST_LEN, MLA_KV_RANK), kr_c, wk, wv)
            ym_l = _mla_lat(qp, kp, v, kp_c, v_c)
            x, w_up, w_down = _out_even(xs, yr_c, yr_l, ym_c, ym_l, mods, mix_post, layer, j, e_out, ffn_w_up,
                                        ffn_w_down)
            new_ckv.append(ckvn.reshape(BATCH, SEQ, MLA_KV_RANK))
            new_kr.append(kr[:, MLA_NOPE:MLA_NOPE + MLA_ROPE].reshape(BATCH, SEQ, MLA_ROPE))
        else:
            u, qkv, *na_kv = _in_odd(xs[0], mods, mix_pre, layer, j, o_in, prev_kv=na_kv)
            m, n2, p2, a = _s5_prep(s5_lambda_re[j], s5_lambda_im[j], s5_log_step[j], s5_b_re[j], s5_b_im[j],
                                    s5_c_re[j], s5_c_im[j])
            h0c = jnp.zeros((S5_PAIRS, 2, 2, BATCH, S5_PL), F32)
            h0 = jnp.stack([state_s5_re[:, j], state_s5_im[:, j]], axis=0)
            h0l = jnp.transpose(h0.reshape(2, DEC_BATCH, 2, S5_PAIRS, S5_PL), (3, 2, 0, 1, 4))
            y_raw, fin = _s5_scan(u, perm, m, n2, p2, a, h0c, h0l)
            yn_c = _na_ctx(qkv)
            yn_l = _na_lat(qkv, cache_na_k[:, j].reshape(DEC_BATCH * PAST_LEN, NA_W),
                           cache_na_v[:, j].reshape(DEC_BATCH * PAST_LEN, NA_W), _na_table(na_rpb[j]))
            x, w_up, w_down = _out_odd(xs[0], y_raw, u, yn_c, yn_l, mods, mix_post, layer, j, row3(s5_d), glu_w,
                                       row3(s5_glu_b), o_out, ffn_w_up, ffn_w_down)
            st = jnp.transpose(fin.reshape(S5_PAIRS, 2, 2, BATCH, 2, S5_P), (2, 3, 1, 0, 4, 5))
            st = st.reshape(2, BATCH, 2, S5_GROUPS, S5_P)
            new_s5_re.append(st[0])
            new_s5_im.append(st[1])
        last = layer == DEPTH - 1
        out = _ffn(x, mods, ffn_pre, ffn_post, layer, w_up, ffn_conv_w, conv_b, w_down, split_out=last)
        xs = tuple(out) if last else (out,)
    stack = lambda a: jnp.stack(a, axis=1)
    heads = lambda a: a.reshape(BATCH, a.shape[1], SEQ, NA_HEADS, NA_DIM)
    return (xs[0].reshape(BATCH, SEQ, D), xs[1].reshape(DEC_BATCH, DEC_SEQ, D), ret_states, stack(new_ckv),
            stack(new_kr), stack(new_s5_re), stack(new_s5_im), heads(na_kv[0]), heads(na_kv[1]))
```

```python
import functools

import numpy as np
import jax
import jax.numpy as jnp
from jax import lax
from jax.experimental import pallas as pl
from jax.experimental.pallas import tpu as pltpu

F32 = jnp.float32
BF16 = jnp.bfloat16

D = 1024
BATCH = 16
SEQ = 256
DEPTH = 4
DEC_BATCH = 2
DEC_SEQ = 2048
PAST_LEN = 512
GRID_W = 64
GRID_H = DEC_SEQ // GRID_W
EPS = 1e-6
LOG2E = 1.4426950408889634

RET_HEADS = 4
RET_W = 512
RET_DIM = 128
RET_CHUNK = 256

MLA_HEADS = 8
MLA_NOPE = 64
MLA_ROPE = 32
MLA_V = 64
MLA_Q_RANK = 256
MLA_KV_RANK = 128
MLA_SCALE = (MLA_NOPE + MLA_ROPE) ** -0.5
ROPE_BASE = 10000.0
HEAD_SLAB = 128
MLA_QW = MLA_HEADS * HEAD_SLAB
MLA_VW = MLA_HEADS * MLA_V

S5_W = 512
S5_GROUP = 16
S5_GROUPS = 32
S5_P = 64
S5_CHUNK = 16
S5_PAIRS = S5_GROUPS // 2

NA_HEADS = 8
NA_W = 512
NA_DIM = 64
NA_WIN_R = 8
NA_WIN_C = 16
NA_SCALE = NA_DIM ** -0.5
NA_QROWS = 4
NA_KROWS = 12
NA_NDR = 2 * NA_WIN_R - 1
NA_NDC = 2 * NA_WIN_C - 1

D_FF = 2816
FF_CHUNK = 256
FF_NCHUNK = D_FF // FF_CHUNK
FF_EXT = 16

TM = 512
TQ = 256
FF_SUB = SEQ
FF_NSUB = TM // FF_SUB
N_TOK_P = BATCH * SEQ
N_TOK_S = DEC_BATCH * DEC_SEQ
N_TOK = N_TOK_P + N_TOK_S
NT_P = N_TOK_P // TM
NT_S = N_TOK_S // TM
NT = NT_P + NT_S
TILES_PER_DEC = DEC_SEQ // TM

VMEM_LIMIT = 56 * 1024 * 1024


def _cp(*sem):
    return pltpu.CompilerParams(dimension_semantics=sem, vmem_limit_bytes=VMEM_LIMIT)


def _dot(a, b):
    return jnp.dot(a, b, preferred_element_type=F32)


def _dot_nt(a, b):
    return lax.dot_general(a, b, (((1,), (1,)), ((), ())), preferred_element_type=F32)


def _dot_tn(a, b):
    return lax.dot_general(a, b, (((0,), (0,)), ((), ())), preferred_element_type=F32)


def _rms(x, g):
    return x * lax.rsqrt(jnp.mean(x * x, axis=-1, keepdims=True) + EPS) * g


def _sigmoid(x):
    return 1.0 / (1.0 + jnp.exp(-x))


def _silu(x):
    return x * _sigmoid(x)


def _cmul(ar, ai, br, bi):
    return ar * br - ai * bi, ar * bi + ai * br


def _mrow(i):
    return jnp.where(i < NT_P, 0, 1 + (i - NT_P) // TILES_PER_DEC)


def _full(shape):
    n = len(shape)
    return pl.BlockSpec(shape, lambda *_: (0,) * n)


def _layer(layer, shape):
    n = len(shape)
    return pl.BlockSpec((None,) + shape, lambda *_: (layer,) + (0,) * n)


def _mod_spec(layer):
    return pl.BlockSpec((None, None, 1, 6 * D), lambda i: (layer, _mrow(i), 0, 0))


def _row(width):
    return pl.BlockSpec((TM, width), lambda i: (i, 0))


def _ctx_row(width):
    return pl.BlockSpec((TM, width), lambda i: (jnp.minimum(i, NT_P - 1), 0))


def _lat_row(width):
    return pl.BlockSpec((TM, width), lambda i: (jnp.maximum(i - NT_P, 0), 0))


def _pick(a_ref, b_ref):
    return jnp.where(pl.program_id(0) < NT_P, a_ref[...], b_ref[...])


def _ada_kernel(c_ref, w_ref, b_ref, o_ref):
    o_ref[...] = _dot(_silu(c_ref[...]).astype(BF16), w_ref[...].astype(BF16)) + b_ref[...]


def _ada_mods(cvec, ada_w, ada_b):
    nb = 4
    bn = 6 * D // nb
    out = pl.pallas_call(
        _ada_kernel,
        grid=(DEPTH, nb),
        in_specs=[
            pl.BlockSpec((8, D), lambda l, n: (0, 0)),
            pl.BlockSpec((None, D, bn), lambda l, n: (l, 0, n)),
            pl.BlockSpec((None, 1, bn), lambda l, n: (l, 0, n)),
        ],
        out_specs=pl.BlockSpec((None, 8, bn), lambda l, n: (l, 0, n)),
        out_shape=jax.ShapeDtypeStruct((DEPTH, 8, 6 * D), F32),
        compiler_params=_cp("arbitrary", "arbitrary"),
        name="ada_mods",
    )(cvec, ada_w, ada_b.reshape(DEPTH, 1, 6 * D))
    return out[:, :3].reshape(DEPTH, 3, 1, 6 * D)


def _in_even_kernel(*refs, split_x):
    nx = 2 if split_x else 1
    x = _pick(*refs[:2]) if split_x else refs[0][...]
    (mod_ref, g_ref, w_ref, wkr_ref, qn_ref, kvn_ref, wuq_ref, wk_ref, wv_ref, cos_ref, sin_ref,
     qkvg_ref, qp_ref, kp_ref, v_ref, ckv_ref, kr_ref) = refs[nx:]
    mod = mod_ref[...]
    h = _rms(x, g_ref[...]) * (1.0 + mod[:, D:2 * D]) + mod[:, :D]
    hb = h.astype(BF16)
    o = 4 * RET_W
    for part in range(4):
        cols = slice(part * RET_W, (part + 1) * RET_W)
        qkvg_ref[:, cols] = _dot(hb, w_ref[:, cols]).astype(BF16)
    r = _dot(hb, w_ref[:, o:o + MLA_Q_RANK + MLA_KV_RANK])
    cq = r[:, :MLA_Q_RANK]
    ckv_raw = r[:, MLA_Q_RANK:]
    r2 = _dot(hb, wkr_ref[...])
    kr = r2[:, :HEAD_SLAB]
    krs = r2[:, HEAD_SLAB:]
    cosf = cos_ref[...]
    sinf = sin_ref[...]
    q2 = _dot(_rms(cq, qn_ref[...]).astype(BF16), wuq_ref[...])
    ckvn = _rms(ckv_raw, kvn_ref[...])

    @pl.when(pl.program_id(0) < NT_P)
    def _():
        ckv_ref[...] = ckvn
        kr_ref[...] = kr

    cb = ckvn.astype(BF16)
    kp = _dot(cb, wk_ref[...])
    v_ref[...] = _dot(cb, wv_ref[...]).astype(BF16)
    krr = kr * cosf + krs * sinf
    for hh in range(MLA_HEADS):
        sl = slice(hh * HEAD_SLAB, (hh + 1) * HEAD_SLAB)
        ss = slice(MLA_QW + hh * HEAD_SLAB, MLA_QW + (hh + 1) * HEAD_SLAB)
        qp_ref[:, sl] = ((q2[:, sl] * cosf + q2[:, ss] * sinf) * (MLA_SCALE * LOG2E)).astype(BF16)
        kp_ref[:, sl] = (kp[:, sl] + krr).astype(BF16)


def _in_even(xs, mods, gains, layer, j, w_in, wkr2, qn, kvn, wuq2, wk, wv, cos_t, sin_t):
    split_x = len(xs) == 2
    x_specs = [_ctx_row(D), _lat_row(D)] if split_x else [_row(D)]
    pos_spec = pl.BlockSpec((TM, HEAD_SLAB), lambda i: (jnp.where(i < NT_P, 0, 1 + (i - NT_P) % TILES_PER_DEC), 0))
    return pl.pallas_call(
        functools.partial(_in_even_kernel, split_x=split_x),
        grid=(NT,),
        in_specs=x_specs + [_mod_spec(layer), _layer(layer, (1, D)), _layer(j, w_in.shape[1:]), _full(wkr2.shape),
                            _layer(j, (1, MLA_Q_RANK)), _layer(j, (1, MLA_KV_RANK)), _full(wuq2.shape),
                            _full(wk.shape), _full(wv.shape), pos_spec, pos_spec],
        out_specs=[_row(4 * RET_W), _row(MLA_QW), _row(MLA_QW), _row(MLA_VW), _ctx_row(MLA_KV_RANK),
                   _ctx_row(HEAD_SLAB)],
        out_shape=[
            jax.ShapeDtypeStruct((N_TOK, 4 * RET_W), BF16),
            jax.ShapeDtypeStruct((N_TOK, MLA_QW), BF16),
            jax.ShapeDtypeStruct((N_TOK, MLA_QW), BF16),
            jax.ShapeDtypeStruct((N_TOK, MLA_VW), BF16),
            jax.ShapeDtypeStruct((N_TOK_P, MLA_KV_RANK), F32),
            jax.ShapeDtypeStruct((N_TOK_P, HEAD_SLAB), F32),
        ],
        compiler_params=_cp("arbitrary"),
        name="in_even",
    )(*xs, mods, gains, w_in, wkr2, qn, kvn, wuq2, wk, wv, cos_t, sin_t)


def _mla_cache_kernel(ckv_ref, kr_ref, wk_ref, wv_ref, kp_ref, v_ref):
    cb = ckv_ref[...].astype(BF16)
    kp = _dot(cb, wk_ref[...])
    kr = kr_ref[...]
    for hh in range(MLA_HEADS):
        sl = slice(hh * HEAD_SLAB, (hh + 1) * HEAD_SLAB)
        kp_ref[:, sl] = (kp[:, sl] + kr).astype(BF16)
    v_ref[...] = _dot(cb, wv_ref[...]).astype(BF16)


def _mla_cache(ckv_c, kr_slab, wk, wv):
    row = lambda w: pl.BlockSpec((PAST_LEN, w), lambda b: (b, 0))
    return pl.pallas_call(
        _mla_cache_kernel,
        grid=(DEC_BATCH,),
        in_specs=[row(MLA_KV_RANK), row(HEAD_SLAB), _full(wk.shape), _full(wv.shape)],
        out_specs=[row(MLA_QW), row(MLA_VW)],
        out_shape=[jax.ShapeDtypeStruct((DEC_BATCH * PAST_LEN, MLA_QW), BF16),
                   jax.ShapeDtypeStruct((DEC_BATCH * PAST_LEN, MLA_VW), BF16)],
        compiler_params=_cp("parallel"),
        name="mla_cache",
    )(ckv_c, kr_slab, wk, wv)


def _ret_kernel(lg_ref, q_ref, k_ref, v_ref, g_ref, gn_ref, s0_ref, *rest, seq_len, n_blk, emit_state, n_prev):
    prev_refs, rest = rest[:min(n_prev, 1)], rest[min(n_prev, 1):]
    if emit_state:
        y_ref, st_ref, sf_scr, sb_scr, dm_scr, w_scr = rest
    else:
        y_ref, sf_scr, sb_scr, dm_scr, w_scr = rest
    C = RET_CHUNK
    nc = seq_len // C

    @pl.when(pl.program_id(1) == 0)
    def _():
        lg = -jnp.log(1.0 + jnp.exp(-lg_ref[...]))
        lg_f = lg[0]
        lg_b = lg[1]
        ii = lax.broadcasted_iota(jnp.int32, (C, C), 0)
        jj = lax.broadcasted_iota(jnp.int32, (C, C), 1)
        diff = (ii - jj).astype(F32)
        dm_scr[...] = (jnp.where(diff >= 0, jnp.exp(lg_f * jnp.maximum(diff, 0.0)), 0.0)
                       + jnp.where(diff <= 0, jnp.exp(lg_b * jnp.maximum(-diff, 0.0)), 0.0))
        pos = lax.broadcasted_iota(jnp.int32, (C, RET_DIM), 0).astype(F32)
        w_scr[0] = jnp.exp(lg_f * (pos + 1.0))
        w_scr[1] = jnp.exp(lg_f * (C - 1.0 - pos))
        w_scr[2] = jnp.exp(lg_b * (C - pos))
        w_scr[3] = jnp.exp(lg_b * pos)
        w_scr[4] = jnp.exp(lg_f * C) + jnp.zeros((C, RET_DIM), F32)
        w_scr[5] = jnp.exp(lg_b * C) + jnp.zeros((C, RET_DIM), F32)

    qw_f, kw_f, qw_b, kw_b = w_scr[0], w_scr[1], w_scr[2], w_scr[3]
    cd_f = w_scr[4, :RET_DIM, :]
    cd_b = w_scr[5, :RET_DIM, :]
    ld = lambda ref, rows: ref[rows, :].astype(F32)
    gn = gn_ref[...]
    dm = dm_scr[...]

    for b in range(n_blk):
        rows_of = lambda n: slice(b * seq_len + n * C, b * seq_len + (n + 1) * C)
        sf = lambda n: sf_scr.at[b * (nc + 1) + n]
        sb = lambda n: sb_scr.at[b * (nc + 1) + n]
        sf(0)[...] = s0_ref[b, 0]
        for n in range(nc):
            kv = _dot_tn((ld(k_ref, rows_of(n)) * kw_f).astype(BF16), v_ref[rows_of(n), :])
            sf(n + 1)[...] = cd_f * sf(n)[...] + kv
        sb(nc)[...] = s0_ref[b, 1]
        for n in reversed(range(nc)):
            kv = _dot_tn((ld(k_ref, rows_of(n)) * kw_b).astype(BF16), v_ref[rows_of(n), :])
            sb(n)[...] = cd_b * sb(n + 1)[...] + kv
        if emit_state:
            for jp in range(n_prev):
                st_ref[b, jp] = prev_refs[0][b, jp]
            st_ref[b, n_prev, 0] = sf(nc)[...]
            st_ref[b, n_prev, 1] = sb(0)[...]
        for n in range(nc):
            rows = rows_of(n)
            q = ld(q_ref, rows) * (RET_DIM ** -0.5)
            s = _dot_nt(q.astype(BF16), k_ref[rows, :]) * dm
            o = (_dot(s.astype(BF16), v_ref[rows, :])
                 + _dot((q * qw_f).astype(BF16), sf(n)[...].astype(BF16))
                 + _dot((q * qw_b).astype(BF16), sb(n + 1)[...].astype(BF16)))
            mu = jnp.mean(o, axis=-1, keepdims=True)
            oc = o - mu
            var = jnp.mean(oc * oc, axis=-1, keepdims=True)
            on = oc * lax.rsqrt(var + EPS) * gn
            y_ref[rows, :] = (_silu(ld(g_ref, rows)) * on).astype(BF16)


def _retention(qkvg, logit, gn, j, s0, *, seq_len, n_seq, n_blk, row0, emit_state, prev_states=None):
    n_prev = 0 if prev_states is None else prev_states.shape[1]
    nc = seq_len // RET_CHUNK
    rows = n_blk * seq_len
    blk0 = row0 // rows
    col = lambda part: pl.BlockSpec((rows, RET_DIM), lambda h, s: (blk0 + s, part * RET_HEADS + h))
    state = pl.BlockSpec((n_blk, 2, None, RET_DIM, RET_DIM), lambda h, s: (s, 0, h, 0, 0))
    in_specs = [
        pl.BlockSpec((None, 2, 1, 1), lambda h, s: (h, 0, 0, 0)),
        col(0), col(1), col(2), col(3),
        pl.BlockSpec((None, 1, RET_DIM), lambda h, s: (j, 0, h)),
        state,
    ]
    out_specs = [pl.BlockSpec((rows, RET_DIM), lambda h, s: (s, h))]
    out_shape = [jax.ShapeDtypeStruct((n_seq * seq_len, RET_W), BF16)]
    args = [logit, qkvg, qkvg, qkvg, qkvg, gn, s0]
    stacked = lambda n: pl.BlockSpec((n_blk, n, 2, None, RET_DIM, RET_DIM), lambda h, s: (s, 0, 0, h, 0, 0))
    if n_prev:
        in_specs.append(stacked(n_prev))
        args.append(prev_states)
    if emit_state:
        out_specs.append(stacked(n_prev + 1))
        out_shape.append(jax.ShapeDtypeStruct((n_seq, n_prev + 1, 2, RET_HEADS, RET_DIM, RET_DIM), F32))
    n_st = n_blk * (nc + 1)
    return pl.pallas_call(
        functools.partial(_ret_kernel, seq_len=seq_len, n_blk=n_blk, emit_state=emit_state, n_prev=n_prev),
        grid=(RET_HEADS, n_seq // n_blk),
        in_specs=in_specs,
        out_specs=out_specs,
        out_shape=out_shape,
        scratch_shapes=[pltpu.VMEM((n_st, RET_DIM, RET_DIM), F32), pltpu.VMEM((n_st, RET_DIM, RET_DIM), F32),
                        pltpu.VMEM((RET_CHUNK, RET_CHUNK), F32), pltpu.VMEM((6, RET_CHUNK, RET_DIM), F32)],
        compiler_params=_cp("parallel", "arbitrary"),
        name="retention_%d" % seq_len,
    )(*args)


def _softmax_pv(score_blocks, value_blocks):
    m = functools.reduce(jnp.maximum, [jnp.max(s, axis=-1, keepdims=True) for s in score_blocks])
    ps = [jnp.exp2(s - m) for s in score_blocks]
    l = functools.reduce(lambda a, b: a + b, [jnp.sum(p, axis=-1, keepdims=True) for p in ps])
    o = functools.reduce(lambda a, b: a + b, [_dot(p.astype(BF16), v) for p, v in zip(ps, value_blocks)])
    return o / l


def _low_half(shape):
    return lax.broadcasted_iota(jnp.int32, shape, 1) < 64


def _heads_pipelined(n_heads, scores, attend, y_ref, rows):
    low = _low_half((rows, 128))
    nxt = scores(0)
    outs = []
    for hh in range(n_heads):
        cur = nxt
        if hh + 1 < n_heads:
            nxt = scores(hh + 1)
        outs.append(attend(hh, cur))
        if hh % 2 == 1:
            hp = hh // 2
            y_ref[:, hp * 128:(hp + 1) * 128] = jnp.where(low, outs[hh - 1], outs[hh]).astype(BF16)


def _mla_ctx_kernel(q_ref, k_ref, v_ref, y_ref):
    slab = lambda hh: slice(hh * HEAD_SLAB, (hh + 1) * HEAD_SLAB)
    pair = lambda hh: slice(hh // 2 * 128, (hh // 2 + 1) * 128)
    scores = lambda hh: [_dot_nt(q_ref[:, slab(hh)], k_ref[:, slab(hh)])]
    attend = lambda hh, s: _softmax_pv(s, [v_ref[:, pair(hh)]])
    _heads_pipelined(MLA_HEADS, scores, attend, y_ref, SEQ)


def _mla_ctx(qp, kp, v):
    return pl.pallas_call(
        _mla_ctx_kernel,
        grid=(BATCH,),
        in_specs=[pl.BlockSpec((SEQ, MLA_QW), lambda b: (b, 0)), pl.BlockSpec((SEQ, MLA_QW), lambda b: (b, 0)),
                  pl.BlockSpec((SEQ, MLA_VW), lambda b: (b, 0))],
        out_specs=pl.BlockSpec((SEQ, MLA_VW), lambda b: (b, 0)),
        out_shape=jax.ShapeDtypeStruct((N_TOK_P, MLA_VW), BF16),
        compiler_params=_cp("parallel"),
        name="mla_ctx",
    )(qp, kp, v)


def _mla_lat_kernel(q_ref, k_ref, v_ref, kc_ref, vc_ref, y_ref):
    slab = lambda hh: slice(hh * HEAD_SLAB, (hh + 1) * HEAD_SLAB)
    pair = lambda hh: slice(hh // 2 * 128, (hh // 2 + 1) * 128)
    scores = lambda hh: [_dot_nt(q_ref[:, slab(hh)], k_ref[:, slab(hh)]),
                         _dot_nt(q_ref[:, slab(hh)], kc_ref[:, slab(hh)])]
    attend = lambda hh, s: _softmax_pv(s, [v_ref[:, pair(hh)], vc_ref[:, pair(hh)]])
    _heads_pipelined(MLA_HEADS, scores, attend, y_ref, TQ)


def _mla_lat(qp, kp, v, kp_c, v_c):
    seq_blk = N_TOK_P // DEC_SEQ
    nq = DEC_SEQ // TQ
    return pl.pallas_call(
        _mla_lat_kernel,
        grid=(DEC_BATCH, nq),
        in_specs=[
            pl.BlockSpec((TQ, MLA_QW), lambda b, t: (N_TOK_P // TQ + b * nq + t, 0)),
            pl.BlockSpec((DEC_SEQ, MLA_QW), lambda b, t: (seq_blk + b, 0)),
            pl.BlockSpec((DEC_SEQ, MLA_VW), lambda b, t: (seq_blk + b, 0)),
            pl.BlockSpec((PAST_LEN, MLA_QW), lambda b, t: (b, 0)),
            pl.BlockSpec((PAST_LEN, MLA_VW), lambda b, t: (b, 0)),
        ],
        out_specs=pl.BlockSpec((TQ, MLA_VW), lambda b, t: (b * nq + t, 0)),
        out_shape=jax.ShapeDtypeStruct((N_TOK_S, MLA_VW), BF16),
        compiler_params=_cp("parallel", "parallel"),
        name="mla_lat",
    )(qp, kp, v, kp_c, v_c)


def _cast_ffn_slab(wu32_ref, wd32_ref, wu_ref, wd_ref):
    wu_ref[...] = wu32_ref[...].astype(BF16)
    wd_ref[...] = wd32_ref[...].astype(BF16)


def _cast_ffn_specs(layer):
    ru, rd = D // NT, D_FF // NT
    cast_in = [pl.BlockSpec((None, ru, 2 * D_FF), lambda i: (layer, i, 0)),
               pl.BlockSpec((None, rd, D), lambda i: (layer, i, 0))]
    cast_out = [pl.BlockSpec((ru, 2 * D_FF), lambda i: (i, 0)), pl.BlockSpec((rd, D), lambda i: (i, 0))]
    cast_shape = [jax.ShapeDtypeStruct((D, 2 * D_FF), BF16), jax.ShapeDtypeStruct((D_FF, D), BF16)]
    return cast_in, cast_out, cast_shape


def _out_even_kernel(*refs, split_x):
    nx = 2 if split_x else 1
    x = _pick(*refs[:2]) if split_x else refs[0][...]
    yrc_ref, yrl_ref, ymc_ref, yml_ref, mod_ref, g_ref, w_ref, wu32_ref, wd32_ref, o_ref, wu_ref, wd_ref = refs[nx:]
    mod = mod_ref[...]
    r = _dot(_pick(yrc_ref, yrl_ref), w_ref[:RET_W, :]) + _dot(_pick(ymc_ref, yml_ref), w_ref[RET_W:, :])
    o_ref[...] = x + mod[:, 2 * D:3 * D] * _rms(r, g_ref[...])
    _cast_ffn_slab(wu32_ref, wd32_ref, wu_ref, wd_ref)


def _out_even(xs, yr_c, yr_l, ym_c, ym_l, mods, gains, layer, j, w, w_up, w_down):
    split_x = len(xs) == 2
    x_specs = [_ctx_row(D), _lat_row(D)] if split_x else [_row(D)]
    cast_in, cast_out, cast_shape = _cast_ffn_specs(layer)
    return pl.pallas_call(
        functools.partial(_out_even_kernel, split_x=split_x),
        grid=(NT,),
        in_specs=x_specs + [_ctx_row(RET_W), _lat_row(RET_W), _ctx_row(MLA_VW), _lat_row(MLA_VW), _mod_spec(layer),
                            _layer(layer, (1, D)), _layer(j, (D, D))] + cast_in,
        out_specs=[_row(D)] + cast_out,
        out_shape=[jax.ShapeDtypeStruct((N_TOK, D), F32)] + cast_shape,
        compiler_params=_cp("parallel"),
        name="out_even",
    )(*xs, yr_c, yr_l, ym_c, ym_l, mods, gains, w, w_up, w_down)


def _gelu_tanh(x):
    return 0.5 * x * (1.0 + jnp.tanh(np.sqrt(2.0 / np.pi).astype(np.float32) * (x + 0.044715 * (x * x * x))))


def _out_odd_kernel(x_ref, yr_ref, u_ref, ync_ref, ynl_ref, mod_ref, g_ref, d_ref, gw_ref, gb_ref, w_ref, wu32_ref,
                    wd32_ref, o_ref, wu_ref, wd_ref):
    mod = mod_ref[...]
    y = _gelu_tanh(yr_ref[...] + d_ref[...] * u_ref[...])
    y = y * _sigmoid(_dot(y.astype(BF16), gw_ref[...]) + gb_ref[...])
    r = _dot(y.astype(BF16), w_ref[:S5_W, :]) + _dot(_pick(ync_ref, ynl_ref), w_ref[S5_W:, :])
    o_ref[...] = x_ref[...] + mod[:, 2 * D:3 * D] * _rms(r, g_ref[...])
    _cast_ffn_slab(wu32_ref, wd32_ref, wu_ref, wd_ref)


def _out_odd(x, y_raw, u, yn_c, yn_l, mods, gains, layer, j, d_skip, glu_w, glu_b, w, w_up, w_down):
    cast_in, cast_out, cast_shape = _cast_ffn_specs(layer)
    return pl.pallas_call(
        _out_odd_kernel,
        grid=(NT,),
        in_specs=[_row(D), _row(S5_W), _row(S5_W), _ctx_row(NA_W), _lat_row(NA_W), _mod_spec(layer),
                  _layer(layer, (1, D)), _layer(j, (1, S5_W)), _layer(j, (S5_W, S5_W)), _layer(j, (1, S5_W)),
                  _layer(j, (D, D))] + cast_in,
        out_specs=[_row(D)] + cast_out,
        out_shape=[jax.ShapeDtypeStruct((N_TOK, D), F32)] + cast_shape,
        compiler_params=_cp("parallel"),
        name="out_odd",
    )(x, y_raw, u, yn_c, yn_l, mods, gains, d_skip, glu_w, glu_b, w, w_up, w_down)


def _ffn_kernel(xm_ref, xp_ref, xn_ref, mod_ref, gpre_ref, gpost_ref, wu_ref, cw_ref, cb_ref, wd_ref, *rest,
                split_out):
    if split_out:
        oc_ref, ol_ref, h_scr, act_scr = rest
    else:
        o_ref, h_scr, act_scr = rest
    i = pl.program_id(0)
    is_lat = i >= NT_P
    t = (i - NT_P) % TILES_PER_DEC
    has_prev = jnp.logical_and(is_lat, t != 0)
    has_next = jnp.logical_and(is_lat, t != TILES_PER_DEC - 1)
    mod = mod_ref[...]
    shift = mod[:, 3 * D:4 * D]
    scale = mod[:, 4 * D:5 * D]
    gate = mod[:, 5 * D:6 * D]
    gpre = gpre_ref[...]
    nm = lambda x: _rms(x, gpre) * (1.0 + scale) + shift
    x = xm_ref[...]
    h = nm(x)
    blk = FF_SUB + FF_EXT
    rows = FF_NSUB * blk
    for k in range(FF_NSUB):
        h_scr[k * blk:k * blk + FF_SUB, :] = h[k * FF_SUB:(k + 1) * FF_SUB].astype(BF16)
        if k + 1 < FF_NSUB:
            after = jnp.where(is_lat, h[(k + 1) * FF_SUB:(k + 1) * FF_SUB + 8], 0.0)
            before = jnp.where(is_lat, h[(k + 1) * FF_SUB - 8:(k + 1) * FF_SUB], 0.0)
        else:
            after = jnp.where(has_next, nm(xn_ref[...]), 0.0)
            before = jnp.where(has_prev, nm(xp_ref[...]), 0.0)
        h_scr[k * blk + FF_SUB:(k + 1) * blk, :] = jnp.concatenate([after, before], axis=0).astype(BF16)
    hb = h_scr[...]

    def up(j):
        ca = slice(j * FF_CHUNK, (j + 1) * FF_CHUNK)
        cg = slice(D_FF + j * FF_CHUNK, D_FF + (j + 1) * FF_CHUNK)
        return (_dot(hb, wu_ref[:, ca]), ca), (_dot(hb, wu_ref[:, cg]), cg)

    def conv(part):
        u, cols = part
        cw = cw_ref[:, cols]
        return (cw[0:1, :] * pltpu.roll(u, 1, axis=0) + cw[1:2, :] * u + cw[2:3, :] * pltpu.roll(u, rows - 1, axis=0)
                + cb_ref[:, cols])

    nxt = up(0)
    for j in range(FF_NCHUNK):
        cur = nxt
        if j + 1 < FF_NCHUNK:
            nxt = up(j + 1)
        act = (_silu(conv(cur[1])) * conv(cur[0])).astype(BF16)
        for k in range(FF_NSUB):
            act_scr[k * FF_SUB:(k + 1) * FF_SUB, j * FF_CHUNK:(j + 1) * FF_CHUNK] = act[k * blk:k * blk + FF_SUB]
    y = _dot(act_scr[...], wd_ref[...])
    out = x + gate * _rms(y, gpost_ref[...])
    if split_out:
        @pl.when(i < NT_P)
        def _():
            oc_ref[...] = out

        @pl.when(i >= NT_P)
        def _():
            ol_ref[...] = out
    else:
        o_ref[...] = out


def _ffn(x, mods, gpre, gpost, layer, wu, cw, cb, wd, split_out):
    hb = TM // 8
    nblk = N_TOK // 8
    prev = pl.BlockSpec((8, D), lambda i: (jnp.maximum(i * hb - 1, 0), 0))
    nxt = pl.BlockSpec((8, D), lambda i: (jnp.minimum((i + 1) * hb, nblk - 1), 0))
    resident = lambda shape: pl.BlockSpec(shape, lambda i: (0, 0), pipeline_mode=pl.Buffered(1))
    if split_out:
        out_specs = [_ctx_row(D), _lat_row(D)]
        out_shape = [jax.ShapeDtypeStruct((N_TOK_P, D), F32), jax.ShapeDtypeStruct((N_TOK_S, D), F32)]
    else:
        out_specs = _row(D)
        out_shape = jax.ShapeDtypeStruct((N_TOK, D), F32)
    return pl.pallas_call(
        functools.partial(_ffn_kernel, split_out=split_out),
        grid=(NT,),
        in_specs=[_row(D), prev, nxt, _mod_spec(layer), _layer(layer, (1, D)), _layer(layer, (1, D)),
                  resident((D, 2 * D_FF)), _layer(layer, (3, 2 * D_FF)), _layer(layer, (1, 2 * D_FF)),
                  resident((D_FF, D))],
        out_specs=out_specs,
        out_shape=out_shape,
        scratch_shapes=[pltpu.VMEM((FF_NSUB * (FF_SUB + FF_EXT), D), BF16), pltpu.VMEM((TM, D_FF), BF16)],
        compiler_params=_cp("arbitrary"),
        name="ffn",
    )(x, x, x, mods, gpre, gpost, wu, cw, cb, wd)


def _in_odd_kernel(x_ref, mod_ref, g_ref, w_ref, *rest, n_prev):
    prev_refs, (u_ref, qkv_ref, kc_ref, vc_ref) = rest[:2 * min(n_prev, 1)], rest[2 * min(n_prev, 1):]
    mod = mod_ref[...]
    h = _rms(x_ref[...], g_ref[...]) * (1.0 + mod[:, D:2 * D]) + mod[:, :D]
    r = _dot(h.astype(BF16), w_ref[...])
    u_ref[...] = r[:, :S5_W]
    qkv_ref[:, :NA_W] = (r[:, S5_W:S5_W + NA_W] * (NA_SCALE * LOG2E)).astype(BF16)
    qkv_ref[:, NA_W:] = r[:, S5_W + NA_W:].astype(BF16)

    @pl.when(pl.program_id(0) < NT_P)
    def _():
        for part, (out_ref, col0) in enumerate([(kc_ref, S5_W + NA_W), (vc_ref, S5_W + 2 * NA_W)]):
            for b in range(TM // SEQ):
                for jp in range(n_prev):
                    out_ref[b, jp] = prev_refs[part][b, jp]
                out_ref[b, n_prev] = r[b * SEQ:(b + 1) * SEQ, col0:col0 + NA_W]


def _in_odd(x, mods, gains, layer, j, w, prev_kv=None):
    n_prev = 0 if prev_kv is None else prev_kv[0].shape[1]
    nb = TM // SEQ
    stacked = lambda n: pl.BlockSpec((nb, n, SEQ, NA_W), lambda i: (jnp.minimum(i, NT_P - 1), 0, 0, 0))
    leaf = jax.ShapeDtypeStruct((BATCH, n_prev + 1, SEQ, NA_W), F32)
    prev_specs = [stacked(n_prev)] * 2 if n_prev else []
    return pl.pallas_call(
        functools.partial(_in_odd_kernel, n_prev=n_prev),
        grid=(NT,),
        in_specs=[_row(D), _mod_spec(layer), _layer(layer, (1, D)), _layer(j, w.shape[1:])] + prev_specs,
        out_specs=[_row(S5_W), _row(3 * NA_W), stacked(n_prev + 1), stacked(n_prev + 1)],
        out_shape=[jax.ShapeDtypeStruct((N_TOK, S5_W), F32), jax.ShapeDtypeStruct((N_TOK, 3 * NA_W), BF16), leaf, leaf],
        compiler_params=_cp("arbitrary"),
        name="in_odd",
    )(x, mods, gains, w, *(prev_kv or ()))


S5_TS = S5_CHUNK * S5_GROUP
S5_PL = 2 * S5_P


S5_PREP_PAIRS = 2


def _s5_prep_kernel(*refs):
    ins, outs = refs[:7], refs[7:]
    for q in range(S5_PREP_PAIRS):
        _s5_prep_pair(*[r.at[:, q] for r in ins], *[r.at[q] for r in outs])


def _s5_prep_pair(lre_ref, lim_ref, ls_ref, btr_ref, bti_ref, cr_ref, ci_ref, m_ref, n_ref, p_ref, a_ref, ct_scr):
    T = S5_CHUNK
    S = S5_GROUP
    hi = lax.Precision.HIGHEST
    low = lax.broadcasted_iota(jnp.int32, (S, S5_PL), 1) < S5_P
    half = [low, jnp.logical_not(low)]
    pick = lambda e, v: jnp.where(half[e], v, 0.0)
    nt = (((1,), (1,)), ((), ()))
    kps = [[None, None], [None, None]]
    for d in range(2):
        lre = lre_ref[d]
        lim = lim_ref[d]
        step = jnp.exp(ls_ref[d])
        mag = jnp.exp(lre * step)
        are = mag * jnp.cos(lim * step)
        aim = mag * jnp.sin(lim * step)
        den = lre * lre + lim * lim
        zr, zi = _cmul(are - 1.0, aim, lre / den, -lim / den)
        bbr, bbi = _cmul(zr, zi, btr_ref[d], bti_ref[d])
        cr = cr_ref[d]
        ci = ci_ref[d]
        pr = jnp.ones_like(are)
        pi = jnp.zeros_like(are)
        for k in range(T + 1):
            er, ei = _cmul(cr, ci, pr, pi)
            if k < T:
                jn = T - 1 - k if d == 0 else k
                wr, wi = _cmul(pr, pi, bbr, bbi)
                for e in range(2):
                    rows = slice(e * S5_TS + jn * S, e * S5_TS + (jn + 1) * S)
                    n_ref[d, rows, 0:S5_PL] = pick(e, wr).astype(BF16)
                    n_ref[d, rows, S5_PL:2 * S5_PL] = pick(e, wi).astype(BF16)
                jc = k if d == 0 else T - 1 - k
                ct_scr[0, jc * S:(jc + 1) * S, :] = er
                ct_scr[1, jc * S:(jc + 1) * S, :] = ei
            if k >= 1:
                t = k - 1 if d == 0 else T - k
                for e in range(2):
                    rows = slice(e * S5_TS + t * S, e * S5_TS + (t + 1) * S)
                    p_ref[rows, 2 * d * S5_PL:(2 * d + 1) * S5_PL] = pick(e, er).astype(BF16)
                    p_ref[rows, (2 * d + 1) * S5_PL:(2 * d + 2) * S5_PL] = pick(e, -ei).astype(BF16)
            if k == T:
                a_ref[d, 0] = pr
                a_ref[d, 1] = pi
            pr, pi = _cmul(pr, pi, are, aim)
        for e in range(2):
            kd = (lax.dot_general(pick(e, bbr), ct_scr[0], nt, precision=hi, preferred_element_type=F32)
                  - lax.dot_general(pick(e, bbi), ct_scr[1], nt, precision=hi, preferred_element_type=F32))
            kps[d][e] = jnp.concatenate([kd, jnp.zeros_like(kd)], axis=1)
    for e in range(2):
        for t in range(T):
            fwd = pltpu.roll(kps[0][e], t * S, axis=1)[:, :S5_TS]
            bwd = pltpu.roll(kps[1][e], (2 * S5_TS - (T - 1 - t) * S) % (2 * S5_TS), axis=1)[:, :S5_TS]
            m_ref[e, t * S:(t + 1) * S, :] = (fwd + bwd).astype(BF16)


def _s5_prep(lre, lim, ls, b_re, b_im, c_re, c_im):
    NP, S, P = S5_PAIRS, S5_GROUP, S5_P
    vec = lambda a: a.reshape(2, NP, 1, S5_PL)
    b_lay = lambda a: jnp.transpose(a.reshape(2, NP, 2, P, S), (0, 1, 4, 2, 3)).reshape(2, NP, S, S5_PL)
    c_lay = lambda a: jnp.transpose(a.reshape(2, NP, 2, S, P), (0, 1, 3, 2, 4)).reshape(2, NP, S, S5_PL)
    args = (vec(lre), vec(lim), vec(jnp.repeat(ls, P, axis=-1)), b_lay(b_re), b_lay(b_im), c_lay(c_re), c_lay(c_im))
    PB = S5_PREP_PAIRS
    vspec = pl.BlockSpec((2, PB, 1, S5_PL), lambda g: (0, g, 0, 0))
    mspec = pl.BlockSpec((2, PB, S, S5_PL), lambda g: (0, g, 0, 0))
    lead = lambda shape: pl.BlockSpec((PB,) + shape, lambda g: (g,) + (0,) * len(shape))
    return pl.pallas_call(
        _s5_prep_kernel,
        grid=(NP // PB,),
        in_specs=[vspec, vspec, vspec, mspec, mspec, mspec, mspec],
        out_specs=[lead((2, S5_TS, S5_TS)), lead((2, 2 * S5_TS, 2 * S5_PL)), lead((2 * S5_TS, 4 * S5_PL)),
                   lead((2, 2, 1, S5_PL))],
        out_shape=[jax.ShapeDtypeStruct((NP, 2, S5_TS, S5_TS), BF16),
                   jax.ShapeDtypeStruct((NP, 2, 2 * S5_TS, 2 * S5_PL), BF16),
                   jax.ShapeDtypeStruct((NP, 2 * S5_TS, 4 * S5_PL), BF16),
                   jax.ShapeDtypeStruct((NP, 2, 2, 1, S5_PL), F32)],
        scratch_shapes=[pltpu.VMEM((S5_PREP_PAIRS, 2, S5_TS, S5_PL), F32)],
        compiler_params=_cp("parallel"),
        name="s5_prep",
    )(*args)


S5_NCH = N_TOK // S5_CHUNK
S5_ROWS_P = N_TOK_P // S5_CHUNK
S5_NC_P = SEQ // S5_CHUNK
S5_NC_S = DEC_SEQ // S5_CHUNK
S5_GPB = 8
S5_PPB = S5_GPB // 2
S5_XL = 8 * 128


def _s5_perm():
    r = np.arange(S5_XL)
    dst = (r // S5_GROUP % S5_GPB) * 128 + (r // 128) * S5_GROUP + r % S5_GROUP
    perm = np.zeros((S5_XL, S5_XL), np.float32)
    perm[r, dst] = 1.0
    return jnp.asarray(perm, BF16)


def _s5_kernel(u_ref, perm_ref, m_ref, n_ref, p_ref, a_ref, h0c_ref, h0l_ref, y_ref, fin_ref, z_scr, up_scr, e_scr,
               hin_scr, yc_scr):
    T = S5_CHUNK
    W = S5_PL
    for t in range(T):
        z_scr[t // 8, :, (t % 8) * 128:(t % 8 + 1) * 128] = u_ref[pl.ds(t, S5_NCH, stride=T), :].astype(BF16)
    perm = perm_ref[...]
    for j in range(2):
        up_scr[j] = _dot(z_scr[j], perm).astype(BF16)

    def scan(pp, d, h0_ref, n_chunks, n_seq, row0):
        are = a_ref[pp, d, 0]
        aim = a_ref[pp, d, 1]
        hr = h0_ref[pp, d, 0]
        hi = h0_ref[pp, d, 1]
        order = range(n_chunks) if d == 0 else reversed(range(n_chunks))
        for c in order:
            rows = pl.ds(row0 + c, n_seq, stride=n_chunks)
            hin_scr[2 * d, rows, :] = hr
            hin_scr[2 * d + 1, rows, :] = hi
            er = e_scr[2 * d, rows, :]
            ei = e_scr[2 * d + 1, rows, :]
            hr, hi = are * hr - aim * hi + er, are * hi + aim * hr + ei
        return hr, hi

    for pp in range(S5_PPB):
        us = []
        for e in range(2):
            sl = slice((2 * pp + e) * 128, (2 * pp + e + 1) * 128)
            us.append(jnp.concatenate([up_scr[0, :, sl], up_scr[1, :, sl]], axis=1))
        u2 = jnp.concatenate(us, axis=1)
        for d in range(2):
            ed = _dot(u2, n_ref[pp, d])
            e_scr[2 * d] = ed[:, :W]
            e_scr[2 * d + 1] = ed[:, W:]
        for d in range(2):
            hr, hi = scan(pp, d, h0c_ref, S5_NC_P, BATCH, 0)
            fin_ref[pp, d, 0] = hr
            fin_ref[pp, d, 1] = hi
            scan(pp, d, h0l_ref, S5_NC_S, DEC_BATCH, S5_ROWS_P)
        hin = jnp.concatenate([hin_scr[k] for k in range(4)], axis=1).astype(BF16)
        for e in range(2):
            y = _dot(us[e], m_ref[pp, e]) + _dot_nt(hin, p_ref[pp, e * S5_TS:(e + 1) * S5_TS, :])
            sl = slice((2 * pp + e) * 128, (2 * pp + e + 1) * 128)
            for j in range(2):
                yc_scr[j, :, sl] = y[:, j * 128:(j + 1) * 128]

    for j in range(2):
        yp = yc_scr[j]
        y_hi = yp.astype(BF16)
        y_lo = (yp - y_hi.astype(F32)).astype(BF16)
        r = _dot_nt(y_hi, perm) + _dot_nt(y_lo, perm)
        for k in range(8):
            y_ref[pl.ds(8 * j + k, S5_NCH, stride=T), :] = r[:, k * 128:(k + 1) * 128]


def _s5_scan(u, perm, m, n2, p2, a, h0c, h0l):
    nb = S5_GROUPS // S5_GPB
    lead = lambda shape: pl.BlockSpec((S5_PPB,) + shape, lambda w: (w,) + (0,) * len(shape))
    col = pl.BlockSpec((N_TOK, 128), lambda w: (0, w))
    return pl.pallas_call(
        _s5_kernel,
        grid=(nb,),
        in_specs=[col, _full(perm.shape), lead((2, S5_TS, S5_TS)), lead((2, 2 * S5_TS, 2 * S5_PL)),
                  lead((2 * S5_TS, 4 * S5_PL)), lead((2, 2, 1, S5_PL)), lead((2, 2, BATCH, S5_PL)),
                  lead((2, 2, DEC_BATCH, S5_PL))],
        out_specs=[col, lead((2, 2, BATCH, S5_PL))],
        out_shape=[jax.ShapeDtypeStruct((N_TOK, S5_W), F32),
                   jax.ShapeDtypeStruct((S5_PAIRS, 2, 2, BATCH, S5_PL), F32)],
        scratch_shapes=[pltpu.VMEM((2, S5_NCH, S5_XL), BF16), pltpu.VMEM((2, S5_NCH, S5_XL), BF16),
                        pltpu.VMEM((4, S5_NCH, S5_PL), F32), pltpu.VMEM((4, S5_NCH, S5_PL), F32),
                        pltpu.VMEM((2, S5_NCH, S5_XL), F32)],
        compiler_params=_cp("parallel"),
        name="s5_scan",
    )(u, perm, m, n2, p2, a, h0c, h0l)


def _na_heads(q_ref, keys, values, y_ref, rows, bias=None):
    low = _low_half((rows, 128))
    pair = lambda hh: slice(hh // 2 * 128, (hh // 2 + 1) * 128)

    def scores(hh):
        q = q_ref[:, pair(hh)]
        qm = jnp.where(low == (hh % 2 == 0), q, jnp.zeros_like(q))
        s = [_dot_nt(qm, k(pair(hh))) for k in keys]
        if bias is not None:
            s[0] = s[0] + bias(hh)
        return s

    attend = lambda hh, s: _softmax_pv(s, [v(pair(hh)) for v in values])
    _heads_pipelined(NA_HEADS, scores, attend, y_ref, rows)


def _na_ctx_kernel(q_ref, k_ref, v_ref, y_ref):
    _na_heads(q_ref, [lambda sl: k_ref[:, sl]], [lambda sl: v_ref[:, sl]], y_ref, SEQ)


def _na_ctx(qkv):
    col = lambda part: pl.BlockSpec((SEQ, NA_W), lambda b: (b, part))
    return pl.pallas_call(
        _na_ctx_kernel,
        grid=(BATCH,),
        in_specs=[col(0), col(1), col(2)],
        out_specs=pl.BlockSpec((SEQ, NA_W), lambda b: (b, 0)),
        out_shape=jax.ShapeDtypeStruct((N_TOK_P, NA_W), BF16),
        compiler_params=_cp("parallel"),
        name="na_ctx",
    )(qkv, qkv, qkv)


def _na_key_row0(rb):
    return jnp.clip(NA_QROWS * rb - NA_WIN_R // 2, 0, GRID_H - NA_KROWS)


def _na_lat_kernel(q_ref, ks_ref, vs_ref, kc_ref, vc_ref, tab_ref, y_ref):
    rb = pl.program_id(1)
    u0 = _na_key_row0(rb)
    start = pl.multiple_of(u0 * GRID_W, GRID_W)
    nk = NA_KROWS * GRID_W
    low_t = _low_half((GRID_W, 128))

    def table_row(i, w):
        qr = NA_QROWS * rb + i
        kr = u0 + w
        rs = jnp.clip(qr - NA_WIN_R // 2, 0, GRID_H - NA_WIN_R)
        inside = jnp.logical_and(kr >= rs, kr < rs + NA_WIN_R)
        return jnp.where(inside, kr - qr + NA_WIN_R - 1, NA_NDR)

    idx = [[table_row(i, w) for w in range(NA_KROWS)] for i in range(NA_QROWS)]

    def bias(h):
        rows = [jnp.concatenate([jnp.where(low_t, tab_ref[h, idx[i][w]], tab_ref[h, idx[i][w + 1]])
                                 for w in range(0, NA_KROWS, 2)], axis=1) for i in range(NA_QROWS)]
        return jnp.concatenate(rows, axis=0)

    keys = [lambda sl: ks_ref[pl.ds(start, nk), sl], lambda sl: kc_ref[:, sl].astype(BF16)]
    values = [lambda sl: vs_ref[pl.ds(start, nk), sl], lambda sl: vc_ref[:, sl].astype(BF16)]
    _na_heads(q_ref, keys, values, y_ref, TQ, bias)


def _na_lat(qkv, k_c, v_c, table):
    seq_blk = N_TOK_P // DEC_SEQ
    nb = GRID_H // NA_QROWS
    return pl.pallas_call(
        _na_lat_kernel,
        grid=(DEC_BATCH, nb),
        in_specs=[
            pl.BlockSpec((TQ, NA_W), lambda b, r: (N_TOK_P // TQ + b * nb + r, 0)),
            pl.BlockSpec((DEC_SEQ, NA_W), lambda b, r: (seq_blk + b, 1)),
            pl.BlockSpec((DEC_SEQ, NA_W), lambda b, r: (seq_blk + b, 2)),
            pl.BlockSpec((PAST_LEN, NA_W), lambda b, r: (b, 0)),
            pl.BlockSpec((PAST_LEN, NA_W), lambda b, r: (b, 0)),
            pl.BlockSpec(table.shape, lambda b, r: (0, 0, 0, 0)),
        ],
        out_specs=pl.BlockSpec((TQ, NA_W), lambda b, r: (b * nb + r, 0)),
        out_shape=jax.ShapeDtypeStruct((N_TOK_S, NA_W), BF16),
        compiler_params=_cp("parallel", "arbitrary"),
        name="na_lat",
    )(qkv, qkv, qkv, k_c, v_c, table)


def _na_table_kernel(rpb_ref, t_ref):
    qc = lax.broadcasted_iota(jnp.int32, (GRID_W, 128), 0)
    kc = lax.broadcasted_iota(jnp.int32, (GRID_W, 128), 1) % GRID_W
    cs = jnp.clip(qc - NA_WIN_C // 2, 0, GRID_W - NA_WIN_C)
    in_band = jnp.logical_and(kc >= cs, kc < cs + NA_WIN_C)
    neg = jnp.full((GRID_W, 128), -jnp.inf, F32)

    def body(n, carry):
        x = jnp.broadcast_to(rpb_ref[n], (GRID_W, 128))
        t = pltpu.roll(x, 128 - (NA_WIN_C - 1), axis=1, stride=1, stride_axis=0)
        t_ref[n // NA_NDR, n % NA_NDR] = jnp.where(in_band, t * LOG2E, neg)
        return carry

    lax.fori_loop(0, NA_HEADS * NA_NDR, body, 0, unroll=8)
    for h in range(NA_HEADS):
        t_ref[h, NA_NDR] = neg


def _na_table(rpb):
    rows = jnp.pad(rpb.reshape(NA_HEADS * NA_NDR, 1, NA_NDC), ((0, 0), (0, 0), (0, GRID_W - NA_NDC)))
    rows = jnp.concatenate([rows, rows], axis=-1)
    return pl.pallas_call(
        _na_table_kernel,
        out_shape=jax.ShapeDtypeStruct((NA_HEADS, NA_NDR + 1, GRID_W, 128), F32),
        name="na_table",
    )(rows)


def _rope_tables():
    n_freq = MLA_ROPE // 4
    inv = ROPE_BASE ** (-jnp.arange(n_freq, dtype=F32) / n_freq)
    t = jnp.arange(DEC_SEQ)
    row = (t // GRID_W).astype(F32)
    colp = (t % GRID_W).astype(F32)
    ang = jnp.concatenate([row[:, None] * inv, colp[:, None] * inv], axis=-1)
    cos, sin = jnp.cos(ang), jnp.sin(ang)
    one = jnp.ones((DEC_SEQ, MLA_NOPE), F32)
    zero = jnp.zeros((DEC_SEQ, MLA_NOPE), F32)
    cos_s = jnp.concatenate([one, cos, cos, one[:, :32]], axis=-1)
    sin_s = jnp.concatenate([zero, -sin, sin, zero[:, :32]], axis=-1)
    cos_t = jnp.concatenate([jnp.ones((TM, HEAD_SLAB), F32), cos_s], axis=0)
    sin_t = jnp.concatenate([jnp.zeros((TM, HEAD_SLAB), F32), sin_s], axis=0)
    return cos_t, sin_t


def _mla_weights(w_in, w_uq, w_ukv):
    half = MLA_ROPE // 2
    wkr = w_in[:, 4 * RET_W + MLA_Q_RANK + MLA_KV_RANK:]
    z64 = jnp.zeros((D, MLA_NOPE), F32)
    z32 = jnp.zeros((D, HEAD_SLAB - MLA_NOPE - MLA_ROPE), F32)
    wkr2 = jnp.concatenate([z64, wkr, z32, z64, wkr[:, half:], wkr[:, :half], z32], axis=1).astype(BF16)
    wq = w_uq.reshape(MLA_Q_RANK, MLA_HEADS, MLA_NOPE + MLA_ROPE)
    nope, rope = wq[..., :MLA_NOPE], wq[..., MLA_NOPE:]
    zq64 = jnp.zeros_like(nope)
    zq32 = jnp.zeros_like(rope)
    q_slab = jnp.concatenate([nope, rope, zq32], axis=-1).reshape(MLA_Q_RANK, MLA_QW)
    q_sw = jnp.concatenate([zq64, rope[..., half:], rope[..., :half], zq32], axis=-1).reshape(MLA_Q_RANK, MLA_QW)
    wuq2 = jnp.concatenate([q_slab, q_sw], axis=1).astype(BF16)
    wkv = w_ukv.reshape(MLA_KV_RANK, MLA_HEADS, MLA_NOPE + MLA_V)
    wk = jnp.concatenate([wkv[..., :MLA_NOPE], jnp.zeros_like(wkv[..., :MLA_NOPE])], axis=-1)
    wk = wk.reshape(MLA_KV_RANK, MLA_QW).astype(BF16)
    wv = wkv[..., MLA_NOPE:].reshape(MLA_KV_RANK, MLA_VW).astype(BF16)
    return wkr2, wuq2, wk, wv


def kernel(x_prompt, x_sample, c, state_ret, cache_mla_ckv, cache_mla_krope, state_s5_re, state_s5_im, cache_na_k, cache_na_v, c_ctx, ada_w, ada_b, mix_pre_g, mix_post_g, ffn_pre_g, ffn_post_g, ffn_w_up, ffn_conv_w, ffn_conv_b, ffn_w_down, even_w_in, even_w_out, ret_logit, ret_gn, mla_q_norm, mla_w_uq, mla_kv_norm, mla_w_ukv, odd_w_in, odd_w_out, s5_lambda_re, s5_lambda_im, s5_log_step, s5_b_re, s5_b_im, s5_c_re, s5_c_im, s5_d, s5_glu_w, s5_glu_b, na_rpb):
    cvec = jnp.concatenate([c_ctx[None, :], c, jnp.zeros((8 - 1 - DEC_BATCH, D), F32)], axis=0)
    mods = _ada_mods(cvec, ada_w, ada_b)
    row3 = lambda a: a.reshape(a.shape[0], 1, a.shape[1])
    mix_pre, mix_post, ffn_pre, ffn_post = row3(mix_pre_g), row3(mix_post_g), row3(ffn_pre_g), row3(ffn_post_g)
    conv_b = row3(ffn_conv_b)
    e_in, e_out = even_w_in.astype(BF16), even_w_out.astype(BF16)
    o_in, o_out, glu_w = odd_w_in.astype(BF16), odd_w_out.astype(BF16), s5_glu_w.astype(BF16)
    cos_t, sin_t = _rope_tables()
    perm = _s5_perm()
    xs = (x_prompt.reshape(N_TOK_P, D), x_sample.reshape(N_TOK_S, D))
    ret_states, na_kv = None, None
    new_ckv, new_kr, new_s5_re, new_s5_im = [], [], [], []
    for layer in range(DEPTH):
        j = layer // 2
        if layer % 2 == 0:
            wkr2, wuq2, wk, wv = _mla_weights(even_w_in[j], mla_w_uq[j], mla_w_ukv[j])
            qkvg, qp, kp, v, ckvn, kr = _in_even(xs, mods, mix_pre, layer, j, e_in, wkr2, row3(mla_q_norm),
                                                 row3(mla_kv_norm), wuq2, wk, wv, cos_t, sin_t)
            logit = jnp.transpose(ret_logit[j]).reshape(RET_HEADS, 2, 1, 1)
            gn = row3(ret_gn)
            s0 = jnp.zeros((BATCH, 2, RET_HEADS, RET_DIM, RET_DIM), F32)
            yr_c, ret_states = _retention(qkvg, logit, gn, j, s0, seq_len=SEQ, n_seq=BATCH, n_blk=4, row0=0,
                                          emit_state=True, prev_states=ret_states)
            (yr_l,) = _retention(qkvg, logit, gn, j, state_ret[:, j], seq_len=DEC_SEQ, n_seq=DEC_BATCH, n_blk=1,
                                 row0=N_TOK_P, emit_state=False)
            ym_c = _mla_ctx(qp, kp, v)
            kr_c = jnp.pad(cache_mla_krope[:, j].reshape(DEC_BATCH * PAST_LEN, MLA_ROPE),
                           ((0, 0), (MLA_NOPE, HEAD_SLAB - MLA_NOPE - MLA_ROPE)))
            kp_c, v_c = _mla_cache(cache_mla_ckv[:, j].reshape(DEC_BATCH * PAST_LEN, MLA_KV_RANK), kr_c, wk, wv)
            ym_l = _mla_lat(qp, kp, v, kp_c, v_c)
            x, w_up, w_down = _out_even(xs, yr_c, yr_l, ym_c, ym_l, mods, mix_post, layer, j, e_out, ffn_w_up,
                                        ffn_w_down)
            new_ckv.append(ckvn.reshape(BATCH, SEQ, MLA_KV_RANK))
            new_kr.append(kr[:, MLA_NOPE:MLA_NOPE + MLA_ROPE].reshape(BATCH, SEQ, MLA_ROPE))
        else:
            u, qkv, *na_kv = _in_odd(xs[0], mods, mix_pre, layer, j, o_in, prev_kv=na_kv)
            m, n2, p2, a = _s5_prep(s5_lambda_re[j], s5_lambda_im[j], s5_log_step[j], s5_b_re[j], s5_b_im[j],
                                    s5_c_re[j], s5_c_im[j])
            h0c = jnp.zeros((S5_PAIRS, 2, 2, BATCH, S5_PL), F32)
            h0 = jnp.stack([state_s5_re[:, j], state_s5_im[:, j]], axis=0)
            h0l = jnp.transpose(h0.reshape(2, DEC_BATCH, 2, S5_PAIRS, S5_PL), (3, 2, 0, 1, 4))
            y_raw, fin = _s5_scan(u, perm, m, n2, p2, a, h0c, h0l)
            yn_c = _na_ctx(qkv)
            yn_l = _na_lat(qkv, cache_na_k[:, j].reshape(DEC_BATCH * PAST_LEN, NA_W),
                           cache_na_v[:, j].reshape(DEC_BATCH * PAST_LEN, NA_W), _na_table(na_rpb[j]))
            x, w_up, w_down = _out_odd(xs[0], y_raw, u, yn_c, yn_l, mods, mix_post, layer, j, row3(s5_d), glu_w,
                                       row3(s5_glu_b), o_out, ffn_w_up, ffn_w_down)
            st = jnp.transpose(fin.reshape(S5_PAIRS, 2, 2, BATCH, 2, S5_P), (2, 3, 1, 0, 4, 5))
            st = st.reshape(2, BATCH, 2, S5_GROUPS, S5_P)
            new_s5_re.append(st[0])
            new_s5_im.append(st[1])
        last = layer == DEPTH - 1
        out = _ffn(x, mods, ffn_pre, ffn_post, layer, w_up, ffn_conv_w, conv_b, w_down, split_out=last)
        xs = tuple(out) if last else (out,)
    stack = lambda a: jnp.stack(a, axis=1)
    heads = lambda a: a.reshape(BATCH, a.shape[1], SEQ, NA_HEADS, NA_DIM)
    return (xs[0].reshape(BATCH, SEQ, D), xs[1].reshape(DEC_BATCH, DEC_SEQ, D), ret_states, stack(new_ckv),
            stack(new_kr), stack(new_s5_re), stack(new_s5_im), heads(na_kv[0]), heads(na_kv[1]))
```

```python
import functools

import numpy as np
import jax
import jax.numpy as jnp
from jax import lax
from jax.experimental import pallas as pl
from jax.experimental.pallas import tpu as pltpu

F32 = jnp.float32
BF16 = jnp.bfloat16

D = 1024
BATCH = 16
SEQ = 256
DEPTH = 4
DEC_BATCH = 2
DEC_SEQ = 2048
PAST_LEN = 512
GRID_W = 64
GRID_H = DEC_SEQ // GRID_W
EPS = 1e-6
LOG2E = 1.4426950408889634

RET_HEADS = 4
RET_W = 512
RET_DIM = 128
RET_CHUNK = 256

MLA_HEADS = 8
MLA_NOPE = 64
MLA_ROPE = 32
MLA_V = 64
MLA_Q_RANK = 256
MLA_KV_RANK = 128
MLA_SCALE = (MLA_NOPE + MLA_ROPE) ** -0.5
ROPE_BASE = 10000.0
HEAD_SLAB = 128
MLA_QW = MLA_HEADS * HEAD_SLAB
MLA_VW = MLA_HEADS * MLA_V

S5_W = 512
S5_GROUP = 16
S5_GROUPS = 32
S5_P = 64
S5_CHUNK = 16
S5_PAIRS = S5_GROUPS // 2

NA_HEADS = 8
NA_W = 512
NA_DIM = 64
NA_WIN_R = 8
NA_WIN_C = 16
NA_SCALE = NA_DIM ** -0.5
NA_QROWS = 4
NA_KROWS = 12
NA_NDR = 2 * NA_WIN_R - 1
NA_NDC = 2 * NA_WIN_C - 1

D_FF = 2816
FF_CHUNK = 256
FF_NCHUNK = D_FF // FF_CHUNK
FF_EXT = 16

TM = 512
TQ = 256
FF_SUB = SEQ
FF_NSUB = TM // FF_SUB
HALO = 16
N_TOK_P = BATCH * SEQ
N_TOK_S = DEC_BATCH * DEC_SEQ
N_TOK = N_TOK_P + N_TOK_S
NT_P = N_TOK_P // TM
NT_S = N_TOK_S // TM
NT = NT_P + NT_S
TILES_PER_DEC = DEC_SEQ // TM

VMEM_LIMIT = 56 * 1024 * 1024


def _cp(*sem):
    return pltpu.CompilerParams(dimension_semantics=sem, vmem_limit_bytes=VMEM_LIMIT)


def _dot(a, b):
    return jnp.dot(a, b, preferred_element_type=F32)


def _dot_nt(a, b):
    return lax.dot_general(a, b, (((1,), (1,)), ((), ())), preferred_element_type=F32)


def _dot_tn(a, b):
    return lax.dot_general(a, b, (((0,), (0,)), ((), ())), preferred_element_type=F32)


def _rms(x, g):
    return x * lax.rsqrt(jnp.mean(x * x, axis=-1, keepdims=True) + EPS) * g


def _sigmoid(x):
    return 1.0 / (1.0 + jnp.exp(-x))


def _silu(x):
    return x * _sigmoid(x)


def _cmul(ar, ai, br, bi):
    return ar * br - ai * bi, ar * bi + ai * br


def _mrow(i):
    return jnp.where(i < NT_P, 0, 1 + (i - NT_P) // TILES_PER_DEC)


def _full(shape):
    n = len(shape)
    return pl.BlockSpec(shape, lambda *_: (0,) * n)


def _layer(layer, shape):
    n = len(shape)
    return pl.BlockSpec((None,) + shape, lambda *_: (layer,) + (0,) * n)


def _mod_spec(layer):
    return pl.BlockSpec((None, None, 1, 6 * D), lambda i: (layer, _mrow(i), 0, 0))


def _row(width):
    return pl.BlockSpec((TM, width), lambda i: (i, 0))


def _ctx_row(width):
    return pl.BlockSpec((TM, width), lambda i: (jnp.minimum(i, NT_P - 1), 0))


def _lat_row(width):
    return pl.BlockSpec((TM, width), lambda i: (jnp.maximum(i - NT_P, 0), 0))


def _pick(a_ref, b_ref):
    return jnp.where(pl.program_id(0) < NT_P, a_ref[...], b_ref[...])


def _ada_kernel(c_ref, w_ref, b_ref, o_ref):
    o_ref[...] = _dot(_silu(c_ref[...]).astype(BF16), w_ref[...].astype(BF16)) + b_ref[...]


def _ada_mods(cvec, ada_w, ada_b):
    nb = 4
    bn = 6 * D // nb
    out = pl.pallas_call(
        _ada_kernel,
        grid=(DEPTH, nb),
        in_specs=[
            pl.BlockSpec((8, D), lambda l, n: (0, 0)),
            pl.BlockSpec((None, D, bn), lambda l, n: (l, 0, n)),
            pl.BlockSpec((None, 1, bn), lambda l, n: (l, 0, n)),
        ],
        out_specs=pl.BlockSpec((None, 8, bn), lambda l, n: (l, 0, n)),
        out_shape=jax.ShapeDtypeStruct((DEPTH, 8, 6 * D), F32),
        compiler_params=_cp("arbitrary", "arbitrary"),
        name="ada_mods",
    )(cvec, ada_w, ada_b.reshape(DEPTH, 1, 6 * D))
    return out[:, :3].reshape(DEPTH, 3, 1, 6 * D)


def _cast_ffn_slab(wu32_ref, wd32_ref, wu_ref, wd_ref):
    wu_ref[...] = wu32_ref[...].astype(BF16)
    wd_ref[...] = wd32_ref[...].astype(BF16)


def _cast_ffn_specs(layer):
    ru, rd = D // NT, D_FF // NT
    cast_in = [pl.BlockSpec((None, ru, 2 * D_FF), lambda i: (layer, i, 0)),
               pl.BlockSpec((None, rd, D), lambda i: (layer, i, 0))]
    cast_out = [pl.BlockSpec((ru, 2 * D_FF), lambda i: (i, 0)), pl.BlockSpec((rd, D), lambda i: (i, 0))]
    cast_shape = [jax.ShapeDtypeStruct((D, 2 * D_FF), BF16), jax.ShapeDtypeStruct((D_FF, D), BF16)]
    return cast_in, cast_out, cast_shape


def _in_even_kernel(xc_ref, xl_ref, mod_ref, g_ref, w_ref, wkr_ref, qn_ref, kvn_ref, wuq_ref, wk_ref, wv_ref, cos_ref,
                    sin_ref, wu32_ref, wd32_ref, qkvg_ref, qp_ref, kp_ref, v_ref, ckv_ref, kr_ref, wu_ref, wd_ref):
    _cast_ffn_slab(wu32_ref, wd32_ref, wu_ref, wd_ref)
    x = _pick(xc_ref, xl_ref)
    mod = mod_ref[...]
    h = _rms(x, g_ref[...]) * (1.0 + mod[:, D:2 * D]) + mod[:, :D]
    hb = h.astype(BF16)
    o = 4 * RET_W
    for part in range(4):
        cols = slice(part * RET_W, (part + 1) * RET_W)
        qkvg_ref[:, cols] = _dot(hb, w_ref[:, cols]).astype(BF16)
    r = _dot(hb, w_ref[:, o:o + MLA_Q_RANK + MLA_KV_RANK])
    cq = r[:, :MLA_Q_RANK]
    ckv_raw = r[:, MLA_Q_RANK:]
    r2 = _dot(hb, wkr_ref[...])
    kr = r2[:, :HEAD_SLAB]
    krs = r2[:, HEAD_SLAB:]
    cosf = cos_ref[...]
    sinf = sin_ref[...]
    q2 = _dot(_rms(cq, qn_ref[...]).astype(BF16), wuq_ref[...])
    ckvn = _rms(ckv_raw, kvn_ref[...])

    @pl.when(pl.program_id(0) < NT_P)
    def _():
        ckv_ref[...] = ckvn
        kr_ref[...] = kr

    cb = ckvn.astype(BF16)
    kp = _dot(cb, wk_ref[...])
    v_ref[...] = _dot(cb, wv_ref[...]).astype(BF16)
    krr = kr * cosf + krs * sinf
    for hh in range(MLA_HEADS):
        sl = slice(hh * HEAD_SLAB, (hh + 1) * HEAD_SLAB)
        ss = slice(MLA_QW + hh * HEAD_SLAB, MLA_QW + (hh + 1) * HEAD_SLAB)
        qp_ref[:, sl] = ((q2[:, sl] * cosf + q2[:, ss] * sinf) * (MLA_SCALE * LOG2E)).astype(BF16)
        kp_ref[:, sl] = (kp[:, sl] + krr).astype(BF16)


def _in_even(xs, mods, gains, layer, j, w_in, wkr2, qn, kvn, wuq2, wk, wv, cos_t, sin_t, w_up, w_down):
    cast_in, cast_out, cast_shape = _cast_ffn_specs(layer)
    pos_spec = pl.BlockSpec((TM, HEAD_SLAB), lambda i: (jnp.where(i < NT_P, 0, 1 + (i - NT_P) % TILES_PER_DEC), 0))
    return pl.pallas_call(
        _in_even_kernel,
        grid=(NT,),
        in_specs=[_ctx_row(D), _lat_row(D), _mod_spec(layer), _layer(layer, (1, D)), _layer(j, w_in.shape[1:]),
                  _full(wkr2.shape), _layer(j, (1, MLA_Q_RANK)), _layer(j, (1, MLA_KV_RANK)), _full(wuq2.shape),
                  _full(wk.shape), _full(wv.shape), pos_spec, pos_spec] + cast_in,
        out_specs=[_row(4 * RET_W), _row(MLA_QW), _row(MLA_QW), _row(MLA_VW), _ctx_row(MLA_KV_RANK),
                   _ctx_row(HEAD_SLAB)] + cast_out,
        out_shape=[
            jax.ShapeDtypeStruct((N_TOK, 4 * RET_W), BF16),
            jax.ShapeDtypeStruct((N_TOK, MLA_QW), BF16),
            jax.ShapeDtypeStruct((N_TOK, MLA_QW), BF16),
            jax.ShapeDtypeStruct((N_TOK, MLA_VW), BF16),
            jax.ShapeDtypeStruct((N_TOK_P, MLA_KV_RANK), F32),
            jax.ShapeDtypeStruct((N_TOK_P, HEAD_SLAB), F32),
        ] + cast_shape,
        compiler_params=_cp("arbitrary"),
        name="in_even",
    )(*xs, mods, gains, w_in, wkr2, qn, kvn, wuq2, wk, wv, cos_t, sin_t, w_up, w_down)


def _mla_cache_kernel(ckv_ref, kr_ref, wk_ref, wv_ref, kp_ref, v_ref):
    cb = ckv_ref[...].astype(BF16)
    kp = _dot(cb, wk_ref[...])
    kr = kr_ref[...]
    for hh in range(MLA_HEADS):
        sl = slice(hh * HEAD_SLAB, (hh + 1) * HEAD_SLAB)
        kp_ref[:, sl] = (kp[:, sl] + kr).astype(BF16)
    v_ref[...] = _dot(cb, wv_ref[...]).astype(BF16)


def _mla_cache(ckv_c, kr_slab, wk, wv):
    row = lambda w: pl.BlockSpec((PAST_LEN, w), lambda b: (b, 0))
    return pl.pallas_call(
        _mla_cache_kernel,
        grid=(DEC_BATCH,),
        in_specs=[row(MLA_KV_RANK), row(HEAD_SLAB), _full(wk.shape), _full(wv.shape)],
        out_specs=[row(MLA_QW), row(MLA_VW)],
        out_shape=[jax.ShapeDtypeStruct((DEC_BATCH * PAST_LEN, MLA_QW), BF16),
                   jax.ShapeDtypeStruct((DEC_BATCH * PAST_LEN, MLA_VW), BF16)],
        compiler_params=_cp("parallel"),
        name="mla_cache",
    )(ckv_c, kr_slab, wk, wv)


def _ret_kernel(lg_ref, q_ref, k_ref, v_ref, g_ref, gn_ref, s0_ref, *rest, seq_len, n_blk, emit_state, n_prev):
    prev_refs, rest = rest[:min(n_prev, 1)], rest[min(n_prev, 1):]
    if emit_state:
        y_ref, st_ref, sf_scr, sb_scr, dm_scr, w_scr = rest
    else:
        y_ref, sf_scr, sb_scr, dm_scr, w_scr = rest
    C = RET_CHUNK
    nc = seq_len // C

    @pl.when(pl.program_id(1) == 0)
    def _():
        lg = -jnp.log(1.0 + jnp.exp(-lg_ref[...]))
        lg_f = lg[0]
        lg_b = lg[1]
        ii = lax.broadcasted_iota(jnp.int32, (C, C), 0)
        jj = lax.broadcasted_iota(jnp.int32, (C, C), 1)
        diff = (ii - jj).astype(F32)
        dm_scr[...] = (jnp.where(diff >= 0, jnp.exp(lg_f * jnp.maximum(diff, 0.0)), 0.0)
                       + jnp.where(diff <= 0, jnp.exp(lg_b * jnp.maximum(-diff, 0.0)), 0.0))
        pos = lax.broadcasted_iota(jnp.int32, (C, RET_DIM), 0).astype(F32)
        w_scr[0] = jnp.exp(lg_f * (pos + 1.0))
        w_scr[1] = jnp.exp(lg_f * (C - 1.0 - pos))
        w_scr[2] = jnp.exp(lg_b * (C - pos))
        w_scr[3] = jnp.exp(lg_b * pos)
        w_scr[4] = jnp.exp(lg_f * C) + jnp.zeros((C, RET_DIM), F32)
        w_scr[5] = jnp.exp(lg_b * C) + jnp.zeros((C, RET_DIM), F32)

    qw_f, kw_f, qw_b, kw_b = w_scr[0], w_scr[1], w_scr[2], w_scr[3]
    cd_f = w_scr[4, :RET_DIM, :]
    cd_b = w_scr[5, :RET_DIM, :]
    ld = lambda ref, rows: ref[rows, :].astype(F32)
    gn = gn_ref[...]
    dm = dm_scr[...]

    for b in range(n_blk):
        rows_of = lambda n: slice(b * seq_len + n * C, b * seq_len + (n + 1) * C)
        sf = lambda n: sf_scr.at[b * (nc + 1) + n]
        sb = lambda n: sb_scr.at[b * (nc + 1) + n]
        sf(0)[...] = s0_ref[b, 0]
        for n in range(nc):
            kv = _dot_tn((ld(k_ref, rows_of(n)) * kw_f).astype(BF16), v_ref[rows_of(n), :])
            sf(n + 1)[...] = cd_f * sf(n)[...] + kv
        sb(nc)[...] = s0_ref[b, 1]
        for n in reversed(range(nc)):
            kv = _dot_tn((ld(k_ref, rows_of(n)) * kw_b).astype(BF16), v_ref[rows_of(n), :])
            sb(n)[...] = cd_b * sb(n + 1)[...] + kv
        if emit_state:
            for jp in range(n_prev):
                st_ref[b, jp] = prev_refs[0][b, jp]
            st_ref[b, n_prev, 0] = sf(nc)[...]
            st_ref[b, n_prev, 1] = sb(0)[...]
        for n in range(nc):
            rows = rows_of(n)
            q = ld(q_ref, rows) * (RET_DIM ** -0.5)
            s = _dot_nt(q.astype(BF16), k_ref[rows, :]) * dm
            o = (_dot(s.astype(BF16), v_ref[rows, :])
                 + _dot((q * qw_f).astype(BF16), sf(n)[...].astype(BF16))
                 + _dot((q * qw_b).astype(BF16), sb(n + 1)[...].astype(BF16)))
            mu = jnp.mean(o, axis=-1, keepdims=True)
            oc = o - mu
            var = jnp.mean(oc * oc, axis=-1, keepdims=True)
            on = oc * lax.rsqrt(var + EPS) * gn
            y_ref[rows, :] = (_silu(ld(g_ref, rows)) * on).astype(BF16)


def _retention(qkvg, logit, gn, j, s0, *, seq_len, n_seq, n_blk, row0, emit_state, prev_states=None):
    n_prev = 0 if prev_states is None else prev_states.shape[1]
    nc = seq_len // RET_CHUNK
    rows = n_blk * seq_len
    blk0 = row0 // rows
    col = lambda part: pl.BlockSpec((rows, RET_DIM), lambda h, s: (blk0 + s, part * RET_HEADS + h))
    state = pl.BlockSpec((n_blk, 2, None, RET_DIM, RET_DIM), lambda h, s: (s, 0, h, 0, 0))
    in_specs = [
        pl.BlockSpec((None, 2, 1, 1), lambda h, s: (h, 0, 0, 0)),
        col(0), col(1), col(2), col(3),
        pl.BlockSpec((None, 1, RET_DIM), lambda h, s: (j, 0, h)),
        state,
    ]
    out_specs = [pl.BlockSpec((rows, RET_DIM), lambda h, s: (s, h))]
    out_shape = [jax.ShapeDtypeStruct((n_seq * seq_len, RET_W), BF16)]
    args = [logit, qkvg, qkvg, qkvg, qkvg, gn, s0]
    stacked = lambda n: pl.BlockSpec((n_blk, n, 2, None, RET_DIM, RET_DIM), lambda h, s: (s, 0, 0, h, 0, 0))
    if n_prev:
        in_specs.append(stacked(n_prev))
        args.append(prev_states)
    if emit_state:
        out_specs.append(stacked(n_prev + 1))
        out_shape.append(jax.ShapeDtypeStruct((n_seq, n_prev + 1, 2, RET_HEADS, RET_DIM, RET_DIM), F32))
    n_st = n_blk * (nc + 1)
    return pl.pallas_call(
        functools.partial(_ret_kernel, seq_len=seq_len, n_blk=n_blk, emit_state=emit_state, n_prev=n_prev),
        grid=(RET_HEADS, n_seq // n_blk),
        in_specs=in_specs,
        out_specs=out_specs,
        out_shape=out_shape,
        scratch_shapes=[pltpu.VMEM((n_st, RET_DIM, RET_DIM), F32), pltpu.VMEM((n_st, RET_DIM, RET_DIM), F32),
                        pltpu.VMEM((RET_CHUNK, RET_CHUNK), F32), pltpu.VMEM((6, RET_CHUNK, RET_DIM), F32)],
        compiler_params=_cp("parallel", "arbitrary"),
        name="retention_%d" % seq_len,
    )(*args)


def _softmax_pv(score_blocks, value_blocks):
    m = functools.reduce(jnp.maximum, [jnp.max(s, axis=-1, keepdims=True) for s in score_blocks])
    ps = [jnp.exp2(s - m) for s in score_blocks]
    l = functools.reduce(lambda a, b: a + b, [jnp.sum(p, axis=-1, keepdims=True) for p in ps])
    o = functools.reduce(lambda a, b: a + b, [_dot(p.astype(BF16), v) for p, v in zip(ps, value_blocks)])
    return o / l


def _low_half(shape):
    return lax.broadcasted_iota(jnp.int32, shape, 1) < 64


def _heads_pipelined(n_heads, scores, attend, y_ref, rows):
    low = _low_half((rows, 128))
    nxt = scores(0)
    outs = []
    for hh in range(n_heads):
        cur = nxt
        if hh + 1 < n_heads:
            nxt = scores(hh + 1)
        outs.append(attend(hh, cur))
        if hh % 2 == 1:
            hp = hh // 2
            y_ref[:, hp * 128:(hp + 1) * 128] = jnp.where(low, outs[hh - 1], outs[hh]).astype(BF16)


def _mla_ctx_kernel(q_ref, k_ref, v_ref, y_ref):
    slab = lambda hh: slice(hh * HEAD_SLAB, (hh + 1) * HEAD_SLAB)
    pair = lambda hh: slice(hh // 2 * 128, (hh // 2 + 1) * 128)
    scores = lambda hh: [_dot_nt(q_ref[:, slab(hh)], k_ref[:, slab(hh)])]
    attend = lambda hh, s: _softmax_pv(s, [v_ref[:, pair(hh)]])
    _heads_pipelined(MLA_HEADS, scores, attend, y_ref, SEQ)


def _mla_ctx(qp, kp, v):
    return pl.pallas_call(
        _mla_ctx_kernel,
        grid=(BATCH,),
        in_specs=[pl.BlockSpec((SEQ, MLA_QW), lambda b: (b, 0)), pl.BlockSpec((SEQ, MLA_QW), lambda b: (b, 0)),
                  pl.BlockSpec((SEQ, MLA_VW), lambda b: (b, 0))],
        out_specs=pl.BlockSpec((SEQ, MLA_VW), lambda b: (b, 0)),
        out_shape=jax.ShapeDtypeStruct((N_TOK_P, MLA_VW), BF16),
        compiler_params=_cp("parallel"),
        name="mla_ctx",
    )(qp, kp, v)


def _mla_lat_kernel(q_ref, k_ref, v_ref, kc_ref, vc_ref, y_ref):
    slab = lambda hh: slice(hh * HEAD_SLAB, (hh + 1) * HEAD_SLAB)
    pair = lambda hh: slice(hh // 2 * 128, (hh // 2 + 1) * 128)
    scores = lambda hh: [_dot_nt(q_ref[:, slab(hh)], k_ref[:, slab(hh)]),
                         _dot_nt(q_ref[:, slab(hh)], kc_ref[:, slab(hh)])]
    attend = lambda hh, s: _softmax_pv(s, [v_ref[:, pair(hh)], vc_ref[:, pair(hh)]])
    _heads_pipelined(MLA_HEADS, scores, attend, y_ref, TQ)


def _mla_lat(qp, kp, v, kp_c, v_c):
    seq_blk = N_TOK_P // DEC_SEQ
    nq = DEC_SEQ // TQ
    return pl.pallas_call(
        _mla_lat_kernel,
        grid=(DEC_BATCH, nq),
        in_specs=[
            pl.BlockSpec((TQ, MLA_QW), lambda b, t: (N_TOK_P // TQ + b * nq + t, 0)),
            pl.BlockSpec((DEC_SEQ, MLA_QW), lambda b, t: (seq_blk + b, 0)),
            pl.BlockSpec((DEC_SEQ, MLA_VW), lambda b, t: (seq_blk + b, 0)),
            pl.BlockSpec((PAST_LEN, MLA_QW), lambda b, t: (b, 0)),
            pl.BlockSpec((PAST_LEN, MLA_VW), lambda b, t: (b, 0)),
        ],
        out_specs=pl.BlockSpec((TQ, MLA_VW), lambda b, t: (b * nq + t, 0)),
        out_shape=jax.ShapeDtypeStruct((N_TOK_S, MLA_VW), BF16),
        compiler_params=_cp("parallel", "parallel"),
        name="mla_lat",
    )(qp, kp, v, kp_c, v_c)


def _gelu_tanh(x):
    return 0.5 * x * (1.0 + jnp.tanh(np.sqrt(2.0 / np.pi).astype(np.float32) * (x + 0.044715 * (x * x * x))))


def _ffn_kernel(*refs, odd):
    xc_ref, xl_ref, xp_ref, xn_ref = refs[:4]
    if odd:
        (yr_ref, yrp_ref, yrn_ref, u_ref, up_ref, un_ref, bc_ref, bl_ref, bp_ref, bn_ref, mod_ref, gmix_ref, gpre_ref,
         gpost_ref, wo_ref, d_ref, gw_ref, gb_ref, wu_ref, cw_ref, cb_ref, wd_ref, oc_ref, ol_ref, h_scr,
         act_scr) = refs[4:]
    else:
        (ac_ref, al_ref, ap_ref, an_ref, bc_ref, bl_ref, bp_ref, bn_ref, mod_ref, gmix_ref, gpre_ref, gpost_ref,
         wo_ref, wu_ref, cw_ref, cb_ref, wd_ref, oc_ref, ol_ref, h_scr, act_scr) = refs[4:]
    i = pl.program_id(0)
    is_lat = i >= NT_P
    t = (i - NT_P) % TILES_PER_DEC
    has_prev = jnp.logical_and(is_lat, t != 0)
    has_next = jnp.logical_and(is_lat, t != TILES_PER_DEC - 1)
    mod = mod_ref[...]
    cat = lambda main, prev_ref, next_ref: jnp.concatenate([main, prev_ref[...], next_ref[...]], axis=0)

    if odd:
        y = _gelu_tanh(cat(yr_ref[...], yrp_ref, yrn_ref) + d_ref[...] * cat(u_ref[...], up_ref, un_ref))
        ya = (y * _sigmoid(_dot(y.astype(BF16), gw_ref[...]) + gb_ref[...])).astype(BF16)
    else:
        ya = cat(_pick(ac_ref, al_ref), ap_ref, an_ref)
    yb = cat(_pick(bc_ref, bl_ref), bp_ref, bn_ref)
    half = ya.shape[1]
    r = _dot(ya, wo_ref[:half, :]) + _dot(yb, wo_ref[half:, :])
    x1 = cat(_pick(xc_ref, xl_ref), xp_ref, xn_ref) + mod[:, 2 * D:3 * D] * _rms(r, gmix_ref[...])

    shift = mod[:, 3 * D:4 * D]
    scale = mod[:, 4 * D:5 * D]
    gate = mod[:, 5 * D:6 * D]
    hall = _rms(x1, gpre_ref[...]) * (1.0 + scale) + shift
    x = x1[:TM]
    h = hall[:TM]
    blk = FF_SUB + FF_EXT
    rows = FF_NSUB * blk
    for k in range(FF_NSUB):
        h_scr[k * blk:k * blk + FF_SUB, :] = h[k * FF_SUB:(k + 1) * FF_SUB].astype(BF16)
        if k + 1 < FF_NSUB:
            after = jnp.where(is_lat, h[(k + 1) * FF_SUB:(k + 1) * FF_SUB + 8], 0.0)
            before = jnp.where(is_lat, h[(k + 1) * FF_SUB - 8:(k + 1) * FF_SUB], 0.0)
        else:
            after = jnp.where(has_next, hall[TM + HALO:TM + HALO + 8], 0.0)
            before = jnp.where(has_prev, hall[TM + HALO - 8:TM + HALO], 0.0)
        h_scr[k * blk + FF_SUB:(k + 1) * blk, :] = jnp.concatenate([after, before], axis=0).astype(BF16)
    hb = h_scr[...]

    def up(j):
        ca = slice(j * FF_CHUNK, (j + 1) * FF_CHUNK)
        cg = slice(D_FF + j * FF_CHUNK, D_FF + (j + 1) * FF_CHUNK)
        return (_dot(hb, wu_ref[:, ca]), ca), (_dot(hb, wu_ref[:, cg]), cg)

    def conv(part):
        u, cols = part
        cw = cw_ref[:, cols]
        return (cw[0:1, :] * pltpu.roll(u, 1, axis=0) + cw[1:2, :] * u + cw[2:3, :] * pltpu.roll(u, rows - 1, axis=0)
                + cb_ref[:, cols])

    nxt = up(0)
    for j in range(FF_NCHUNK):
        cur = nxt
        if j + 1 < FF_NCHUNK:
            nxt = up(j + 1)
        act = (_silu(conv(cur[1])) * conv(cur[0])).astype(BF16)
        for k in range(FF_NSUB):
            act_scr[k * FF_SUB:(k + 1) * FF_SUB, j * FF_CHUNK:(j + 1) * FF_CHUNK] = act[k * blk:k * blk + FF_SUB]
    out = x + gate * _rms(_dot(act_scr[...], wd_ref[...]), gpost_ref[...])

    @pl.when(i < NT_P)
    def _():
        oc_ref[...] = out

    @pl.when(i >= NT_P)
    def _():
        ol_ref[...] = out


def _halo_specs(width, n_rows, tile0):
    per = TM // HALO
    last = n_rows // HALO - 1
    prev = pl.BlockSpec((HALO, width), lambda i: (jnp.clip((i - tile0) * per - 1, 0, last), 0))
    nxt = pl.BlockSpec((HALO, width), lambda i: (jnp.clip((i - tile0 + 1) * per, 0, last), 0))
    return [prev, nxt]


def _ffn(xs, mix, mods, gmix, gpre, gpost, layer, j, w_out, wu, cw, cb, wd, s5=None):
    odd = s5 is not None
    pair = lambda w: [_ctx_row(w), _lat_row(w)]
    lat_halo = lambda w: _halo_specs(w, N_TOK_S, NT_P)
    all_halo = lambda w: _halo_specs(w, N_TOK, 0)
    resident = lambda shape: pl.BlockSpec(shape, lambda i: (0, 0), pipeline_mode=pl.Buffered(1))
    x_specs = pair(D) + lat_halo(D)
    x_args = [xs[0], xs[1], xs[1], xs[1]]
    if odd:
        yr, u, b_c, b_l = mix
        mix_specs = [_row(S5_W)] + all_halo(S5_W) + [_row(S5_W)] + all_halo(S5_W) + pair(NA_W) + lat_halo(NA_W)
        mix_args = [yr, yr, yr, u, u, u, b_c, b_l, b_l, b_l]
        s5_specs = [_layer(j, (1, S5_W)), _layer(j, (S5_W, S5_W)), _layer(j, (1, S5_W))]
        s5_args = list(s5)
    else:
        a_c, a_l, b_c, b_l = mix
        mix_specs = pair(RET_W) + lat_halo(RET_W) + pair(MLA_VW) + lat_halo(MLA_VW)
        mix_args = [a_c, a_l, a_l, a_l, b_c, b_l, b_l, b_l]
        s5_specs, s5_args = [], []
    return pl.pallas_call(
        functools.partial(_ffn_kernel, odd=odd),
        grid=(NT,),
        in_specs=x_specs + mix_specs + [_mod_spec(layer), _layer(layer, (1, D)), _layer(layer, (1, D)),
                                        _layer(layer, (1, D)), _layer(j, (D, D))] + s5_specs
        + [resident((D, 2 * D_FF)), _layer(layer, (3, 2 * D_FF)), _layer(layer, (1, 2 * D_FF)), resident((D_FF, D))],
        out_specs=[_ctx_row(D), _lat_row(D)],
        out_shape=[jax.ShapeDtypeStruct((N_TOK_P, D), F32), jax.ShapeDtypeStruct((N_TOK_S, D), F32)],
        scratch_shapes=[pltpu.VMEM((FF_NSUB * (FF_SUB + FF_EXT), D), BF16), pltpu.VMEM((TM, D_FF), BF16)],
        compiler_params=_cp("arbitrary"),
        name="ffn_odd" if odd else "ffn_even",
    )(*x_args, *mix_args, mods, gmix, gpre, gpost, w_out, *s5_args, wu, cw, cb, wd)


def _in_odd_kernel(xc_ref, xl_ref, mod_ref, g_ref, w_ref, wu32_ref, wd32_ref, *rest, n_prev):
    np2 = 2 * min(n_prev, 1)
    prev_refs, (u_ref, qkv_ref, kc_ref, vc_ref, wu_ref, wd_ref) = rest[:np2], rest[np2:]
    _cast_ffn_slab(wu32_ref, wd32_ref, wu_ref, wd_ref)
    mod = mod_ref[...]
    h = _rms(_pick(xc_ref, xl_ref), g_ref[...]) * (1.0 + mod[:, D:2 * D]) + mod[:, :D]
    r = _dot(h.astype(BF16), w_ref[...])
    u_ref[...] = r[:, :S5_W]
    qkv_ref[:, :NA_W] = (r[:, S5_W:S5_W + NA_W] * (NA_SCALE * LOG2E)).astype(BF16)
    qkv_ref[:, NA_W:] = r[:, S5_W + NA_W:].astype(BF16)

    @pl.when(pl.program_id(0) < NT_P)
    def _():
        for part, (out_ref, col0) in enumerate([(kc_ref, S5_W + NA_W), (vc_ref, S5_W + 2 * NA_W)]):
            for b in range(TM // SEQ):
                for jp in range(n_prev):
                    out_ref[b, jp] = prev_refs[part][b, jp]
                out_ref[b, n_prev] = r[b * SEQ:(b + 1) * SEQ, col0:col0 + NA_W]


def _in_odd(xs, mods, gains, layer, j, w, w_up, w_down, prev_kv=None):
    cast_in, cast_out, cast_shape = _cast_ffn_specs(layer)
    n_prev = 0 if prev_kv is None else prev_kv[0].shape[1]
    nb = TM // SEQ
    stacked = lambda n: pl.BlockSpec((nb, n, SEQ, NA_W), lambda i: (jnp.minimum(i, NT_P - 1), 0, 0, 0))
    leaf = jax.ShapeDtypeStruct((BATCH, n_prev + 1, SEQ, NA_W), F32)
    prev_specs = [stacked(n_prev)] * 2 if n_prev else []
    return pl.pallas_call(
        functools.partial(_in_odd_kernel, n_prev=n_prev),
        grid=(NT,),
        in_specs=[_ctx_row(D), _lat_row(D), _mod_spec(layer), _layer(layer, (1, D)), _layer(j, w.shape[1:])] + cast_in
        + prev_specs,
        out_specs=[_row(S5_W), _row(3 * NA_W), stacked(n_prev + 1), stacked(n_prev + 1)] + cast_out,
        out_shape=[jax.ShapeDtypeStruct((N_TOK, S5_W), F32), jax.ShapeDtypeStruct((N_TOK, 3 * NA_W), BF16), leaf, leaf]
        + cast_shape,
        compiler_params=_cp("arbitrary"),
        name="in_odd",
    )(*xs, mods, gains, w, w_up, w_down, *(prev_kv or ()))


S5_TS = S5_CHUNK * S5_GROUP
S5_PL = 2 * S5_P
S5_PREP_PAIRS = 2


def _s5_prep_kernel(*refs):
    ins, outs = refs[:7], refs[7:]
    for q in range(S5_PREP_PAIRS):
        _s5_prep_pair(*[r.at[:, q] for r in ins], *[r.at[q] for r in outs])


def _s5_prep_pair(lre_ref, lim_ref, ls_ref, btr_ref, bti_ref, cr_ref, ci_ref, m_ref, n_ref, p_ref, a_ref, ct_scr):
    T = S5_CHUNK
    S = S5_GROUP
    hi = lax.Precision.HIGHEST
    low = lax.broadcasted_iota(jnp.int32, (S, S5_PL), 1) < S5_P
    half = [low, jnp.logical_not(low)]
    pick = lambda e, v: jnp.where(half[e], v, 0.0)
    nt = (((1,), (1,)), ((), ()))
    kps = [[None, None], [None, None]]
    for d in range(2):
        lre = lre_ref[d]
        lim = lim_ref[d]
        step = jnp.exp(ls_ref[d])
        mag = jnp.exp(lre * step)
        are = mag * jnp.cos(lim * step)
        aim = mag * jnp.sin(lim * step)
        den = lre * lre + lim * lim
        zr, zi = _cmul(are - 1.0, aim, lre / den, -lim / den)
        bbr, bbi = _cmul(zr, zi, btr_ref[d], bti_ref[d])
        cr = cr_ref[d]
        ci = ci_ref[d]
        pr = jnp.ones_like(are)
        pi = jnp.zeros_like(are)
        for k in range(T + 1):
            er, ei = _cmul(cr, ci, pr, pi)
            if k < T:
                jn = T - 1 - k if d == 0 else k
                wr, wi = _cmul(pr, pi, bbr, bbi)
                for e in range(2):
                    rows = slice(e * S5_TS + jn * S, e * S5_TS + (jn + 1) * S)
                    n_ref[d, rows, 0:S5_PL] = pick(e, wr).astype(BF16)
                    n_ref[d, rows, S5_PL:2 * S5_PL] = pick(e, wi).astype(BF16)
                jc = k if d == 0 else T - 1 - k
                ct_scr[0, jc * S:(jc + 1) * S, :] = er
                ct_scr[1, jc * S:(jc + 1) * S, :] = ei
            if k >= 1:
                t = k - 1 if d == 0 else T - k
                for e in range(2):
                    rows = slice(e * S5_TS + t * S, e * S5_TS + (t + 1) * S)
                    p_ref[rows, 2 * d * S5_PL:(2 * d + 1) * S5_PL] = pick(e, er).astype(BF16)
                    p_ref[rows, (2 * d + 1) * S5_PL:(2 * d + 2) * S5_PL] = pick(e, -ei).astype(BF16)
            if k == T:
                a_ref[d, 0] = pr
                a_ref[d, 1] = pi
            pr, pi = _cmul(pr, pi, are, aim)
        for e in range(2):
            kd = (lax.dot_general(pick(e, bbr), ct_scr[0], nt, precision=hi, preferred_element_type=F32)
                  - lax.dot_general(pick(e, bbi), ct_scr[1], nt, precision=hi, preferred_element_type=F32))
            kps[d][e] = jnp.concatenate([kd, jnp.zeros_like(kd)], axis=1)
    for e in range(2):
        for t in range(T):
            fwd = pltpu.roll(kps[0][e], t * S, axis=1)[:, :S5_TS]
            bwd = pltpu.roll(kps[1][e], (2 * S5_TS - (T - 1 - t) * S) % (2 * S5_TS), axis=1)[:, :S5_TS]
            m_ref[e, t * S:(t + 1) * S, :] = (fwd + bwd).astype(BF16)


def _s5_prep(lre, lim, ls, b_re, b_im, c_re, c_im):
    NP, S, P = S5_PAIRS, S5_GROUP, S5_P
    vec = lambda a: a.reshape(2, NP, 1, S5_PL)
    b_lay = lambda a: jnp.transpose(a.reshape(2, NP, 2, P, S), (0, 1, 4, 2, 3)).reshape(2, NP, S, S5_PL)
    c_lay = lambda a: jnp.transpose(a.reshape(2, NP, 2, S, P), (0, 1, 3, 2, 4)).reshape(2, NP, S, S5_PL)
    args = (vec(lre), vec(lim), vec(jnp.repeat(ls, P, axis=-1)), b_lay(b_re), b_lay(b_im), c_lay(c_re), c_lay(c_im))
    PB = S5_PREP_PAIRS
    vspec = pl.BlockSpec((2, PB, 1, S5_PL), lambda g: (0, g, 0, 0))
    mspec = pl.BlockSpec((2, PB, S, S5_PL), lambda g: (0, g, 0, 0))
    lead = lambda shape: pl.BlockSpec((PB,) + shape, lambda g: (g,) + (0,) * len(shape))
    return pl.pallas_call(
        _s5_prep_kernel,
        grid=(NP // PB,),
        in_specs=[vspec, vspec, vspec, mspec, mspec, mspec, mspec],
        out_specs=[lead((2, S5_TS, S5_TS)), lead((2, 2 * S5_TS, 2 * S5_PL)), lead((2 * S5_TS, 4 * S5_PL)),
                   lead((2, 2, 1, S5_PL))],
        out_shape=[jax.ShapeDtypeStruct((NP, 2, S5_TS, S5_TS), BF16),
                   jax.ShapeDtypeStruct((NP, 2, 2 * S5_TS, 2 * S5_PL), BF16),
                   jax.ShapeDtypeStruct((NP, 2 * S5_TS, 4 * S5_PL), BF16),
                   jax.ShapeDtypeStruct((NP, 2, 2, 1, S5_PL), F32)],
        scratch_shapes=[pltpu.VMEM((S5_PREP_PAIRS, 2, S5_TS, S5_PL), F32)],
        compiler_params=_cp("parallel"),
        name="s5_prep",
    )(*args)


S5_NCH = N_TOK // S5_CHUNK
S5_ROWS_P = N_TOK_P // S5_CHUNK
S5_NC_P = SEQ // S5_CHUNK
S5_NC_S = DEC_SEQ // S5_CHUNK
S5_GPB = 8
S5_PPB = S5_GPB // 2
S5_XL = 8 * 128


def _s5_perm():
    r = np.arange(S5_XL)
    dst = (r // S5_GROUP % S5_GPB) * 128 + (r // 128) * S5_GROUP + r % S5_GROUP
    perm = np.zeros((S5_XL, S5_XL), np.float32)
    perm[r, dst] = 1.0
    return jnp.asarray(perm, BF16)


def _s5_kernel(u_ref, perm_ref, m_ref, n_ref, p_ref, a_ref, h0c_ref, h0l_ref, y_ref, fin_ref, z_scr, up_scr, e_scr,
               hin_scr, yc_scr):
    T = S5_CHUNK
    W = S5_PL
    for t in range(T):
        z_scr[t // 8, :, (t % 8) * 128:(t % 8 + 1) * 128] = u_ref[pl.ds(t, S5_NCH, stride=T), :].astype(BF16)
    perm = perm_ref[...]
    for j in range(2):
        up_scr[j] = _dot(z_scr[j], perm).astype(BF16)

    def scan(pp, d, h0_ref, n_chunks, n_seq, row0):
        are = a_ref[pp, d, 0]
        aim = a_ref[pp, d, 1]
        hr = h0_ref[pp, d, 0]
        hi = h0_ref[pp, d, 1]
        order = range(n_chunks) if d == 0 else reversed(range(n_chunks))
        for c in order:
            rows = pl.ds(row0 + c, n_seq, stride=n_chunks)
            hin_scr[2 * d, rows, :] = hr
            hin_scr[2 * d + 1, rows, :] = hi
            er = e_scr[2 * d, rows, :]
            ei = e_scr[2 * d + 1, rows, :]
            hr, hi = are * hr - aim * hi + er, are * hi + aim * hr + ei
        return hr, hi

    for pp in range(S5_PPB):
        us = []
        for e in range(2):
            sl = slice((2 * pp + e) * 128, (2 * pp + e + 1) * 128)
            us.append(jnp.concatenate([up_scr[0, :, sl], up_scr[1, :, sl]], axis=1))
        u2 = jnp.concatenate(us, axis=1)
        for d in range(2):
            ed = _dot(u2, n_ref[pp, d])
            e_scr[2 * d] = ed[:, :W]
            e_scr[2 * d + 1] = ed[:, W:]
        for d in range(2):
            hr, hi = scan(pp, d, h0c_ref, S5_NC_P, BATCH, 0)
            fin_ref[pp, d, 0] = hr
            fin_ref[pp, d, 1] = hi
            scan(pp, d, h0l_ref, S5_NC_S, DEC_BATCH, S5_ROWS_P)
        hin = jnp.concatenate([hin_scr[k] for k in range(4)], axis=1).astype(BF16)
        for e in range(2):
            y = _dot(us[e], m_ref[pp, e]) + _dot_nt(hin, p_ref[pp, e * S5_TS:(e + 1) * S5_TS, :])
            sl = slice((2 * pp + e) * 128, (2 * pp + e + 1) * 128)
            for j in range(2):
                yc_scr[j, :, sl] = y[:, j * 128:(j + 1) * 128]

    for j in range(2):
        yp = yc_scr[j]
        y_hi = yp.astype(BF16)
        y_lo = (yp - y_hi.astype(F32)).astype(BF16)
        r = _dot_nt(y_hi, perm) + _dot_nt(y_lo, perm)
        for k in range(8):
            y_ref[pl.ds(8 * j + k, S5_NCH, stride=T), :] = r[:, k * 128:(k + 1) * 128]


def _s5_scan(u, perm, m, n2, p2, a, h0c, h0l):
    nb = S5_GROUPS // S5_GPB
    lead = lambda shape: pl.BlockSpec((S5_PPB,) + shape, lambda w: (w,) + (0,) * len(shape))
    col = pl.BlockSpec((N_TOK, 128), lambda w: (0, w))
    return pl.pallas_call(
        _s5_kernel,
        grid=(nb,),
        in_specs=[col, _full(perm.shape), lead((2, S5_TS, S5_TS)), lead((2, 2 * S5_TS, 2 * S5_PL)),
                  lead((2 * S5_TS, 4 * S5_PL)), lead((2, 2, 1, S5_PL)), lead((2, 2, BATCH, S5_PL)),
                  lead((2, 2, DEC_BATCH, S5_PL))],
        out_specs=[col, lead((2, 2, BATCH, S5_PL))],
        out_shape=[jax.ShapeDtypeStruct((N_TOK, S5_W), F32),
                   jax.ShapeDtypeStruct((S5_PAIRS, 2, 2, BATCH, S5_PL), F32)],
        scratch_shapes=[pltpu.VMEM((2, S5_NCH, S5_XL), BF16), pltpu.VMEM((2, S5_NCH, S5_XL), BF16),
                        pltpu.VMEM((4, S5_NCH, S5_PL), F32), pltpu.VMEM((4, S5_NCH, S5_PL), F32),
                        pltpu.VMEM((2, S5_NCH, S5_XL), F32)],
        compiler_params=_cp("parallel"),
        name="s5_scan",
    )(u, perm, m, n2, p2, a, h0c, h0l)


def _na_heads(q_ref, keys, values, y_ref, rows, bias=None):
    low = _low_half((rows, 128))
    pair = lambda hh: slice(hh // 2 * 128, (hh // 2 + 1) * 128)

    def scores(hh):
        q = q_ref[:, pair(hh)]
        qm = jnp.where(low == (hh % 2 == 0), q, jnp.zeros_like(q))
        s = [_dot_nt(qm, k(pair(hh))) for k in keys]
        if bias is not None:
            s[0] = s[0] + bias(hh)
        return s

    attend = lambda hh, s: _softmax_pv(s, [v(pair(hh)) for v in values])
    _heads_pipelined(NA_HEADS, scores, attend, y_ref, rows)


def _na_ctx_kernel(q_ref, k_ref, v_ref, y_ref):
    _na_heads(q_ref, [lambda sl: k_ref[:, sl]], [lambda sl: v_ref[:, sl]], y_ref, SEQ)


def _na_ctx(qkv):
    col = lambda part: pl.BlockSpec((SEQ, NA_W), lambda b: (b, part))
    return pl.pallas_call(
        _na_ctx_kernel,
        grid=(BATCH,),
        in_specs=[col(0), col(1), col(2)],
        out_specs=pl.BlockSpec((SEQ, NA_W), lambda b: (b, 0)),
        out_shape=jax.ShapeDtypeStruct((N_TOK_P, NA_W), BF16),
        compiler_params=_cp("parallel"),
        name="na_ctx",
    )(qkv, qkv, qkv)


def _na_key_row0(rb):
    return jnp.clip(NA_QROWS * rb - NA_WIN_R // 2, 0, GRID_H - NA_KROWS)


def _na_lat_kernel(q_ref, ks_ref, vs_ref, kc_ref, vc_ref, tab_ref, y_ref):
    rb = pl.program_id(1)
    u0 = _na_key_row0(rb)
    start = pl.multiple_of(u0 * GRID_W, GRID_W)
    nk = NA_KROWS * GRID_W
    low_t = _low_half((GRID_W, 128))

    def table_row(i, w):
        qr = NA_QROWS * rb + i
        kr = u0 + w
        rs = jnp.clip(qr - NA_WIN_R // 2, 0, GRID_H - NA_WIN_R)
        inside = jnp.logical_and(kr >= rs, kr < rs + NA_WIN_R)
        return jnp.where(inside, kr - qr + NA_WIN_R - 1, NA_NDR)

    idx = [[table_row(i, w) for w in range(NA_KROWS)] for i in range(NA_QROWS)]

    def bias(h):
        rows = [jnp.concatenate([jnp.where(low_t, tab_ref[h, idx[i][w]], tab_ref[h, idx[i][w + 1]])
                                 for w in range(0, NA_KROWS, 2)], axis=1) for i in range(NA_QROWS)]
        return jnp.concatenate(rows, axis=0)

    keys = [lambda sl: ks_ref[pl.ds(start, nk), sl], lambda sl: kc_ref[:, sl].astype(BF16)]
    values = [lambda sl: vs_ref[pl.ds(start, nk), sl], lambda sl: vc_ref[:, sl].astype(BF16)]
    _na_heads(q_ref, keys, values, y_ref, TQ, bias)


def _na_lat(qkv, k_c, v_c, table):
    seq_blk = N_TOK_P // DEC_SEQ
    nb = GRID_H // NA_QROWS
    return pl.pallas_call(
        _na_lat_kernel,
        grid=(DEC_BATCH, nb),
        in_specs=[
            pl.BlockSpec((TQ, NA_W), lambda b, r: (N_TOK_P // TQ + b * nb + r, 0)),
            pl.BlockSpec((DEC_SEQ, NA_W), lambda b, r: (seq_blk + b, 1)),
            pl.BlockSpec((DEC_SEQ, NA_W), lambda b, r: (seq_blk + b, 2)),
            pl.BlockSpec((PAST_LEN, NA_W), lambda b, r: (b, 0)),
            pl.BlockSpec((PAST_LEN, NA_W), lambda b, r: (b, 0)),
            pl.BlockSpec(table.shape, lambda b, r: (0, 0, 0, 0)),
        ],
        out_specs=pl.BlockSpec((TQ, NA_W), lambda b, r: (b * nb + r, 0)),
        out_shape=jax.ShapeDtypeStruct((N_TOK_S, NA_W), BF16),
        compiler_params=_cp("parallel", "arbitrary"),
        name="na_lat",
    )(qkv, qkv, qkv, k_c, v_c, table)


def _na_table_kernel(rpb_ref, t_ref):
    qc = lax.broadcasted_iota(jnp.int32, (GRID_W, 128), 0)
    kc = lax.broadcasted_iota(jnp.int32, (GRID_W, 128), 1) % GRID_W
    cs = jnp.clip(qc - NA_WIN_C // 2, 0, GRID_W - NA_WIN_C)
    in_band = jnp.logical_and(kc >= cs, kc < cs + NA_WIN_C)
    neg = jnp.full((GRID_W, 128), -jnp.inf, F32)

    def body(n, carry):
        x = jnp.broadcast_to(rpb_ref[n], (GRID_W, 128))
        t = pltpu.roll(x, 128 - (NA_WIN_C - 1), axis=1, stride=1, stride_axis=0)
        t_ref[n // NA_NDR, n % NA_NDR] = jnp.where(in_band, t * LOG2E, neg)
        return carry

    lax.fori_loop(0, NA_HEADS * NA_NDR, body, 0, unroll=8)
    for h in range(NA_HEADS):
        t_ref[h, NA_NDR] = neg


def _na_table(rpb):
    rows = jnp.pad(rpb.reshape(NA_HEADS * NA_NDR, 1, NA_NDC), ((0, 0), (0, 0), (0, GRID_W - NA_NDC)))
    rows = jnp.concatenate([rows, rows], axis=-1)
    return pl.pallas_call(
        _na_table_kernel,
        out_shape=jax.ShapeDtypeStruct((NA_HEADS, NA_NDR + 1, GRID_W, 128), F32),
        name="na_table",
    )(rows)


def _rope_tables():
    n_freq = MLA_ROPE // 4
    inv = ROPE_BASE ** (-jnp.arange(n_freq, dtype=F32) / n_freq)
    t = jnp.arange(DEC_SEQ)
    row = (t // GRID_W).astype(F32)
    colp = (t % GRID_W).astype(F32)
    ang = jnp.concatenate([row[:, None] * inv, colp[:, None] * inv], axis=-1)
    cos, sin = jnp.cos(ang), jnp.sin(ang)
    one = jnp.ones((DEC_SEQ, MLA_NOPE), F32)
    zero = jnp.zeros((DEC_SEQ, MLA_NOPE), F32)
    cos_s = jnp.concatenate([one, cos, cos, one[:, :32]], axis=-1)
    sin_s = jnp.concatenate([zero, -sin, sin, zero[:, :32]], axis=-1)
    cos_t = jnp.concatenate([jnp.ones((TM, HEAD_SLAB), F32), cos_s], axis=0)
    sin_t = jnp.concatenate([jnp.zeros((TM, HEAD_SLAB), F32), sin_s], axis=0)
    return cos_t, sin_t


def _mla_weights(w_in, w_uq, w_ukv):
    half = MLA_ROPE // 2
    wkr = w_in[:, 4 * RET_W + MLA_Q_RANK + MLA_KV_RANK:]
    z64 = jnp.zeros((D, MLA_NOPE), F32)
    z32 = jnp.zeros((D, HEAD_SLAB - MLA_NOPE - MLA_ROPE), F32)
    wkr2 = jnp.concatenate([z64, wkr, z32, z64, wkr[:, half:], wkr[:, :half], z32], axis=1).astype(BF16)
    wq = w_uq.reshape(MLA_Q_RANK, MLA_HEADS, MLA_NOPE + MLA_ROPE)
    nope, rope = wq[..., :MLA_NOPE], wq[..., MLA_NOPE:]
    zq64 = jnp.zeros_like(nope)
    zq32 = jnp.zeros_like(rope)
    q_slab = jnp.concatenate([nope, rope, zq32], axis=-1).reshape(MLA_Q_RANK, MLA_QW)
    q_sw = jnp.concatenate([zq64, rope[..., half:], rope[..., :half], zq32], axis=-1).reshape(MLA_Q_RANK, MLA_QW)
    wuq2 = jnp.concatenate([q_slab, q_sw], axis=1).astype(BF16)
    wkv = w_ukv.reshape(MLA_KV_RANK, MLA_HEADS, MLA_NOPE + MLA_V)
    wk = jnp.concatenate([wkv[..., :MLA_NOPE], jnp.zeros_like(wkv[..., :MLA_NOPE])], axis=-1)
    wk = wk.reshape(MLA_KV_RANK, MLA_QW).astype(BF16)
    wv = wkv[..., MLA_NOPE:].reshape(MLA_KV_RANK, MLA_VW).astype(BF16)
    return wkr2, wuq2, wk, wv


def kernel(x_prompt, x_sample, c, state_ret, cache_mla_ckv, cache_mla_krope, state_s5_re, state_s5_im, cache_na_k, cache_na_v, c_ctx, ada_w, ada_b, mix_pre_g, mix_post_g, ffn_pre_g, ffn_post_g, ffn_w_up, ffn_conv_w, ffn_conv_b, ffn_w_down, even_w_in, even_w_out, ret_logit, ret_gn, mla_q_norm, mla_w_uq, mla_kv_norm, mla_w_ukv, odd_w_in, odd_w_out, s5_lambda_re, s5_lambda_im, s5_log_step, s5_b_re, s5_b_im, s5_c_re, s5_c_im, s5_d, s5_glu_w, s5_glu_b, na_rpb):
    cvec = jnp.concatenate([c_ctx[None, :], c, jnp.zeros((8 - 1 - DEC_BATCH, D), F32)], axis=0)
    mods = _ada_mods(cvec, ada_w, ada_b)
    row3 = lambda a: a.reshape(a.shape[0], 1, a.shape[1])
    mix_pre, mix_post, ffn_pre, ffn_post = row3(mix_pre_g), row3(mix_post_g), row3(ffn_pre_g), row3(ffn_post_g)
    conv_b = row3(ffn_conv_b)
    e_in, e_out = even_w_in.astype(BF16), even_w_out.astype(BF16)
    o_in, o_out, glu_w = odd_w_in.astype(BF16), odd_w_out.astype(BF16), s5_glu_w.astype(BF16)
    cos_t, sin_t = _rope_tables()
    perm = _s5_perm()
    xs = (x_prompt.reshape(N_TOK_P, D), x_sample.reshape(N_TOK_S, D))
    ret_states, na_kv = None, None
    new_ckv, new_kr, new_s5_re, new_s5_im = [], [], [], []
    for layer in range(DEPTH):
        j = layer // 2
        if layer % 2 == 0:
            wkr2, wuq2, wk, wv = _mla_weights(even_w_in[j], mla_w_uq[j], mla_w_ukv[j])
            qkvg, qp, kp, v, ckvn, kr, w_up, w_down = _in_even(
                xs, mods, mix_pre, layer, j, e_in, wkr2, row3(mla_q_norm), row3(mla_kv_norm), wuq2, wk, wv, cos_t,
                sin_t, ffn_w_up, ffn_w_down)
            logit = jnp.transpose(ret_logit[j]).reshape(RET_HEADS, 2, 1, 1)
            gn = row3(ret_gn)
            s0 = jnp.zeros((BATCH, 2, RET_HEADS, RET_DIM, RET_DIM), F32)
            yr_c, ret_states = _retention(qkvg, logit, gn, j, s0, seq_len=SEQ, n_seq=BATCH, n_blk=4, row0=0,
                                          emit_state=True, prev_states=ret_states)
            (yr_l,) = _retention(qkvg, logit, gn, j, state_ret[:, j], seq_len=DEC_SEQ, n_seq=DEC_BATCH, n_blk=1,
                                 row0=N_TOK_P, emit_state=False)
            ym_c = _mla_ctx(qp, kp, v)
            kr_c = jnp.pad(cache_mla_krope[:, j].reshape(DEC_BATCH * PAST_LEN, MLA_ROPE),
                           ((0, 0), (MLA_NOPE, HEAD_SLAB - MLA_NOPE - MLA_ROPE)))
            kp_c, v_c = _mla_cache(cache_mla_ckv[:, j].reshape(DEC_BATCH * PAST_LEN, MLA_KV_RANK), kr_c, wk, wv)
            ym_l = _mla_lat(qp, kp, v, kp_c, v_c)
            mix, w_out, s5 = (yr_c, yr_l, ym_c, ym_l), e_out, None
            new_ckv.append(ckvn.reshape(BATCH, SEQ, MLA_KV_RANK))
            new_kr.append(kr[:, MLA_NOPE:MLA_NOPE + MLA_ROPE].reshape(BATCH, SEQ, MLA_ROPE))
        else:
            u, qkv, kc, vc, w_up, w_down = _in_odd(xs, mods, mix_pre, layer, j, o_in, ffn_w_up, ffn_w_down,
                                                   prev_kv=na_kv)
            na_kv = (kc, vc)
            m, n2, p2, a = _s5_prep(s5_lambda_re[j], s5_lambda_im[j], s5_log_step[j], s5_b_re[j], s5_b_im[j],
                                    s5_c_re[j], s5_c_im[j])
            h0c = jnp.zeros((S5_PAIRS, 2, 2, BATCH, S5_PL), F32)
            h0 = jnp.stack([state_s5_re[:, j], state_s5_im[:, j]], axis=0)
            h0l = jnp.transpose(h0.reshape(2, DEC_BATCH, 2, S5_PAIRS, S5_PL), (3, 2, 0, 1, 4))
            y_raw, fin = _s5_scan(u, perm, m, n2, p2, a, h0c, h0l)
            yn_c = _na_ctx(qkv)
            yn_l = _na_lat(qkv, cache_na_k[:, j].reshape(DEC_BATCH * PAST_LEN, NA_W),
                           cache_na_v[:, j].reshape(DEC_BATCH * PAST_LEN, NA_W), _na_table(na_rpb[j]))
            mix, w_out, s5 = (y_raw, u, yn_c, yn_l), o_out, (row3(s5_d), glu_w, row3(s5_glu_b))
            st = jnp.transpose(fin.reshape(S5_PAIRS, 2, 2, BATCH, 2, S5_P), (2, 3, 1, 0, 4, 5))
            st = st.reshape(2, BATCH, 2, S5_GROUPS, S5_P)
            new_s5_re.append(st[0])
            new_s5_im.append(st[1])
        xs = _ffn(xs, mix, mods, mix_post, ffn_pre, ffn_post, layer, j, w_out, w_up, ffn_conv_w, conv_b, w_down, s5)
    stack = lambda a: jnp.stack(a, axis=1)
    heads = lambda a: a.reshape(BATCH, a.shape[1], SEQ, NA_HEADS, NA_DIM)
    return (xs[0].reshape(BATCH, SEQ, D), xs[1].reshape(DEC_BATCH, DEC_SEQ, D), ret_states, stack(new_ckv),
            stack(new_kr), stack(new_s5_re), stack(new_s5_im), heads(na_kv[0]), heads(na_kv[1]))
```

```python
import functools

import numpy as np
import jax
import jax.numpy as jnp
from jax import lax
from jax.experimental import pallas as pl
from jax.experimental.pallas import tpu as pltpu

F32 = jnp.float32
BF16 = jnp.bfloat16

D = 1024
BATCH = 16
SEQ = 256
DEPTH = 4
DEC_BATCH = 2
DEC_SEQ = 2048
PAST_LEN = 512
GRID_W = 64
GRID_H = DEC_SEQ // GRID_W
EPS = 1e-6
LOG2E = 1.4426950408889634

RET_HEADS = 4
RET_W = 512
RET_DIM = 128
RET_CHUNK = 256

MLA_HEADS = 8
MLA_NOPE = 64
MLA_ROPE = 32
MLA_V = 64
MLA_Q_RANK = 256
MLA_KV_RANK = 128
MLA_SCALE = (MLA_NOPE + MLA_ROPE) ** -0.5
ROPE_BASE = 10000.0
HEAD_SLAB = 128
MLA_QW = MLA_HEADS * HEAD_SLAB
MLA_VW = MLA_HEADS * MLA_V

S5_W = 512
S5_GROUP = 16
S5_GROUPS = 32
S5_P = 64
S5_CHUNK = 16
S5_PAIRS = S5_GROUPS // 2

NA_HEADS = 8
NA_W = 512
NA_DIM = 64
NA_WIN_R = 8
NA_WIN_C = 16
NA_SCALE = NA_DIM ** -0.5
NA_QROWS = 4
NA_KROWS = 12
NA_NDR = 2 * NA_WIN_R - 1
NA_NDC = 2 * NA_WIN_C - 1

D_FF = 2816
FF_CHUNK = 256
FF_NCHUNK = D_FF // FF_CHUNK
FF_EXT = 16

TM = 512
TQ = 256
FF_SUB = SEQ
FF_NSUB = TM // FF_SUB
HALO = 16
N_TOK_P = BATCH * SEQ
N_TOK_S = DEC_BATCH * DEC_SEQ
N_TOK = N_TOK_P + N_TOK_S
NT_P = N_TOK_P // TM
NT_S = N_TOK_S // TM
NT = NT_P + NT_S
TILES_PER_DEC = DEC_SEQ // TM

VMEM_LIMIT = 56 * 1024 * 1024


def _cp(*sem):
    return pltpu.CompilerParams(dimension_semantics=sem, vmem_limit_bytes=VMEM_LIMIT)


def _dot(a, b):
    return jnp.dot(a, b, preferred_element_type=F32)


def _dot_nt(a, b):
    return lax.dot_general(a, b, (((1,), (1,)), ((), ())), preferred_element_type=F32)


def _dot_tn(a, b):
    return lax.dot_general(a, b, (((0,), (0,)), ((), ())), preferred_element_type=F32)


def _rms(x, g):
    return x * lax.rsqrt(jnp.mean(x * x, axis=-1, keepdims=True) + EPS) * g


def _sigmoid(x):
    return 1.0 / (1.0 + jnp.exp(-x))


def _silu(x):
    return x * _sigmoid(x)


def _cmul(ar, ai, br, bi):
    return ar * br - ai * bi, ar * bi + ai * br


def _mrow(i):
    return jnp.where(i < NT_P, 0, 1 + (i - NT_P) // TILES_PER_DEC)


def _full(shape):
    n = len(shape)
    return pl.BlockSpec(shape, lambda *_: (0,) * n)


def _layer(layer, shape):
    n = len(shape)
    return pl.BlockSpec((None,) + shape, lambda *_: (layer,) + (0,) * n)


def _mod_spec(layer):
    return pl.BlockSpec((None, None, 1, 6 * D), lambda i: (layer, _mrow(i), 0, 0))


def _row(width):
    return pl.BlockSpec((TM, width), lambda i: (i, 0))


def _ctx_row(width):
    return pl.BlockSpec((TM, width), lambda i: (jnp.minimum(i, NT_P - 1), 0))


def _lat_row(width):
    return pl.BlockSpec((TM, width), lambda i: (jnp.maximum(i - NT_P, 0), 0))


def _pick(a_ref, b_ref):
    return jnp.where(pl.program_id(0) < NT_P, a_ref[...], b_ref[...])


def _ada_kernel(c_ref, w_ref, b_ref, o_ref):
    o_ref[...] = _dot(_silu(c_ref[...]).astype(BF16), w_ref[...].astype(BF16)) + b_ref[...]


def _ada_mods(cvec, ada_w, ada_b):
    nb = 4
    bn = 6 * D // nb
    out = pl.pallas_call(
        _ada_kernel,
        grid=(DEPTH, nb),
        in_specs=[
            pl.BlockSpec((8, D), lambda l, n: (0, 0)),
            pl.BlockSpec((None, D, bn), lambda l, n: (l, 0, n)),
            pl.BlockSpec((None, 1, bn), lambda l, n: (l, 0, n)),
        ],
        out_specs=pl.BlockSpec((None, 8, bn), lambda l, n: (l, 0, n)),
        out_shape=jax.ShapeDtypeStruct((DEPTH, 8, 6 * D), F32),
        compiler_params=_cp("arbitrary", "arbitrary"),
        name="ada_mods",
    )(cvec, ada_w, ada_b.reshape(DEPTH, 1, 6 * D))
    return out[:, :3].reshape(DEPTH, 3, 1, 6 * D)


def _cast_ffn_slab(wu32_ref, wd32_ref, wu_ref, wd_ref):
    wu_ref[...] = wu32_ref[...].astype(BF16)
    wd_ref[...] = wd32_ref[...].astype(BF16)


def _cast_ffn_specs(layer, n_inner):
    n = DEC_BATCH * n_inner
    ru, rd = D // n, D_FF // n
    cast_in = [pl.BlockSpec((None, ru, 2 * D_FF), lambda b, t: (layer, b * n_inner + t, 0)),
               pl.BlockSpec((None, rd, D), lambda b, t: (layer, b * n_inner + t, 0))]
    cast_out = [pl.BlockSpec((ru, 2 * D_FF), lambda b, t: (b * n_inner + t, 0)),
                pl.BlockSpec((rd, D), lambda b, t: (b * n_inner + t, 0))]
    cast_shape = [jax.ShapeDtypeStruct((D, 2 * D_FF), BF16), jax.ShapeDtypeStruct((D_FF, D), BF16)]
    return cast_in, cast_out, cast_shape


def _in_even_kernel(xc_ref, xl_ref, mod_ref, g_ref, w_ref, wkr_ref, qn_ref, kvn_ref, wuq_ref, wk_ref, wv_ref, cos_ref,
                    sin_ref, qkvg_ref, qp_ref, kp_ref, v_ref, ckv_ref, kr_ref):
    x = _pick(xc_ref, xl_ref)
    mod = mod_ref[...]
    h = _rms(x, g_ref[...]) * (1.0 + mod[:, D:2 * D]) + mod[:, :D]
    hb = h.astype(BF16)
    o = 4 * RET_W
    for part in range(4):
        cols = slice(part * RET_W, (part + 1) * RET_W)
        qkvg_ref[:, cols] = _dot(hb, w_ref[:, cols]).astype(BF16)
    r = _dot(hb, w_ref[:, o:o + MLA_Q_RANK + MLA_KV_RANK])
    cq = r[:, :MLA_Q_RANK]
    ckv_raw = r[:, MLA_Q_RANK:]
    r2 = _dot(hb, wkr_ref[...])
    kr = r2[:, :HEAD_SLAB]
    krs = r2[:, HEAD_SLAB:]
    cosf = cos_ref[...]
    sinf = sin_ref[...]
    q2 = _dot(_rms(cq, qn_ref[...]).astype(BF16), wuq_ref[...])
    ckvn = _rms(ckv_raw, kvn_ref[...])

    @pl.when(pl.program_id(0) < NT_P)
    def _():
        ckv_ref[...] = ckvn
        kr_ref[...] = kr

    cb = ckvn.astype(BF16)
    kp = _dot(cb, wk_ref[...])
    v_ref[...] = _dot(cb, wv_ref[...]).astype(BF16)
    krr = kr * cosf + krs * sinf
    for hh in range(MLA_HEADS):
        sl = slice(hh * HEAD_SLAB, (hh + 1) * HEAD_SLAB)
        ss = slice(MLA_QW + hh * HEAD_SLAB, MLA_QW + (hh + 1) * HEAD_SLAB)
        qp_ref[:, sl] = ((q2[:, sl] * cosf + q2[:, ss] * sinf) * (MLA_SCALE * LOG2E)).astype(BF16)
        kp_ref[:, sl] = (kp[:, sl] + krr).astype(BF16)


def _in_even(xs, mods, gains, layer, j, w_in, wkr2, qn, kvn, wuq2, wk, wv, cos_t, sin_t):
    pos_spec = pl.BlockSpec((TM, HEAD_SLAB), lambda i: (jnp.where(i < NT_P, 0, 1 + (i - NT_P) % TILES_PER_DEC), 0))
    return pl.pallas_call(
        _in_even_kernel,
        grid=(NT,),
        in_specs=[_ctx_row(D), _lat_row(D), _mod_spec(layer), _layer(layer, (1, D)), _layer(j, w_in.shape[1:]),
                  _full(wkr2.shape), _layer(j, (1, MLA_Q_RANK)), _layer(j, (1, MLA_KV_RANK)), _full(wuq2.shape),
                  _full(wk.shape), _full(wv.shape), pos_spec, pos_spec],
        out_specs=[_row(4 * RET_W), _row(MLA_QW), _row(MLA_QW), _row(MLA_VW), _ctx_row(MLA_KV_RANK),
                   _ctx_row(HEAD_SLAB)],
        out_shape=[
            jax.ShapeDtypeStruct((N_TOK, 4 * RET_W), BF16),
            jax.ShapeDtypeStruct((N_TOK, MLA_QW), BF16),
            jax.ShapeDtypeStruct((N_TOK, MLA_QW), BF16),
            jax.ShapeDtypeStruct((N_TOK, MLA_VW), BF16),
            jax.ShapeDtypeStruct((N_TOK_P, MLA_KV_RANK), F32),
            jax.ShapeDtypeStruct((N_TOK_P, HEAD_SLAB), F32),
        ],
        compiler_params=_cp("arbitrary"),
        name="in_even",
    )(*xs, mods, gains, w_in, wkr2, qn, kvn, wuq2, wk, wv, cos_t, sin_t)


def _mla_cache_kernel(ckv_ref, kr_ref, wk_ref, wv_ref, kp_ref, v_ref):
    cb = ckv_ref[...].astype(BF16)
    kp = _dot(cb, wk_ref[...])
    kr = kr_ref[...]
    for hh in range(MLA_HEADS):
        sl = slice(hh * HEAD_SLAB, (hh + 1) * HEAD_SLAB)
        kp_ref[:, sl] = (kp[:, sl] + kr).astype(BF16)
    v_ref[...] = _dot(cb, wv_ref[...]).astype(BF16)


def _mla_cache(ckv_c, kr_slab, wk, wv):
    row = lambda w: pl.BlockSpec((PAST_LEN, w), lambda b: (b, 0))
    return pl.pallas_call(
        _mla_cache_kernel,
        grid=(DEC_BATCH,),
        in_specs=[row(MLA_KV_RANK), row(HEAD_SLAB), _full(wk.shape), _full(wv.shape)],
        out_specs=[row(MLA_QW), row(MLA_VW)],
        out_shape=[jax.ShapeDtypeStruct((DEC_BATCH * PAST_LEN, MLA_QW), BF16),
                   jax.ShapeDtypeStruct((DEC_BATCH * PAST_LEN, MLA_VW), BF16)],
        compiler_params=_cp("parallel"),
        name="mla_cache",
    )(ckv_c, kr_slab, wk, wv)


def _ret_kernel(lg_ref, q_ref, k_ref, v_ref, g_ref, gn_ref, s0_ref, *rest, seq_len, n_blk, emit_state, n_prev):
    prev_refs, rest = rest[:min(n_prev, 1)], rest[min(n_prev, 1):]
    if emit_state:
        y_ref, st_ref, sf_scr, sb_scr, dm_scr, w_scr = rest
    else:
        y_ref, sf_scr, sb_scr, dm_scr, w_scr = rest
    C = RET_CHUNK
    nc = seq_len // C

    @pl.when(pl.program_id(1) == 0)
    def _():
        lg = -jnp.log(1.0 + jnp.exp(-lg_ref[...]))
        lg_f = lg[0]
        lg_b = lg[1]
        ii = lax.broadcasted_iota(jnp.int32, (C, C), 0)
        jj = lax.broadcasted_iota(jnp.int32, (C, C), 1)
        diff = (ii - jj).astype(F32)
        dm_scr[...] = (jnp.where(diff >= 0, jnp.exp(lg_f * jnp.maximum(diff, 0.0)), 0.0)
                       + jnp.where(diff <= 0, jnp.exp(lg_b * jnp.maximum(-diff, 0.0)), 0.0))
        pos = lax.broadcasted_iota(jnp.int32, (C, RET_DIM), 0).astype(F32)
        w_scr[0] = jnp.exp(lg_f * (pos + 1.0))
        w_scr[1] = jnp.exp(lg_f * (C - 1.0 - pos))
        w_scr[2] = jnp.exp(lg_b * (C - pos))
        w_scr[3] = jnp.exp(lg_b * pos)
        w_scr[4] = jnp.exp(lg_f * C) + jnp.zeros((C, RET_DIM), F32)
        w_scr[5] = jnp.exp(lg_b * C) + jnp.zeros((C, RET_DIM), F32)

    qw_f, kw_f, qw_b, kw_b = w_scr[0], w_scr[1], w_scr[2], w_scr[3]
    cd_f = w_scr[4, :RET_DIM, :]
    cd_b = w_scr[5, :RET_DIM, :]
    ld = lambda ref, rows: ref[rows, :].astype(F32)
    gn = gn_ref[...]
    dm = dm_scr[...]

    for b in range(n_blk):
        rows_of = lambda n: slice(b * seq_len + n * C, b * seq_len + (n + 1) * C)
        sf = lambda n: sf_scr.at[b * (nc + 1) + n]
        sb = lambda n: sb_scr.at[b * (nc + 1) + n]
        sf(0)[...] = s0_ref[b, 0]
        for n in range(nc):
            kv = _dot_tn((ld(k_ref, rows_of(n)) * kw_f).astype(BF16), v_ref[rows_of(n), :])
            sf(n + 1)[...] = cd_f * sf(n)[...] + kv
        sb(nc)[...] = s0_ref[b, 1]
        for n in reversed(range(nc)):
            kv = _dot_tn((ld(k_ref, rows_of(n)) * kw_b).astype(BF16), v_ref[rows_of(n), :])
            sb(n)[...] = cd_b * sb(n + 1)[...] + kv
        if emit_state:
            for jp in range(n_prev):
                st_ref[b, jp] = prev_refs[0][b, jp]
            st_ref[b, n_prev, 0] = sf(nc)[...]
            st_ref[b, n_prev, 1] = sb(0)[...]
        for n in range(nc):
            rows = rows_of(n)
            q = ld(q_ref, rows) * (RET_DIM ** -0.5)
            s = _dot_nt(q.astype(BF16), k_ref[rows, :]) * dm
            o = (_dot(s.astype(BF16), v_ref[rows, :])
                 + _dot((q * qw_f).astype(BF16), sf(n)[...].astype(BF16))
                 + _dot((q * qw_b).astype(BF16), sb(n + 1)[...].astype(BF16)))
            mu = jnp.mean(o, axis=-1, keepdims=True)
            oc = o - mu
            var = jnp.mean(oc * oc, axis=-1, keepdims=True)
            on = oc * lax.rsqrt(var + EPS) * gn
            y_ref[rows, :] = (_silu(ld(g_ref, rows)) * on).astype(BF16)


def _retention(qkvg, logit, gn, j, s0, *, seq_len, n_seq, n_blk, row0, emit_state, prev_states=None):
    n_prev = 0 if prev_states is None else prev_states.shape[1]
    nc = seq_len // RET_CHUNK
    rows = n_blk * seq_len
    blk0 = row0 // rows
    col = lambda part: pl.BlockSpec((rows, RET_DIM), lambda h, s: (blk0 + s, part * RET_HEADS + h))
    state = pl.BlockSpec((n_blk, 2, None, RET_DIM, RET_DIM), lambda h, s: (s, 0, h, 0, 0))
    in_specs = [
        pl.BlockSpec((None, 2, 1, 1), lambda h, s: (h, 0, 0, 0)),
        col(0), col(1), col(2), col(3),
        pl.BlockSpec((None, 1, RET_DIM), lambda h, s: (j, 0, h)),
        state,
    ]
    out_specs = [pl.BlockSpec((rows, RET_DIM), lambda h, s: (s, h))]
    out_shape = [jax.ShapeDtypeStruct((n_seq * seq_len, RET_W), BF16)]
    args = [logit, qkvg, qkvg, qkvg, qkvg, gn, s0]
    stacked = lambda n: pl.BlockSpec((n_blk, n, 2, None, RET_DIM, RET_DIM), lambda h, s: (s, 0, 0, h, 0, 0))
    if n_prev:
        in_specs.append(stacked(n_prev))
        args.append(prev_states)
    if emit_state:
        out_specs.append(stacked(n_prev + 1))
        out_shape.append(jax.ShapeDtypeStruct((n_seq, n_prev + 1, 2, RET_HEADS, RET_DIM, RET_DIM), F32))
    n_st = n_blk * (nc + 1)
    return pl.pallas_call(
        functools.partial(_ret_kernel, seq_len=seq_len, n_blk=n_blk, emit_state=emit_state, n_prev=n_prev),
        grid=(RET_HEADS, n_seq // n_blk),
        in_specs=in_specs,
        out_specs=out_specs,
        out_shape=out_shape,
        scratch_shapes=[pltpu.VMEM((n_st, RET_DIM, RET_DIM), F32), pltpu.VMEM((n_st, RET_DIM, RET_DIM), F32),
                        pltpu.VMEM((RET_CHUNK, RET_CHUNK), F32), pltpu.VMEM((6, RET_CHUNK, RET_DIM), F32)],
        compiler_params=_cp("parallel", "arbitrary"),
        name="retention_%d" % seq_len,
    )(*args)


def _softmax_pv(score_blocks, value_blocks):
    m = functools.reduce(jnp.maximum, [jnp.max(s, axis=-1, keepdims=True) for s in score_blocks])
    ps = [jnp.exp2(s - m) for s in score_blocks]
    l = functools.reduce(lambda a, b: a + b, [jnp.sum(p, axis=-1, keepdims=True) for p in ps])
    o = functools.reduce(lambda a, b: a + b, [_dot(p.astype(BF16), v) for p, v in zip(ps, value_blocks)])
    return o / l


def _low_half(shape):
    return lax.broadcasted_iota(jnp.int32, shape, 1) < 64


def _heads_pipelined(n_heads, scores, attend, y_ref, rows):
    low = _low_half((rows, 128))
    nxt = scores(0)
    outs = []
    for hh in range(n_heads):
        cur = nxt
        if hh + 1 < n_heads:
            nxt = scores(hh + 1)
        outs.append(attend(hh, cur))
        if hh % 2 == 1:
            hp = hh // 2
            y_ref[:, hp * 128:(hp + 1) * 128] = jnp.where(low, outs[hh - 1], outs[hh]).astype(BF16)


def _mla_ctx_kernel(q_ref, k_ref, v_ref, y_ref):
    slab = lambda hh: slice(hh * HEAD_SLAB, (hh + 1) * HEAD_SLAB)
    pair = lambda hh: slice(hh // 2 * 128, (hh // 2 + 1) * 128)
    scores = lambda hh: [_dot_nt(q_ref[:, slab(hh)], k_ref[:, slab(hh)])]
    attend = lambda hh, s: _softmax_pv(s, [v_ref[:, pair(hh)]])
    _heads_pipelined(MLA_HEADS, scores, attend, y_ref, SEQ)


def _mla_ctx(qp, kp, v):
    return pl.pallas_call(
        _mla_ctx_kernel,
        grid=(BATCH,),
        in_specs=[pl.BlockSpec((SEQ, MLA_QW), lambda b: (b, 0)), pl.BlockSpec((SEQ, MLA_QW), lambda b: (b, 0)),
                  pl.BlockSpec((SEQ, MLA_VW), lambda b: (b, 0))],
        out_specs=pl.BlockSpec((SEQ, MLA_VW), lambda b: (b, 0)),
        out_shape=jax.ShapeDtypeStruct((N_TOK_P, MLA_VW), BF16),
        compiler_params=_cp("parallel"),
        name="mla_ctx",
    )(qp, kp, v)


def _mla_lat_kernel(q_ref, k_ref, v_ref, kc_ref, vc_ref, wu32_ref, wd32_ref, y_ref, wu_ref, wd_ref):
    _cast_ffn_slab(wu32_ref, wd32_ref, wu_ref, wd_ref)
    slab = lambda hh: slice(hh * HEAD_SLAB, (hh + 1) * HEAD_SLAB)
    pair = lambda hh: slice(hh // 2 * 128, (hh // 2 + 1) * 128)
    scores = lambda hh: [_dot_nt(q_ref[:, slab(hh)], k_ref[:, slab(hh)]),
                         _dot_nt(q_ref[:, slab(hh)], kc_ref[:, slab(hh)])]
    attend = lambda hh, s: _softmax_pv(s, [v_ref[:, pair(hh)], vc_ref[:, pair(hh)]])
    _heads_pipelined(MLA_HEADS, scores, attend, y_ref, TQ)


def _mla_lat(qp, kp, v, kp_c, v_c, layer, w_up, w_down):
    seq_blk = N_TOK_P // DEC_SEQ
    nq = DEC_SEQ // TQ
    cast_in, cast_out, cast_shape = _cast_ffn_specs(layer, nq)
    return pl.pallas_call(
        _mla_lat_kernel,
        grid=(DEC_BATCH, nq),
        in_specs=[
            pl.BlockSpec((TQ, MLA_QW), lambda b, t: (N_TOK_P // TQ + b * nq + t, 0)),
            pl.BlockSpec((DEC_SEQ, MLA_QW), lambda b, t: (seq_blk + b, 0)),
            pl.BlockSpec((DEC_SEQ, MLA_VW), lambda b, t: (seq_blk + b, 0)),
            pl.BlockSpec((PAST_LEN, MLA_QW), lambda b, t: (b, 0)),
            pl.BlockSpec((PAST_LEN, MLA_VW), lambda b, t: (b, 0)),
        ] + cast_in,
        out_specs=[pl.BlockSpec((TQ, MLA_VW), lambda b, t: (b * nq + t, 0))] + cast_out,
        out_shape=[jax.ShapeDtypeStruct((N_TOK_S, MLA_VW), BF16)] + cast_shape,
        compiler_params=_cp("parallel", "parallel"),
        name="mla_lat",
    )(qp, kp, v, kp_c, v_c, w_up, w_down)


def _gelu_tanh(x):
    return 0.5 * x * (1.0 + jnp.tanh(np.sqrt(2.0 / np.pi).astype(np.float32) * (x + 0.044715 * (x * x * x))))


def _ffn_kernel(*refs, odd):
    xc_ref, xl_ref, xp_ref, xn_ref = refs[:4]
    if odd:
        (yr_ref, yrp_ref, yrn_ref, u_ref, up_ref, un_ref, bc_ref, bl_ref, bp_ref, bn_ref, mod_ref, gmix_ref, gpre_ref,
         gpost_ref, wo_ref, d_ref, gw_ref, gb_ref, wu_ref, cw_ref, cb_ref, wd_ref, oc_ref, ol_ref, h_scr,
         act_scr) = refs[4:]
    else:
        (ac_ref, al_ref, ap_ref, an_ref, bc_ref, bl_ref, bp_ref, bn_ref, mod_ref, gmix_ref, gpre_ref, gpost_ref,
         wo_ref, wu_ref, cw_ref, cb_ref, wd_ref, oc_ref, ol_ref, h_scr, act_scr) = refs[4:]
    i = pl.program_id(0)
    is_lat = i >= NT_P
    t = (i - NT_P) % TILES_PER_DEC
    has_prev = jnp.logical_and(is_lat, t != 0)
    has_next = jnp.logical_and(is_lat, t != TILES_PER_DEC - 1)
    mod = mod_ref[...]
    cat = lambda main, prev_ref, next_ref: jnp.concatenate([main, prev_ref[...], next_ref[...]], axis=0)

    if odd:
        y = _gelu_tanh(cat(yr_ref[...], yrp_ref, yrn_ref) + d_ref[...] * cat(u_ref[...], up_ref, un_ref))
        ya = (y * _sigmoid(_dot(y.astype(BF16), gw_ref[...]) + gb_ref[...])).astype(BF16)
    else:
        ya = cat(_pick(ac_ref, al_ref), ap_ref, an_ref)
    yb = cat(_pick(bc_ref, bl_ref), bp_ref, bn_ref)
    half = ya.shape[1]
    r = _dot(ya, wo_ref[:half, :]) + _dot(yb, wo_ref[half:, :])
    x1 = cat(_pick(xc_ref, xl_ref), xp_ref, xn_ref) + mod[:, 2 * D:3 * D] * _rms(r, gmix_ref[...])

    shift = mod[:, 3 * D:4 * D]
    scale = mod[:, 4 * D:5 * D]
    gate = mod[:, 5 * D:6 * D]
    hall = _rms(x1, gpre_ref[...]) * (1.0 + scale) + shift
    x = x1[:TM]
    h = hall[:TM]
    blk = FF_SUB + FF_EXT
    rows = FF_NSUB * blk
    for k in range(FF_NSUB):
        h_scr[k * blk:k * blk + FF_SUB, :] = h[k * FF_SUB:(k + 1) * FF_SUB].astype(BF16)
        if k + 1 < FF_NSUB:
            after = jnp.where(is_lat, h[(k + 1) * FF_SUB:(k + 1) * FF_SUB + 8], 0.0)
            before = jnp.where(is_lat, h[(k + 1) * FF_SUB - 8:(k + 1) * FF_SUB], 0.0)
        else:
            after = jnp.where(has_next, hall[TM + HALO:TM + HALO + 8], 0.0)
            before = jnp.where(has_prev, hall[TM + HALO - 8:TM + HALO], 0.0)
        h_scr[k * blk + FF_SUB:(k + 1) * blk, :] = jnp.concatenate([after, before], axis=0).astype(BF16)
    hb = h_scr[...]

    def up(j):
        ca = slice(j * FF_CHUNK, (j + 1) * FF_CHUNK)
        cg = slice(D_FF + j * FF_CHUNK, D_FF + (j + 1) * FF_CHUNK)
        return (_dot(hb, wu_ref[:, ca]), ca), (_dot(hb, wu_ref[:, cg]), cg)

    def conv(part):
        u, cols = part
        cw = cw_ref[:, cols]
        return (cw[0:1, :] * pltpu.roll(u, 1, axis=0) + cw[1:2, :] * u + cw[2:3, :] * pltpu.roll(u, rows - 1, axis=0)
                + cb_ref[:, cols])

    nxt = up(0)
    for j in range(FF_NCHUNK):
        cur = nxt
        if j + 1 < FF_NCHUNK:
            nxt = up(j + 1)
        act = (_silu(conv(cur[1])) * conv(cur[0])).astype(BF16)
        for k in range(FF_NSUB):
            act_scr[k * FF_SUB:(k + 1) * FF_SUB, j * FF_CHUNK:(j + 1) * FF_CHUNK] = act[k * blk:k * blk + FF_SUB]
    out = x + gate * _rms(_dot(act_scr[...], wd_ref[...]), gpost_ref[...])

    @pl.when(i < NT_P)
    def _():
        oc_ref[...] = out

    @pl.when(i >= NT_P)
    def _():
        ol_ref[...] = out


def _halo_specs(width, n_rows, tile0):
    per = TM // HALO
    last = n_rows // HALO - 1
    prev = pl.BlockSpec((HALO, width), lambda i: (jnp.clip((i - tile0) * per - 1, 0, last), 0))
    nxt = pl.BlockSpec((HALO, width), lambda i: (jnp.clip((i - tile0 + 1) * per, 0, last), 0))
    return [prev, nxt]


def _ffn(xs, mix, mods, gmix, gpre, gpost, layer, j, w_out, wu, cw, cb, wd, s5=None):
    odd = s5 is not None
    pair = lambda w: [_ctx_row(w), _lat_row(w)]
    lat_halo = lambda w: _halo_specs(w, N_TOK_S, NT_P)
    all_halo = lambda w: _halo_specs(w, N_TOK, 0)
    resident = lambda shape: pl.BlockSpec(shape, lambda i: (0, 0), pipeline_mode=pl.Buffered(1))
    x_specs = pair(D) + lat_halo(D)
    x_args = [xs[0], xs[1], xs[1], xs[1]]
    if odd:
        yr, u, b_c, b_l = mix
        mix_specs = [_row(S5_W)] + all_halo(S5_W) + [_row(S5_W)] + all_halo(S5_W) + pair(NA_W) + lat_halo(NA_W)
        mix_args = [yr, yr, yr, u, u, u, b_c, b_l, b_l, b_l]
        s5_specs = [_layer(j, (1, S5_W)), _layer(j, (S5_W, S5_W)), _layer(j, (1, S5_W))]
        s5_args = list(s5)
    else:
        a_c, a_l, b_c, b_l = mix
        mix_specs = pair(RET_W) + lat_halo(RET_W) + pair(MLA_VW) + lat_halo(MLA_VW)
        mix_args = [a_c, a_l, a_l, a_l, b_c, b_l, b_l, b_l]
        s5_specs, s5_args = [], []
    return pl.pallas_call(
        functools.partial(_ffn_kernel, odd=odd),
        grid=(NT,),
        in_specs=x_specs + mix_specs + [_mod_spec(layer), _layer(layer, (1, D)), _layer(layer, (1, D)),
                                        _layer(layer, (1, D)), _layer(j, (D, D))] + s5_specs
        + [resident((D, 2 * D_FF)), _layer(layer, (3, 2 * D_FF)), _layer(layer, (1, 2 * D_FF)), resident((D_FF, D))],
        out_specs=[_ctx_row(D), _lat_row(D)],
        out_shape=[jax.ShapeDtypeStruct((N_TOK_P, D), F32), jax.ShapeDtypeStruct((N_TOK_S, D), F32)],
        scratch_shapes=[pltpu.VMEM((FF_NSUB * (FF_SUB + FF_EXT), D), BF16), pltpu.VMEM((TM, D_FF), BF16)],
        compiler_params=_cp("arbitrary"),
        name="ffn_odd" if odd else "ffn_even",
    )(*x_args, *mix_args, mods, gmix, gpre, gpost, w_out, *s5_args, wu, cw, cb, wd)


def _in_odd_kernel(xc_ref, xl_ref, mod_ref, g_ref, w_ref, *rest, n_prev):
    np2 = 2 * min(n_prev, 1)
    prev_refs, (u_ref, qkv_ref, kc_ref, vc_ref) = rest[:np2], rest[np2:]
    mod = mod_ref[...]
    h = _rms(_pick(xc_ref, xl_ref), g_ref[...]) * (1.0 + mod[:, D:2 * D]) + mod[:, :D]
    r = _dot(h.astype(BF16), w_ref[...])
    u_ref[...] = r[:, :S5_W]
    qkv_ref[:, :NA_W] = (r[:, S5_W:S5_W + NA_W] * (NA_SCALE * LOG2E)).astype(BF16)
    qkv_ref[:, NA_W:] = r[:, S5_W + NA_W:].astype(BF16)

    @pl.when(pl.program_id(0) < NT_P)
    def _():
        for part, (out_ref, col0) in enumerate([(kc_ref, S5_W + NA_W), (vc_ref, S5_W + 2 * NA_W)]):
            for b in range(TM // SEQ):
                for jp in range(n_prev):
                    out_ref[b, jp] = prev_refs[part][b, jp]
                out_ref[b, n_prev] = r[b * SEQ:(b + 1) * SEQ, col0:col0 + NA_W]


def _in_odd(xs, mods, gains, layer, j, w, prev_kv=None):
    n_prev = 0 if prev_kv is None else prev_kv[0].shape[1]
    nb = TM // SEQ
    stacked = lambda n: pl.BlockSpec((nb, n, SEQ, NA_W), lambda i: (jnp.minimum(i, NT_P - 1), 0, 0, 0))
    leaf = jax.ShapeDtypeStruct((BATCH, n_prev + 1, SEQ, NA_W), F32)
    prev_specs = [stacked(n_prev)] * 2 if n_prev else []
    return pl.pallas_call(
        functools.partial(_in_odd_kernel, n_prev=n_prev),
        grid=(NT,),
        in_specs=[_ctx_row(D), _lat_row(D), _mod_spec(layer), _layer(layer, (1, D)), _layer(j, w.shape[1:])]
        + prev_specs,
        out_specs=[_row(S5_W), _row(3 * NA_W), stacked(n_prev + 1), stacked(n_prev + 1)],
        out_shape=[jax.ShapeDtypeStruct((N_TOK, S5_W), F32), jax.ShapeDtypeStruct((N_TOK, 3 * NA_W), BF16), leaf, leaf],
        compiler_params=_cp("arbitrary"),
        name="in_odd",
    )(*xs, mods, gains, w, *(prev_kv or ()))


S5_TS = S5_CHUNK * S5_GROUP
S5_PL = 2 * S5_P
S5_PREP_PAIRS = 2


def _s5_prep_kernel(*refs):
    ins, outs = refs[:7], refs[7:]
    for q in range(S5_PREP_PAIRS):
        _s5_prep_pair(*[r.at[:, q] for r in ins], *[r.at[q] for r in outs])


def _s5_prep_pair(lre_ref, lim_ref, ls_ref, btr_ref, bti_ref, cr_ref, ci_ref, m_ref, n_ref, p_ref, a_ref, ct_scr):
    T = S5_CHUNK
    S = S5_GROUP
    hi = lax.Precision.HIGHEST
    low = lax.broadcasted_iota(jnp.int32, (S, S5_PL), 1) < S5_P
    half = [low, jnp.logical_not(low)]
    pick = lambda e, v: jnp.where(half[e], v, 0.0)
    nt = (((1,), (1,)), ((), ()))
    kps = [[None, None], [None, None]]
    for d in range(2):
        lre = lre_ref[d]
        lim = lim_ref[d]
        step = jnp.exp(ls_ref[d])
        mag = jnp.exp(lre * step)
        are = mag * jnp.cos(lim * step)
        aim = mag * jnp.sin(lim * step)
        den = lre * lre + lim * lim
        zr, zi = _cmul(are - 1.0, aim, lre / den, -lim / den)
        bbr, bbi = _cmul(zr, zi, btr_ref[d], bti_ref[d])
        cr = cr_ref[d]
        ci = ci_ref[d]
        pr = jnp.ones_like(are)
        pi = jnp.zeros_like(are)
        for k in range(T + 1):
            er, ei = _cmul(cr, ci, pr, pi)
            if k < T:
                jn = T - 1 - k if d == 0 else k
                wr, wi = _cmul(pr, pi, bbr, bbi)
                for e in range(2):
                    rows = slice(e * S5_TS + jn * S, e * S5_TS + (jn + 1) * S)
                    n_ref[d, rows, 0:S5_PL] = pick(e, wr).astype(BF16)
                    n_ref[d, rows, S5_PL:2 * S5_PL] = pick(e, wi).astype(BF16)
                jc = k if d == 0 else T - 1 - k
                ct_scr[0, jc * S:(jc + 1) * S, :] = er
                ct_scr[1, jc * S:(jc + 1) * S, :] = ei
            if k >= 1:
                t = k - 1 if d == 0 else T - k
                for e in range(2):
                    rows = slice(e * S5_TS + t * S, e * S5_TS + (t + 1) * S)
                    p_ref[rows, 2 * d * S5_PL:(2 * d + 1) * S5_PL] = pick(e, er).astype(BF16)
                    p_ref[rows, (2 * d + 1) * S5_PL:(2 * d + 2) * S5_PL] = pick(e, -ei).astype(BF16)
            if k == T:
                a_ref[d, 0] = pr
                a_ref[d, 1] = pi
            pr, pi = _cmul(pr, pi, are, aim)
        for e in range(2):
            kd = (lax.dot_general(pick(e, bbr), ct_scr[0], nt, precision=hi, preferred_element_type=F32)
                  - lax.dot_general(pick(e, bbi), ct_scr[1], nt, precision=hi, preferred_element_type=F32))
            kps[d][e] = jnp.concatenate([kd, jnp.zeros_like(kd)], axis=1)
    for e in range(2):
        for t in range(T):
            fwd = pltpu.roll(kps[0][e], t * S, axis=1)[:, :S5_TS]
            bwd = pltpu.roll(kps[1][e], (2 * S5_TS - (T - 1 - t) * S) % (2 * S5_TS), axis=1)[:, :S5_TS]
            m_ref[e, t * S:(t + 1) * S, :] = (fwd + bwd).astype(BF16)


def _s5_prep(lre, lim, ls, b_re, b_im, c_re, c_im):
    NP, S, P = S5_PAIRS, S5_GROUP, S5_P
    vec = lambda a: a.reshape(2, NP, 1, S5_PL)
    b_lay = lambda a: jnp.transpose(a.reshape(2, NP, 2, P, S), (0, 1, 4, 2, 3)).reshape(2, NP, S, S5_PL)
    c_lay = lambda a: jnp.transpose(a.reshape(2, NP, 2, S, P), (0, 1, 3, 2, 4)).reshape(2, NP, S, S5_PL)
    args = (vec(lre), vec(lim), vec(jnp.repeat(ls, P, axis=-1)), b_lay(b_re), b_lay(b_im), c_lay(c_re), c_lay(c_im))
    PB = S5_PREP_PAIRS
    vspec = pl.BlockSpec((2, PB, 1, S5_PL), lambda g: (0, g, 0, 0))
    mspec = pl.BlockSpec((2, PB, S, S5_PL), lambda g: (0, g, 0, 0))
    lead = lambda shape: pl.BlockSpec((PB,) + shape, lambda g: (g,) + (0,) * len(shape))
    return pl.pallas_call(
        _s5_prep_kernel,
        grid=(NP // PB,),
        in_specs=[vspec, vspec, vspec, mspec, mspec, mspec, mspec],
        out_specs=[lead((2, S5_TS, S5_TS)), lead((2, 2 * S5_TS, 2 * S5_PL)), lead((2 * S5_TS, 4 * S5_PL)),
                   lead((2, 2, 1, S5_PL))],
        out_shape=[jax.ShapeDtypeStruct((NP, 2, S5_TS, S5_TS), BF16),
                   jax.ShapeDtypeStruct((NP, 2, 2 * S5_TS, 2 * S5_PL), BF16),
                   jax.ShapeDtypeStruct((NP, 2 * S5_TS, 4 * S5_PL), BF16),
                   jax.ShapeDtypeStruct((NP, 2, 2, 1, S5_PL), F32)],
        scratch_shapes=[pltpu.VMEM((S5_PREP_PAIRS, 2, S5_TS, S5_PL), F32)],
        compiler_params=_cp("parallel"),
        name="s5_prep",
    )(*args)


S5_NCH = N_TOK // S5_CHUNK
S5_ROWS_P = N_TOK_P // S5_CHUNK
S5_NC_P = SEQ // S5_CHUNK
S5_NC_S = DEC_SEQ // S5_CHUNK
S5_GPB = 8
S5_PPB = S5_GPB // 2
S5_XL = 8 * 128


def _s5_perm():
    r = np.arange(S5_XL)
    dst = (r // S5_GROUP % S5_GPB) * 128 + (r // 128) * S5_GROUP + r % S5_GROUP
    perm = np.zeros((S5_XL, S5_XL), np.float32)
    perm[r, dst] = 1.0
    return jnp.asarray(perm, BF16)


def _s5_kernel(u_ref, perm_ref, m_ref, n_ref, p_ref, a_ref, h0c_ref, h0l_ref, y_ref, fin_ref, z_scr, up_scr, e_scr,
               hin_scr, yc_scr):
    T = S5_CHUNK
    W = S5_PL
    for t in range(T):
        z_scr[t // 8, :, (t % 8) * 128:(t % 8 + 1) * 128] = u_ref[pl.ds(t, S5_NCH, stride=T), :].astype(BF16)
    perm = perm_ref[...]
    for j in range(2):
        up_scr[j] = _dot(z_scr[j], perm).astype(BF16)

    def scan(pp, d, h0_ref, n_chunks, n_seq, row0):
        are = a_ref[pp, d, 0]
        aim = a_ref[pp, d, 1]
        hr = h0_ref[pp, d, 0]
        hi = h0_ref[pp, d, 1]
        order = range(n_chunks) if d == 0 else reversed(range(n_chunks))
        for c in order:
            rows = pl.ds(row0 + c, n_seq, stride=n_chunks)
            hin_scr[2 * d, rows, :] = hr
            hin_scr[2 * d + 1, rows, :] = hi
            er = e_scr[2 * d, rows, :]
            ei = e_scr[2 * d + 1, rows, :]
            hr, hi = are * hr - aim * hi + er, are * hi + aim * hr + ei
        return hr, hi

    for pp in range(S5_PPB):
        us = []
        for e in range(2):
            sl = slice((2 * pp + e) * 128, (2 * pp + e + 1) * 128)
            us.append(jnp.concatenate([up_scr[0, :, sl], up_scr[1, :, sl]], axis=1))
        u2 = jnp.concatenate(us, axis=1)
        for d in range(2):
            ed = _dot(u2, n_ref[pp, d])
            e_scr[2 * d] = ed[:, :W]
            e_scr[2 * d + 1] = ed[:, W:]
        for d in range(2):
            hr, hi = scan(pp, d, h0c_ref, S5_NC_P, BATCH, 0)
            fin_ref[pp, d, 0] = hr
            fin_ref[pp, d, 1] = hi
            scan(pp, d, h0l_ref, S5_NC_S, DEC_BATCH, S5_ROWS_P)
        hin = jnp.concatenate([hin_scr[k] for k in range(4)], axis=1).astype(BF16)
        for e in range(2):
            y = _dot(us[e], m_ref[pp, e]) + _dot_nt(hin, p_ref[pp, e * S5_TS:(e + 1) * S5_TS, :])
            sl = slice((2 * pp + e) * 128, (2 * pp + e + 1) * 128)
            for j in range(2):
                yc_scr[j, :, sl] = y[:, j * 128:(j + 1) * 128]

    for j in range(2):
        yp = yc_scr[j]
        y_hi = yp.astype(BF16)
        y_lo = (yp - y_hi.astype(F32)).astype(BF16)
        r = _dot_nt(y_hi, perm) + _dot_nt(y_lo, perm)
        for k in range(8):
            y_ref[pl.ds(8 * j + k, S5_NCH, stride=T), :] = r[:, k * 128:(k + 1) * 128]


def _s5_scan(u, perm, m, n2, p2, a, h0c, h0l):
    nb = S5_GROUPS // S5_GPB
    lead = lambda shape: pl.BlockSpec((S5_PPB,) + shape, lambda w: (w,) + (0,) * len(shape))
    col = pl.BlockSpec((N_TOK, 128), lambda w: (0, w))
    return pl.pallas_call(
        _s5_kernel,
        grid=(nb,),
        in_specs=[col, _full(perm.shape), lead((2, S5_TS, S5_TS)), lead((2, 2 * S5_TS, 2 * S5_PL)),
                  lead((2 * S5_TS, 4 * S5_PL)), lead((2, 2, 1, S5_PL)), lead((2, 2, BATCH, S5_PL)),
                  lead((2, 2, DEC_BATCH, S5_PL))],
        out_specs=[col, lead((2, 2, BATCH, S5_PL))],
        out_shape=[jax.ShapeDtypeStruct((N_TOK, S5_W), F32),
                   jax.ShapeDtypeStruct((S5_PAIRS, 2, 2, BATCH, S5_PL), F32)],
        scratch_shapes=[pltpu.VMEM((2, S5_NCH, S5_XL), BF16), pltpu.VMEM((2, S5_NCH, S5_XL), BF16),
                        pltpu.VMEM((4, S5_NCH, S5_PL), F32), pltpu.VMEM((4, S5_NCH, S5_PL), F32),
                        pltpu.VMEM((2, S5_NCH, S5_XL), F32)],
        compiler_params=_cp("parallel"),
        name="s5_scan",
    )(u, perm, m, n2, p2, a, h0c, h0l)


def _na_heads(q_ref, keys, values, y_ref, rows, bias=None):
    low = _low_half((rows, 128))
    pair = lambda hh: slice(hh // 2 * 128, (hh // 2 + 1) * 128)

    def scores(hh):
        q = q_ref[:, pair(hh)]
        qm = jnp.where(low == (hh % 2 == 0), q, jnp.zeros_like(q))
        s = [_dot_nt(qm, k(pair(hh))) for k in keys]
        if bias is not None:
            s[0] = s[0] + bias(hh)
        return s

    attend = lambda hh, s: _softmax_pv(s, [v(pair(hh)) for v in values])
    _heads_pipelined(NA_HEADS, scores, attend, y_ref, rows)


def _na_ctx_kernel(q_ref, k_ref, v_ref, y_ref):
    _na_heads(q_ref, [lambda sl: k_ref[:, sl]], [lambda sl: v_ref[:, sl]], y_ref, SEQ)


def _na_ctx(qkv):
    col = lambda part: pl.BlockSpec((SEQ, NA_W), lambda b: (b, part))
    return pl.pallas_call(
        _na_ctx_kernel,
        grid=(BATCH,),
        in_specs=[col(0), col(1), col(2)],
        out_specs=pl.BlockSpec((SEQ, NA_W), lambda b: (b, 0)),
        out_shape=jax.ShapeDtypeStruct((N_TOK_P, NA_W), BF16),
        compiler_params=_cp("parallel"),
        name="na_ctx",
    )(qkv, qkv, qkv)


def _na_key_row0(rb):
    return jnp.clip(NA_QROWS * rb - NA_WIN_R // 2, 0, GRID_H - NA_KROWS)


def _na_lat_kernel(q_ref, ks_ref, vs_ref, kc_ref, vc_ref, tab_ref, wu32_ref, wd32_ref, y_ref, wu_ref, wd_ref):
    _cast_ffn_slab(wu32_ref, wd32_ref, wu_ref, wd_ref)
    rb = pl.program_id(1)
    u0 = _na_key_row0(rb)
    start = pl.multiple_of(u0 * GRID_W, GRID_W)
    nk = NA_KROWS * GRID_W
    low_t = _low_half((GRID_W, 128))

    def table_row(i, w):
        qr = NA_QROWS * rb + i
        kr = u0 + w
        rs = jnp.clip(qr - NA_WIN_R // 2, 0, GRID_H - NA_WIN_R)
        inside = jnp.logical_and(kr >= rs, kr < rs + NA_WIN_R)
        return jnp.where(inside, kr - qr + NA_WIN_R - 1, NA_NDR)

    idx = [[table_row(i, w) for w in range(NA_KROWS)] for i in range(NA_QROWS)]

    def bias(h):
        rows = [jnp.concatenate([jnp.where(low_t, tab_ref[h, idx[i][w]], tab_ref[h, idx[i][w + 1]])
                                 for w in range(0, NA_KROWS, 2)], axis=1) for i in range(NA_QROWS)]
        return jnp.concatenate(rows, axis=0)

    keys = [lambda sl: ks_ref[pl.ds(start, nk), sl], lambda sl: kc_ref[:, sl].astype(BF16)]
    values = [lambda sl: vs_ref[pl.ds(start, nk), sl], lambda sl: vc_ref[:, sl].astype(BF16)]
    _na_heads(q_ref, keys, values, y_ref, TQ, bias)


def _na_lat(qkv, k_c, v_c, table, layer, w_up, w_down):
    seq_blk = N_TOK_P // DEC_SEQ
    nb = GRID_H // NA_QROWS
    cast_in, cast_out, cast_shape = _cast_ffn_specs(layer, nb)
    return pl.pallas_call(
        _na_lat_kernel,
        grid=(DEC_BATCH, nb),
        in_specs=[
            pl.BlockSpec((TQ, NA_W), lambda b, r: (N_TOK_P // TQ + b * nb + r, 0)),
            pl.BlockSpec((DEC_SEQ, NA_W), lambda b, r: (seq_blk + b, 1)),
            pl.BlockSpec((DEC_SEQ, NA_W), lambda b, r: (seq_blk + b, 2)),
            pl.BlockSpec((PAST_LEN, NA_W), lambda b, r: (b, 0)),
            pl.BlockSpec((PAST_LEN, NA_W), lambda b, r: (b, 0)),
            pl.BlockSpec(table.shape, lambda b, r: (0, 0, 0, 0)),
        ] + cast_in,
        out_specs=[pl.BlockSpec((TQ, NA_W), lambda b, r: (b * nb + r, 0))] + cast_out,
        out_shape=[jax.ShapeDtypeStruct((N_TOK_S, NA_W), BF16)] + cast_shape,
        compiler_params=_cp("parallel", "arbitrary"),
        name="na_lat",
    )(qkv, qkv, qkv, k_c, v_c, table, w_up, w_down)


def _na_table_kernel(rpb_ref, t_ref):
    qc = lax.broadcasted_iota(jnp.int32, (GRID_W, 128), 0)
    kc = lax.broadcasted_iota(jnp.int32, (GRID_W, 128), 1) % GRID_W
    cs = jnp.clip(qc - NA_WIN_C // 2, 0, GRID_W - NA_WIN_C)
    in_band = jnp.logical_and(kc >= cs, kc < cs + NA_WIN_C)
    neg = jnp.full((GRID_W, 128), -jnp.inf, F32)

    def body(n, carry):
        x = jnp.broadcast_to(rpb_ref[n], (GRID_W, 128))
        t = pltpu.roll(x, 128 - (NA_WIN_C - 1), axis=1, stride=1, stride_axis=0)
        t_ref[n // NA_NDR, n % NA_NDR] = jnp.where(in_band, t * LOG2E, neg)
        return carry

    lax.fori_loop(0, NA_HEADS * NA_NDR, body, 0, unroll=8)
    for h in range(NA_HEADS):
        t_ref[h, NA_NDR] = neg


def _na_table(rpb):
    rows = jnp.pad(rpb.reshape(NA_HEADS * NA_NDR, 1, NA_NDC), ((0, 0), (0, 0), (0, GRID_W - NA_NDC)))
    rows = jnp.concatenate([rows, rows], axis=-1)
    return pl.pallas_call(
        _na_table_kernel,
        out_shape=jax.ShapeDtypeStruct((NA_HEADS, NA_NDR + 1, GRID_W, 128), F32),
        name="na_table",
    )(rows)


def _rope_tables():
    n_freq = MLA_ROPE // 4
    inv = ROPE_BASE ** (-jnp.arange(n_freq, dtype=F32) / n_freq)
    t = jnp.arange(DEC_SEQ)
    row = (t // GRID_W).astype(F32)
    colp = (t % GRID_W).astype(F32)
    ang = jnp.concatenate([row[:, None] * inv, colp[:, None] * inv], axis=-1)
    cos, sin = jnp.cos(ang), jnp.sin(ang)
    one = jnp.ones((DEC_SEQ, MLA_NOPE), F32)
    zero = jnp.zeros((DEC_SEQ, MLA_NOPE), F32)
    cos_s = jnp.concatenate([one, cos, cos, one[:, :32]], axis=-1)
    sin_s = jnp.concatenate([zero, -sin, sin, zero[:, :32]], axis=-1)
    cos_t = jnp.concatenate([jnp.ones((TM, HEAD_SLAB), F32), cos_s], axis=0)
    sin_t = jnp.concatenate([jnp.zeros((TM, HEAD_SLAB), F32), sin_s], axis=0)
    return cos_t, sin_t


def _mla_weights(w_in, w_uq, w_ukv):
    half = MLA_ROPE // 2
    wkr = w_in[:, 4 * RET_W + MLA_Q_RANK + MLA_KV_RANK:]
    z64 = jnp.zeros((D, MLA_NOPE), F32)
    z32 = jnp.zeros((D, HEAD_SLAB - MLA_NOPE - MLA_ROPE), F32)
    wkr2 = jnp.concatenate([z64, wkr, z32, z64, wkr[:, half:], wkr[:, :half], z32], axis=1).astype(BF16)
    wq = w_uq.reshape(MLA_Q_RANK, MLA_HEADS, MLA_NOPE + MLA_ROPE)
    nope, rope = wq[..., :MLA_NOPE], wq[..., MLA_NOPE:]
    zq64 = jnp.zeros_like(nope)
    zq32 = jnp.zeros_like(rope)
    q_slab = jnp.concatenate([nope, rope, zq32], axis=-1).reshape(MLA_Q_RANK, MLA_QW)
    q_sw = jnp.concatenate([zq64, rope[..., half:], rope[..., :half], zq32], axis=-1).reshape(MLA_Q_RANK, MLA_QW)
    wuq2 = jnp.concatenate([q_slab, q_sw], axis=1).astype(BF16)
    wkv = w_ukv.reshape(MLA_KV_RANK, MLA_HEADS, MLA_NOPE + MLA_V)
    wk = jnp.concatenate([wkv[..., :MLA_NOPE], jnp.zeros_like(wkv[..., :MLA_NOPE])], axis=-1)
    wk = wk.reshape(MLA_KV_RANK, MLA_QW).astype(BF16)
    wv = wkv[..., MLA_NOPE:].reshape(MLA_KV_RANK, MLA_VW).astype(BF16)
    return wkr2, wuq2, wk, wv


def kernel(x_prompt, x_sample, c, state_ret, cache_mla_ckv, cache_mla_krope, state_s5_re, state_s5_im, cache_na_k, cache_na_v, c_ctx, ada_w, ada_b, mix_pre_g, mix_post_g, ffn_pre_g, ffn_post_g, ffn_w_up, ffn_conv_w, ffn_conv_b, ffn_w_down, even_w_in, even_w_out, ret_logit, ret_gn, mla_q_norm, mla_w_uq, mla_kv_norm, mla_w_ukv, odd_w_in, odd_w_out, s5_lambda_re, s5_lambda_im, s5_log_step, s5_b_re, s5_b_im, s5_c_re, s5_c_im, s5_d, s5_glu_w, s5_glu_b, na_rpb):
    cvec = jnp.concatenate([c_ctx[None, :], c, jnp.zeros((8 - 1 - DEC_BATCH, D), F32)], axis=0)
    mods = _ada_mods(cvec, ada_w, ada_b)
    row3 = lambda a: a.reshape(a.shape[0], 1, a.shape[1])
    mix_pre, mix_post, ffn_pre, ffn_post = row3(mix_pre_g), row3(mix_post_g), row3(ffn_pre_g), row3(ffn_post_g)
    conv_b = row3(ffn_conv_b)
    e_in, e_out = even_w_in.astype(BF16), even_w_out.astype(BF16)
    o_in, o_out, glu_w = odd_w_in.astype(BF16), odd_w_out.astype(BF16), s5_glu_w.astype(BF16)
    cos_t, sin_t = _rope_tables()
    perm = _s5_perm()
    xs = (x_prompt.reshape(N_TOK_P, D), x_sample.reshape(N_TOK_S, D))
    ret_states, na_kv = None, None
    new_ckv, new_kr, new_s5_re, new_s5_im = [], [], [], []
    for layer in range(DEPTH):
        j = layer // 2
        if layer % 2 == 0:
            wkr2, wuq2, wk, wv = _mla_weights(even_w_in[j], mla_w_uq[j], mla_w_ukv[j])
            qkvg, qp, kp, v, ckvn, kr = _in_even(xs, mods, mix_pre, layer, j, e_in, wkr2, row3(mla_q_norm),
                                                 row3(mla_kv_norm), wuq2, wk, wv, cos_t, sin_t)
            logit = jnp.transpose(ret_logit[j]).reshape(RET_HEADS, 2, 1, 1)
            gn = row3(ret_gn)
            s0 = jnp.zeros((BATCH, 2, RET_HEADS, RET_DIM, RET_DIM), F32)
            yr_c, ret_states = _retention(qkvg, logit, gn, j, s0, seq_len=SEQ, n_seq=BATCH, n_blk=4, row0=0,
                                          emit_state=True, prev_states=ret_states)
            (yr_l,) = _retention(qkvg, logit, gn, j, state_ret[:, j], seq_len=DEC_SEQ, n_seq=DEC_BATCH, n_blk=1,
                                 row0=N_TOK_P, emit_state=False)
            ym_c = _mla_ctx(qp, kp, v)
            kr_c = jnp.pad(cache_mla_krope[:, j].reshape(DEC_BATCH * PAST_LEN, MLA_ROPE),
                           ((0, 0), (MLA_NOPE, HEAD_SLAB - MLA_NOPE - MLA_ROPE)))
            kp_c, v_c = _mla_cache(cache_mla_ckv[:, j].reshape(DEC_BATCH * PAST_LEN, MLA_KV_RANK), kr_c, wk, wv)
            ym_l, w_up, w_down = _mla_lat(qp, kp, v, kp_c, v_c, layer, ffn_w_up, ffn_w_down)
            mix, w_out, s5 = (yr_c, yr_l, ym_c, ym_l), e_out, None
            new_ckv.append(ckvn.reshape(BATCH, SEQ, MLA_KV_RANK))
            new_kr.append(kr[:, MLA_NOPE:MLA_NOPE + MLA_ROPE].reshape(BATCH, SEQ, MLA_ROPE))
        else:
            u, qkv, *na_kv = _in_odd(xs, mods, mix_pre, layer, j, o_in, prev_kv=na_kv)
            m, n2, p2, a = _s5_prep(s5_lambda_re[j], s5_lambda_im[j], s5_log_step[j], s5_b_re[j], s5_b_im[j],
                                    s5_c_re[j], s5_c_im[j])
            h0c = jnp.zeros((S5_PAIRS, 2, 2, BATCH, S5_PL), F32)
            h0 = jnp.stack([state_s5_re[:, j], state_s5_im[:, j]], axis=0)
            h0l = jnp.transpose(h0.reshape(2, DEC_BATCH, 2, S5_PAIRS, S5_PL), (3, 2, 0, 1, 4))
            y_raw, fin = _s5_scan(u, perm, m, n2, p2, a, h0c, h0l)
            yn_c = _na_ctx(qkv)
            yn_l, w_up, w_down = _na_lat(qkv, cache_na_k[:, j].reshape(DEC_BATCH * PAST_LEN, NA_W),
                                         cache_na_v[:, j].reshape(DEC_BATCH * PAST_LEN, NA_W), _na_table(na_rpb[j]),
                                         layer, ffn_w_up, ffn_w_down)
            mix, w_out, s5 = (y_raw, u, yn_c, yn_l), o_out, (row3(s5_d), glu_w, row3(s5_glu_b))
            st = jnp.transpose(fin.reshape(S5_PAIRS, 2, 2, BATCH, 2, S5_P), (2, 3, 1, 0, 4, 5))
            st = st.reshape(2, BATCH, 2, S5_GROUPS, S5_P)
            new_s5_re.append(st[0])
            new_s5_im.append(st[1])
        xs = _ffn(xs, mix, mods, mix_post, ffn_pre, ffn_post, layer, j, w_out, w_up, ffn_conv_w, conv_b, w_down, s5)
    stack = lambda a: jnp.stack(a, axis=1)
    heads = lambda a: a.reshape(BATCH, a.shape[1], SEQ, NA_HEADS, NA_DIM)
    return (xs[0].reshape(BATCH, SEQ, D), xs[1].reshape(DEC_BATCH, DEC_SEQ, D), ret_states, stack(new_ckv),
            stack(new_kr), stack(new_s5_re), stack(new_s5_im), heads(na_kv[0]), heads(na_kv[1]))
```

```python
import functools

import numpy as np
import jax
import jax.numpy as jnp
from jax import lax
from jax.experimental import pallas as pl
from jax.experimental.pallas import tpu as pltpu

F32 = jnp.float32
BF16 = jnp.bfloat16

D = 1024
BATCH = 16
SEQ = 256
DEPTH = 4
DEC_BATCH = 2
DEC_SEQ = 2048
PAST_LEN = 512
GRID_W = 64
GRID_H = DEC_SEQ // GRID_W
EPS = 1e-6
LOG2E = 1.4426950408889634

RET_HEADS = 4
RET_W = 512
RET_DIM = 128
RET_CHUNK = 256

MLA_HEADS = 8
MLA_NOPE = 64
MLA_ROPE = 32
MLA_V = 64
MLA_Q_RANK = 256
MLA_KV_RANK = 128
MLA_SCALE = (MLA_NOPE + MLA_ROPE) ** -0.5
ROPE_BASE = 10000.0
HEAD_SLAB = 128
MLA_QW = MLA_HEADS * HEAD_SLAB
MLA_VW = MLA_HEADS * MLA_V

S5_W = 512
S5_GROUP = 16
S5_GROUPS = 32
S5_P = 64
S5_CHUNK = 16
S5_PAIRS = S5_GROUPS // 2

NA_HEADS = 8
NA_W = 512
NA_DIM = 64
NA_WIN_R = 8
NA_WIN_C = 16
NA_SCALE = NA_DIM ** -0.5
NA_QROWS = 4
NA_KROWS = 12
NA_NDR = 2 * NA_WIN_R - 1
NA_NDC = 2 * NA_WIN_C - 1

D_FF = 2816
FF_CHUNK = 256
FF_NCHUNK = D_FF // FF_CHUNK
FF_EXT = 16

TM = 512
TQ = 256
FF_SUB = SEQ
FF_NSUB = TM // FF_SUB
HALO = 16
N_TOK_P = BATCH * SEQ
N_TOK_S = DEC_BATCH * DEC_SEQ
N_TOK = N_TOK_P + N_TOK_S
NT_P = N_TOK_P // TM
NT_S = N_TOK_S // TM
NT = NT_P + NT_S
TILES_PER_DEC = DEC_SEQ // TM

VMEM_LIMIT = 56 * 1024 * 1024


def _cp(*sem):
    return pltpu.CompilerParams(dimension_semantics=sem, vmem_limit_bytes=VMEM_LIMIT)


def _dot(a, b):
    return jnp.dot(a, b, preferred_element_type=F32)


def _dot_nt(a, b):
    return lax.dot_general(a, b, (((1,), (1,)), ((), ())), preferred_element_type=F32)


def _dot_tn(a, b):
    return lax.dot_general(a, b, (((0,), (0,)), ((), ())), preferred_element_type=F32)


def _rms(x, g):
    return x * lax.rsqrt(jnp.mean(x * x, axis=-1, keepdims=True) + EPS) * g


def _sigmoid(x):
    return 1.0 / (1.0 + jnp.exp(-x))


def _silu(x):
    return x * _sigmoid(x)


def _cmul(ar, ai, br, bi):
    return ar * br - ai * bi, ar * bi + ai * br


def _mrow(i):
    return jnp.where(i < NT_P, 0, 1 + (i - NT_P) // TILES_PER_DEC)


def _full(shape):
    n = len(shape)
    return pl.BlockSpec(shape, lambda *_: (0,) * n)


def _layer(layer, shape):
    n = len(shape)
    return pl.BlockSpec((None,) + shape, lambda *_: (layer,) + (0,) * n)


def _mod_spec(layer):
    return pl.BlockSpec((None, None, 1, 6 * D), lambda i: (layer, _mrow(i), 0, 0))


def _row(width):
    return pl.BlockSpec((TM, width), lambda i: (i, 0))


def _ctx_row(width):
    return pl.BlockSpec((TM, width), lambda i: (jnp.minimum(i, NT_P - 1), 0))


def _lat_row(width):
    return pl.BlockSpec((TM, width), lambda i: (jnp.maximum(i - NT_P, 0), 0))


def _pick(a_ref, b_ref):
    return jnp.where(pl.program_id(0) < NT_P, a_ref[...], b_ref[...])


def _ada_kernel(c_ref, w_ref, b_ref, *rest):
    n_w = (len(rest) - 1) // 2
    o_ref = rest[n_w]
    o_ref[...] = _dot(_silu(c_ref[...]).astype(BF16), w_ref[...].astype(BF16)) + b_ref[...]
    for src, dst in zip(rest[:n_w], rest[n_w + 1:]):
        dst[...] = src[...].astype(BF16)


def _ada_mods(cvec, ada_w, ada_b, weights):
    nb = 4
    bn = 6 * D // nb
    steps = DEPTH * nb
    flat = [w.reshape(-1, w.shape[-1]) for w in weights]
    slab = lambda w: pl.BlockSpec((w.shape[0] // steps, w.shape[1]), lambda l, n: (l * nb + n, 0))
    out = pl.pallas_call(
        _ada_kernel,
        grid=(DEPTH, nb),
        in_specs=[
            pl.BlockSpec((8, D), lambda l, n: (0, 0)),
            pl.BlockSpec((None, D, bn), lambda l, n: (l, 0, n)),
            pl.BlockSpec((None, 1, bn), lambda l, n: (l, 0, n)),
        ] + [slab(w) for w in flat],
        out_specs=[pl.BlockSpec((None, 8, bn), lambda l, n: (l, 0, n))] + [slab(w) for w in flat],
        out_shape=[jax.ShapeDtypeStruct((DEPTH, 8, 6 * D), F32)]
        + [jax.ShapeDtypeStruct(w.shape, BF16) for w in flat],
        compiler_params=_cp("arbitrary", "arbitrary"),
        name="ada_mods",
    )(cvec, ada_w, ada_b.reshape(DEPTH, 1, 6 * D), *flat)
    return out[0][:, :3].reshape(DEPTH, 3, 1, 6 * D), [o.reshape(w.shape) for o, w in zip(out[1:], weights)]


def _cast_ffn_slab(wu32_ref, wd32_ref, wu_ref, wd_ref):
    wu_ref[...] = wu32_ref[...].astype(BF16)
    wd_ref[...] = wd32_ref[...].astype(BF16)


def _cast_ffn_specs(layer, n_inner):
    n = DEC_BATCH * n_inner
    ru, rd = D // n, D_FF // n
    cast_in = [pl.BlockSpec((None, ru, 2 * D_FF), lambda b, t: (layer, b * n_inner + t, 0)),
               pl.BlockSpec((None, rd, D), lambda b, t: (layer, b * n_inner + t, 0))]
    cast_out = [pl.BlockSpec((ru, 2 * D_FF), lambda b, t: (b * n_inner + t, 0)),
                pl.BlockSpec((rd, D), lambda b, t: (b * n_inner + t, 0))]
    cast_shape = [jax.ShapeDtypeStruct((D, 2 * D_FF), BF16), jax.ShapeDtypeStruct((D_FF, D), BF16)]
    return cast_in, cast_out, cast_shape


def _in_even_kernel(xc_ref, xl_ref, mod_ref, g_ref, w_ref, wkr_ref, qn_ref, kvn_ref, wuq_ref, wk_ref, wv_ref, cos_ref,
                    sin_ref, qkvg_ref, qp_ref, kp_ref, v_ref, ckv_ref, kr_ref):
    x = _pick(xc_ref, xl_ref)
    mod = mod_ref[...]
    h = _rms(x, g_ref[...]) * (1.0 + mod[:, D:2 * D]) + mod[:, :D]
    hb = h.astype(BF16)
    o = 4 * RET_W
    for part in range(4):
        cols = slice(part * RET_W, (part + 1) * RET_W)
        qkvg_ref[:, cols] = _dot(hb, w_ref[:, cols]).astype(BF16)
    r = _dot(hb, w_ref[:, o:o + MLA_Q_RANK + MLA_KV_RANK])
    cq = r[:, :MLA_Q_RANK]
    ckv_raw = r[:, MLA_Q_RANK:]
    r2 = _dot(hb, wkr_ref[...])
    kr = r2[:, :HEAD_SLAB]
    krs = r2[:, HEAD_SLAB:]
    cosf = cos_ref[...]
    sinf = sin_ref[...]
    q2 = _dot(_rms(cq, qn_ref[...]).astype(BF16), wuq_ref[...])
    ckvn = _rms(ckv_raw, kvn_ref[...])

    @pl.when(pl.program_id(0) < NT_P)
    def _():
        ckv_ref[...] = ckvn
        kr_ref[...] = kr

    cb = ckvn.astype(BF16)
    kp = _dot(cb, wk_ref[...])
    v_ref[...] = _dot(cb, wv_ref[...]).astype(BF16)
    krr = kr * cosf + krs * sinf
    for hh in range(MLA_HEADS):
        sl = slice(hh * HEAD_SLAB, (hh + 1) * HEAD_SLAB)
        ss = slice(MLA_QW + hh * HEAD_SLAB, MLA_QW + (hh + 1) * HEAD_SLAB)
        qp_ref[:, sl] = ((q2[:, sl] * cosf + q2[:, ss] * sinf) * (MLA_SCALE * LOG2E)).astype(BF16)
        kp_ref[:, sl] = (kp[:, sl] + krr).astype(BF16)


def _in_even(xs, mods, gains, layer, j, w_in, wkr2, qn, kvn, wuq2, wk, wv, cos_t, sin_t):
    pos_spec = pl.BlockSpec((TM, HEAD_SLAB), lambda i: (jnp.where(i < NT_P, 0, 1 + (i - NT_P) % TILES_PER_DEC), 0))
    return pl.pallas_call(
        _in_even_kernel,
        grid=(NT,),
        in_specs=[_ctx_row(D), _lat_row(D), _mod_spec(layer), _layer(layer, (1, D)), _layer(j, w_in.shape[1:]),
                  _full(wkr2.shape), _layer(j, (1, MLA_Q_RANK)), _layer(j, (1, MLA_KV_RANK)), _full(wuq2.shape),
                  _full(wk.shape), _full(wv.shape), pos_spec, pos_spec],
        out_specs=[_row(4 * RET_W), _row(MLA_QW), _row(MLA_QW), _row(MLA_VW), _ctx_row(MLA_KV_RANK),
                   _ctx_row(HEAD_SLAB)],
        out_shape=[
            jax.ShapeDtypeStruct((N_TOK, 4 * RET_W), BF16),
            jax.ShapeDtypeStruct((N_TOK, MLA_QW), BF16),
            jax.ShapeDtypeStruct((N_TOK, MLA_QW), BF16),
            jax.ShapeDtypeStruct((N_TOK, MLA_VW), BF16),
            jax.ShapeDtypeStruct((N_TOK_P, MLA_KV_RANK), F32),
            jax.ShapeDtypeStruct((N_TOK_P, HEAD_SLAB), F32),
        ],
        compiler_params=_cp("arbitrary"),
        name="in_even",
    )(*xs, mods, gains, w_in, wkr2, qn, kvn, wuq2, wk, wv, cos_t, sin_t)


def _mla_cache_kernel(ckv_ref, kr_ref, wk_ref, wv_ref, kp_ref, v_ref):
    cb = ckv_ref[...].astype(BF16)
    kp = _dot(cb, wk_ref[...])
    kr = kr_ref[...]
    for hh in range(MLA_HEADS):
        sl = slice(hh * HEAD_SLAB, (hh + 1) * HEAD_SLAB)
        kp_ref[:, sl] = (kp[:, sl] + kr).astype(BF16)
    v_ref[...] = _dot(cb, wv_ref[...]).astype(BF16)


def _mla_cache(ckv_c, kr_slab, j, wk, wv):
    row = lambda w: pl.BlockSpec((PAST_LEN, w), lambda b: (b, 0))
    cache = lambda w: pl.BlockSpec((None, None, PAST_LEN, w), lambda b: (b, j, 0, 0))
    return pl.pallas_call(
        _mla_cache_kernel,
        grid=(DEC_BATCH,),
        in_specs=[cache(MLA_KV_RANK), cache(HEAD_SLAB), _full(wk.shape), _full(wv.shape)],
        out_specs=[row(MLA_QW), row(MLA_VW)],
        out_shape=[jax.ShapeDtypeStruct((DEC_BATCH * PAST_LEN, MLA_QW), BF16),
                   jax.ShapeDtypeStruct((DEC_BATCH * PAST_LEN, MLA_VW), BF16)],
        compiler_params=_cp("parallel"),
        name="mla_cache",
    )(ckv_c, kr_slab, wk, wv)


def _ret_kernel(lg_ref, q_ref, k_ref, v_ref, g_ref, gn_ref, *rest, seq_len, n_blk, emit_state, n_prev, zero_init):
    if zero_init:
        s0 = lambda b, d: jnp.zeros((RET_DIM, RET_DIM), F32)
    else:
        s0_ref, rest = rest[0], rest[1:]
        s0 = lambda b, d: s0_ref[b, d]
    prev_refs, rest = rest[:min(n_prev, 1)], rest[min(n_prev, 1):]
    if emit_state:
        y_ref, st_ref, sf_scr, sb_scr, dm_scr, w_scr = rest
    else:
        y_ref, sf_scr, sb_scr, dm_scr, w_scr = rest
    C = RET_CHUNK
    nc = seq_len // C

    @pl.when(pl.program_id(1) == 0)
    def _():
        lg = -jnp.log(1.0 + jnp.exp(-lg_ref[...]))
        lg_f = lg[0]
        lg_b = lg[1]
        ii = lax.broadcasted_iota(jnp.int32, (C, C), 0)
        jj = lax.broadcasted_iota(jnp.int32, (C, C), 1)
        diff = (ii - jj).astype(F32)
        dm_scr[...] = (jnp.where(diff >= 0, jnp.exp(lg_f * jnp.maximum(diff, 0.0)), 0.0)
                       + jnp.where(diff <= 0, jnp.exp(lg_b * jnp.maximum(-diff, 0.0)), 0.0))
        pos = lax.broadcasted_iota(jnp.int32, (C, RET_DIM), 0).astype(F32)
        w_scr[0] = jnp.exp(lg_f * (pos + 1.0))
        w_scr[1] = jnp.exp(lg_f * (C - 1.0 - pos))
        w_scr[2] = jnp.exp(lg_b * (C - pos))
        w_scr[3] = jnp.exp(lg_b * pos)
        w_scr[4] = jnp.exp(lg_f * C) + jnp.zeros((C, RET_DIM), F32)
        w_scr[5] = jnp.exp(lg_b * C) + jnp.zeros((C, RET_DIM), F32)

    qw_f, kw_f, qw_b, kw_b = w_scr[0], w_scr[1], w_scr[2], w_scr[3]
    cd_f = w_scr[4, :RET_DIM, :]
    cd_b = w_scr[5, :RET_DIM, :]
    ld = lambda ref, rows: ref[rows, :].astype(F32)
    gn = gn_ref[...]
    dm = dm_scr[...]

    for b in range(n_blk):
        rows_of = lambda n: slice(b * seq_len + n * C, b * seq_len + (n + 1) * C)
        sf = lambda n: sf_scr.at[b * (nc + 1) + n]
        sb = lambda n: sb_scr.at[b * (nc + 1) + n]
        sf(0)[...] = s0(b, 0)
        for n in range(nc):
            kv = _dot_tn((ld(k_ref, rows_of(n)) * kw_f).astype(BF16), v_ref[rows_of(n), :])
            sf(n + 1)[...] = cd_f * sf(n)[...] + kv
        sb(nc)[...] = s0(b, 1)
        for n in reversed(range(nc)):
            kv = _dot_tn((ld(k_ref, rows_of(n)) * kw_b).astype(BF16), v_ref[rows_of(n), :])
            sb(n)[...] = cd_b * sb(n + 1)[...] + kv
        if emit_state:
            for jp in range(n_prev):
                st_ref[b, jp] = prev_refs[0][b, jp]
            st_ref[b, n_prev, 0] = sf(nc)[...]
            st_ref[b, n_prev, 1] = sb(0)[...]
        for n in range(nc):
            rows = rows_of(n)
            q = ld(q_ref, rows) * (RET_DIM ** -0.5)
            s = _dot_nt(q.astype(BF16), k_ref[rows, :]) * dm
            o = (_dot(s.astype(BF16), v_ref[rows, :])
                 + _dot((q * qw_f).astype(BF16), sf(n)[...].astype(BF16))
                 + _dot((q * qw_b).astype(BF16), sb(n + 1)[...].astype(BF16)))
            mu = jnp.mean(o, axis=-1, keepdims=True)
            oc = o - mu
            var = jnp.mean(oc * oc, axis=-1, keepdims=True)
            on = oc * lax.rsqrt(var + EPS) * gn
            y_ref[rows, :] = (_silu(ld(g_ref, rows)) * on).astype(BF16)


def _retention(qkvg, logit, gn, j, s0, *, seq_len, n_seq, n_blk, row0, emit_state, prev_states=None):
    n_prev = 0 if prev_states is None else prev_states.shape[1]
    nc = seq_len // RET_CHUNK
    rows = n_blk * seq_len
    blk0 = row0 // rows
    col = lambda part: pl.BlockSpec((rows, RET_DIM), lambda h, s: (blk0 + s, part * RET_HEADS + h))
    in_specs = [
        pl.BlockSpec((None, 2, 1, 1), lambda h, s: (h, 0, 0, 0)),
        col(0), col(1), col(2), col(3),
        pl.BlockSpec((None, 1, RET_DIM), lambda h, s: (j, 0, h)),
    ]
    out_specs = [pl.BlockSpec((rows, RET_DIM), lambda h, s: (s, h))]
    out_shape = [jax.ShapeDtypeStruct((n_seq * seq_len, RET_W), BF16)]
    args = [logit, qkvg, qkvg, qkvg, qkvg, gn]
    if s0 is not None:
        in_specs.append(pl.BlockSpec((n_blk, None, 2, None, RET_DIM, RET_DIM), lambda h, s: (s, j, 0, h, 0, 0)))
        args.append(s0)
    stacked = lambda n: pl.BlockSpec((n_blk, n, 2, None, RET_DIM, RET_DIM), lambda h, s: (s, 0, 0, h, 0, 0))
    if n_prev:
        in_specs.append(stacked(n_prev))
        args.append(prev_states)
    if emit_state:
        out_specs.append(stacked(n_prev + 1))
        out_shape.append(jax.ShapeDtypeStruct((n_seq, n_prev + 1, 2, RET_HEADS, RET_DIM, RET_DIM), F32))
    n_st = n_blk * (nc + 1)
    return pl.pallas_call(
        functools.partial(_ret_kernel, seq_len=seq_len, n_blk=n_blk, emit_state=emit_state, n_prev=n_prev,
                          zero_init=s0 is None),
        grid=(RET_HEADS, n_seq // n_blk),
        in_specs=in_specs,
        out_specs=out_specs,
        out_shape=out_shape,
        scratch_shapes=[pltpu.VMEM((n_st, RET_DIM, RET_DIM), F32), pltpu.VMEM((n_st, RET_DIM, RET_DIM), F32),
                        pltpu.VMEM((RET_CHUNK, RET_CHUNK), F32), pltpu.VMEM((6, RET_CHUNK, RET_DIM), F32)],
        compiler_params=_cp("parallel", "arbitrary"),
        name="retention_%d" % seq_len,
    )(*args)


def _softmax_pv(score_blocks, value_blocks):
    m = functools.reduce(jnp.maximum, [jnp.max(s, axis=-1, keepdims=True) for s in score_blocks])
    ps = [jnp.exp2(s - m) for s in score_blocks]
    l = functools.reduce(lambda a, b: a + b, [jnp.sum(p, axis=-1, keepdims=True) for p in ps])
    o = functools.reduce(lambda a, b: a + b, [_dot(p.astype(BF16), v) for p, v in zip(ps, value_blocks)])
    return o / l


def _low_half(shape):
    return lax.broadcasted_iota(jnp.int32, shape, 1) < 64


def _heads_pipelined(n_heads, scores, attend, y_ref, rows):
    low = _low_half((rows, 128))
    nxt = scores(0)
    outs = []
    for hh in range(n_heads):
        cur = nxt
        if hh + 1 < n_heads:
            nxt = scores(hh + 1)
        outs.append(attend(hh, cur))
        if hh % 2 == 1:
            hp = hh // 2
            y_ref[:, hp * 128:(hp + 1) * 128] = jnp.where(low, outs[hh - 1], outs[hh]).astype(BF16)


def _mla_ctx_kernel(q_ref, k_ref, v_ref, y_ref):
    slab = lambda hh: slice(hh * HEAD_SLAB, (hh + 1) * HEAD_SLAB)
    pair = lambda hh: slice(hh // 2 * 128, (hh // 2 + 1) * 128)
    scores = lambda hh: [_dot_nt(q_ref[:, slab(hh)], k_ref[:, slab(hh)])]
    attend = lambda hh, s: _softmax_pv(s, [v_ref[:, pair(hh)]])
    _heads_pipelined(MLA_HEADS, scores, attend, y_ref, SEQ)


def _mla_ctx(qp, kp, v):
    return pl.pallas_call(
        _mla_ctx_kernel,
        grid=(BATCH,),
        in_specs=[pl.BlockSpec((SEQ, MLA_QW), lambda b: (b, 0)), pl.BlockSpec((SEQ, MLA_QW), lambda b: (b, 0)),
                  pl.BlockSpec((SEQ, MLA_VW), lambda b: (b, 0))],
        out_specs=pl.BlockSpec((SEQ, MLA_VW), lambda b: (b, 0)),
        out_shape=jax.ShapeDtypeStruct((N_TOK_P, MLA_VW), BF16),
        compiler_params=_cp("parallel"),
        name="mla_ctx",
    )(qp, kp, v)


def _mla_lat_kernel(q_ref, k_ref, v_ref, kc_ref, vc_ref, wu32_ref, wd32_ref, y_ref, wu_ref, wd_ref):
    _cast_ffn_slab(wu32_ref, wd32_ref, wu_ref, wd_ref)
    slab = lambda hh: slice(hh * HEAD_SLAB, (hh + 1) * HEAD_SLAB)
    pair = lambda hh: slice(hh // 2 * 128, (hh // 2 + 1) * 128)
    scores = lambda hh: [_dot_nt(q_ref[:, slab(hh)], k_ref[:, slab(hh)]),
                         _dot_nt(q_ref[:, slab(hh)], kc_ref[:, slab(hh)])]
    attend = lambda hh, s: _softmax_pv(s, [v_ref[:, pair(hh)], vc_ref[:, pair(hh)]])
    _heads_pipelined(MLA_HEADS, scores, attend, y_ref, TQ)


def _mla_lat(qp, kp, v, kp_c, v_c, layer, w_up, w_down):
    seq_blk = N_TOK_P // DEC_SEQ
    nq = DEC_SEQ // TQ
    cast_in, cast_out, cast_shape = _cast_ffn_specs(layer, nq)
    return pl.pallas_call(
        _mla_lat_kernel,
        grid=(DEC_BATCH, nq),
        in_specs=[
            pl.BlockSpec((TQ, MLA_QW), lambda b, t: (N_TOK_P // TQ + b * nq + t, 0)),
            pl.BlockSpec((DEC_SEQ, MLA_QW), lambda b, t: (seq_blk + b, 0)),
            pl.BlockSpec((DEC_SEQ, MLA_VW), lambda b, t: (seq_blk + b, 0)),
            pl.BlockSpec((PAST_LEN, MLA_QW), lambda b, t: (b, 0)),
            pl.BlockSpec((PAST_LEN, MLA_VW), lambda b, t: (b, 0)),
        ] + cast_in,
        out_specs=[pl.BlockSpec((TQ, MLA_VW), lambda b, t: (b * nq + t, 0))] + cast_out,
        out_shape=[jax.ShapeDtypeStruct((N_TOK_S, MLA_VW), BF16)] + cast_shape,
        compiler_params=_cp("parallel", "parallel"),
        name="mla_lat",
    )(qp, kp, v, kp_c, v_c, w_up, w_down)


def _gelu_tanh(x):
    return 0.5 * x * (1.0 + jnp.tanh(np.sqrt(2.0 / np.pi).astype(np.float32) * (x + 0.044715 * (x * x * x))))


def _ffn_kernel(*refs, odd):
    xc_ref, xl_ref, xp_ref, xn_ref = refs[:4]
    if odd:
        (yr_ref, yrp_ref, yrn_ref, u_ref, up_ref, un_ref, bc_ref, bl_ref, bp_ref, bn_ref, mod_ref, gmix_ref, gpre_ref,
         gpost_ref, wo_ref, d_ref, gw_ref, gb_ref, wu_ref, cw_ref, cb_ref, wd_ref, oc_ref, ol_ref, h_scr,
         act_scr) = refs[4:]
    else:
        (ac_ref, al_ref, ap_ref, an_ref, bc_ref, bl_ref, bp_ref, bn_ref, mod_ref, gmix_ref, gpre_ref, gpost_ref,
         wo_ref, wu_ref, cw_ref, cb_ref, wd_ref, oc_ref, ol_ref, h_scr, act_scr) = refs[4:]
    i = pl.program_id(0)
    is_lat = i >= NT_P
    t = (i - NT_P) % TILES_PER_DEC
    has_prev = jnp.logical_and(is_lat, t != 0)
    has_next = jnp.logical_and(is_lat, t != TILES_PER_DEC - 1)
    mod = mod_ref[...]
    cat = lambda main, prev_ref, next_ref: jnp.concatenate([main, prev_ref[...], next_ref[...]], axis=0)

    if odd:
        y = _gelu_tanh(cat(yr_ref[...], yrp_ref, yrn_ref) + d_ref[...] * cat(u_ref[...], up_ref, un_ref))
        ya = (y * _sigmoid(_dot(y.astype(BF16), gw_ref[...]) + gb_ref[...])).astype(BF16)
    else:
        ya = cat(_pick(ac_ref, al_ref), ap_ref, an_ref)
    yb = cat(_pick(bc_ref, bl_ref), bp_ref, bn_ref)
    half = ya.shape[1]
    r = _dot(ya, wo_ref[:half, :]) + _dot(yb, wo_ref[half:, :])
    x1 = cat(_pick(xc_ref, xl_ref), xp_ref, xn_ref) + mod[:, 2 * D:3 * D] * _rms(r, gmix_ref[...])

    shift = mod[:, 3 * D:4 * D]
    scale = mod[:, 4 * D:5 * D]
    gate = mod[:, 5 * D:6 * D]
    hall = _rms(x1, gpre_ref[...]) * (1.0 + scale) + shift
    x = x1[:TM]
    h = hall[:TM]
    blk = FF_SUB + FF_EXT
    rows = FF_NSUB * blk
    for k in range(FF_NSUB):
        h_scr[k * blk:k * blk + FF_SUB, :] = h[k * FF_SUB:(k + 1) * FF_SUB].astype(BF16)
        if k + 1 < FF_NSUB:
            after = jnp.where(is_lat, h[(k + 1) * FF_SUB:(k + 1) * FF_SUB + 8], 0.0)
            before = jnp.where(is_lat, h[(k + 1) * FF_SUB - 8:(k + 1) * FF_SUB], 0.0)
        else:
            after = jnp.where(has_next, hall[TM + HALO:TM + HALO + 8], 0.0)
            before = jnp.where(has_prev, hall[TM + HALO - 8:TM + HALO], 0.0)
        h_scr[k * blk + FF_SUB:(k + 1) * blk, :] = jnp.concatenate([after, before], axis=0).astype(BF16)
    hb = h_scr[...]

    def up(j):
        ca = slice(j * FF_CHUNK, (j + 1) * FF_CHUNK)
        cg = slice(D_FF + j * FF_CHUNK, D_FF + (j + 1) * FF_CHUNK)
        return (_dot(hb, wu_ref[:, ca]), ca), (_dot(hb, wu_ref[:, cg]), cg)

    def conv(part):
        u, cols = part
        cw = cw_ref[:, cols]
        return (cw[0:1, :] * pltpu.roll(u, 1, axis=0) + cw[1:2, :] * u + cw[2:3, :] * pltpu.roll(u, rows - 1, axis=0)
                + cb_ref[:, cols])

    nxt = up(0)
    for j in range(FF_NCHUNK):
        cur = nxt
        if j + 1 < FF_NCHUNK:
            nxt = up(j + 1)
        act = (_silu(conv(cur[1])) * conv(cur[0])).astype(BF16)
        for k in range(FF_NSUB):
            act_scr[k * FF_SUB:(k + 1) * FF_SUB, j * FF_CHUNK:(j + 1) * FF_CHUNK] = act[k * blk:k * blk + FF_SUB]
    out = x + gate * _rms(_dot(act_scr[...], wd_ref[...]), gpost_ref[...])

    @pl.when(i < NT_P)
    def _():
        oc_ref[...] = out

    @pl.when(i >= NT_P)
    def _():
        ol_ref[...] = out


def _halo_specs(width, n_rows, tile0):
    per = TM // HALO
    last = n_rows // HALO - 1
    prev = pl.BlockSpec((HALO, width), lambda i: (jnp.clip((i - tile0) * per - 1, 0, last), 0))
    nxt = pl.BlockSpec((HALO, width), lambda i: (jnp.clip((i - tile0 + 1) * per, 0, last), 0))
    return [prev, nxt]


def _ffn(xs, mix, mods, gmix, gpre, gpost, layer, j, w_out, wu, cw, cb, wd, s5=None):
    odd = s5 is not None
    pair = lambda w: [_ctx_row(w), _lat_row(w)]
    lat_halo = lambda w: _halo_specs(w, N_TOK_S, NT_P)
    all_halo = lambda w: _halo_specs(w, N_TOK, 0)
    resident = lambda shape: pl.BlockSpec(shape, lambda i: (0, 0), pipeline_mode=pl.Buffered(1))
    x_specs = pair(D) + lat_halo(D)
    x_args = [xs[0], xs[1], xs[1], xs[1]]
    if odd:
        yr, u, b_c, b_l = mix
        mix_specs = [_row(S5_W)] + all_halo(S5_W) + [_row(S5_W)] + all_halo(S5_W) + pair(NA_W) + lat_halo(NA_W)
        mix_args = [yr, yr, yr, u, u, u, b_c, b_l, b_l, b_l]
        s5_specs = [_layer(j, (1, S5_W)), _layer(j, (S5_W, S5_W)), _layer(j, (1, S5_W))]
        s5_args = list(s5)
    else:
        a_c, a_l, b_c, b_l = mix
        mix_specs = pair(RET_W) + lat_halo(RET_W) + pair(MLA_VW) + lat_halo(MLA_VW)
        mix_args = [a_c, a_l, a_l, a_l, b_c, b_l, b_l, b_l]
        s5_specs, s5_args = [], []
    return pl.pallas_call(
        functools.partial(_ffn_kernel, odd=odd),
        grid=(NT,),
        in_specs=x_specs + mix_specs + [_mod_spec(layer), _layer(layer, (1, D)), _layer(layer, (1, D)),
                                        _layer(layer, (1, D)), _layer(j, (D, D))] + s5_specs
        + [resident((D, 2 * D_FF)), _layer(layer, (3, 2 * D_FF)), _layer(layer, (1, 2 * D_FF)), resident((D_FF, D))],
        out_specs=[_ctx_row(D), _lat_row(D)],
        out_shape=[jax.ShapeDtypeStruct((N_TOK_P, D), F32), jax.ShapeDtypeStruct((N_TOK_S, D), F32)],
        scratch_shapes=[pltpu.VMEM((FF_NSUB * (FF_SUB + FF_EXT), D), BF16), pltpu.VMEM((TM, D_FF), BF16)],
        compiler_params=_cp("arbitrary"),
        name="ffn_odd" if odd else "ffn_even",
    )(*x_args, *mix_args, mods, gmix, gpre, gpost, w_out, *s5_args, wu, cw, cb, wd)


def _in_odd_kernel(xc_ref, xl_ref, mod_ref, g_ref, w_ref, *rest, n_prev):
    np2 = 2 * min(n_prev, 1)
    prev_refs, (u_ref, qkv_ref, kc_ref, vc_ref) = rest[:np2], rest[np2:]
    mod = mod_ref[...]
    h = _rms(_pick(xc_ref, xl_ref), g_ref[...]) * (1.0 + mod[:, D:2 * D]) + mod[:, :D]
    r = _dot(h.astype(BF16), w_ref[...])
    u_ref[...] = r[:, :S5_W]
    qkv_ref[:, :NA_W] = (r[:, S5_W:S5_W + NA_W] * (NA_SCALE * LOG2E)).astype(BF16)
    qkv_ref[:, NA_W:] = r[:, S5_W + NA_W:].astype(BF16)

    @pl.when(pl.program_id(0) < NT_P)
    def _():
        for part, (out_ref, col0) in enumerate([(kc_ref, S5_W + NA_W), (vc_ref, S5_W + 2 * NA_W)]):
            for b in range(TM // SEQ):
                for jp in range(n_prev):
                    out_ref[b, jp] = prev_refs[part][b, jp]
                out_ref[b, n_prev] = r[b * SEQ:(b + 1) * SEQ, col0:col0 + NA_W]


def _in_odd(xs, mods, gains, layer, j, w, prev_kv=None):
    n_prev = 0 if prev_kv is None else prev_kv[0].shape[1]
    nb = TM // SEQ
    stacked = lambda n: pl.BlockSpec((nb, n, SEQ, NA_W), lambda i: (jnp.minimum(i, NT_P - 1), 0, 0, 0))
    leaf = jax.ShapeDtypeStruct((BATCH, n_prev + 1, SEQ, NA_W), F32)
    prev_specs = [stacked(n_prev)] * 2 if n_prev else []
    return pl.pallas_call(
        functools.partial(_in_odd_kernel, n_prev=n_prev),
        grid=(NT,),
        in_specs=[_ctx_row(D), _lat_row(D), _mod_spec(layer), _layer(layer, (1, D)), _layer(j, w.shape[1:])]
        + prev_specs,
        out_specs=[_row(S5_W), _row(3 * NA_W), stacked(n_prev + 1), stacked(n_prev + 1)],
        out_shape=[jax.ShapeDtypeStruct((N_TOK, S5_W), F32), jax.ShapeDtypeStruct((N_TOK, 3 * NA_W), BF16), leaf, leaf],
        compiler_params=_cp("arbitrary"),
        name="in_odd",
    )(*xs, mods, gains, w, *(prev_kv or ()))


S5_TS = S5_CHUNK * S5_GROUP
S5_PL = 2 * S5_P
S5_PREP_PAIRS = 2


def _s5_prep_kernel(*refs):
    ins, outs = refs[:7], refs[7:]
    for q in range(S5_PREP_PAIRS):
        _s5_prep_pair(*[r.at[:, q] for r in ins], *[r.at[q] for r in outs])


def _s5_prep_pair(lre_ref, lim_ref, ls_ref, btr_ref, bti_ref, cr_ref, ci_ref, m_ref, n_ref, p_ref, a_ref, ct_scr):
    T = S5_CHUNK
    S = S5_GROUP
    hi = lax.Precision.HIGHEST
    low = lax.broadcasted_iota(jnp.int32, (S, S5_PL), 1) < S5_P
    half = [low, jnp.logical_not(low)]
    pick = lambda e, v: jnp.where(half[e], v, 0.0)
    nt = (((1,), (1,)), ((), ()))
    kps = [[None, None], [None, None]]
    for d in range(2):
        lre = lre_ref[d]
        lim = lim_ref[d]
        step = jnp.exp(ls_ref[d])
        mag = jnp.exp(lre * step)
        are = mag * jnp.cos(lim * step)
        aim = mag * jnp.sin(lim * step)
        den = lre * lre + lim * lim
        zr, zi = _cmul(are - 1.0, aim, lre / den, -lim / den)
        bbr, bbi = _cmul(zr, zi, btr_ref[d], bti_ref[d])
        cr = cr_ref[d]
        ci = ci_ref[d]
        pr = jnp.ones_like(are)
        pi = jnp.zeros_like(are)
        for k in range(T + 1):
            er, ei = _cmul(cr, ci, pr, pi)
            if k < T:
                jn = T - 1 - k if d == 0 else k
                wr, wi = _cmul(pr, pi, bbr, bbi)
                for e in range(2):
                    rows = slice(e * S5_TS + jn * S, e * S5_TS + (jn + 1) * S)
                    n_ref[d, rows, 0:S5_PL] = pick(e, wr).astype(BF16)
                    n_ref[d, rows, S5_PL:2 * S5_PL] = pick(e, wi).astype(BF16)
                jc = k if d == 0 else T - 1 - k
                ct_scr[0, jc * S:(jc + 1) * S, :] = er
                ct_scr[1, jc * S:(jc + 1) * S, :] = ei
            if k >= 1:
                t = k - 1 if d == 0 else T - k
                for e in range(2):
                    rows = slice(e * S5_TS + t * S, e * S5_TS + (t + 1) * S)
                    p_ref[rows, 2 * d * S5_PL:(2 * d + 1) * S5_PL] = pick(e, er).astype(BF16)
                    p_ref[rows, (2 * d + 1) * S5_PL:(2 * d + 2) * S5_PL] = pick(e, -ei).astype(BF16)
            if k == T:
                a_ref[d, 0] = pr
                a_ref[d, 1] = pi
            pr, pi = _cmul(pr, pi, are, aim)
        for e in range(2):
            kd = (lax.dot_general(pick(e, bbr), ct_scr[0], nt, precision=hi, preferred_element_type=F32)
                  - lax.dot_general(pick(e, bbi), ct_scr[1], nt, precision=hi, preferred_element_type=F32))
            kps[d][e] = jnp.concatenate([kd, jnp.zeros_like(kd)], axis=1)
    for e in range(2):
        for t in range(T):
            fwd = pltpu.roll(kps[0][e], t * S, axis=1)[:, :S5_TS]
            bwd = pltpu.roll(kps[1][e], (2 * S5_TS - (T - 1 - t) * S) % (2 * S5_TS), axis=1)[:, :S5_TS]
            m_ref[e, t * S:(t + 1) * S, :] = (fwd + bwd).astype(BF16)


def _s5_prep(lre, lim, ls, b_re, b_im, c_re, c_im):
    NP, S, P = S5_PAIRS, S5_GROUP, S5_P
    vec = lambda a: a.reshape(2, NP, 1, S5_PL)
    b_lay = lambda a: jnp.transpose(a.reshape(2, NP, 2, P, S), (0, 1, 4, 2, 3)).reshape(2, NP, S, S5_PL)
    c_lay = lambda a: jnp.transpose(a.reshape(2, NP, 2, S, P), (0, 1, 3, 2, 4)).reshape(2, NP, S, S5_PL)
    args = (vec(lre), vec(lim), vec(jnp.repeat(ls, P, axis=-1)), b_lay(b_re), b_lay(b_im), c_lay(c_re), c_lay(c_im))
    PB = S5_PREP_PAIRS
    vspec = pl.BlockSpec((2, PB, 1, S5_PL), lambda g: (0, g, 0, 0))
    mspec = pl.BlockSpec((2, PB, S, S5_PL), lambda g: (0, g, 0, 0))
    lead = lambda shape: pl.BlockSpec((PB,) + shape, lambda g: (g,) + (0,) * len(shape))
    return pl.pallas_call(
        _s5_prep_kernel,
        grid=(NP // PB,),
        in_specs=[vspec, vspec, vspec, mspec, mspec, mspec, mspec],
        out_specs=[lead((2, S5_TS, S5_TS)), lead((2, 2 * S5_TS, 2 * S5_PL)), lead((2 * S5_TS, 4 * S5_PL)),
                   lead((2, 2, 1, S5_PL))],
        out_shape=[jax.ShapeDtypeStruct((NP, 2, S5_TS, S5_TS), BF16),
                   jax.ShapeDtypeStruct((NP, 2, 2 * S5_TS, 2 * S5_PL), BF16),
                   jax.ShapeDtypeStruct((NP, 2 * S5_TS, 4 * S5_PL), BF16),
                   jax.ShapeDtypeStruct((NP, 2, 2, 1, S5_PL), F32)],
        scratch_shapes=[pltpu.VMEM((S5_PREP_PAIRS, 2, S5_TS, S5_PL), F32)],
        compiler_params=_cp("parallel"),
        name="s5_prep",
    )(*args)


S5_NCH = N_TOK // S5_CHUNK
S5_ROWS_P = N_TOK_P // S5_CHUNK
S5_NC_P = SEQ // S5_CHUNK
S5_NC_S = DEC_SEQ // S5_CHUNK
S5_GPB = 8
S5_PPB = S5_GPB // 2
S5_XL = 8 * 128


def _s5_perm():
    r = np.arange(S5_XL)
    dst = (r // S5_GROUP % S5_GPB) * 128 + (r // 128) * S5_GROUP + r % S5_GROUP
    perm = np.zeros((S5_XL, S5_XL), np.float32)
    perm[r, dst] = 1.0
    return jnp.asarray(perm, BF16)


def _s5_kernel(u_ref, perm_ref, m_ref, n_ref, p_ref, a_ref, h0c_ref, h0l_ref, y_ref, fin_ref, z_scr, up_scr, e_scr,
               hin_scr, yc_scr):
    T = S5_CHUNK
    W = S5_PL
    for t in range(T):
        z_scr[t // 8, :, (t % 8) * 128:(t % 8 + 1) * 128] = u_ref[pl.ds(t, S5_NCH, stride=T), :].astype(BF16)
    perm = perm_ref[...]
    for j in range(2):
        up_scr[j] = _dot(z_scr[j], perm).astype(BF16)

    def scan(pp, d, h0_ref, n_chunks, n_seq, row0):
        are = a_ref[pp, d, 0]
        aim = a_ref[pp, d, 1]
        hr = h0_ref[pp, d, 0]
        hi = h0_ref[pp, d, 1]
        order = range(n_chunks) if d == 0 else reversed(range(n_chunks))
        for c in order:
            rows = pl.ds(row0 + c, n_seq, stride=n_chunks)
            hin_scr[2 * d, rows, :] = hr
            hin_scr[2 * d + 1, rows, :] = hi
            er = e_scr[2 * d, rows, :]
            ei = e_scr[2 * d + 1, rows, :]
            hr, hi = are * hr - aim * hi + er, are * hi + aim * hr + ei
        return hr, hi

    for pp in range(S5_PPB):
        us = []
        for e in range(2):
            sl = slice((2 * pp + e) * 128, (2 * pp + e + 1) * 128)
            us.append(jnp.concatenate([up_scr[0, :, sl], up_scr[1, :, sl]], axis=1))
        u2 = jnp.concatenate(us, axis=1)
        for d in range(2):
            ed = _dot(u2, n_ref[pp, d])
            e_scr[2 * d] = ed[:, :W]
            e_scr[2 * d + 1] = ed[:, W:]
        for d in range(2):
            hr, hi = scan(pp, d, h0c_ref, S5_NC_P, BATCH, 0)
            fin_ref[pp, d, 0] = hr
            fin_ref[pp, d, 1] = hi
            scan(pp, d, h0l_ref, S5_NC_S, DEC_BATCH, S5_ROWS_P)
        hin = jnp.concatenate([hin_scr[k] for k in range(4)], axis=1).astype(BF16)
        for e in range(2):
            y = _dot(us[e], m_ref[pp, e]) + _dot_nt(hin, p_ref[pp, e * S5_TS:(e + 1) * S5_TS, :])
            sl = slice((2 * pp + e) * 128, (2 * pp + e + 1) * 128)
            for j in range(2):
                yc_scr[j, :, sl] = y[:, j * 128:(j + 1) * 128]

    for j in range(2):
        yp = yc_scr[j]
        y_hi = yp.astype(BF16)
        y_lo = (yp - y_hi.astype(F32)).astype(BF16)
        r = _dot_nt(y_hi, perm) + _dot_nt(y_lo, perm)
        for k in range(8):
            y_ref[pl.ds(8 * j + k, S5_NCH, stride=T), :] = r[:, k * 128:(k + 1) * 128]


def _s5_scan(u, perm, m, n2, p2, a, h0c, h0l):
    nb = S5_GROUPS // S5_GPB
    lead = lambda shape: pl.BlockSpec((S5_PPB,) + shape, lambda w: (w,) + (0,) * len(shape))
    col = pl.BlockSpec((N_TOK, 128), lambda w: (0, w))
    return pl.pallas_call(
        _s5_kernel,
        grid=(nb,),
        in_specs=[col, _full(perm.shape), lead((2, S5_TS, S5_TS)), lead((2, 2 * S5_TS, 2 * S5_PL)),
                  lead((2 * S5_TS, 4 * S5_PL)), lead((2, 2, 1, S5_PL)), lead((2, 2, BATCH, S5_PL)),
                  lead((2, 2, DEC_BATCH, S5_PL))],
        out_specs=[col, lead((2, 2, BATCH, S5_PL))],
        out_shape=[jax.ShapeDtypeStruct((N_TOK, S5_W), F32),
                   jax.ShapeDtypeStruct((S5_PAIRS, 2, 2, BATCH, S5_PL), F32)],
        scratch_shapes=[pltpu.VMEM((2, S5_NCH, S5_XL), BF16), pltpu.VMEM((2, S5_NCH, S5_XL), BF16),
                        pltpu.VMEM((4, S5_NCH, S5_PL), F32), pltpu.VMEM((4, S5_NCH, S5_PL), F32),
                        pltpu.VMEM((2, S5_NCH, S5_XL), F32)],
        compiler_params=_cp("parallel"),
        name="s5_scan",
    )(u, perm, m, n2, p2, a, h0c, h0l)


def _na_heads(q_ref, keys, values, y_ref, rows, bias=None):
    low = _low_half((rows, 128))
    pair = lambda hh: slice(hh // 2 * 128, (hh // 2 + 1) * 128)

    def scores(hh):
        q = q_ref[:, pair(hh)]
        qm = jnp.where(low == (hh % 2 == 0), q, jnp.zeros_like(q))
        s = [_dot_nt(qm, k(pair(hh))) for k in keys]
        if bias is not None:
            s[0] = s[0] + bias(hh)
        return s

    attend = lambda hh, s: _softmax_pv(s, [v(pair(hh)) for v in values])
    _heads_pipelined(NA_HEADS, scores, attend, y_ref, rows)


def _na_ctx_kernel(q_ref, k_ref, v_ref, y_ref):
    _na_heads(q_ref, [lambda sl: k_ref[:, sl]], [lambda sl: v_ref[:, sl]], y_ref, SEQ)


def _na_ctx(qkv):
    col = lambda part: pl.BlockSpec((SEQ, NA_W), lambda b: (b, part))
    return pl.pallas_call(
        _na_ctx_kernel,
        grid=(BATCH,),
        in_specs=[col(0), col(1), col(2)],
        out_specs=pl.BlockSpec((SEQ, NA_W), lambda b: (b, 0)),
        out_shape=jax.ShapeDtypeStruct((N_TOK_P, NA_W), BF16),
        compiler_params=_cp("parallel"),
        name="na_ctx",
    )(qkv, qkv, qkv)


def _na_key_row0(rb):
    return jnp.clip(NA_QROWS * rb - NA_WIN_R // 2, 0, GRID_H - NA_KROWS)


def _na_lat_kernel(q_ref, ks_ref, vs_ref, kc_ref, vc_ref, tab_ref, wu32_ref, wd32_ref, y_ref, wu_ref, wd_ref):
    _cast_ffn_slab(wu32_ref, wd32_ref, wu_ref, wd_ref)
    rb = pl.program_id(1)
    u0 = _na_key_row0(rb)
    start = pl.multiple_of(u0 * GRID_W, GRID_W)
    nk = NA_KROWS * GRID_W
    low_t = _low_half((GRID_W, 128))

    def table_row(i, w):
        qr = NA_QROWS * rb + i
        kr = u0 + w
        rs = jnp.clip(qr - NA_WIN_R // 2, 0, GRID_H - NA_WIN_R)
        inside = jnp.logical_and(kr >= rs, kr < rs + NA_WIN_R)
        return jnp.where(inside, kr - qr + NA_WIN_R - 1, NA_NDR)

    idx = [[table_row(i, w) for w in range(NA_KROWS)] for i in range(NA_QROWS)]

    def bias(h):
        rows = [jnp.concatenate([jnp.where(low_t, tab_ref[h, idx[i][w]], tab_ref[h, idx[i][w + 1]])
                                 for w in range(0, NA_KROWS, 2)], axis=1) for i in range(NA_QROWS)]
        return jnp.concatenate(rows, axis=0)

    keys = [lambda sl: ks_ref[pl.ds(start, nk), sl], lambda sl: kc_ref[:, sl].astype(BF16)]
    values = [lambda sl: vs_ref[pl.ds(start, nk), sl], lambda sl: vc_ref[:, sl].astype(BF16)]
    _na_heads(q_ref, keys, values, y_ref, TQ, bias)


def _na_lat(qkv, k_c, v_c, j, table, layer, w_up, w_down):
    seq_blk = N_TOK_P // DEC_SEQ
    nb = GRID_H // NA_QROWS
    cast_in, cast_out, cast_shape = _cast_ffn_specs(layer, nb)
    return pl.pallas_call(
        _na_lat_kernel,
        grid=(DEC_BATCH, nb),
        in_specs=[
            pl.BlockSpec((TQ, NA_W), lambda b, r: (N_TOK_P // TQ + b * nb + r, 0)),
            pl.BlockSpec((DEC_SEQ, NA_W), lambda b, r: (seq_blk + b, 1)),
            pl.BlockSpec((DEC_SEQ, NA_W), lambda b, r: (seq_blk + b, 2)),
            pl.BlockSpec((None, None, PAST_LEN, NA_W), lambda b, r: (b, j, 0, 0)),
            pl.BlockSpec((None, None, PAST_LEN, NA_W), lambda b, r: (b, j, 0, 0)),
            pl.BlockSpec(table.shape, lambda b, r: (0, 0, 0, 0)),
        ] + cast_in,
        out_specs=[pl.BlockSpec((TQ, NA_W), lambda b, r: (b * nb + r, 0))] + cast_out,
        out_shape=[jax.ShapeDtypeStruct((N_TOK_S, NA_W), BF16)] + cast_shape,
        compiler_params=_cp("parallel", "arbitrary"),
        name="na_lat",
    )(qkv, qkv, qkv, k_c, v_c, table, w_up, w_down)


def _na_table_kernel(rpb_ref, t_ref):
    qc = lax.broadcasted_iota(jnp.int32, (GRID_W, 128), 0)
    kc = lax.broadcasted_iota(jnp.int32, (GRID_W, 128), 1) % GRID_W
    cs = jnp.clip(qc - NA_WIN_C // 2, 0, GRID_W - NA_WIN_C)
    in_band = jnp.logical_and(kc >= cs, kc < cs + NA_WIN_C)
    neg = jnp.full((GRID_W, 128), -jnp.inf, F32)

    def body(n, carry):
        x = jnp.broadcast_to(rpb_ref[n], (GRID_W, 128))
        t = pltpu.roll(x, 128 - (NA_WIN_C - 1), axis=1, stride=1, stride_axis=0)
        t_ref[n // NA_NDR, n % NA_NDR] = jnp.where(in_band, t * LOG2E, neg)
        return carry

    lax.fori_loop(0, NA_HEADS * NA_NDR, body, 0, unroll=8)
    for h in range(NA_HEADS):
        t_ref[h, NA_NDR] = neg


def _na_table(rpb):
    rows = jnp.pad(rpb.reshape(NA_HEADS * NA_NDR, 1, NA_NDC), ((0, 0), (0, 0), (0, GRID_W - NA_NDC)))
    rows = jnp.concatenate([rows, rows], axis=-1)
    return pl.pallas_call(
        _na_table_kernel,
        out_shape=jax.ShapeDtypeStruct((NA_HEADS, NA_NDR + 1, GRID_W, 128), F32),
        name="na_table",
    )(rows)


def _rope_tables():
    n_freq = MLA_ROPE // 4
    inv = ROPE_BASE ** (-jnp.arange(n_freq, dtype=F32) / n_freq)
    t = jnp.arange(DEC_SEQ)
    row = (t // GRID_W).astype(F32)
    colp = (t % GRID_W).astype(F32)
    ang = jnp.concatenate([row[:, None] * inv, colp[:, None] * inv], axis=-1)
    cos, sin = jnp.cos(ang), jnp.sin(ang)
    one = jnp.ones((DEC_SEQ, MLA_NOPE), F32)
    zero = jnp.zeros((DEC_SEQ, MLA_NOPE), F32)
    cos_s = jnp.concatenate([one, cos, cos, one[:, :32]], axis=-1)
    sin_s = jnp.concatenate([zero, -sin, sin, zero[:, :32]], axis=-1)
    cos_t = jnp.concatenate([jnp.ones((TM, HEAD_SLAB), F32), cos_s], axis=0)
    sin_t = jnp.concatenate([jnp.zeros((TM, HEAD_SLAB), F32), sin_s], axis=0)
    return cos_t, sin_t


def _mla_weights(w_in, w_uq, w_ukv):
    half = MLA_ROPE // 2
    wkr = w_in[:, 4 * RET_W + MLA_Q_RANK + MLA_KV_RANK:]
    z64 = jnp.zeros((D, MLA_NOPE), F32)
    z32 = jnp.zeros((D, HEAD_SLAB - MLA_NOPE - MLA_ROPE), F32)
    wkr2 = jnp.concatenate([z64, wkr, z32, z64, wkr[:, half:], wkr[:, :half], z32], axis=1).astype(BF16)
    wq = w_uq.reshape(MLA_Q_RANK, MLA_HEADS, MLA_NOPE + MLA_ROPE)
    nope, rope = wq[..., :MLA_NOPE], wq[..., MLA_NOPE:]
    zq64 = jnp.zeros_like(nope)
    zq32 = jnp.zeros_like(rope)
    q_slab = jnp.concatenate([nope, rope, zq32], axis=-1).reshape(MLA_Q_RANK, MLA_QW)
    q_sw = jnp.concatenate([zq64, rope[..., half:], rope[..., :half], zq32], axis=-1).reshape(MLA_Q_RANK, MLA_QW)
    wuq2 = jnp.concatenate([q_slab, q_sw], axis=1).astype(BF16)
    wkv = w_ukv.reshape(MLA_KV_RANK, MLA_HEADS, MLA_NOPE + MLA_V)
    wk = jnp.concatenate([wkv[..., :MLA_NOPE], jnp.zeros_like(wkv[..., :MLA_NOPE])], axis=-1)
    wk = wk.reshape(MLA_KV_RANK, MLA_QW).astype(BF16)
    wv = wkv[..., MLA_NOPE:].reshape(MLA_KV_RANK, MLA_VW).astype(BF16)
    return wkr2, wuq2, wk, wv


def kernel(x_prompt, x_sample, c, state_ret, cache_mla_ckv, cache_mla_krope, state_s5_re, state_s5_im, cache_na_k, cache_na_v, c_ctx, ada_w, ada_b, mix_pre_g, mix_post_g, ffn_pre_g, ffn_post_g, ffn_w_up, ffn_conv_w, ffn_conv_b, ffn_w_down, even_w_in, even_w_out, ret_logit, ret_gn, mla_q_norm, mla_w_uq, mla_kv_norm, mla_w_ukv, odd_w_in, odd_w_out, s5_lambda_re, s5_lambda_im, s5_log_step, s5_b_re, s5_b_im, s5_c_re, s5_c_im, s5_d, s5_glu_w, s5_glu_b, na_rpb):
    cvec = jnp.concatenate([c_ctx[None, :], c, jnp.zeros((8 - 1 - DEC_BATCH, D), F32)], axis=0)
    mods, (e_in, e_out, o_in, o_out, glu_w) = _ada_mods(cvec, ada_w, ada_b,
                                                        [even_w_in, even_w_out, odd_w_in, odd_w_out, s5_glu_w])
    row3 = lambda a: a.reshape(a.shape[0], 1, a.shape[1])
    mix_pre, mix_post, ffn_pre, ffn_post = row3(mix_pre_g), row3(mix_post_g), row3(ffn_pre_g), row3(ffn_post_g)
    conv_b = row3(ffn_conv_b)
    cos_t, sin_t = _rope_tables()
    kr_cache = jnp.pad(cache_mla_krope, ((0, 0), (0, 0), (0, 0), (MLA_NOPE, HEAD_SLAB - MLA_NOPE - MLA_ROPE)))
    na_k_cache = cache_na_k.reshape(DEC_BATCH, -1, PAST_LEN, NA_W)
    na_v_cache = cache_na_v.reshape(DEC_BATCH, -1, PAST_LEN, NA_W)
    perm = _s5_perm()
    xs = (x_prompt.reshape(N_TOK_P, D), x_sample.reshape(N_TOK_S, D))
    ret_states, na_kv = None, None
    new_ckv, new_kr, new_s5_re, new_s5_im = [], [], [], []
    for layer in range(DEPTH):
        j = layer // 2
        if layer % 2 == 0:
            wkr2, wuq2, wk, wv = _mla_weights(even_w_in[j], mla_w_uq[j], mla_w_ukv[j])
            qkvg, qp, kp, v, ckvn, kr = _in_even(xs, mods, mix_pre, layer, j, e_in, wkr2, row3(mla_q_norm),
                                                 row3(mla_kv_norm), wuq2, wk, wv, cos_t, sin_t)
            logit = jnp.transpose(ret_logit[j]).reshape(RET_HEADS, 2, 1, 1)
            gn = row3(ret_gn)
            yr_c, ret_states = _retention(qkvg, logit, gn, j, None, seq_len=SEQ, n_seq=BATCH, n_blk=4, row0=0,
                                          emit_state=True, prev_states=ret_states)
            (yr_l,) = _retention(qkvg, logit, gn, j, state_ret, seq_len=DEC_SEQ, n_seq=DEC_BATCH, n_blk=1,
                                 row0=N_TOK_P, emit_state=False)
            ym_c = _mla_ctx(qp, kp, v)
            kp_c, v_c = _mla_cache(cache_mla_ckv, kr_cache, j, wk, wv)
            ym_l, w_up, w_down = _mla_lat(qp, kp, v, kp_c, v_c, layer, ffn_w_up, ffn_w_down)
            mix, w_out, s5 = (yr_c, yr_l, ym_c, ym_l), e_out, None
            new_ckv.append(ckvn.reshape(BATCH, SEQ, MLA_KV_RANK))
            new_kr.append(kr[:, MLA_NOPE:MLA_NOPE + MLA_ROPE].reshape(BATCH, SEQ, MLA_ROPE))
        else:
            u, qkv, *na_kv = _in_odd(xs, mods, mix_pre, layer, j, o_in, prev_kv=na_kv)
            m, n2, p2, a = _s5_prep(s5_lambda_re[j], s5_lambda_im[j], s5_log_step[j], s5_b_re[j], s5_b_im[j],
                                    s5_c_re[j], s5_c_im[j])
            h0c = jnp.zeros((S5_PAIRS, 2, 2, BATCH, S5_PL), F32)
            h0 = jnp.stack([state_s5_re[:, j], state_s5_im[:, j]], axis=0)
            h0l = jnp.transpose(h0.reshape(2, DEC_BATCH, 2, S5_PAIRS, S5_PL), (3, 2, 0, 1, 4))
            y_raw, fin = _s5_scan(u, perm, m, n2, p2, a, h0c, h0l)
            yn_c = _na_ctx(qkv)
            yn_l, w_up, w_down = _na_lat(qkv, na_k_cache, na_v_cache, j, _na_table(na_rpb[j]), layer, ffn_w_up,
                                         ffn_w_down)
            mix, w_out, s5 = (y_raw, u, yn_c, yn_l), o_out, (row3(s5_d), glu_w, row3(s5_glu_b))
            st = jnp.transpose(fin.reshape(S5_PAIRS, 2, 2, BATCH, 2, S5_P), (2, 3, 1, 0, 4, 5))
            st = st.reshape(2, BATCH, 2, S5_GROUPS, S5_P)
            new_s5_re.append(st[0])
            new_s5_im.append(st[1])
        xs = _ffn(xs, mix, mods, mix_post, ffn_pre, ffn_post, layer, j, w_out, w_up, ffn_conv_w, conv_b, w_down, s5)
    stack = lambda a: jnp.stack(a, axis=1)
    heads = lambda a: a.reshape(BATCH, a.shape[1], SEQ, NA_HEADS, NA_DIM)
    return (xs[0].reshape(BATCH, SEQ, D), xs[1].reshape(DEC_BATCH, DEC_SEQ, D), ret_states, stack(new_ckv),
            stack(new_kr), stack(new_s5_re), stack(new_s5_im), heads(na_kv[0]), heads(na_kv[1]))
```

```python
import functools

import numpy as np
import jax
import jax.numpy as jnp
from jax import lax
from jax.experimental import pallas as pl
from jax.experimental.pallas import tpu as pltpu

F32 = jnp.float32
BF16 = jnp.bfloat16

D = 1024
BATCH = 16
SEQ = 256
DEPTH = 4
DEC_BATCH = 2
DEC_SEQ = 2048
PAST_LEN = 512
GRID_W = 64
GRID_H = DEC_SEQ // GRID_W
EPS = 1e-6
LOG2E = 1.4426950408889634

RET_HEADS = 4
RET_W = 512
RET_DIM = 128
RET_CHUNK = 256

MLA_HEADS = 8
MLA_NOPE = 64
MLA_ROPE = 32
MLA_V = 64
MLA_Q_RANK = 256
MLA_KV_RANK = 128
MLA_SCALE = (MLA_NOPE + MLA_ROPE) ** -0.5
ROPE_BASE = 10000.0
HEAD_SLAB = 128
MLA_QW = MLA_HEADS * HEAD_SLAB
MLA_VW = MLA_HEADS * MLA_V

S5_W = 512
S5_GROUP = 16
S5_GROUPS = 32
S5_P = 64
S5_CHUNK = 16
S5_PAIRS = S5_GROUPS // 2

NA_HEADS = 8
NA_W = 512
NA_DIM = 64
NA_WIN_R = 8
NA_WIN_C = 16
NA_SCALE = NA_DIM ** -0.5
NA_QROWS = 4
NA_KROWS = 12
NA_NDR = 2 * NA_WIN_R - 1
NA_NDC = 2 * NA_WIN_C - 1

D_FF = 2816
FF_CHUNK = 256
FF_NCHUNK = D_FF // FF_CHUNK
FF_EXT = 16

TM = 512
TQ = 256
FF_SUB = SEQ
FF_NSUB = TM // FF_SUB
HALO = 16
N_TOK_P = BATCH * SEQ
N_TOK_S = DEC_BATCH * DEC_SEQ
N_TOK = N_TOK_P + N_TOK_S
NT_P = N_TOK_P // TM
NT_S = N_TOK_S // TM
NT = NT_P + NT_S
TILES_PER_DEC = DEC_SEQ // TM

VMEM_LIMIT = 56 * 1024 * 1024


def _cp(*sem):
    return pltpu.CompilerParams(dimension_semantics=sem, vmem_limit_bytes=VMEM_LIMIT)


def _dot(a, b):
    return jnp.dot(a, b, preferred_element_type=F32)


def _dot_nt(a, b):
    return lax.dot_general(a, b, (((1,), (1,)), ((), ())), preferred_element_type=F32)


def _dot_tn(a, b):
    return lax.dot_general(a, b, (((0,), (0,)), ((), ())), preferred_element_type=F32)


def _rms(x, g):
    return x * lax.rsqrt(jnp.mean(x * x, axis=-1, keepdims=True) + EPS) * g


def _sigmoid(x):
    return 1.0 / (1.0 + jnp.exp(-x))


def _silu(x):
    return x * _sigmoid(x)


def _cmul(ar, ai, br, bi):
    return ar * br - ai * bi, ar * bi + ai * br


def _mrow(i):
    return jnp.where(i < NT_P, 0, 1 + (i - NT_P) // TILES_PER_DEC)


def _full(shape):
    n = len(shape)
    return pl.BlockSpec(shape, lambda *_: (0,) * n)


def _layer(layer, shape):
    n = len(shape)
    return pl.BlockSpec((None,) + shape, lambda *_: (layer,) + (0,) * n)


def _flat_layer(j, rows, cols):
    return pl.BlockSpec((rows, cols), lambda *_: (j, 0))


def _mod_spec(layer):
    return pl.BlockSpec((None, None, 1, 6 * D), lambda i: (layer, _mrow(i), 0, 0))


def _row(width):
    return pl.BlockSpec((TM, width), lambda i: (i, 0))


def _ctx_row(width):
    return pl.BlockSpec((TM, width), lambda i: (jnp.minimum(i, NT_P - 1), 0))


def _lat_row(width):
    return pl.BlockSpec((TM, width), lambda i: (jnp.maximum(i - NT_P, 0), 0))


def _pick(a_ref, b_ref):
    return jnp.where(pl.program_id(0) < NT_P, a_ref[...], b_ref[...])


def _ada_kernel(c_ref, w_ref, b_ref, *rest):
    n_w = (len(rest) - 1) // 2
    o_ref = rest[n_w]
    o_ref[...] = _dot(_silu(c_ref[...]).astype(BF16), w_ref[...].astype(BF16)) + b_ref[...]
    for src, dst in zip(rest[:n_w], rest[n_w + 1:]):
        dst[...] = src[...].astype(BF16)


def _ada_mods(cvec, ada_w, ada_b, weights):
    nb = 4
    bn = 6 * D // nb
    steps = DEPTH * nb
    flat = [w.reshape(-1, w.shape[-1]) for w in weights]
    slab = lambda w: pl.BlockSpec((w.shape[0] // steps, w.shape[1]), lambda l, n: (l * nb + n, 0))
    out = pl.pallas_call(
        _ada_kernel,
        grid=(DEPTH, nb),
        in_specs=[
            pl.BlockSpec((8, D), lambda l, n: (0, 0)),
            pl.BlockSpec((None, D, bn), lambda l, n: (l, 0, n)),
            pl.BlockSpec((None, 1, bn), lambda l, n: (l, 0, n)),
        ] + [slab(w) for w in flat],
        out_specs=[pl.BlockSpec((None, 8, bn), lambda l, n: (l, 0, n))] + [slab(w) for w in flat],
        out_shape=[jax.ShapeDtypeStruct((DEPTH, 8, 6 * D), F32)]
        + [jax.ShapeDtypeStruct(w.shape, BF16) for w in flat],
        compiler_params=_cp("arbitrary", "arbitrary"),
        name="ada_mods",
    )(cvec, ada_w, ada_b.reshape(DEPTH, 1, 6 * D), *flat)
    return out[0][:, :3].reshape(DEPTH, 3, 1, 6 * D), out[1:]


def _cast_ffn_slab(wu32_ref, wd32_ref, wu_ref, wd_ref):
    wu_ref[...] = wu32_ref[...].astype(BF16)
    wd_ref[...] = wd32_ref[...].astype(BF16)


def _cast_ffn_specs(layer, n_inner):
    n = DEC_BATCH * n_inner
    ru, rd = D // n, D_FF // n
    cast_in = [pl.BlockSpec((None, ru, 2 * D_FF), lambda b, t: (layer, b * n_inner + t, 0)),
               pl.BlockSpec((None, rd, D), lambda b, t: (layer, b * n_inner + t, 0))]
    cast_out = [pl.BlockSpec((ru, 2 * D_FF), lambda b, t: (b * n_inner + t, 0)),
                pl.BlockSpec((rd, D), lambda b, t: (b * n_inner + t, 0))]
    cast_shape = [jax.ShapeDtypeStruct((D, 2 * D_FF), BF16), jax.ShapeDtypeStruct((D_FF, D), BF16)]
    return cast_in, cast_out, cast_shape


def _in_even_kernel(xc_ref, xl_ref, mod_ref, g_ref, w_ref, wkr_ref, qn_ref, kvn_ref, wuq_ref, wk_ref, wv_ref, cos_ref,
                    sin_ref, qkvg_ref, qp_ref, kp_ref, v_ref, ckv_ref, kr_ref):
    x = _pick(xc_ref, xl_ref)
    mod = mod_ref[...]
    h = _rms(x, g_ref[...]) * (1.0 + mod[:, D:2 * D]) + mod[:, :D]
    hb = h.astype(BF16)
    o = 4 * RET_W
    for part in range(4):
        cols = slice(part * RET_W, (part + 1) * RET_W)
        qkvg_ref[:, cols] = _dot(hb, w_ref[:, cols]).astype(BF16)
    r = _dot(hb, w_ref[:, o:o + MLA_Q_RANK + MLA_KV_RANK])
    cq = r[:, :MLA_Q_RANK]
    ckv_raw = r[:, MLA_Q_RANK:]
    r2 = _dot(hb, wkr_ref[...])
    kr = r2[:, :HEAD_SLAB]
    krs = r2[:, HEAD_SLAB:]
    cosf = cos_ref[...]
    sinf = sin_ref[...]
    q2 = _dot(_rms(cq, qn_ref[...]).astype(BF16), wuq_ref[...])
    ckvn = _rms(ckv_raw, kvn_ref[...])

    @pl.when(pl.program_id(0) < NT_P)
    def _():
        ckv_ref[...] = ckvn
        kr_ref[...] = kr

    cb = ckvn.astype(BF16)
    kp = _dot(cb, wk_ref[...])
    v_ref[...] = _dot(cb, wv_ref[...]).astype(BF16)
    krr = kr * cosf + krs * sinf
    for hh in range(MLA_HEADS):
        sl = slice(hh * HEAD_SLAB, (hh + 1) * HEAD_SLAB)
        ss = slice(MLA_QW + hh * HEAD_SLAB, MLA_QW + (hh + 1) * HEAD_SLAB)
        qp_ref[:, sl] = ((q2[:, sl] * cosf + q2[:, ss] * sinf) * (MLA_SCALE * LOG2E)).astype(BF16)
        kp_ref[:, sl] = (kp[:, sl] + krr).astype(BF16)


def _in_even(xs, mods, gains, layer, j, w_in, wkr2, qn, kvn, wuq2, wk, wv, cos_t, sin_t):
    pos_spec = pl.BlockSpec((TM, HEAD_SLAB), lambda i: (jnp.where(i < NT_P, 0, 1 + (i - NT_P) % TILES_PER_DEC), 0))
    return pl.pallas_call(
        _in_even_kernel,
        grid=(NT,),
        in_specs=[_ctx_row(D), _lat_row(D), _mod_spec(layer), _layer(layer, (1, D)), _flat_layer(j, D, w_in.shape[1]),
                  _full(wkr2.shape), _layer(j, (1, MLA_Q_RANK)), _layer(j, (1, MLA_KV_RANK)), _full(wuq2.shape),
                  _full(wk.shape), _full(wv.shape), pos_spec, pos_spec],
        out_specs=[_row(4 * RET_W), _row(MLA_QW), _row(MLA_QW), _row(MLA_VW), _ctx_row(MLA_KV_RANK),
                   _ctx_row(HEAD_SLAB)],
        out_shape=[
            jax.ShapeDtypeStruct((N_TOK, 4 * RET_W), BF16),
            jax.ShapeDtypeStruct((N_TOK, MLA_QW), BF16),
            jax.ShapeDtypeStruct((N_TOK, MLA_QW), BF16),
            jax.ShapeDtypeStruct((N_TOK, MLA_VW), BF16),
            jax.ShapeDtypeStruct((N_TOK_P, MLA_KV_RANK), F32),
            jax.ShapeDtypeStruct((N_TOK_P, HEAD_SLAB), F32),
        ],
        compiler_params=_cp("arbitrary"),
        name="in_even",
    )(*xs, mods, gains, w_in, wkr2, qn, kvn, wuq2, wk, wv, cos_t, sin_t)


def _mla_cache_kernel(ckv_ref, kr_ref, wk_ref, wv_ref, kp_ref, v_ref):
    cb = ckv_ref[...].astype(BF16)
    kp = _dot(cb, wk_ref[...])
    kr = kr_ref[...]
    for hh in range(MLA_HEADS):
        sl = slice(hh * HEAD_SLAB, (hh + 1) * HEAD_SLAB)
        kp_ref[:, sl] = (kp[:, sl] + kr).astype(BF16)
    v_ref[...] = _dot(cb, wv_ref[...]).astype(BF16)


def _mla_cache(ckv_c, kr_slab, j, wk, wv):
    row = lambda w: pl.BlockSpec((PAST_LEN, w), lambda b: (b, 0))
    cache = lambda w: pl.BlockSpec((None, None, PAST_LEN, w), lambda b: (b, j, 0, 0))
    return pl.pallas_call(
        _mla_cache_kernel,
        grid=(DEC_BATCH,),
        in_specs=[cache(MLA_KV_RANK), cache(HEAD_SLAB), _full(wk.shape), _full(wv.shape)],
        out_specs=[row(MLA_QW), row(MLA_VW)],
        out_shape=[jax.ShapeDtypeStruct((DEC_BATCH * PAST_LEN, MLA_QW), BF16),
                   jax.ShapeDtypeStruct((DEC_BATCH * PAST_LEN, MLA_VW), BF16)],
        compiler_params=_cp("parallel"),
        name="mla_cache",
    )(ckv_c, kr_slab, wk, wv)


def _ret_kernel(lg_ref, q_ref, k_ref, v_ref, g_ref, gn_ref, *rest, seq_len, n_blk, emit_state, n_prev, zero_init):
    if zero_init:
        s0 = lambda b, d: jnp.zeros((RET_DIM, RET_DIM), F32)
    else:
        s0_ref, rest = rest[0], rest[1:]
        s0 = lambda b, d: s0_ref[b, d]
    prev_refs, rest = rest[:min(n_prev, 1)], rest[min(n_prev, 1):]
    if emit_state:
        y_ref, st_ref, sf_scr, sb_scr, dm_scr, w_scr = rest
    else:
        y_ref, sf_scr, sb_scr, dm_scr, w_scr = rest
    C = RET_CHUNK
    nc = seq_len // C

    @pl.when(pl.program_id(1) == 0)
    def _():
        lg = -jnp.log(1.0 + jnp.exp(-lg_ref[...]))
        lg_f = lg[0]
        lg_b = lg[1]
        ii = lax.broadcasted_iota(jnp.int32, (C, C), 0)
        jj = lax.broadcasted_iota(jnp.int32, (C, C), 1)
        diff = (ii - jj).astype(F32)
        dm_scr[...] = (jnp.where(diff >= 0, jnp.exp(lg_f * jnp.maximum(diff, 0.0)), 0.0)
                       + jnp.where(diff <= 0, jnp.exp(lg_b * jnp.maximum(-diff, 0.0)), 0.0))
        pos = lax.broadcasted_iota(jnp.int32, (C, RET_DIM), 0).astype(F32)
        w_scr[0] = jnp.exp(lg_f * (pos + 1.0))
        w_scr[1] = jnp.exp(lg_f * (C - 1.0 - pos))
        w_scr[2] = jnp.exp(lg_b * (C - pos))
        w_scr[3] = jnp.exp(lg_b * pos)
        w_scr[4] = jnp.exp(lg_f * C) + jnp.zeros((C, RET_DIM), F32)
        w_scr[5] = jnp.exp(lg_b * C) + jnp.zeros((C, RET_DIM), F32)

    qw_f, kw_f, qw_b, kw_b = w_scr[0], w_scr[1], w_scr[2], w_scr[3]
    cd_f = w_scr[4, :RET_DIM, :]
    cd_b = w_scr[5, :RET_DIM, :]
    ld = lambda ref, rows: ref[rows, :].astype(F32)
    gn = gn_ref[...]
    dm = dm_scr[...]

    for b in range(n_blk):
        rows_of = lambda n: slice(b * seq_len + n * C, b * seq_len + (n + 1) * C)
        sf = lambda n: sf_scr.at[b * (nc + 1) + n]
        sb = lambda n: sb_scr.at[b * (nc + 1) + n]
        sf(0)[...] = s0(b, 0)
        for n in range(nc):
            kv = _dot_tn((ld(k_ref, rows_of(n)) * kw_f).astype(BF16), v_ref[rows_of(n), :])
            sf(n + 1)[...] = cd_f * sf(n)[...] + kv
        sb(nc)[...] = s0(b, 1)
        for n in reversed(range(nc)):
            kv = _dot_tn((ld(k_ref, rows_of(n)) * kw_b).astype(BF16), v_ref[rows_of(n), :])
            sb(n)[...] = cd_b * sb(n + 1)[...] + kv
        if emit_state:
            for jp in range(n_prev):
                st_ref[b, jp] = prev_refs[0][b, jp]
            st_ref[b, n_prev, 0] = sf(nc)[...]
            st_ref[b, n_prev, 1] = sb(0)[...]
        for n in range(nc):
            rows = rows_of(n)
            q = ld(q_ref, rows) * (RET_DIM ** -0.5)
            s = _dot_nt(q.astype(BF16), k_ref[rows, :]) * dm
            o = (_dot(s.astype(BF16), v_ref[rows, :])
                 + _dot((q * qw_f).astype(BF16), sf(n)[...].astype(BF16))
                 + _dot((q * qw_b).astype(BF16), sb(n + 1)[...].astype(BF16)))
            mu = jnp.mean(o, axis=-1, keepdims=True)
            oc = o - mu
            var = jnp.mean(oc * oc, axis=-1, keepdims=True)
            on = oc * lax.rsqrt(var + EPS) * gn
            y_ref[rows, :] = (_silu(ld(g_ref, rows)) * on).astype(BF16)


def _retention(qkvg, logit, gn, j, s0, *, seq_len, n_seq, n_blk, row0, emit_state, prev_states=None):
    n_prev = 0 if prev_states is None else prev_states.shape[1]
    nc = seq_len // RET_CHUNK
    rows = n_blk * seq_len
    blk0 = row0 // rows
    col = lambda part: pl.BlockSpec((rows, RET_DIM), lambda h, s: (blk0 + s, part * RET_HEADS + h))
    in_specs = [
        pl.BlockSpec((None, 2, 1, 1), lambda h, s: (h, 0, 0, 0)),
        col(0), col(1), col(2), col(3),
        pl.BlockSpec((None, 1, RET_DIM), lambda h, s: (j, 0, h)),
    ]
    out_specs = [pl.BlockSpec((rows, RET_DIM), lambda h, s: (s, h))]
    out_shape = [jax.ShapeDtypeStruct((n_seq * seq_len, RET_W), BF16)]
    args = [logit, qkvg, qkvg, qkvg, qkvg, gn]
    if s0 is not None:
        in_specs.append(pl.BlockSpec((n_blk, None, 2, None, RET_DIM, RET_DIM), lambda h, s: (s, j, 0, h, 0, 0)))
        args.append(s0)
    stacked = lambda n: pl.BlockSpec((n_blk, n, 2, None, RET_DIM, RET_DIM), lambda h, s: (s, 0, 0, h, 0, 0))
    if n_prev:
        in_specs.append(stacked(n_prev))
        args.append(prev_states)
    if emit_state:
        out_specs.append(stacked(n_prev + 1))
        out_shape.append(jax.ShapeDtypeStruct((n_seq, n_prev + 1, 2, RET_HEADS, RET_DIM, RET_DIM), F32))
    n_st = n_blk * (nc + 1)
    return pl.pallas_call(
        functools.partial(_ret_kernel, seq_len=seq_len, n_blk=n_blk, emit_state=emit_state, n_prev=n_prev,
                          zero_init=s0 is None),
        grid=(RET_HEADS, n_seq // n_blk),
        in_specs=in_specs,
        out_specs=out_specs,
        out_shape=out_shape,
        scratch_shapes=[pltpu.VMEM((n_st, RET_DIM, RET_DIM), F32), pltpu.VMEM((n_st, RET_DIM, RET_DIM), F32),
                        pltpu.VMEM((RET_CHUNK, RET_CHUNK), F32), pltpu.VMEM((6, RET_CHUNK, RET_DIM), F32)],
        compiler_params=_cp("parallel", "arbitrary"),
        name="retention_%d" % seq_len,
    )(*args)


def _softmax_pv(score_blocks, value_blocks):
    m = functools.reduce(jnp.maximum, [jnp.max(s, axis=-1, keepdims=True) for s in score_blocks])
    ps = [jnp.exp2(s - m) for s in score_blocks]
    l = functools.reduce(lambda a, b: a + b, [jnp.sum(p, axis=-1, keepdims=True) for p in ps])
    o = functools.reduce(lambda a, b: a + b, [_dot(p.astype(BF16), v) for p, v in zip(ps, value_blocks)])
    return o / l


def _low_half(shape):
    return lax.broadcasted_iota(jnp.int32, shape, 1) < 64


def _heads_pipelined(n_heads, scores, attend, y_ref, rows):
    low = _low_half((rows, 128))
    nxt = scores(0)
    outs = []
    for hh in range(n_heads):
        cur = nxt
        if hh + 1 < n_heads:
            nxt = scores(hh + 1)
        outs.append(attend(hh, cur))
        if hh % 2 == 1:
            hp = hh // 2
            y_ref[:, hp * 128:(hp + 1) * 128] = jnp.where(low, outs[hh - 1], outs[hh]).astype(BF16)


def _mla_ctx_kernel(q_ref, k_ref, v_ref, y_ref):
    slab = lambda hh: slice(hh * HEAD_SLAB, (hh + 1) * HEAD_SLAB)
    pair = lambda hh: slice(hh // 2 * 128, (hh // 2 + 1) * 128)
    scores = lambda hh: [_dot_nt(q_ref[:, slab(hh)], k_ref[:, slab(hh)])]
    attend = lambda hh, s: _softmax_pv(s, [v_ref[:, pair(hh)]])
    _heads_pipelined(MLA_HEADS, scores, attend, y_ref, SEQ)


def _mla_ctx(qp, kp, v):
    return pl.pallas_call(
        _mla_ctx_kernel,
        grid=(BATCH,),
        in_specs=[pl.BlockSpec((SEQ, MLA_QW), lambda b: (b, 0)), pl.BlockSpec((SEQ, MLA_QW), lambda b: (b, 0)),
                  pl.BlockSpec((SEQ, MLA_VW), lambda b: (b, 0))],
        out_specs=pl.BlockSpec((SEQ, MLA_VW), lambda b: (b, 0)),
        out_shape=jax.ShapeDtypeStruct((N_TOK_P, MLA_VW), BF16),
        compiler_params=_cp("parallel"),
        name="mla_ctx",
    )(qp, kp, v)


def _mla_lat_kernel(q_ref, k_ref, v_ref, kc_ref, vc_ref, wu32_ref, wd32_ref, y_ref, wu_ref, wd_ref):
    _cast_ffn_slab(wu32_ref, wd32_ref, wu_ref, wd_ref)
    slab = lambda hh: slice(hh * HEAD_SLAB, (hh + 1) * HEAD_SLAB)
    pair = lambda hh: slice(hh // 2 * 128, (hh // 2 + 1) * 128)
    scores = lambda hh: [_dot_nt(q_ref[:, slab(hh)], k_ref[:, slab(hh)]),
                         _dot_nt(q_ref[:, slab(hh)], kc_ref[:, slab(hh)])]
    attend = lambda hh, s: _softmax_pv(s, [v_ref[:, pair(hh)], vc_ref[:, pair(hh)]])
    _heads_pipelined(MLA_HEADS, scores, attend, y_ref, TQ)


def _mla_lat(qp, kp, v, kp_c, v_c, layer, w_up, w_down):
    seq_blk = N_TOK_P // DEC_SEQ
    nq = DEC_SEQ // TQ
    cast_in, cast_out, cast_shape = _cast_ffn_specs(layer, nq)
    return pl.pallas_call(
        _mla_lat_kernel,
        grid=(DEC_BATCH, nq),
        in_specs=[
            pl.BlockSpec((TQ, MLA_QW), lambda b, t: (N_TOK_P // TQ + b * nq + t, 0)),
            pl.BlockSpec((DEC_SEQ, MLA_QW), lambda b, t: (seq_blk + b, 0)),
            pl.BlockSpec((DEC_SEQ, MLA_VW), lambda b, t: (seq_blk + b, 0)),
            pl.BlockSpec((PAST_LEN, MLA_QW), lambda b, t: (b, 0)),
            pl.BlockSpec((PAST_LEN, MLA_VW), lambda b, t: (b, 0)),
        ] + cast_in,
        out_specs=[pl.BlockSpec((TQ, MLA_VW), lambda b, t: (b * nq + t, 0))] + cast_out,
        out_shape=[jax.ShapeDtypeStruct((N_TOK_S, MLA_VW), BF16)] + cast_shape,
        compiler_params=_cp("parallel", "parallel"),
        name="mla_lat",
    )(qp, kp, v, kp_c, v_c, w_up, w_down)


def _gelu_tanh(x):
    return 0.5 * x * (1.0 + jnp.tanh(np.sqrt(2.0 / np.pi).astype(np.float32) * (x + 0.044715 * (x * x * x))))


def _ffn_kernel(*refs, odd):
    xc_ref, xl_ref, xp_ref, xn_ref = refs[:4]
    if odd:
        (yr_ref, yrp_ref, yrn_ref, u_ref, up_ref, un_ref, bc_ref, bl_ref, bp_ref, bn_ref, mod_ref, gmix_ref, gpre_ref,
         gpost_ref, wo_ref, d_ref, gw_ref, gb_ref, wu_ref, cw_ref, cb_ref, wd_ref, oc_ref, ol_ref, h_scr,
         act_scr) = refs[4:]
    else:
        (ac_ref, al_ref, ap_ref, an_ref, bc_ref, bl_ref, bp_ref, bn_ref, mod_ref, gmix_ref, gpre_ref, gpost_ref,
         wo_ref, wu_ref, cw_ref, cb_ref, wd_ref, oc_ref, ol_ref, h_scr, act_scr) = refs[4:]
    i = pl.program_id(0)
    is_lat = i >= NT_P
    t = (i - NT_P) % TILES_PER_DEC
    has_prev = jnp.logical_and(is_lat, t != 0)
    has_next = jnp.logical_and(is_lat, t != TILES_PER_DEC - 1)
    mod = mod_ref[...]
    cat = lambda main, prev_ref, next_ref: jnp.concatenate([main, prev_ref[...], next_ref[...]], axis=0)

    if odd:
        y = _gelu_tanh(cat(yr_ref[...], yrp_ref, yrn_ref) + d_ref[...] * cat(u_ref[...], up_ref, un_ref))
        ya = (y * _sigmoid(_dot(y.astype(BF16), gw_ref[...]) + gb_ref[...])).astype(BF16)
    else:
        ya = cat(_pick(ac_ref, al_ref), ap_ref, an_ref)
    yb = cat(_pick(bc_ref, bl_ref), bp_ref, bn_ref)
    half = ya.shape[1]
    r = _dot(ya, wo_ref[:half, :]) + _dot(yb, wo_ref[half:, :])
    x1 = cat(_pick(xc_ref, xl_ref), xp_ref, xn_ref) + mod[:, 2 * D:3 * D] * _rms(r, gmix_ref[...])

    shift = mod[:, 3 * D:4 * D]
    scale = mod[:, 4 * D:5 * D]
    gate = mod[:, 5 * D:6 * D]
    hall = _rms(x1, gpre_ref[...]) * (1.0 + scale) + shift
    x = x1[:TM]
    h = hall[:TM]
    blk = FF_SUB + FF_EXT
    rows = FF_NSUB * blk
    for k in range(FF_NSUB):
        h_scr[k * blk:k * blk + FF_SUB, :] = h[k * FF_SUB:(k + 1) * FF_SUB].astype(BF16)
        if k + 1 < FF_NSUB:
            after = jnp.where(is_lat, h[(k + 1) * FF_SUB:(k + 1) * FF_SUB + 8], 0.0)
            before = jnp.where(is_lat, h[(k + 1) * FF_SUB - 8:(k + 1) * FF_SUB], 0.0)
        else:
            after = jnp.where(has_next, hall[TM + HALO:TM + HALO + 8], 0.0)
            before = jnp.where(has_prev, hall[TM + HALO - 8:TM + HALO], 0.0)
        h_scr[k * blk + FF_SUB:(k + 1) * blk, :] = jnp.concatenate([after, before], axis=0).astype(BF16)
    hb = h_scr[...]

    def up(j):
        ca = slice(j * FF_CHUNK, (j + 1) * FF_CHUNK)
        cg = slice(D_FF + j * FF_CHUNK, D_FF + (j + 1) * FF_CHUNK)
        return (_dot(hb, wu_ref[:, ca]), ca), (_dot(hb, wu_ref[:, cg]), cg)

    def conv(part):
        u, cols = part
        cw = cw_ref[:, cols]
        return (cw[0:1, :] * pltpu.roll(u, 1, axis=0) + cw[1:2, :] * u + cw[2:3, :] * pltpu.roll(u, rows - 1, axis=0)
                + cb_ref[:, cols])

    nxt = up(0)
    for j in range(FF_NCHUNK):
        cur = nxt
        if j + 1 < FF_NCHUNK:
            nxt = up(j + 1)
        act = (_silu(conv(cur[1])) * conv(cur[0])).astype(BF16)
        for k in range(FF_NSUB):
            act_scr[k * FF_SUB:(k + 1) * FF_SUB, j * FF_CHUNK:(j + 1) * FF_CHUNK] = act[k * blk:k * blk + FF_SUB]
    out = x + gate * _rms(_dot(act_scr[...], wd_ref[...]), gpost_ref[...])

    @pl.when(i < NT_P)
    def _():
        oc_ref[...] = out

    @pl.when(i >= NT_P)
    def _():
        ol_ref[...] = out


def _halo_specs(width, n_rows, tile0):
    per = TM // HALO
    last = n_rows // HALO - 1
    prev = pl.BlockSpec((HALO, width), lambda i: (jnp.clip((i - tile0) * per - 1, 0, last), 0))
    nxt = pl.BlockSpec((HALO, width), lambda i: (jnp.clip((i - tile0 + 1) * per, 0, last), 0))
    return [prev, nxt]


def _ffn(xs, mix, mods, gmix, gpre, gpost, layer, j, w_out, wu, cw, cb, wd, s5=None):
    odd = s5 is not None
    pair = lambda w: [_ctx_row(w), _lat_row(w)]
    lat_halo = lambda w: _halo_specs(w, N_TOK_S, NT_P)
    all_halo = lambda w: _halo_specs(w, N_TOK, 0)
    resident = lambda shape: pl.BlockSpec(shape, lambda i: (0, 0), pipeline_mode=pl.Buffered(1))
    x_specs = pair(D) + lat_halo(D)
    x_args = [xs[0], xs[1], xs[1], xs[1]]
    if odd:
        yr, u, b_c, b_l = mix
        mix_specs = [_row(S5_W)] + all_halo(S5_W) + [_row(S5_W)] + all_halo(S5_W) + pair(NA_W) + lat_halo(NA_W)
        mix_args = [yr, yr, yr, u, u, u, b_c, b_l, b_l, b_l]
        s5_specs = [_layer(j, (1, S5_W)), _flat_layer(j, S5_W, S5_W), _layer(j, (1, S5_W))]
        s5_args = list(s5)
    else:
        a_c, a_l, b_c, b_l = mix
        mix_specs = pair(RET_W) + lat_halo(RET_W) + pair(MLA_VW) + lat_halo(MLA_VW)
        mix_args = [a_c, a_l, a_l, a_l, b_c, b_l, b_l, b_l]
        s5_specs, s5_args = [], []
    return pl.pallas_call(
        functools.partial(_ffn_kernel, odd=odd),
        grid=(NT,),
        in_specs=x_specs + mix_specs + [_mod_spec(layer), _layer(layer, (1, D)), _layer(layer, (1, D)),
                                        _layer(layer, (1, D)), _flat_layer(j, D, D)] + s5_specs
        + [resident((D, 2 * D_FF)), _layer(layer, (3, 2 * D_FF)), _layer(layer, (1, 2 * D_FF)), resident((D_FF, D))],
        out_specs=[_ctx_row(D), _lat_row(D)],
        out_shape=[jax.ShapeDtypeStruct((N_TOK_P, D), F32), jax.ShapeDtypeStruct((N_TOK_S, D), F32)],
        scratch_shapes=[pltpu.VMEM((FF_NSUB * (FF_SUB + FF_EXT), D), BF16), pltpu.VMEM((TM, D_FF), BF16)],
        compiler_params=_cp("arbitrary"),
        name="ffn_odd" if odd else "ffn_even",
    )(*x_args, *mix_args, mods, gmix, gpre, gpost, w_out, *s5_args, wu, cw, cb, wd)


def _in_odd_kernel(xc_ref, xl_ref, mod_ref, g_ref, w_ref, *rest, n_prev):
    np2 = 2 * min(n_prev, 1)
    prev_refs, (u_ref, qkv_ref, kc_ref, vc_ref) = rest[:np2], rest[np2:]
    mod = mod_ref[...]
    h = _rms(_pick(xc_ref, xl_ref), g_ref[...]) * (1.0 + mod[:, D:2 * D]) + mod[:, :D]
    r = _dot(h.astype(BF16), w_ref[...])
    u_ref[...] = r[:, :S5_W]
    qkv_ref[:, :NA_W] = (r[:, S5_W:S5_W + NA_W] * (NA_SCALE * LOG2E)).astype(BF16)
    qkv_ref[:, NA_W:] = r[:, S5_W + NA_W:].astype(BF16)

    @pl.when(pl.program_id(0) < NT_P)
    def _():
        for part, (out_ref, col0) in enumerate([(kc_ref, S5_W + NA_W), (vc_ref, S5_W + 2 * NA_W)]):
            for b in range(TM // SEQ):
                for jp in range(n_prev):
                    out_ref[b, jp] = prev_refs[part][b, jp]
                out_ref[b, n_prev] = r[b * SEQ:(b + 1) * SEQ, col0:col0 + NA_W]


def _in_odd(xs, mods, gains, layer, j, w, prev_kv=None):
    n_prev = 0 if prev_kv is None else prev_kv[0].shape[1]
    nb = TM // SEQ
    stacked = lambda n: pl.BlockSpec((nb, n, SEQ, NA_W), lambda i: (jnp.minimum(i, NT_P - 1), 0, 0, 0))
    leaf = jax.ShapeDtypeStruct((BATCH, n_prev + 1, SEQ, NA_W), F32)
    prev_specs = [stacked(n_prev)] * 2 if n_prev else []
    return pl.pallas_call(
        functools.partial(_in_odd_kernel, n_prev=n_prev),
        grid=(NT,),
        in_specs=[_ctx_row(D), _lat_row(D), _mod_spec(layer), _layer(layer, (1, D)), _flat_layer(j, D, w.shape[1])]
        + prev_specs,
        out_specs=[_row(S5_W), _row(3 * NA_W), stacked(n_prev + 1), stacked(n_prev + 1)],
        out_shape=[jax.ShapeDtypeStruct((N_TOK, S5_W), F32), jax.ShapeDtypeStruct((N_TOK, 3 * NA_W), BF16), leaf, leaf],
        compiler_params=_cp("arbitrary"),
        name="in_odd",
    )(*xs, mods, gains, w, *(prev_kv or ()))


S5_TS = S5_CHUNK * S5_GROUP
S5_PL = 2 * S5_P
S5_PREP_PAIRS = 2


def _s5_prep_kernel(*refs):
    ins, outs = refs[:7], refs[7:]
    for q in range(S5_PREP_PAIRS):
        _s5_prep_pair(*[r.at[:, q] for r in ins], *[r.at[q] for r in outs])


def _s5_prep_pair(lre_ref, lim_ref, ls_ref, btr_ref, bti_ref, cr_ref, ci_ref, m_ref, n_ref, p_ref, a_ref, ct_scr):
    T = S5_CHUNK
    S = S5_GROUP
    hi = lax.Precision.HIGHEST
    low = lax.broadcasted_iota(jnp.int32, (S, S5_PL), 1) < S5_P
    half = [low, jnp.logical_not(low)]
    pick = lambda e, v: jnp.where(half[e], v, 0.0)
    nt = (((1,), (1,)), ((), ()))
    kps = [[None, None], [None, None]]
    for d in range(2):
        lre = lre_ref[d]
        lim = lim_ref[d]
        step = jnp.exp(ls_ref[d])
        mag = jnp.exp(lre * step)
        are = mag * jnp.cos(lim * step)
        aim = mag * jnp.sin(lim * step)
        den = lre * lre + lim * lim
        zr, zi = _cmul(are - 1.0, aim, lre / den, -lim / den)
        bbr, bbi = _cmul(zr, zi, btr_ref[d], bti_ref[d])
        cr = cr_ref[d]
        ci = ci_ref[d]
        pr = jnp.ones_like(are)
        pi = jnp.zeros_like(are)
        for k in range(T + 1):
            er, ei = _cmul(cr, ci, pr, pi)
            if k < T:
                jn = T - 1 - k if d == 0 else k
                wr, wi = _cmul(pr, pi, bbr, bbi)
                for e in range(2):
                    rows = slice(e * S5_TS + jn * S, e * S5_TS + (jn + 1) * S)
                    n_ref[d, rows, 0:S5_PL] = pick(e, wr).astype(BF16)
                    n_ref[d, rows, S5_PL:2 * S5_PL] = pick(e, wi).astype(BF16)
                jc = k if d == 0 else T - 1 - k
                ct_scr[0, jc * S:(jc + 1) * S, :] = er
                ct_scr[1, jc * S:(jc + 1) * S, :] = ei
            if k >= 1:
                t = k - 1 if d == 0 else T - k
                for e in range(2):
                    rows = slice(e * S5_TS + t * S, e * S5_TS + (t + 1) * S)
                    p_ref[rows, 2 * d * S5_PL:(2 * d + 1) * S5_PL] = pick(e, er).astype(BF16)
                    p_ref[rows, (2 * d + 1) * S5_PL:(2 * d + 2) * S5_PL] = pick(e, -ei).astype(BF16)
            if k == T:
                a_ref[d, 0] = pr
                a_ref[d, 1] = pi
            pr, pi = _cmul(pr, pi, are, aim)
        for e in range(2):
            kd = (lax.dot_general(pick(e, bbr), ct_scr[0], nt, precision=hi, preferred_element_type=F32)
                  - lax.dot_general(pick(e, bbi), ct_scr[1], nt, precision=hi, preferred_element_type=F32))
            kps[d][e] = jnp.concatenate([kd, jnp.zeros_like(kd)], axis=1)
    for e in range(2):
        for t in range(T):
            fwd = pltpu.roll(kps[0][e], t * S, axis=1)[:, :S5_TS]
            bwd = pltpu.roll(kps[1][e], (2 * S5_TS - (T - 1 - t) * S) % (2 * S5_TS), axis=1)[:, :S5_TS]
            m_ref[e, t * S:(t + 1) * S, :] = (fwd + bwd).astype(BF16)


def _s5_prep(lre, lim, ls, b_re, b_im, c_re, c_im):
    NP, S, P = S5_PAIRS, S5_GROUP, S5_P
    vec = lambda a: a.reshape(2, NP, 1, S5_PL)
    b_lay = lambda a: jnp.transpose(a.reshape(2, NP, 2, P, S), (0, 1, 4, 2, 3)).reshape(2, NP, S, S5_PL)
    c_lay = lambda a: jnp.transpose(a.reshape(2, NP, 2, S, P), (0, 1, 3, 2, 4)).reshape(2, NP, S, S5_PL)
    args = (vec(lre), vec(lim), vec(jnp.repeat(ls, P, axis=-1)), b_lay(b_re), b_lay(b_im), c_lay(c_re), c_lay(c_im))
    PB = S5_PREP_PAIRS
    vspec = pl.BlockSpec((2, PB, 1, S5_PL), lambda g: (0, g, 0, 0))
    mspec = pl.BlockSpec((2, PB, S, S5_PL), lambda g: (0, g, 0, 0))
    lead = lambda shape: pl.BlockSpec((PB,) + shape, lambda g: (g,) + (0,) * len(shape))
    return pl.pallas_call(
        _s5_prep_kernel,
        grid=(NP // PB,),
        in_specs=[vspec, vspec, vspec, mspec, mspec, mspec, mspec],
        out_specs=[lead((2, S5_TS, S5_TS)), lead((2, 2 * S5_TS, 2 * S5_PL)), lead((2 * S5_TS, 4 * S5_PL)),
                   lead((2, 2, 1, S5_PL))],
        out_shape=[jax.ShapeDtypeStruct((NP, 2, S5_TS, S5_TS), BF16),
                   jax.ShapeDtypeStruct((NP, 2, 2 * S5_TS, 2 * S5_PL), BF16),
                   jax.ShapeDtypeStruct((NP, 2 * S5_TS, 4 * S5_PL), BF16),
                   jax.ShapeDtypeStruct((NP, 2, 2, 1, S5_PL), F32)],
        scratch_shapes=[pltpu.VMEM((S5_PREP_PAIRS, 2, S5_TS, S5_PL), F32)],
        compiler_params=_cp("parallel"),
        name="s5_prep",
    )(*args)


S5_NCH = N_TOK // S5_CHUNK
S5_ROWS_P = N_TOK_P // S5_CHUNK
S5_NC_P = SEQ // S5_CHUNK
S5_NC_S = DEC_SEQ // S5_CHUNK
S5_GPB = 8
S5_PPB = S5_GPB // 2
S5_XL = 8 * 128


def _s5_perm():
    r = np.arange(S5_XL)
    dst = (r // S5_GROUP % S5_GPB) * 128 + (r // 128) * S5_GROUP + r % S5_GROUP
    perm = np.zeros((S5_XL, S5_XL), np.float32)
    perm[r, dst] = 1.0
    return jnp.asarray(perm, BF16)


def _s5_kernel(u_ref, perm_ref, m_ref, n_ref, p_ref, a_ref, h0c_ref, h0l_ref, y_ref, fin_ref, z_scr, up_scr, e_scr,
               hin_scr, yc_scr):
    T = S5_CHUNK
    W = S5_PL
    for t in range(T):
        z_scr[t // 8, :, (t % 8) * 128:(t % 8 + 1) * 128] = u_ref[pl.ds(t, S5_NCH, stride=T), :].astype(BF16)
    perm = perm_ref[...]
    for j in range(2):
        up_scr[j] = _dot(z_scr[j], perm).astype(BF16)

    def scan(pp, d, h0_ref, n_chunks, n_seq, row0):
        are = a_ref[pp, d, 0]
        aim = a_ref[pp, d, 1]
        hr = h0_ref[pp, d, 0]
        hi = h0_ref[pp, d, 1]
        order = range(n_chunks) if d == 0 else reversed(range(n_chunks))
        for c in order:
            rows = pl.ds(row0 + c, n_seq, stride=n_chunks)
            hin_scr[2 * d, rows, :] = hr
            hin_scr[2 * d + 1, rows, :] = hi
            er = e_scr[2 * d, rows, :]
            ei = e_scr[2 * d + 1, rows, :]
            hr, hi = are * hr - aim * hi + er, are * hi + aim * hr + ei
        return hr, hi

    for pp in range(S5_PPB):
        us = []
        for e in range(2):
            sl = slice((2 * pp + e) * 128, (2 * pp + e + 1) * 128)
            us.append(jnp.concatenate([up_scr[0, :, sl], up_scr[1, :, sl]], axis=1))
        u2 = jnp.concatenate(us, axis=1)
        for d in range(2):
            ed = _dot(u2, n_ref[pp, d])
            e_scr[2 * d] = ed[:, :W]
            e_scr[2 * d + 1] = ed[:, W:]
        for d in range(2):
            hr, hi = scan(pp, d, h0c_ref, S5_NC_P, BATCH, 0)
            fin_ref[pp, d, 0] = hr
            fin_ref[pp, d, 1] = hi
            scan(pp, d, h0l_ref, S5_NC_S, DEC_BATCH, S5_ROWS_P)
        hin = jnp.concatenate([hin_scr[k] for k in range(4)], axis=1).astype(BF16)
        for e in range(2):
            y = _dot(us[e], m_ref[pp, e]) + _dot_nt(hin, p_ref[pp, e * S5_TS:(e + 1) * S5_TS, :])
            sl = slice((2 * pp + e) * 128, (2 * pp + e + 1) * 128)
            for j in range(2):
                yc_scr[j, :, sl] = y[:, j * 128:(j + 1) * 128]

    for j in range(2):
        yp = yc_scr[j]
        y_hi = yp.astype(BF16)
        y_lo = (yp - y_hi.astype(F32)).astype(BF16)
        r = _dot_nt(y_hi, perm) + _dot_nt(y_lo, perm)
        for k in range(8):
            y_ref[pl.ds(8 * j + k, S5_NCH, stride=T), :] = r[:, k * 128:(k + 1) * 128]


def _s5_scan(u, perm, m, n2, p2, a, h0c, h0l):
    nb = S5_GROUPS // S5_GPB
    lead = lambda shape: pl.BlockSpec((S5_PPB,) + shape, lambda w: (w,) + (0,) * len(shape))
    col = pl.BlockSpec((N_TOK, 128), lambda w: (0, w))
    return pl.pallas_call(
        _s5_kernel,
        grid=(nb,),
        in_specs=[col, _full(perm.shape), lead((2, S5_TS, S5_TS)), lead((2, 2 * S5_TS, 2 * S5_PL)),
                  lead((2 * S5_TS, 4 * S5_PL)), lead((2, 2, 1, S5_PL)), lead((2, 2, BATCH, S5_PL)),
                  lead((2, 2, DEC_BATCH, S5_PL))],
        out_specs=[col, lead((2, 2, BATCH, S5_PL))],
        out_shape=[jax.ShapeDtypeStruct((N_TOK, S5_W), F32),
                   jax.ShapeDtypeStruct((S5_PAIRS, 2, 2, BATCH, S5_PL), F32)],
        scratch_shapes=[pltpu.VMEM((2, S5_NCH, S5_XL), BF16), pltpu.VMEM((2, S5_NCH, S5_XL), BF16),
                        pltpu.VMEM((4, S5_NCH, S5_PL), F32), pltpu.VMEM((4, S5_NCH, S5_PL), F32),
                        pltpu.VMEM((2, S5_NCH, S5_XL), F32)],
        compiler_params=_cp("parallel"),
        name="s5_scan",
    )(u, perm, m, n2, p2, a, h0c, h0l)


def _na_heads(q_ref, keys, values, y_ref, rows, bias=None):
    low = _low_half((rows, 128))
    pair = lambda hh: slice(hh // 2 * 128, (hh // 2 + 1) * 128)

    def scores(hh):
        q = q_ref[:, pair(hh)]
        qm = jnp.where(low == (hh % 2 == 0), q, jnp.zeros_like(q))
        s = [_dot_nt(qm, k(pair(hh))) for k in keys]
        if bias is not None:
            s[0] = s[0] + bias(hh)
        return s

    attend = lambda hh, s: _softmax_pv(s, [v(pair(hh)) for v in values])
    _heads_pipelined(NA_HEADS, scores, attend, y_ref, rows)


def _na_ctx_kernel(q_ref, k_ref, v_ref, y_ref):
    _na_heads(q_ref, [lambda sl: k_ref[:, sl]], [lambda sl: v_ref[:, sl]], y_ref, SEQ)


def _na_ctx(qkv):
    col = lambda part: pl.BlockSpec((SEQ, NA_W), lambda b: (b, part))
    return pl.pallas_call(
        _na_ctx_kernel,
        grid=(BATCH,),
        in_specs=[col(0), col(1), col(2)],
        out_specs=pl.BlockSpec((SEQ, NA_W), lambda b: (b, 0)),
        out_shape=jax.ShapeDtypeStruct((N_TOK_P, NA_W), BF16),
        compiler_params=_cp("parallel"),
        name="na_ctx",
    )(qkv, qkv, qkv)


def _na_key_row0(rb):
    return jnp.clip(NA_QROWS * rb - NA_WIN_R // 2, 0, GRID_H - NA_KROWS)


def _na_lat_kernel(q_ref, ks_ref, vs_ref, kc_ref, vc_ref, tab_ref, wu32_ref, wd32_ref, y_ref, wu_ref, wd_ref):
    _cast_ffn_slab(wu32_ref, wd32_ref, wu_ref, wd_ref)
    rb = pl.program_id(1)
    u0 = _na_key_row0(rb)
    start = pl.multiple_of(u0 * GRID_W, GRID_W)
    nk = NA_KROWS * GRID_W
    low_t = _low_half((GRID_W, 128))

    def table_row(i, w):
        qr = NA_QROWS * rb + i
        kr = u0 + w
        rs = jnp.clip(qr - NA_WIN_R // 2, 0, GRID_H - NA_WIN_R)
        inside = jnp.logical_and(kr >= rs, kr < rs + NA_WIN_R)
        return jnp.where(inside, kr - qr + NA_WIN_R - 1, NA_NDR)

    idx = [[table_row(i, w) for w in range(NA_KROWS)] for i in range(NA_QROWS)]

    def bias(h):
        rows = [jnp.concatenate([jnp.where(low_t, tab_ref[h, idx[i][w]], tab_ref[h, idx[i][w + 1]])
                                 for w in range(0, NA_KROWS, 2)], axis=1) for i in range(NA_QROWS)]
        return jnp.concatenate(rows, axis=0)

    keys = [lambda sl: ks_ref[pl.ds(start, nk), sl], lambda sl: kc_ref[:, sl].astype(BF16)]
    values = [lambda sl: vs_ref[pl.ds(start, nk), sl], lambda sl: vc_ref[:, sl].astype(BF16)]
    _na_heads(q_ref, keys, values, y_ref, TQ, bias)


def _na_lat(qkv, k_c, v_c, j, table, layer, w_up, w_down):
    seq_blk = N_TOK_P // DEC_SEQ
    nb = GRID_H // NA_QROWS
    cast_in, cast_out, cast_shape = _cast_ffn_specs(layer, nb)
    return pl.pallas_call(
        _na_lat_kernel,
        grid=(DEC_BATCH, nb),
        in_specs=[
            pl.BlockSpec((TQ, NA_W), lambda b, r: (N_TOK_P // TQ + b * nb + r, 0)),
            pl.BlockSpec((DEC_SEQ, NA_W), lambda b, r: (seq_blk + b, 1)),
            pl.BlockSpec((DEC_SEQ, NA_W), lambda b, r: (seq_blk + b, 2)),
            pl.BlockSpec((None, None, PAST_LEN, NA_W), lambda b, r: (b, j, 0, 0)),
            pl.BlockSpec((None, None, PAST_LEN, NA_W), lambda b, r: (b, j, 0, 0)),
            pl.BlockSpec(table.shape, lambda b, r: (0, 0, 0, 0)),
        ] + cast_in,
        out_specs=[pl.BlockSpec((TQ, NA_W), lambda b, r: (b * nb + r, 0))] + cast_out,
        out_shape=[jax.ShapeDtypeStruct((N_TOK_S, NA_W), BF16)] + cast_shape,
        compiler_params=_cp("parallel", "arbitrary"),
        name="na_lat",
    )(qkv, qkv, qkv, k_c, v_c, table, w_up, w_down)


def _na_table_kernel(rpb_ref, t_ref):
    qc = lax.broadcasted_iota(jnp.int32, (GRID_W, 128), 0)
    kc = lax.broadcasted_iota(jnp.int32, (GRID_W, 128), 1) % GRID_W
    cs = jnp.clip(qc - NA_WIN_C // 2, 0, GRID_W - NA_WIN_C)
    in_band = jnp.logical_and(kc >= cs, kc < cs + NA_WIN_C)
    neg = jnp.full((GRID_W, 128), -jnp.inf, F32)

    def body(n, carry):
        x = jnp.broadcast_to(rpb_ref[n], (GRID_W, 128))
        t = pltpu.roll(x, 128 - (NA_WIN_C - 1), axis=1, stride=1, stride_axis=0)
        t_ref[n // NA_NDR, n % NA_NDR] = jnp.where(in_band, t * LOG2E, neg)
        return carry

    lax.fori_loop(0, NA_HEADS * NA_NDR, body, 0, unroll=8)
    for h in range(NA_HEADS):
        t_ref[h, NA_NDR] = neg


def _na_table(rpb):
    rows = jnp.pad(rpb.reshape(NA_HEADS * NA_NDR, 1, NA_NDC), ((0, 0), (0, 0), (0, GRID_W - NA_NDC)))
    rows = jnp.concatenate([rows, rows], axis=-1)
    return pl.pallas_call(
        _na_table_kernel,
        out_shape=jax.ShapeDtypeStruct((NA_HEADS, NA_NDR + 1, GRID_W, 128), F32),
        name="na_table",
    )(rows)


def _rope_tables():
    n_freq = MLA_ROPE // 4
    inv = ROPE_BASE ** (-jnp.arange(n_freq, dtype=F32) / n_freq)
    t = jnp.arange(DEC_SEQ)
    row = (t // GRID_W).astype(F32)
    colp = (t % GRID_W).astype(F32)
    ang = jnp.concatenate([row[:, None] * inv, colp[:, None] * inv], axis=-1)
    cos, sin = jnp.cos(ang), jnp.sin(ang)
    one = jnp.ones((DEC_SEQ, MLA_NOPE), F32)
    zero = jnp.zeros((DEC_SEQ, MLA_NOPE), F32)
    cos_s = jnp.concatenate([one, cos, cos, one[:, :32]], axis=-1)
    sin_s = jnp.concatenate([zero, -sin, sin, zero[:, :32]], axis=-1)
    cos_t = jnp.concatenate([jnp.ones((TM, HEAD_SLAB), F32), cos_s], axis=0)
    sin_t = jnp.concatenate([jnp.zeros((TM, HEAD_SLAB), F32), sin_s], axis=0)
    return cos_t, sin_t


def _mla_weights(w_in, w_uq, w_ukv):
    half = MLA_ROPE // 2
    wkr = w_in[:, 4 * RET_W + MLA_Q_RANK + MLA_KV_RANK:]
    z64 = jnp.zeros((D, MLA_NOPE), F32)
    z32 = jnp.zeros((D, HEAD_SLAB - MLA_NOPE - MLA_ROPE), F32)
    wkr2 = jnp.concatenate([z64, wkr, z32, z64, wkr[:, half:], wkr[:, :half], z32], axis=1).astype(BF16)
    wq = w_uq.reshape(MLA_Q_RANK, MLA_HEADS, MLA_NOPE + MLA_ROPE)
    nope, rope = wq[..., :MLA_NOPE], wq[..., MLA_NOPE:]
    zq64 = jnp.zeros_like(nope)
    zq32 = jnp.zeros_like(rope)
    q_slab = jnp.concatenate([nope, rope, zq32], axis=-1).reshape(MLA_Q_RANK, MLA_QW)
    q_sw = jnp.concatenate([zq64, rope[..., half:], rope[..., :half], zq32], axis=-1).reshape(MLA_Q_RANK, MLA_QW)
    wuq2 = jnp.concatenate([q_slab, q_sw], axis=1).astype(BF16)
    wkv = w_ukv.reshape(MLA_KV_RANK, MLA_HEADS, MLA_NOPE + MLA_V)
    wk = jnp.concatenate([wkv[..., :MLA_NOPE], jnp.zeros_like(wkv[..., :MLA_NOPE])], axis=-1)
    wk = wk.reshape(MLA_KV_RANK, MLA_QW).astype(BF16)
    wv = wkv[..., MLA_NOPE:].reshape(MLA_KV_RANK, MLA_VW).astype(BF16)
    return wkr2, wuq2, wk, wv


def kernel(x_prompt, x_sample, c, state_ret, cache_mla_ckv, cache_mla_krope, state_s5_re, state_s5_im, cache_na_k, cache_na_v, c_ctx, ada_w, ada_b, mix_pre_g, mix_post_g, ffn_pre_g, ffn_post_g, ffn_w_up, ffn_conv_w, ffn_conv_b, ffn_w_down, even_w_in, even_w_out, ret_logit, ret_gn, mla_q_norm, mla_w_uq, mla_kv_norm, mla_w_ukv, odd_w_in, odd_w_out, s5_lambda_re, s5_lambda_im, s5_log_step, s5_b_re, s5_b_im, s5_c_re, s5_c_im, s5_d, s5_glu_w, s5_glu_b, na_rpb):
    cvec = jnp.concatenate([c_ctx[None, :], c, jnp.zeros((8 - 1 - DEC_BATCH, D), F32)], axis=0)
    mods, (e_in, e_out, o_in, o_out, glu_w) = _ada_mods(cvec, ada_w, ada_b,
                                                        [even_w_in, even_w_out, odd_w_in, odd_w_out, s5_glu_w])
    row3 = lambda a: a.reshape(a.shape[0], 1, a.shape[1])
    mix_pre, mix_post, ffn_pre, ffn_post = row3(mix_pre_g), row3(mix_post_g), row3(ffn_pre_g), row3(ffn_post_g)
    conv_b = row3(ffn_conv_b)
    cos_t, sin_t = _rope_tables()
    kr_cache = jnp.pad(cache_mla_krope, ((0, 0), (0, 0), (0, 0), (MLA_NOPE, HEAD_SLAB - MLA_NOPE - MLA_ROPE)))
    na_k_cache = cache_na_k.reshape(DEC_BATCH, -1, PAST_LEN, NA_W)
    na_v_cache = cache_na_v.reshape(DEC_BATCH, -1, PAST_LEN, NA_W)
    perm = _s5_perm()
    xs = (x_prompt.reshape(N_TOK_P, D), x_sample.reshape(N_TOK_S, D))
    ret_states, na_kv = None, None
    new_ckv, new_kr, new_s5_re, new_s5_im = [], [], [], []
    for layer in range(DEPTH):
        j = layer // 2
        if layer % 2 == 0:
            wkr2, wuq2, wk, wv = _mla_weights(even_w_in[j], mla_w_uq[j], mla_w_ukv[j])
            qkvg, qp, kp, v, ckvn, kr = _in_even(xs, mods, mix_pre, layer, j, e_in, wkr2, row3(mla_q_norm),
                                                 row3(mla_kv_norm), wuq2, wk, wv, cos_t, sin_t)
            logit = jnp.transpose(ret_logit[j]).reshape(RET_HEADS, 2, 1, 1)
            gn = row3(ret_gn)
            yr_c, ret_states = _retention(qkvg, logit, gn, j, None, seq_len=SEQ, n_seq=BATCH, n_blk=4, row0=0,
                                          emit_state=True, prev_states=ret_states)
            (yr_l,) = _retention(qkvg, logit, gn, j, state_ret, seq_len=DEC_SEQ, n_seq=DEC_BATCH, n_blk=1,
                                 row0=N_TOK_P, emit_state=False)
            ym_c = _mla_ctx(qp, kp, v)
            kp_c, v_c = _mla_cache(cache_mla_ckv, kr_cache, j, wk, wv)
            ym_l, w_up, w_down = _mla_lat(qp, kp, v, kp_c, v_c, layer, ffn_w_up, ffn_w_down)
            mix, w_out, s5 = (yr_c, yr_l, ym_c, ym_l), e_out, None
            new_ckv.append(ckvn.reshape(BATCH, SEQ, MLA_KV_RANK))
            new_kr.append(kr[:, MLA_NOPE:MLA_NOPE + MLA_ROPE].reshape(BATCH, SEQ, MLA_ROPE))
        else:
            u, qkv, *na_kv = _in_odd(xs, mods, mix_pre, layer, j, o_in, prev_kv=na_kv)
            m, n2, p2, a = _s5_prep(s5_lambda_re[j], s5_lambda_im[j], s5_log_step[j], s5_b_re[j], s5_b_im[j],
                                    s5_c_re[j], s5_c_im[j])
            h0c = jnp.zeros((S5_PAIRS, 2, 2, BATCH, S5_PL), F32)
            h0 = jnp.stack([state_s5_re[:, j], state_s5_im[:, j]], axis=0)
            h0l = jnp.transpose(h0.reshape(2, DEC_BATCH, 2, S5_PAIRS, S5_PL), (3, 2, 0, 1, 4))
            y_raw, fin = _s5_scan(u, perm, m, n2, p2, a, h0c, h0l)
            yn_c = _na_ctx(qkv)
            yn_l, w_up, w_down = _na_lat(qkv, na_k_cache, na_v_cache, j, _na_table(na_rpb[j]), layer, ffn_w_up,
                                         ffn_w_down)
            mix, w_out, s5 = (y_raw, u, yn_c, yn_l), o_out, (row3(s5_d), glu_w, row3(s5_glu_b))
            st = jnp.transpose(fin.reshape(S5_PAIRS, 2, 2, BATCH, 2, S5_P), (2, 3, 1, 0, 4, 5))
            st = st.reshape(2, BATCH, 2, S5_GROUPS, S5_P)
            new_s5_re.append(st[0])
            new_s5_im.append(st[1])
        xs = _ffn(xs, mix, mods, mix_post, ffn_pre, ffn_post, layer, j, w_out, w_up, ffn_conv_w, conv_b, w_down, s5)
    stack = lambda a: jnp.stack(a, axis=1)
    heads = lambda a: a.reshape(BATCH, a.shape[1], SEQ, NA_HEADS, NA_DIM)
    return (xs[0].reshape(BATCH, SEQ, D), xs[1].reshape(DEC_BATCH, DEC_SEQ, D), ret_states, stack(new_ckv),
            stack(new_kr), stack(new_s5_re), stack(new_s5_im), heads(na_kv[0]), heads(na_kv[1]))
```

```python
import functools

import numpy as np
import jax
import jax.numpy as jnp
from jax import lax
from jax.experimental import pallas as pl
from jax.experimental.pallas import tpu as pltpu

F32 = jnp.float32
BF16 = jnp.bfloat16

D = 1024
BATCH = 16
SEQ = 256
DEPTH = 4
DEC_BATCH = 2
DEC_SEQ = 2048
PAST_LEN = 512
GRID_W = 64
GRID_H = DEC_SEQ // GRID_W
EPS = 1e-6
LOG2E = 1.4426950408889634

RET_HEADS = 4
RET_W = 512
RET_DIM = 128
RET_CHUNK = 256

MLA_HEADS = 8
MLA_NOPE = 64
MLA_ROPE = 32
MLA_V = 64
MLA_Q_RANK = 256
MLA_KV_RANK = 128
MLA_SCALE = (MLA_NOPE + MLA_ROPE) ** -0.5
ROPE_BASE = 10000.0
HEAD_SLAB = 128
MLA_QW = MLA_HEADS * HEAD_SLAB
MLA_VW = MLA_HEADS * MLA_V

S5_W = 512
S5_GROUP = 16
S5_GROUPS = 32
S5_P = 64
S5_CHUNK = 16
S5_PAIRS = S5_GROUPS // 2

NA_HEADS = 8
NA_W = 512
NA_DIM = 64
NA_WIN_R = 8
NA_WIN_C = 16
NA_SCALE = NA_DIM ** -0.5
NA_QROWS = 4
NA_KROWS = 12
NA_NDR = 2 * NA_WIN_R - 1
NA_NDC = 2 * NA_WIN_C - 1

D_FF = 2816
FF_CHUNK = 256
FF_NCHUNK = D_FF // FF_CHUNK
FF_EXT = 16

TM = 512
TQ = 256
FF_SUB = SEQ
FF_NSUB = TM // FF_SUB
HALO = 16
N_TOK_P = BATCH * SEQ
N_TOK_S = DEC_BATCH * DEC_SEQ
N_TOK = N_TOK_P + N_TOK_S
NT_P = N_TOK_P // TM
NT_S = N_TOK_S // TM
NT = NT_P + NT_S
TILES_PER_DEC = DEC_SEQ // TM

VMEM_LIMIT = 56 * 1024 * 1024


def _cp(*sem):
    return pltpu.CompilerParams(dimension_semantics=sem, vmem_limit_bytes=VMEM_LIMIT)


def _dot(a, b):
    return jnp.dot(a, b, preferred_element_type=F32)


def _dot_nt(a, b):
    return lax.dot_general(a, b, (((1,), (1,)), ((), ())), preferred_element_type=F32)


def _dot_tn(a, b):
    return lax.dot_general(a, b, (((0,), (0,)), ((), ())), preferred_element_type=F32)


def _rms(x, g):
    return x * lax.rsqrt(jnp.mean(x * x, axis=-1, keepdims=True) + EPS) * g


def _sigmoid(x):
    return 1.0 / (1.0 + jnp.exp(-x))


def _silu(x):
    return x * _sigmoid(x)


def _cmul(ar, ai, br, bi):
    return ar * br - ai * bi, ar * bi + ai * br


def _mrow(i):
    return jnp.where(i < NT_P, 0, 1 + (i - NT_P) // TILES_PER_DEC)


def _full(shape):
    n = len(shape)
    return pl.BlockSpec(shape, lambda *_: (0,) * n)


def _layer(layer, shape):
    n = len(shape)
    return pl.BlockSpec((None,) + shape, lambda *_: (layer,) + (0,) * n)


def _flat_layer(j, rows, cols):
    return pl.BlockSpec((rows, cols), lambda *_: (j, 0))


def _mod_spec(layer):
    return pl.BlockSpec((None, None, 1, 6 * D), lambda i: (layer, _mrow(i), 0, 0))


def _row(width):
    return pl.BlockSpec((TM, width), lambda i: (i, 0))


def _ctx_row(width):
    return pl.BlockSpec((TM, width), lambda i: (jnp.minimum(i, NT_P - 1), 0))


def _lat_row(width):
    return pl.BlockSpec((TM, width), lambda i: (jnp.maximum(i - NT_P, 0), 0))


def _pick(a_ref, b_ref):
    return jnp.where(pl.program_id(0) < NT_P, a_ref[...], b_ref[...])


def _ada_kernel(c_ref, w_ref, b_ref, *rest):
    n_w = (len(rest) - 1) // 2
    o_ref = rest[n_w]
    o_ref[...] = _dot(_silu(c_ref[...]).astype(BF16), w_ref[...].astype(BF16)) + b_ref[...]
    for src, dst in zip(rest[:n_w], rest[n_w + 1:]):
        dst[...] = src[...].astype(BF16)


def _ada_mods(cvec, ada_w, ada_b, weights):
    nb = 4
    bn = 6 * D // nb
    steps = DEPTH * nb
    flat = [w.reshape(-1, w.shape[-1]) for w in weights]
    slab = lambda w: pl.BlockSpec((w.shape[0] // steps, w.shape[1]), lambda l, n: (l * nb + n, 0))
    out = pl.pallas_call(
        _ada_kernel,
        grid=(DEPTH, nb),
        in_specs=[
            pl.BlockSpec((8, D), lambda l, n: (0, 0)),
            pl.BlockSpec((None, D, bn), lambda l, n: (l, 0, n)),
            pl.BlockSpec((None, 1, bn), lambda l, n: (l, 0, n)),
        ] + [slab(w) for w in flat],
        out_specs=[pl.BlockSpec((None, 8, bn), lambda l, n: (l, 0, n))] + [slab(w) for w in flat],
        out_shape=[jax.ShapeDtypeStruct((DEPTH, 8, 6 * D), F32)]
        + [jax.ShapeDtypeStruct(w.shape, BF16) for w in flat],
        compiler_params=_cp("arbitrary", "arbitrary"),
        name="ada_mods",
    )(cvec, ada_w, ada_b.reshape(DEPTH, 1, 6 * D), *flat)
    return out[0][:, :3].reshape(DEPTH, 3, 1, 6 * D), out[1:]


def _cast_ffn_slab(wu32_ref, wd32_ref, wu_ref, wd_ref):
    wu_ref[...] = wu32_ref[...].astype(BF16)
    wd_ref[...] = wd32_ref[...].astype(BF16)


def _cast_ffn_specs(layer, n_inner):
    n = DEC_BATCH * n_inner
    ru, rd = D // n, D_FF // n
    cast_in = [pl.BlockSpec((None, ru, 2 * D_FF), lambda b, t: (layer, b * n_inner + t, 0)),
               pl.BlockSpec((None, rd, D), lambda b, t: (layer, b * n_inner + t, 0))]
    cast_out = [pl.BlockSpec((ru, 2 * D_FF), lambda b, t: (b * n_inner + t, 0)),
                pl.BlockSpec((rd, D), lambda b, t: (b * n_inner + t, 0))]
    cast_shape = [jax.ShapeDtypeStruct((D, 2 * D_FF), BF16), jax.ShapeDtypeStruct((D_FF, D), BF16)]
    return cast_in, cast_out, cast_shape


def _in_even_kernel(xc_ref, xl_ref, mod_ref, g_ref, w_ref, wkr_ref, qn_ref, kvn_ref, wuq_ref, wk_ref, wv_ref, cos_ref,
                    sin_ref, qkvg_ref, qp_ref, kp_ref, v_ref, ckv_ref, kr_ref):
    x = _pick(xc_ref, xl_ref)
    mod = mod_ref[...]
    h = _rms(x, g_ref[...]) * (1.0 + mod[:, D:2 * D]) + mod[:, :D]
    hb = h.astype(BF16)
    o = 4 * RET_W
    for part in range(4):
        cols = slice(part * RET_W, (part + 1) * RET_W)
        qkvg_ref[:, cols] = _dot(hb, w_ref[:, cols]).astype(BF16)
    r = _dot(hb, w_ref[:, o:o + MLA_Q_RANK + MLA_KV_RANK])
    cq = r[:, :MLA_Q_RANK]
    ckv_raw = r[:, MLA_Q_RANK:]
    r2 = _dot(hb, wkr_ref[...])
    kr = r2[:, :HEAD_SLAB]
    krs = r2[:, HEAD_SLAB:]
    cosf = cos_ref[...]
    sinf = sin_ref[...]
    q2 = _dot(_rms(cq, qn_ref[...]).astype(BF16), wuq_ref[...])
    ckvn = _rms(ckv_raw, kvn_ref[...])

    @pl.when(pl.program_id(0) < NT_P)
    def _():
        ckv_ref[...] = ckvn
        kr_ref[...] = kr

    cb = ckvn.astype(BF16)
    kp = _dot(cb, wk_ref[...])
    v_ref[...] = _dot(cb, wv_ref[...]).astype(BF16)
    krr = kr * cosf + krs * sinf
    for hh in range(MLA_HEADS):
        sl = slice(hh * HEAD_SLAB, (hh + 1) * HEAD_SLAB)
        ss = slice(MLA_QW + hh * HEAD_SLAB, MLA_QW + (hh + 1) * HEAD_SLAB)
        qp_ref[:, sl] = ((q2[:, sl] * cosf + q2[:, ss] * sinf) * (MLA_SCALE * LOG2E)).astype(BF16)
        kp_ref[:, sl] = (kp[:, sl] + krr).astype(BF16)


def _in_even(xs, mods, gains, layer, j, w_in, wkr2, qn, kvn, wuq2, wk, wv, cos_t, sin_t):
    pos_spec = pl.BlockSpec((TM, HEAD_SLAB), lambda i: (jnp.where(i < NT_P, 0, 1 + (i - NT_P) % TILES_PER_DEC), 0))
    return pl.pallas_call(
        _in_even_kernel,
        grid=(NT,),
        in_specs=[_ctx_row(D), _lat_row(D), _mod_spec(layer), _layer(layer, (1, D)), _flat_layer(j, D, w_in.shape[1]),
                  _full(wkr2.shape), _layer(j, (1, MLA_Q_RANK)), _layer(j, (1, MLA_KV_RANK)), _full(wuq2.shape),
                  _full(wk.shape), _full(wv.shape), pos_spec, pos_spec],
        out_specs=[_row(4 * RET_W), _row(MLA_QW), _row(MLA_QW), _row(MLA_VW), _ctx_row(MLA_KV_RANK),
                   _ctx_row(HEAD_SLAB)],
        out_shape=[
            jax.ShapeDtypeStruct((N_TOK, 4 * RET_W), BF16),
            jax.ShapeDtypeStruct((N_TOK, MLA_QW), BF16),
            jax.ShapeDtypeStruct((N_TOK, MLA_QW), BF16),
            jax.ShapeDtypeStruct((N_TOK, MLA_VW), BF16),
            jax.ShapeDtypeStruct((N_TOK_P, MLA_KV_RANK), F32),
            jax.ShapeDtypeStruct((N_TOK_P, HEAD_SLAB), F32),
        ],
        compiler_params=_cp("arbitrary"),
        name="in_even",
    )(*xs, mods, gains, w_in, wkr2, qn, kvn, wuq2, wk, wv, cos_t, sin_t)


def _mla_cache_kernel(ckv_ref, kr_ref, wk_ref, wv_ref, kp_ref, v_ref):
    cb = ckv_ref[...].astype(BF16)
    kp = _dot(cb, wk_ref[...])
    kr = kr_ref[...]
    for hh in range(MLA_HEADS):
        sl = slice(hh * HEAD_SLAB, (hh + 1) * HEAD_SLAB)
        kp_ref[:, sl] = (kp[:, sl] + kr).astype(BF16)
    v_ref[...] = _dot(cb, wv_ref[...]).astype(BF16)


def _mla_cache(ckv_c, kr_slab, j, wk, wv):
    row = lambda w: pl.BlockSpec((PAST_LEN, w), lambda b: (b, 0))
    cache = lambda w: pl.BlockSpec((None, None, PAST_LEN, w), lambda b: (b, j, 0, 0))
    return pl.pallas_call(
        _mla_cache_kernel,
        grid=(DEC_BATCH,),
        in_specs=[cache(MLA_KV_RANK), cache(HEAD_SLAB), _full(wk.shape), _full(wv.shape)],
        out_specs=[row(MLA_QW), row(MLA_VW)],
        out_shape=[jax.ShapeDtypeStruct((DEC_BATCH * PAST_LEN, MLA_QW), BF16),
                   jax.ShapeDtypeStruct((DEC_BATCH * PAST_LEN, MLA_VW), BF16)],
        compiler_params=_cp("parallel"),
        name="mla_cache",
    )(ckv_c, kr_slab, wk, wv)


def _ret_kernel(lg_ref, q_ref, k_ref, v_ref, g_ref, gn_ref, *rest, seq_len, n_blk, emit_state, n_prev, zero_init):
    if zero_init:
        s0 = lambda b, d: jnp.zeros((RET_DIM, RET_DIM), F32)
    else:
        s0_ref, rest = rest[0], rest[1:]
        s0 = lambda b, d: s0_ref[b, d]
    prev_refs, rest = rest[:min(n_prev, 1)], rest[min(n_prev, 1):]
    if emit_state:
        y_ref, st_ref, sf_scr, sb_scr, dm_scr, w_scr = rest
    else:
        y_ref, sf_scr, sb_scr, dm_scr, w_scr = rest
    C = RET_CHUNK
    nc = seq_len // C

    @pl.when(pl.program_id(1) == 0)
    def _():
        lg = -jnp.log(1.0 + jnp.exp(-lg_ref[...]))
        lg_f = lg[0]
        lg_b = lg[1]
        ii = lax.broadcasted_iota(jnp.int32, (C, C), 0)
        jj = lax.broadcasted_iota(jnp.int32, (C, C), 1)
        diff = (ii - jj).astype(F32)
        dm_scr[...] = (jnp.where(diff >= 0, jnp.exp(lg_f * jnp.maximum(diff, 0.0)), 0.0)
                       + jnp.where(diff <= 0, jnp.exp(lg_b * jnp.maximum(-diff, 0.0)), 0.0))
        pos = lax.broadcasted_iota(jnp.int32, (C, RET_DIM), 0).astype(F32)
        w_scr[0] = jnp.exp(lg_f * (pos + 1.0))
        w_scr[1] = jnp.exp(lg_f * (C - 1.0 - pos))
        w_scr[2] = jnp.exp(lg_b * (C - pos))
        w_scr[3] = jnp.exp(lg_b * pos)
        w_scr[4] = jnp.exp(lg_f * C) + jnp.zeros((C, RET_DIM), F32)
        w_scr[5] = jnp.exp(lg_b * C) + jnp.zeros((C, RET_DIM), F32)

    qw_f, kw_f, qw_b, kw_b = w_scr[0], w_scr[1], w_scr[2], w_scr[3]
    cd_f = w_scr[4, :RET_DIM, :]
    cd_b = w_scr[5, :RET_DIM, :]
    ld = lambda ref, rows: ref[rows, :].astype(F32)
    gn = gn_ref[...]
    dm = dm_scr[...]

    for b in range(n_blk):
        rows_of = lambda n: slice(b * seq_len + n * C, b * seq_len + (n + 1) * C)
        sf = lambda n: sf_scr.at[b * (nc + 1) + n]
        sb = lambda n: sb_scr.at[b * (nc + 1) + n]
        sf(0)[...] = s0(b, 0)
        for n in range(nc):
            kv = _dot_tn((ld(k_ref, rows_of(n)) * kw_f).astype(BF16), v_ref[rows_of(n), :])
            sf(n + 1)[...] = cd_f * sf(n)[...] + kv
        sb(nc)[...] = s0(b, 1)
        for n in reversed(range(nc)):
            kv = _dot_tn((ld(k_ref, rows_of(n)) * kw_b).astype(BF16), v_ref[rows_of(n), :])
            sb(n)[...] = cd_b * sb(n + 1)[...] + kv
        if emit_state:
            for jp in range(n_prev):
                st_ref[b, jp] = prev_refs[0][b, jp]
            st_ref[b, n_prev, 0] = sf(nc)[...]
            st_ref[b, n_prev, 1] = sb(0)[...]
        for n in range(nc):
            rows = rows_of(n)
            q = ld(q_ref, rows) * (RET_DIM ** -0.5)
            s = _dot_nt(q.astype(BF16), k_ref[rows, :]) * dm
            o = (_dot(s.astype(BF16), v_ref[rows, :])
                 + _dot((q * qw_f).astype(BF16), sf(n)[...].astype(BF16))
                 + _dot((q * qw_b).astype(BF16), sb(n + 1)[...].astype(BF16)))
            mu = jnp.mean(o, axis=-1, keepdims=True)
            oc = o - mu
            var = jnp.mean(oc * oc, axis=-1, keepdims=True)
            on = oc * lax.rsqrt(var + EPS) * gn
            y_ref[rows, :] = (_silu(ld(g_ref, rows)) * on).astype(BF16)


def _retention(qkvg, logit, gn, j, s0, *, seq_len, n_seq, n_blk, row0, emit_state, prev_states=None):
    n_prev = 0 if prev_states is None else prev_states.shape[1]
    nc = seq_len // RET_CHUNK
    rows = n_blk * seq_len
    blk0 = row0 // rows
    col = lambda part: pl.BlockSpec((rows, RET_DIM), lambda h, s: (blk0 + s, part * RET_HEADS + h))
    in_specs = [
        pl.BlockSpec((None, 2, 1, 1), lambda h, s: (h, 0, 0, 0)),
        col(0), col(1), col(2), col(3),
        pl.BlockSpec((None, 1, RET_DIM), lambda h, s: (j, 0, h)),
    ]
    out_specs = [pl.BlockSpec((rows, RET_DIM), lambda h, s: (s, h))]
    out_shape = [jax.ShapeDtypeStruct((n_seq * seq_len, RET_W), BF16)]
    args = [logit, qkvg, qkvg, qkvg, qkvg, gn]
    if s0 is not None:
        in_specs.append(pl.BlockSpec((n_blk, None, 2, None, RET_DIM, RET_DIM), lambda h, s: (s, j, 0, h, 0, 0)))
        args.append(s0)
    stacked = lambda n: pl.BlockSpec((n_blk, n, 2, None, RET_DIM, RET_DIM), lambda h, s: (s, 0, 0, h, 0, 0))
    if n_prev:
        in_specs.append(stacked(n_prev))
        args.append(prev_states)
    if emit_state:
        out_specs.append(stacked(n_prev + 1))
        out_shape.append(jax.ShapeDtypeStruct((n_seq, n_prev + 1, 2, RET_HEADS, RET_DIM, RET_DIM), F32))
    n_st = n_blk * (nc + 1)
    return pl.pallas_call(
        functools.partial(_ret_kernel, seq_len=seq_len, n_blk=n_blk, emit_state=emit_state, n_prev=n_prev,
                          zero_init=s0 is None),
        grid=(RET_HEADS, n_seq // n_blk),
        in_specs=in_specs,
        out_specs=out_specs,
        out_shape=out_shape,
        scratch_shapes=[pltpu.VMEM((n_st, RET_DIM, RET_DIM), F32), pltpu.VMEM((n_st, RET_DIM, RET_DIM), F32),
                        pltpu.VMEM((RET_CHUNK, RET_CHUNK), F32), pltpu.VMEM((6, RET_CHUNK, RET_DIM), F32)],
        compiler_params=_cp("parallel", "arbitrary"),
        name="retention_%d" % seq_len,
    )(*args)


def _softmax_pv(score_blocks, value_blocks):
    m = functools.reduce(jnp.maximum, [jnp.max(s, axis=-1, keepdims=True) for s in score_blocks])
    ps = [jnp.exp2(s - m) for s in score_blocks]
    l = functools.reduce(lambda a, b: a + b, [jnp.sum(p, axis=-1, keepdims=True) for p in ps])
    o = functools.reduce(lambda a, b: a + b, [_dot(p.astype(BF16), v) for p, v in zip(ps, value_blocks)])
    return o / l


def _low_half(shape):
    return lax.broadcasted_iota(jnp.int32, shape, 1) < 64


def _heads_pipelined(n_heads, scores, attend, y_ref, rows):
    low = _low_half((rows, 128))
    nxt = scores(0)
    outs = []
    for hh in range(n_heads):
        cur = nxt
        if hh + 1 < n_heads:
            nxt = scores(hh + 1)
        outs.append(attend(hh, cur))
        if hh % 2 == 1:
            hp = hh // 2
            y_ref[:, hp * 128:(hp + 1) * 128] = jnp.where(low, outs[hh - 1], outs[hh]).astype(BF16)


def _mla_ctx_kernel(q_ref, k_ref, v_ref, y_ref):
    slab = lambda hh: slice(hh * HEAD_SLAB, (hh + 1) * HEAD_SLAB)
    pair = lambda hh: slice(hh // 2 * 128, (hh // 2 + 1) * 128)
    scores = lambda hh: [_dot_nt(q_ref[:, slab(hh)], k_ref[:, slab(hh)])]
    attend = lambda hh, s: _softmax_pv(s, [v_ref[:, pair(hh)]])
    _heads_pipelined(MLA_HEADS, scores, attend, y_ref, SEQ)


def _mla_ctx(qp, kp, v):
    return pl.pallas_call(
        _mla_ctx_kernel,
        grid=(BATCH,),
        in_specs=[pl.BlockSpec((SEQ, MLA_QW), lambda b: (b, 0)), pl.BlockSpec((SEQ, MLA_QW), lambda b: (b, 0)),
                  pl.BlockSpec((SEQ, MLA_VW), lambda b: (b, 0))],
        out_specs=pl.BlockSpec((SEQ, MLA_VW), lambda b: (b, 0)),
        out_shape=jax.ShapeDtypeStruct((N_TOK_P, MLA_VW), BF16),
        compiler_params=_cp("parallel"),
        name="mla_ctx",
    )(qp, kp, v)


def _mla_lat_kernel(q_ref, k_ref, v_ref, kc_ref, vc_ref, wu32_ref, wd32_ref, y_ref, wu_ref, wd_ref):
    _cast_ffn_slab(wu32_ref, wd32_ref, wu_ref, wd_ref)
    slab = lambda hh: slice(hh * HEAD_SLAB, (hh + 1) * HEAD_SLAB)
    pair = lambda hh: slice(hh // 2 * 128, (hh // 2 + 1) * 128)
    scores = lambda hh: [_dot_nt(q_ref[:, slab(hh)], k_ref[:, slab(hh)]),
                         _dot_nt(q_ref[:, slab(hh)], kc_ref[:, slab(hh)])]
    attend = lambda hh, s: _softmax_pv(s, [v_ref[:, pair(hh)], vc_ref[:, pair(hh)]])
    _heads_pipelined(MLA_HEADS, scores, attend, y_ref, TQ)


def _mla_lat(qp, kp, v, kp_c, v_c, layer, w_up, w_down):
    seq_blk = N_TOK_P // DEC_SEQ
    nq = DEC_SEQ // TQ
    cast_in, cast_out, cast_shape = _cast_ffn_specs(layer, nq)
    return pl.pallas_call(
        _mla_lat_kernel,
        grid=(DEC_BATCH, nq),
        in_specs=[
            pl.BlockSpec((TQ, MLA_QW), lambda b, t: (N_TOK_P // TQ + b * nq + t, 0)),
            pl.BlockSpec((DEC_SEQ, MLA_QW), lambda b, t: (seq_blk + b, 0)),
            pl.BlockSpec((DEC_SEQ, MLA_VW), lambda b, t: (seq_blk + b, 0)),
            pl.BlockSpec((PAST_LEN, MLA_QW), lambda b, t: (b, 0)),
            pl.BlockSpec((PAST_LEN, MLA_VW), lambda b, t: (b, 0)),
        ] + cast_in,
        out_specs=[pl.BlockSpec((TQ, MLA_VW), lambda b, t: (b * nq + t, 0))] + cast_out,
        out_shape=[jax.ShapeDtypeStruct((N_TOK_S, MLA_VW), BF16)] + cast_shape,
        compiler_params=_cp("parallel", "parallel"),
        name="mla_lat",
    )(qp, kp, v, kp_c, v_c, w_up, w_down)


def _gelu_tanh(x):
    return 0.5 * x * (1.0 + jnp.tanh(np.sqrt(2.0 / np.pi).astype(np.float32) * (x + 0.044715 * (x * x * x))))


def _ffn_kernel(*refs, odd):
    xc_ref, xl_ref, xp_ref, xn_ref = refs[:4]
    if odd:
        (yr_ref, yrp_ref, yrn_ref, u_ref, up_ref, un_ref, bc_ref, bl_ref, bp_ref, bn_ref, mod_ref, gmix_ref, gpre_ref,
         gpost_ref, wo_ref, d_ref, gw_ref, gb_ref, wu_ref, cw_ref, cb_ref, wd_ref, oc_ref, ol_ref, h_scr,
         act_scr) = refs[4:]
    else:
        (ac_ref, al_ref, ap_ref, an_ref, bc_ref, bl_ref, bp_ref, bn_ref, mod_ref, gmix_ref, gpre_ref, gpost_ref,
         wo_ref, wu_ref, cw_ref, cb_ref, wd_ref, oc_ref, ol_ref, h_scr, act_scr) = refs[4:]
    i = pl.program_id(0)
    is_lat = i >= NT_P
    t = (i - NT_P) % TILES_PER_DEC
    has_prev = jnp.logical_and(is_lat, t != 0)
    has_next = jnp.logical_and(is_lat, t != TILES_PER_DEC - 1)
    mod = mod_ref[...]
    cat = lambda main, prev_ref, next_ref: jnp.concatenate([main, prev_ref[...], next_ref[...]], axis=0)

    if odd:
        y = _gelu_tanh(cat(yr_ref[...], yrp_ref, yrn_ref) + d_ref[...] * cat(u_ref[...], up_ref, un_ref))
        ya = (y * _sigmoid(_dot(y.astype(BF16), gw_ref[...]) + gb_ref[...])).astype(BF16)
    else:
        ya = cat(_pick(ac_ref, al_ref), ap_ref, an_ref)
    yb = cat(_pick(bc_ref, bl_ref), bp_ref, bn_ref)
    half = ya.shape[1]
    r = _dot(ya, wo_ref[:half, :]) + _dot(yb, wo_ref[half:, :])
    x1 = cat(_pick(xc_ref, xl_ref), xp_ref, xn_ref) + mod[:, 2 * D:3 * D] * _rms(r, gmix_ref[...])

    shift = mod[:, 3 * D:4 * D]
    scale = mod[:, 4 * D:5 * D]
    gate = mod[:, 5 * D:6 * D]
    hall = _rms(x1, gpre_ref[...]) * (1.0 + scale) + shift
    x = x1[:TM]
    h = hall[:TM]
    blk = FF_SUB + FF_EXT
    rows = FF_NSUB * blk
    for k in range(FF_NSUB):
        h_scr[k * blk:k * blk + FF_SUB, :] = h[k * FF_SUB:(k + 1) * FF_SUB].astype(BF16)
        if k + 1 < FF_NSUB:
            after = jnp.where(is_lat, h[(k + 1) * FF_SUB:(k + 1) * FF_SUB + 8], 0.0)
            before = jnp.where(is_lat, h[(k + 1) * FF_SUB - 8:(k + 1) * FF_SUB], 0.0)
        else:
            after = jnp.where(has_next, hall[TM + HALO:TM + HALO + 8], 0.0)
            before = jnp.where(has_prev, hall[TM + HALO - 8:TM + HALO], 0.0)
        h_scr[k * blk + FF_SUB:(k + 1) * blk, :] = jnp.concatenate([after, before], axis=0).astype(BF16)
    hb = h_scr[...]

    def up(j):
        ca = slice(j * FF_CHUNK, (j + 1) * FF_CHUNK)
        cg = slice(D_FF + j * FF_CHUNK, D_FF + (j + 1) * FF_CHUNK)
        return (_dot(hb, wu_ref[:, ca]), ca), (_dot(hb, wu_ref[:, cg]), cg)

    def conv(part):
        u, cols = part
        cw = cw_ref[:, cols]
        return (cw[0:1, :] * pltpu.roll(u, 1, axis=0) + cw[1:2, :] * u + cw[2:3, :] * pltpu.roll(u, rows - 1, axis=0)
                + cb_ref[:, cols])

    nxt = up(0)
    for j in range(FF_NCHUNK):
        cur = nxt
        if j + 1 < FF_NCHUNK:
            nxt = up(j + 1)
        act = (_silu(conv(cur[1])) * conv(cur[0])).astype(BF16)
        for k in range(FF_NSUB):
            act_scr[k * FF_SUB:(k + 1) * FF_SUB, j * FF_CHUNK:(j + 1) * FF_CHUNK] = act[k * blk:k * blk + FF_SUB]
    out = x + gate * _rms(_dot(act_scr[...], wd_ref[...]), gpost_ref[...])

    @pl.when(i < NT_P)
    def _():
        oc_ref[...] = out

    @pl.when(i >= NT_P)
    def _():
        ol_ref[...] = out


def _halo_specs(width, n_rows, tile0):
    per = TM // HALO
    last = n_rows // HALO - 1
    prev = pl.BlockSpec((HALO, width), lambda i: (jnp.clip((i - tile0) * per - 1, 0, last), 0))
    nxt = pl.BlockSpec((HALO, width), lambda i: (jnp.clip((i - tile0 + 1) * per, 0, last), 0))
    return [prev, nxt]


def _ffn(xs, mix, mods, gmix, gpre, gpost, layer, j, w_out, wu, cw, cb, wd, s5=None):
    odd = s5 is not None
    pair = lambda w: [_ctx_row(w), _lat_row(w)]
    lat_halo = lambda w: _halo_specs(w, N_TOK_S, NT_P)
    all_halo = lambda w: _halo_specs(w, N_TOK, 0)
    resident = lambda shape: pl.BlockSpec(shape, lambda i: (0, 0), pipeline_mode=pl.Buffered(1))
    x_specs = pair(D) + lat_halo(D)
    x_args = [xs[0], xs[1], xs[1], xs[1]]
    if odd:
        yr, u, b_c, b_l = mix
        mix_specs = [_row(S5_W)] + all_halo(S5_W) + [_row(S5_W)] + all_halo(S5_W) + pair(NA_W) + lat_halo(NA_W)
        mix_args = [yr, yr, yr, u, u, u, b_c, b_l, b_l, b_l]
        s5_specs = [_layer(j, (1, S5_W)), _flat_layer(j, S5_W, S5_W), _layer(j, (1, S5_W))]
        s5_args = list(s5)
    else:
        a_c, a_l, b_c, b_l = mix
        mix_specs = pair(RET_W) + lat_halo(RET_W) + pair(MLA_VW) + lat_halo(MLA_VW)
        mix_args = [a_c, a_l, a_l, a_l, b_c, b_l, b_l, b_l]
        s5_specs, s5_args = [], []
    return pl.pallas_call(
        functools.partial(_ffn_kernel, odd=odd),
        grid=(NT,),
        in_specs=x_specs + mix_specs + [_mod_spec(layer), _layer(layer, (1, D)), _layer(layer, (1, D)),
                                        _layer(layer, (1, D)), _flat_layer(j, D, D)] + s5_specs
        + [resident((D, 2 * D_FF)), _layer(layer, (3, 2 * D_FF)), _layer(layer, (1, 2 * D_FF)), resident((D_FF, D))],
        out_specs=[_ctx_row(D), _lat_row(D)],
        out_shape=[jax.ShapeDtypeStruct((N_TOK_P, D), F32), jax.ShapeDtypeStruct((N_TOK_S, D), F32)],
        scratch_shapes=[pltpu.VMEM((FF_NSUB * (FF_SUB + FF_EXT), D), BF16), pltpu.VMEM((TM, D_FF), BF16)],
        compiler_params=_cp("arbitrary"),
        name="ffn_odd" if odd else "ffn_even",
    )(*x_args, *mix_args, mods, gmix, gpre, gpost, w_out, *s5_args, wu, cw, cb, wd)


def _in_odd_kernel(xc_ref, xl_ref, mod_ref, g_ref, w_ref, *rest, n_prev):
    np2 = 2 * min(n_prev, 1)
    prev_refs, (u_ref, qkv_ref, kc_ref, vc_ref) = rest[:np2], rest[np2:]
    mod = mod_ref[...]
    h = _rms(_pick(xc_ref, xl_ref), g_ref[...]) * (1.0 + mod[:, D:2 * D]) + mod[:, :D]
    r = _dot(h.astype(BF16), w_ref[...])
    u_ref[...] = r[:, :S5_W]
    qkv_ref[:, :NA_W] = (r[:, S5_W:S5_W + NA_W] * (NA_SCALE * LOG2E)).astype(BF16)
    qkv_ref[:, NA_W:] = r[:, S5_W + NA_W:].astype(BF16)

    @pl.when(pl.program_id(0) < NT_P)
    def _():
        for part, (out_ref, col0) in enumerate([(kc_ref, S5_W + NA_W), (vc_ref, S5_W + 2 * NA_W)]):
            for b in range(TM // SEQ):
                for jp in range(n_prev):
                    out_ref[b, jp] = prev_refs[part][b, jp]
                out_ref[b, n_prev] = r[b * SEQ:(b + 1) * SEQ, col0:col0 + NA_W]


def _in_odd(xs, mods, gains, layer, j, w, prev_kv=None):
    n_prev = 0 if prev_kv is None else prev_kv[0].shape[1]
    nb = TM // SEQ
    stacked = lambda n: pl.BlockSpec((nb, n, SEQ, NA_W), lambda i: (jnp.minimum(i, NT_P - 1), 0, 0, 0))
    leaf = jax.ShapeDtypeStruct((BATCH, n_prev + 1, SEQ, NA_W), F32)
    prev_specs = [stacked(n_prev)] * 2 if n_prev else []
    return pl.pallas_call(
        functools.partial(_in_odd_kernel, n_prev=n_prev),
        grid=(NT,),
        in_specs=[_ctx_row(D), _lat_row(D), _mod_spec(layer), _layer(layer, (1, D)), _flat_layer(j, D, w.shape[1])]
        + prev_specs,
        out_specs=[_row(S5_W), _row(3 * NA_W), stacked(n_prev + 1), stacked(n_prev + 1)],
        out_shape=[jax.ShapeDtypeStruct((N_TOK, S5_W), F32), jax.ShapeDtypeStruct((N_TOK, 3 * NA_W), BF16), leaf, leaf],
        compiler_params=_cp("arbitrary"),
        name="in_odd",
    )(*xs, mods, gains, w, *(prev_kv or ()))


S5_TS = S5_CHUNK * S5_GROUP
S5_PL = 2 * S5_P
S5_PREP_PAIRS = 2


def _s5_prep_kernel(*refs):
    ins, outs = refs[:7], refs[7:]
    for q in range(S5_PREP_PAIRS):
        _s5_prep_pair(*[r.at[:, q] for r in ins], *[r.at[q] for r in outs])


def _s5_prep_pair(lre_ref, lim_ref, ls_ref, btr_ref, bti_ref, cr_ref, ci_ref, m_ref, n_ref, p_ref, a_ref, ct_scr):
    T = S5_CHUNK
    S = S5_GROUP
    hi = lax.Precision.HIGHEST
    low = lax.broadcasted_iota(jnp.int32, (S, S5_PL), 1) < S5_P
    half = [low, jnp.logical_not(low)]
    pick = lambda e, v: jnp.where(half[e], v, 0.0)
    nt = (((1,), (1,)), ((), ()))
    kps = [[None, None], [None, None]]
    for d in range(2):
        lre = lre_ref[d]
        lim = lim_ref[d]
        step = jnp.exp(ls_ref[d])
        mag = jnp.exp(lre * step)
        are = mag * jnp.cos(lim * step)
        aim = mag * jnp.sin(lim * step)
        den = lre * lre + lim * lim
        zr, zi = _cmul(are - 1.0, aim, lre / den, -lim / den)
        bbr, bbi = _cmul(zr, zi, btr_ref[d], bti_ref[d])
        cr = cr_ref[d]
        ci = ci_ref[d]
        pr = jnp.ones_like(are)
        pi = jnp.zeros_like(are)
        for k in range(T + 1):
            er, ei = _cmul(cr, ci, pr, pi)
            if k < T:
                jn = T - 1 - k if d == 0 else k
                wr, wi = _cmul(pr, pi, bbr, bbi)
                for e in range(2):
                    rows = slice(e * S5_TS + jn * S, e * S5_TS + (jn + 1) * S)
                    n_ref[d, rows, 0:S5_PL] = pick(e, wr).astype(BF16)
                    n_ref[d, rows, S5_PL:2 * S5_PL] = pick(e, wi).astype(BF16)
                jc = k if d == 0 else T - 1 - k
                ct_scr[0, jc * S:(jc + 1) * S, :] = er
                ct_scr[1, jc * S:(jc + 1) * S, :] = ei
            if k >= 1:
                t = k - 1 if d == 0 else T - k
                for e in range(2):
                    rows = slice(e * S5_TS + t * S, e * S5_TS + (t + 1) * S)
                    p_ref[rows, 2 * d * S5_PL:(2 * d + 1) * S5_PL] = pick(e, er).astype(BF16)
                    p_ref[rows, (2 * d + 1) * S5_PL:(2 * d + 2) * S5_PL] = pick(e, -ei).astype(BF16)
            if k == T:
                a_ref[d, 0] = pr
                a_ref[d, 1] = pi
            pr, pi = _cmul(pr, pi, are, aim)
        for e in range(2):
            kd = (lax.dot_general(pick(e, bbr), ct_scr[0], nt, precision=hi, preferred_element_type=F32)
                  - lax.dot_general(pick(e, bbi), ct_scr[1], nt, precision=hi, preferred_element_type=F32))
            kps[d][e] = jnp.concatenate([kd, jnp.zeros_like(kd)], axis=1)
    for e in range(2):
        for t in range(T):
            fwd = pltpu.roll(kps[0][e], t * S, axis=1)[:, :S5_TS]
            bwd = pltpu.roll(kps[1][e], (2 * S5_TS - (T - 1 - t) * S) % (2 * S5_TS), axis=1)[:, :S5_TS]
            m_ref[e, t * S:(t + 1) * S, :] = (fwd + bwd).astype(BF16)


def _s5_prep(lre, lim, ls, b_re, b_im, c_re, c_im):
    NP, S, P = S5_PAIRS, S5_GROUP, S5_P
    vec = lambda a: a.reshape(2, NP, 1, S5_PL)
    b_lay = lambda a: jnp.transpose(a.reshape(2, NP, 2, P, S), (0, 1, 4, 2, 3)).reshape(2, NP, S, S5_PL)
    c_lay = lambda a: jnp.transpose(a.reshape(2, NP, 2, S, P), (0, 1, 3, 2, 4)).reshape(2, NP, S, S5_PL)
    args = (vec(lre), vec(lim), vec(jnp.repeat(ls, P, axis=-1)), b_lay(b_re), b_lay(b_im), c_lay(c_re), c_lay(c_im))
    PB = S5_PREP_PAIRS
    vspec = pl.BlockSpec((2, PB, 1, S5_PL), lambda g: (0, g, 0, 0))
    mspec = pl.BlockSpec((2, PB, S, S5_PL), lambda g: (0, g, 0, 0))
    lead = lambda shape: pl.BlockSpec((PB,) + shape, lambda g: (g,) + (0,) * len(shape))
    return pl.pallas_call(
        _s5_prep_kernel,
        grid=(NP // PB,),
        in_specs=[vspec, vspec, vspec, mspec, mspec, mspec, mspec],
        out_specs=[lead((2, S5_TS, S5_TS)), lead((2, 2 * S5_TS, 2 * S5_PL)), lead((2 * S5_TS, 4 * S5_PL)),
                   lead((2, 2, 1, S5_PL))],
        out_shape=[jax.ShapeDtypeStruct((NP, 2, S5_TS, S5_TS), BF16),
                   jax.ShapeDtypeStruct((NP, 2, 2 * S5_TS, 2 * S5_PL), BF16),
                   jax.ShapeDtypeStruct((NP, 2 * S5_TS, 4 * S5_PL), BF16),
                   jax.ShapeDtypeStruct((NP, 2, 2, 1, S5_PL), F32)],
        scratch_shapes=[pltpu.VMEM((S5_PREP_PAIRS, 2, S5_TS, S5_PL), F32)],
        compiler_params=_cp("parallel"),
        name="s5_prep",
    )(*args)


S5_NCH = N_TOK // S5_CHUNK
S5_ROWS_P = N_TOK_P // S5_CHUNK
S5_NC_P = SEQ // S5_CHUNK
S5_NC_S = DEC_SEQ // S5_CHUNK
S5_GPB = 8
S5_PPB = S5_GPB // 2
S5_XL = 8 * 128


def _s5_perm():
    r = np.arange(S5_XL)
    dst = (r // S5_GROUP % S5_GPB) * 128 + (r // 128) * S5_GROUP + r % S5_GROUP
    perm = np.zeros((S5_XL, S5_XL), np.float32)
    perm[r, dst] = 1.0
    return jnp.asarray(perm, BF16)


def _s5_kernel(u_ref, perm_ref, m_ref, n_ref, p_ref, a_ref, h0c_ref, h0l_ref, y_ref, fin_ref, z_scr, up_scr, e_scr,
               hin_scr, yc_scr):
    T = S5_CHUNK
    W = S5_PL
    for t in range(T):
        z_scr[t // 8, :, (t % 8) * 128:(t % 8 + 1) * 128] = u_ref[pl.ds(t, S5_NCH, stride=T), :].astype(BF16)
    perm = perm_ref[...]
    for j in range(2):
        up_scr[j] = _dot(z_scr[j], perm).astype(BF16)

    def scan(pp, d, h0_ref, n_chunks, n_seq, row0):
        are = a_ref[pp, d, 0]
        aim = a_ref[pp, d, 1]
        hr = h0_ref[pp, d, 0]
        hi = h0_ref[pp, d, 1]
        order = range(n_chunks) if d == 0 else reversed(range(n_chunks))
        for c in order:
            rows = pl.ds(row0 + c, n_seq, stride=n_chunks)
            hin_scr[2 * d, rows, :] = hr
            hin_scr[2 * d + 1, rows, :] = hi
            er = e_scr[2 * d, rows, :]
            ei = e_scr[2 * d + 1, rows, :]
            hr, hi = are * hr - aim * hi + er, are * hi + aim * hr + ei
        return hr, hi

    for pp in range(S5_PPB):
        us = []
        for e in range(2):
            sl = slice((2 * pp + e) * 128, (2 * pp + e + 1) * 128)
            us.append(jnp.concatenate([up_scr[0, :, sl], up_scr[1, :, sl]], axis=1))
        u2 = jnp.concatenate(us, axis=1)
        for d in range(2):
            ed = _dot(u2, n_ref[pp, d])
            e_scr[2 * d] = ed[:, :W]
            e_scr[2 * d + 1] = ed[:, W:]
        for d in range(2):
            hr, hi = scan(pp, d, h0c_ref, S5_NC_P, BATCH, 0)
            fin_ref[pp, d, 0] = hr
            fin_ref[pp, d, 1] = hi
            scan(pp, d, h0l_ref, S5_NC_S, DEC_BATCH, S5_ROWS_P)
        hin = jnp.concatenate([hin_scr[k] for k in range(4)], axis=1).astype(BF16)
        for e in range(2):
            y = _dot(us[e], m_ref[pp, e]) + _dot_nt(hin, p_ref[pp, e * S5_TS:(e + 1) * S5_TS, :])
            sl = slice((2 * pp + e) * 128, (2 * pp + e + 1) * 128)
            for j in range(2):
                yc_scr[j, :, sl] = y[:, j * 128:(j + 1) * 128]

    for j in range(2):
        r = _dot_nt(yc_scr[j].astype(BF16), perm)
        for k in range(8):
            y_ref[pl.ds(8 * j + k, S5_NCH, stride=T), :] = r[:, k * 128:(k + 1) * 128]


def _s5_scan(u, perm, m, n2, p2, a, h0c, h0l):
    nb = S5_GROUPS // S5_GPB
    lead = lambda shape: pl.BlockSpec((S5_PPB,) + shape, lambda w: (w,) + (0,) * len(shape))
    col = pl.BlockSpec((N_TOK, 128), lambda w: (0, w))
    return pl.pallas_call(
        _s5_kernel,
        grid=(nb,),
        in_specs=[col, _full(perm.shape), lead((2, S5_TS, S5_TS)), lead((2, 2 * S5_TS, 2 * S5_PL)),
                  lead((2 * S5_TS, 4 * S5_PL)), lead((2, 2, 1, S5_PL)), lead((2, 2, BATCH, S5_PL)),
                  lead((2, 2, DEC_BATCH, S5_PL))],
        out_specs=[col, lead((2, 2, BATCH, S5_PL))],
        out_shape=[jax.ShapeDtypeStruct((N_TOK, S5_W), F32),
                   jax.ShapeDtypeStruct((S5_PAIRS, 2, 2, BATCH, S5_PL), F32)],
        scratch_shapes=[pltpu.VMEM((2, S5_NCH, S5_XL), BF16), pltpu.VMEM((2, S5_NCH, S5_XL), BF16),
                        pltpu.VMEM((4, S5_NCH, S5_PL), F32), pltpu.VMEM((4, S5_NCH, S5_PL), F32),
                        pltpu.VMEM((2, S5_NCH, S5_XL), F32)],
        compiler_params=_cp("parallel"),
        name="s5_scan",
    )(u, perm, m, n2, p2, a, h0c, h0l)


def _na_heads(q_ref, keys, values, y_ref, rows, bias=None):
    low = _low_half((rows, 128))
    pair = lambda hh: slice(hh // 2 * 128, (hh // 2 + 1) * 128)

    def scores(hh):
        q = q_ref[:, pair(hh)]
        qm = jnp.where(low == (hh % 2 == 0), q, jnp.zeros_like(q))
        s = [_dot_nt(qm, k(pair(hh))) for k in keys]
        if bias is not None:
            s[0] = s[0] + bias(hh)
        return s

    attend = lambda hh, s: _softmax_pv(s, [v(pair(hh)) for v in values])
    _heads_pipelined(NA_HEADS, scores, attend, y_ref, rows)


def _na_ctx_kernel(q_ref, k_ref, v_ref, y_ref):
    _na_heads(q_ref, [lambda sl: k_ref[:, sl]], [lambda sl: v_ref[:, sl]], y_ref, SEQ)


def _na_ctx(qkv):
    col = lambda part: pl.BlockSpec((SEQ, NA_W), lambda b: (b, part))
    return pl.pallas_call(
        _na_ctx_kernel,
        grid=(BATCH,),
        in_specs=[col(0), col(1), col(2)],
        out_specs=pl.BlockSpec((SEQ, NA_W), lambda b: (b, 0)),
        out_shape=jax.ShapeDtypeStruct((N_TOK_P, NA_W), BF16),
        compiler_params=_cp("parallel"),
        name="na_ctx",
    )(qkv, qkv, qkv)


def _na_key_row0(rb):
    return jnp.clip(NA_QROWS * rb - NA_WIN_R // 2, 0, GRID_H - NA_KROWS)


def _na_lat_kernel(q_ref, ks_ref, vs_ref, kc_ref, vc_ref, tab_ref, wu32_ref, wd32_ref, y_ref, wu_ref, wd_ref):
    _cast_ffn_slab(wu32_ref, wd32_ref, wu_ref, wd_ref)
    rb = pl.program_id(1)
    u0 = _na_key_row0(rb)
    start = pl.multiple_of(u0 * GRID_W, GRID_W)
    nk = NA_KROWS * GRID_W
    low_t = _low_half((GRID_W, 128))

    def table_row(i, w):
        qr = NA_QROWS * rb + i
        kr = u0 + w
        rs = jnp.clip(qr - NA_WIN_R // 2, 0, GRID_H - NA_WIN_R)
        inside = jnp.logical_and(kr >= rs, kr < rs + NA_WIN_R)
        return jnp.where(inside, kr - qr + NA_WIN_R - 1, NA_NDR)

    idx = [[table_row(i, w) for w in range(NA_KROWS)] for i in range(NA_QROWS)]

    def bias(h):
        rows = [jnp.concatenate([jnp.where(low_t, tab_ref[h, idx[i][w]], tab_ref[h, idx[i][w + 1]])
                                 for w in range(0, NA_KROWS, 2)], axis=1) for i in range(NA_QROWS)]
        return jnp.concatenate(rows, axis=0)

    keys = [lambda sl: ks_ref[pl.ds(start, nk), sl], lambda sl: kc_ref[:, sl].astype(BF16)]
    values = [lambda sl: vs_ref[pl.ds(start, nk), sl], lambda sl: vc_ref[:, sl].astype(BF16)]
    _na_heads(q_ref, keys, values, y_ref, TQ, bias)


def _na_lat(qkv, k_c, v_c, j, table, layer, w_up, w_down):
    seq_blk = N_TOK_P // DEC_SEQ
    nb = GRID_H // NA_QROWS
    cast_in, cast_out, cast_shape = _cast_ffn_specs(layer, nb)
    return pl.pallas_call(
        _na_lat_kernel,
        grid=(DEC_BATCH, nb),
        in_specs=[
            pl.BlockSpec((TQ, NA_W), lambda b, r: (N_TOK_P // TQ + b * nb + r, 0)),
            pl.BlockSpec((DEC_SEQ, NA_W), lambda b, r: (seq_blk + b, 1)),
            pl.BlockSpec((DEC_SEQ, NA_W), lambda b, r: (seq_blk + b, 2)),
            pl.BlockSpec((None, None, PAST_LEN, NA_W), lambda b, r: (b, j, 0, 0)),
            pl.BlockSpec((None, None, PAST_LEN, NA_W), lambda b, r: (b, j, 0, 0)),
            pl.BlockSpec(table.shape, lambda b, r: (0, 0, 0, 0)),
        ] + cast_in,
        out_specs=[pl.BlockSpec((TQ, NA_W), lambda b, r: (b * nb + r, 0))] + cast_out,
        out_shape=[jax.ShapeDtypeStruct((N_TOK_S, NA_W), BF16)] + cast_shape,
        compiler_params=_cp("parallel", "arbitrary"),
        name="na_lat",
    )(qkv, qkv, qkv, k_c, v_c, table, w_up, w_down)


def _na_table_kernel(rpb_ref, t_ref):
    qc = lax.broadcasted_iota(jnp.int32, (GRID_W, 128), 0)
    kc = lax.broadcasted_iota(jnp.int32, (GRID_W, 128), 1) % GRID_W
    cs = jnp.clip(qc - NA_WIN_C // 2, 0, GRID_W - NA_WIN_C)
    in_band = jnp.logical_and(kc >= cs, kc < cs + NA_WIN_C)
    neg = jnp.full((GRID_W, 128), -jnp.inf, F32)

    def body(n, carry):
        x = jnp.broadcast_to(rpb_ref[n], (GRID_W, 128))
        t = pltpu.roll(x, 128 - (NA_WIN_C - 1), axis=1, stride=1, stride_axis=0)
        t_ref[n // NA_NDR, n % NA_NDR] = jnp.where(in_band, t * LOG2E, neg)
        return carry

    lax.fori_loop(0, NA_HEADS * NA_NDR, body, 0, unroll=8)
    for h in range(NA_HEADS):
        t_ref[h, NA_NDR] = neg


def _na_table(rpb):
    rows = jnp.pad(rpb.reshape(NA_HEADS * NA_NDR, 1, NA_NDC), ((0, 0), (0, 0), (0, GRID_W - NA_NDC)))
    rows = jnp.concatenate([rows, rows], axis=-1)
    return pl.pallas_call(
        _na_table_kernel,
        out_shape=jax.ShapeDtypeStruct((NA_HEADS, NA_NDR + 1, GRID_W, 128), F32),
        name="na_table",
    )(rows)


def _rope_tables():
    n_freq = MLA_ROPE // 4
    inv = ROPE_BASE ** (-jnp.arange(n_freq, dtype=F32) / n_freq)
    t = jnp.arange(DEC_SEQ)
    row = (t // GRID_W).astype(F32)
    colp = (t % GRID_W).astype(F32)
    ang = jnp.concatenate([row[:, None] * inv, colp[:, None] * inv], axis=-1)
    cos, sin = jnp.cos(ang), jnp.sin(ang)
    one = jnp.ones((DEC_SEQ, MLA_NOPE), F32)
    zero = jnp.zeros((DEC_SEQ, MLA_NOPE), F32)
    cos_s = jnp.concatenate([one, cos, cos, one[:, :32]], axis=-1)
    sin_s = jnp.concatenate([zero, -sin, sin, zero[:, :32]], axis=-1)
    cos_t = jnp.concatenate([jnp.ones((TM, HEAD_SLAB), F32), cos_s], axis=0)
    sin_t = jnp.concatenate([jnp.zeros((TM, HEAD_SLAB), F32), sin_s], axis=0)
    return cos_t, sin_t


def _mla_weights(w_in, w_uq, w_ukv):
    half = MLA_ROPE // 2
    wkr = w_in[:, 4 * RET_W + MLA_Q_RANK + MLA_KV_RANK:]
    z64 = jnp.zeros((D, MLA_NOPE), F32)
    z32 = jnp.zeros((D, HEAD_SLAB - MLA_NOPE - MLA_ROPE), F32)
    wkr2 = jnp.concatenate([z64, wkr, z32, z64, wkr[:, half:], wkr[:, :half], z32], axis=1).astype(BF16)
    wq = w_uq.reshape(MLA_Q_RANK, MLA_HEADS, MLA_NOPE + MLA_ROPE)
    nope, rope = wq[..., :MLA_NOPE], wq[..., MLA_NOPE:]
    zq64 = jnp.zeros_like(nope)
    zq32 = jnp.zeros_like(rope)
    q_slab = jnp.concatenate([nope, rope, zq32], axis=-1).reshape(MLA_Q_RANK, MLA_QW)
    q_sw = jnp.concatenate([zq64, rope[..., half:], rope[..., :half], zq32], axis=-1).reshape(MLA_Q_RANK, MLA_QW)
    wuq2 = jnp.concatenate([q_slab, q_sw], axis=1).astype(BF16)
    wkv = w_ukv.reshape(MLA_KV_RANK, MLA_HEADS, MLA_NOPE + MLA_V)
    wk = jnp.concatenate([wkv[..., :MLA_NOPE], jnp.zeros_like(wkv[..., :MLA_NOPE])], axis=-1)
    wk = wk.reshape(MLA_KV_RANK, MLA_QW).astype(BF16)
    wv = wkv[..., MLA_NOPE:].reshape(MLA_KV_RANK, MLA_VW).astype(BF16)
    return wkr2, wuq2, wk, wv


def kernel(x_prompt, x_sample, c, state_ret, cache_mla_ckv, cache_mla_krope, state_s5_re, state_s5_im, cache_na_k, cache_na_v, c_ctx, ada_w, ada_b, mix_pre_g, mix_post_g, ffn_pre_g, ffn_post_g, ffn_w_up, ffn_conv_w, ffn_conv_b, ffn_w_down, even_w_in, even_w_out, ret_logit, ret_gn, mla_q_norm, mla_w_uq, mla_kv_norm, mla_w_ukv, odd_w_in, odd_w_out, s5_lambda_re, s5_lambda_im, s5_log_step, s5_b_re, s5_b_im, s5_c_re, s5_c_im, s5_d, s5_glu_w, s5_glu_b, na_rpb):
    cvec = jnp.concatenate([c_ctx[None, :], c, jnp.zeros((8 - 1 - DEC_BATCH, D), F32)], axis=0)
    mods, (e_in, e_out, o_in, o_out, glu_w) = _ada_mods(cvec, ada_w, ada_b,
                                                        [even_w_in, even_w_out, odd_w_in, odd_w_out, s5_glu_w])
    row3 = lambda a: a.reshape(a.shape[0], 1, a.shape[1])
    mix_pre, mix_post, ffn_pre, ffn_post = row3(mix_pre_g), row3(mix_post_g), row3(ffn_pre_g), row3(ffn_post_g)
    conv_b = row3(ffn_conv_b)
    cos_t, sin_t = _rope_tables()
    kr_cache = jnp.pad(cache_mla_krope, ((0, 0), (0, 0), (0, 0), (MLA_NOPE, HEAD_SLAB - MLA_NOPE - MLA_ROPE)))
    na_k_cache = cache_na_k.reshape(DEC_BATCH, -1, PAST_LEN, NA_W)
    na_v_cache = cache_na_v.reshape(DEC_BATCH, -1, PAST_LEN, NA_W)
    perm = _s5_perm()
    xs = (x_prompt.reshape(N_TOK_P, D), x_sample.reshape(N_TOK_S, D))
    ret_states, na_kv = None, None
    new_ckv, new_kr, new_s5_re, new_s5_im = [], [], [], []
    for layer in range(DEPTH):
        j = layer // 2
        if layer % 2 == 0:
            wkr2, wuq2, wk, wv = _mla_weights(even_w_in[j], mla_w_uq[j], mla_w_ukv[j])
            qkvg, qp, kp, v, ckvn, kr = _in_even(xs, mods, mix_pre, layer, j, e_in, wkr2, row3(mla_q_norm),
                                                 row3(mla_kv_norm), wuq2, wk, wv, cos_t, sin_t)
            logit = jnp.transpose(ret_logit[j]).reshape(RET_HEADS, 2, 1, 1)
            gn = row3(ret_gn)
            yr_c, ret_states = _retention(qkvg, logit, gn, j, None, seq_len=SEQ, n_seq=BATCH, n_blk=4, row0=0,
                                          emit_state=True, prev_states=ret_states)
            (yr_l,) = _retention(qkvg, logit, gn, j, state_ret, seq_len=DEC_SEQ, n_seq=DEC_BATCH, n_blk=1,
                                 row0=N_TOK_P, emit_state=False)
            ym_c = _mla_ctx(qp, kp, v)
            kp_c, v_c = _mla_cache(cache_mla_ckv, kr_cache, j, wk, wv)
            ym_l, w_up, w_down = _mla_lat(qp, kp, v, kp_c, v_c, layer, ffn_w_up, ffn_w_down)
            mix, w_out, s5 = (yr_c, yr_l, ym_c, ym_l), e_out, None
            new_ckv.append(ckvn.reshape(BATCH, SEQ, MLA_KV_RANK))
            new_kr.append(kr[:, MLA_NOPE:MLA_NOPE + MLA_ROPE].reshape(BATCH, SEQ, MLA_ROPE))
        else:
            u, qkv, *na_kv = _in_odd(xs, mods, mix_pre, layer, j, o_in, prev_kv=na_kv)
            m, n2, p2, a = _s5_prep(s5_lambda_re[j], s5_lambda_im[j], s5_log_step[j], s5_b_re[j], s5_b_im[j],
                                    s5_c_re[j], s5_c_im[j])
            h0c = jnp.zeros((S5_PAIRS, 2, 2, BATCH, S5_PL), F32)
            h0 = jnp.stack([state_s5_re[:, j], state_s5_im[:, j]], axis=0)
            h0l = jnp.transpose(h0.reshape(2, DEC_BATCH, 2, S5_PAIRS, S5_PL), (3, 2, 0, 1, 4))
            y_raw, fin = _s5_scan(u, perm, m, n2, p2, a, h0c, h0l)
            yn_c = _na_ctx(qkv)
            yn_l, w_up, w_down = _na_lat(qkv, na_k_cache, na_v_cache, j, _na_table(na_rpb[j]), layer, ffn_w_up,
                                         ffn_w_down)
            mix, w_out, s5 = (y_raw, u, yn_c, yn_l), o_out, (row3(s5_d), glu_w, row3(s5_glu_b))
            st = jnp.transpose(fin.reshape(S5_PAIRS, 2, 2, BATCH, 2, S5_P), (2, 3, 1, 0, 4, 5))
            st = st.reshape(2, BATCH, 2, S5_GROUPS, S5_P)
            new_s5_re.append(st[0])
            new_s5_im.append(st[1])
        xs = _ffn(xs, mix, mods, mix_post, ffn_pre, ffn_post, layer, j, w_out, w_up, ffn_conv_w, conv_b, w_down, s5)
    stack = lambda a: jnp.stack(a, axis=1)
    heads = lambda a: a.reshape(BATCH, a.shape[1], SEQ, NA_HEADS, NA_DIM)
    return (xs[0].reshape(BATCH, SEQ, D), xs[1].reshape(DEC_BATCH, DEC_SEQ, D), ret_states, stack(new_ckv),
            stack(new_kr), stack(new_s5_re), stack(new_s5_im), heads(na_kv[0]), heads(na_kv[1]))
```

```python
import functools

import numpy as np
import jax
import jax.numpy as jnp
from jax import lax
from jax.experimental import pallas as pl
from jax.experimental.pallas import tpu as pltpu

F32 = jnp.float32
BF16 = jnp.bfloat16

D = 1024
BATCH = 16
SEQ = 256
DEPTH = 4
DEC_BATCH = 2
DEC_SEQ = 2048
PAST_LEN = 512
GRID_W = 64
GRID_H = DEC_SEQ // GRID_W
EPS = 1e-6
LOG2E = 1.4426950408889634

RET_HEADS = 4
RET_W = 512
RET_DIM = 128
RET_CHUNK = 256

MLA_HEADS = 8
MLA_NOPE = 64
MLA_ROPE = 32
MLA_V = 64
MLA_Q_RANK = 256
MLA_KV_RANK = 128
MLA_SCALE = (MLA_NOPE + MLA_ROPE) ** -0.5
ROPE_BASE = 10000.0
HEAD_SLAB = 128
MLA_QW = MLA_HEADS * HEAD_SLAB
MLA_VW = MLA_HEADS * MLA_V

S5_W = 512
S5_GROUP = 16
S5_GROUPS = 32
S5_P = 64
S5_CHUNK = 16
S5_PAIRS = S5_GROUPS // 2

NA_HEADS = 8
NA_W = 512
NA_DIM = 64
NA_WIN_R = 8
NA_WIN_C = 16
NA_SCALE = NA_DIM ** -0.5
NA_QROWS = 4
NA_KROWS = 12
NA_NDR = 2 * NA_WIN_R - 1
NA_NDC = 2 * NA_WIN_C - 1

D_FF = 2816
FF_CHUNK = 256
FF_NCHUNK = D_FF // FF_CHUNK
FF_EXT = 16

TM = 512
TQ = 256
FF_SUB = SEQ
FF_NSUB = TM // FF_SUB
HALO = 16
N_TOK_P = BATCH * SEQ
N_TOK_S = DEC_BATCH * DEC_SEQ
N_TOK = N_TOK_P + N_TOK_S
NT_P = N_TOK_P // TM
NT_S = N_TOK_S // TM
NT = NT_P + NT_S
TILES_PER_DEC = DEC_SEQ // TM

VMEM_LIMIT = 56 * 1024 * 1024


def _cp(*sem):
    return pltpu.CompilerParams(dimension_semantics=sem, vmem_limit_bytes=VMEM_LIMIT)


def _dot(a, b):
    return jnp.dot(a, b, preferred_element_type=F32)


def _dot_nt(a, b):
    return lax.dot_general(a, b, (((1,), (1,)), ((), ())), preferred_element_type=F32)


def _dot_tn(a, b):
    return lax.dot_general(a, b, (((0,), (0,)), ((), ())), preferred_element_type=F32)


def _rms(x, g):
    return x * lax.rsqrt(jnp.mean(x * x, axis=-1, keepdims=True) + EPS) * g


def _sigmoid(x):
    return 1.0 / (1.0 + jnp.exp(-x))


def _silu(x):
    return x * _sigmoid(x)


def _cmul(ar, ai, br, bi):
    return ar * br - ai * bi, ar * bi + ai * br


def _mrow(i):
    return jnp.where(i < NT_P, 0, 1 + (i - NT_P) // TILES_PER_DEC)


def _full(shape):
    n = len(shape)
    return pl.BlockSpec(shape, lambda *_: (0,) * n)


def _layer(layer, shape):
    n = len(shape)
    return pl.BlockSpec((None,) + shape, lambda *_: (layer,) + (0,) * n)


def _flat_layer(j, rows, cols):
    return pl.BlockSpec((rows, cols), lambda *_: (j, 0))


def _mod_spec(layer):
    return pl.BlockSpec((None, None, 1, 6 * D), lambda i: (layer, _mrow(i), 0, 0))


def _row(width):
    return pl.BlockSpec((TM, width), lambda i: (i, 0))


def _ctx_row(width):
    return pl.BlockSpec((TM, width), lambda i: (jnp.minimum(i, NT_P - 1), 0))


def _lat_row(width):
    return pl.BlockSpec((TM, width), lambda i: (jnp.maximum(i - NT_P, 0), 0))


def _pick(a_ref, b_ref):
    return jnp.where(pl.program_id(0) < NT_P, a_ref[...], b_ref[...])


def _ada_kernel(c_ref, w_ref, b_ref, *rest):
    n_w = (len(rest) - 1) // 2
    o_ref = rest[n_w]
    o_ref[...] = _dot(_silu(c_ref[...]).astype(BF16), w_ref[...].astype(BF16)) + b_ref[...]
    for src, dst in zip(rest[:n_w], rest[n_w + 1:]):
        dst[...] = src[...].astype(BF16)


def _ada_mods(cvec, ada_w, ada_b, weights):
    nb = 4
    bn = 6 * D // nb
    steps = DEPTH * nb
    flat = [w.reshape(-1, w.shape[-1]) for w in weights]
    slab = lambda w: pl.BlockSpec((w.shape[0] // steps, w.shape[1]), lambda l, n: (l * nb + n, 0))
    out = pl.pallas_call(
        _ada_kernel,
        grid=(DEPTH, nb),
        in_specs=[
            pl.BlockSpec((8, D), lambda l, n: (0, 0)),
            pl.BlockSpec((None, D, bn), lambda l, n: (l, 0, n)),
            pl.BlockSpec((None, 1, bn), lambda l, n: (l, 0, n)),
        ] + [slab(w) for w in flat],
        out_specs=[pl.BlockSpec((None, 8, bn), lambda l, n: (l, 0, n))] + [slab(w) for w in flat],
        out_shape=[jax.ShapeDtypeStruct((DEPTH, 8, 6 * D), F32)]
        + [jax.ShapeDtypeStruct(w.shape, BF16) for w in flat],
        compiler_params=_cp("arbitrary", "arbitrary"),
        name="ada_mods",
    )(cvec, ada_w, ada_b.reshape(DEPTH, 1, 6 * D), *flat)
    return out[0][:, :3].reshape(DEPTH, 3, 1, 6 * D), out[1:]


def _cast_ffn_slab(wu32_ref, wd32_ref, wu_ref, wd_ref):
    wu_ref[...] = wu32_ref[...].astype(BF16)
    wd_ref[...] = wd32_ref[...].astype(BF16)


def _cast_ffn_specs(layer, n_inner):
    n = DEC_BATCH * n_inner
    ru, rd = D // n, D_FF // n
    cast_in = [pl.BlockSpec((None, ru, 2 * D_FF), lambda b, t: (layer, b * n_inner + t, 0)),
               pl.BlockSpec((None, rd, D), lambda b, t: (layer, b * n_inner + t, 0))]
    cast_out = [pl.BlockSpec((ru, 2 * D_FF), lambda b, t: (b * n_inner + t, 0)),
                pl.BlockSpec((rd, D), lambda b, t: (b * n_inner + t, 0))]
    cast_shape = [jax.ShapeDtypeStruct((D, 2 * D_FF), BF16), jax.ShapeDtypeStruct((D_FF, D), BF16)]
    return cast_in, cast_out, cast_shape


def _in_even_kernel(xc_ref, xl_ref, mod_ref, g_ref, w_ref, wkr_ref, qn_ref, kvn_ref, wuq_ref, wk_ref, wv_ref, cos_ref,
                    sin_ref, qkvg_ref, qp_ref, kp_ref, v_ref, ckv_ref, kr_ref):
    x = _pick(xc_ref, xl_ref)
    mod = mod_ref[...]
    h = _rms(x, g_ref[...]) * (1.0 + mod[:, D:2 * D]) + mod[:, :D]
    hb = h.astype(BF16)
    o = 4 * RET_W
    for part in range(4):
        cols = slice(part * RET_W, (part + 1) * RET_W)
        qkvg_ref[:, cols] = _dot(hb, w_ref[:, cols]).astype(BF16)
    r = _dot(hb, w_ref[:, o:o + MLA_Q_RANK + MLA_KV_RANK])
    cq = r[:, :MLA_Q_RANK]
    ckv_raw = r[:, MLA_Q_RANK:]
    r2 = _dot(hb, wkr_ref[...])
    kr = r2[:, :HEAD_SLAB]
    krs = r2[:, HEAD_SLAB:]
    cosf = cos_ref[...]
    sinf = sin_ref[...]
    q2 = _dot(_rms(cq, qn_ref[...]).astype(BF16), wuq_ref[...])
    ckvn = _rms(ckv_raw, kvn_ref[...])

    @pl.when(pl.program_id(0) < NT_P)
    def _():
        ckv_ref[...] = ckvn
        kr_ref[...] = kr

    cb = ckvn.astype(BF16)
    kp = _dot(cb, wk_ref[...])
    v_ref[...] = _dot(cb, wv_ref[...]).astype(BF16)
    krr = kr * cosf + krs * sinf
    for hh in range(MLA_HEADS):
        sl = slice(hh * HEAD_SLAB, (hh + 1) * HEAD_SLAB)
        ss = slice(MLA_QW + hh * HEAD_SLAB, MLA_QW + (hh + 1) * HEAD_SLAB)
        qp_ref[:, sl] = ((q2[:, sl] * cosf + q2[:, ss] * sinf) * (MLA_SCALE * LOG2E)).astype(BF16)
        kp_ref[:, sl] = (kp[:, sl] + krr).astype(BF16)


def _in_even(xs, mods, gains, layer, j, w_in, wkr2, qn, kvn, wuq2, wk, wv, cos_t, sin_t):
    pos_spec = pl.BlockSpec((TM, HEAD_SLAB), lambda i: (jnp.where(i < NT_P, 0, 1 + (i - NT_P) % TILES_PER_DEC), 0))
    return pl.pallas_call(
        _in_even_kernel,
        grid=(NT,),
        in_specs=[_ctx_row(D), _lat_row(D), _mod_spec(layer), _layer(layer, (1, D)), _flat_layer(j, D, w_in.shape[1]),
                  _full(wkr2.shape), _layer(j, (1, MLA_Q_RANK)), _layer(j, (1, MLA_KV_RANK)), _full(wuq2.shape),
                  _full(wk.shape), _full(wv.shape), pos_spec, pos_spec],
        out_specs=[_row(4 * RET_W), _row(MLA_QW), _row(MLA_QW), _row(MLA_VW), _ctx_row(MLA_KV_RANK),
                   _ctx_row(HEAD_SLAB)],
        out_shape=[
            jax.ShapeDtypeStruct((N_TOK, 4 * RET_W), BF16),
            jax.ShapeDtypeStruct((N_TOK, MLA_QW), BF16),
            jax.ShapeDtypeStruct((N_TOK, MLA_QW), BF16),
            jax.ShapeDtypeStruct((N_TOK, MLA_VW), BF16),
            jax.ShapeDtypeStruct((N_TOK_P, MLA_KV_RANK), F32),
            jax.ShapeDtypeStruct((N_TOK_P, HEAD_SLAB), F32),
        ],
        compiler_params=_cp("arbitrary"),
        name="in_even",
    )(*xs, mods, gains, w_in, wkr2, qn, kvn, wuq2, wk, wv, cos_t, sin_t)


def _mla_cache_kernel(ckv_ref, kr_ref, wk_ref, wv_ref, kp_ref, v_ref):
    cb = ckv_ref[...].astype(BF16)
    kp = _dot(cb, wk_ref[...])
    kr = kr_ref[...]
    for hh in range(MLA_HEADS):
        sl = slice(hh * HEAD_SLAB, (hh + 1) * HEAD_SLAB)
        kp_ref[:, sl] = (kp[:, sl] + kr).astype(BF16)
    v_ref[...] = _dot(cb, wv_ref[...]).astype(BF16)


def _mla_cache(ckv_c, kr_slab, j, wk, wv):
    row = lambda w: pl.BlockSpec((PAST_LEN, w), lambda b: (b, 0))
    cache = lambda w: pl.BlockSpec((None, None, PAST_LEN, w), lambda b: (b, j, 0, 0))
    return pl.pallas_call(
        _mla_cache_kernel,
        grid=(DEC_BATCH,),
        in_specs=[cache(MLA_KV_RANK), cache(HEAD_SLAB), _full(wk.shape), _full(wv.shape)],
        out_specs=[row(MLA_QW), row(MLA_VW)],
        out_shape=[jax.ShapeDtypeStruct((DEC_BATCH * PAST_LEN, MLA_QW), BF16),
                   jax.ShapeDtypeStruct((DEC_BATCH * PAST_LEN, MLA_VW), BF16)],
        compiler_params=_cp("parallel"),
        name="mla_cache",
    )(ckv_c, kr_slab, wk, wv)


def _ret_kernel(lg_ref, q_ref, k_ref, v_ref, g_ref, gn_ref, *rest, seq_len, n_blk, emit_state, n_prev, zero_init):
    if zero_init:
        s0 = lambda b, d: jnp.zeros((RET_DIM, RET_DIM), F32)
    else:
        s0_ref, rest = rest[0], rest[1:]
        s0 = lambda b, d: s0_ref[b, d]
    prev_refs, rest = rest[:min(n_prev, 1)], rest[min(n_prev, 1):]
    if emit_state:
        y_ref, st_ref, sf_scr, sb_scr, dm_scr, w_scr = rest
    else:
        y_ref, sf_scr, sb_scr, dm_scr, w_scr = rest
    C = RET_CHUNK
    nc = seq_len // C

    @pl.when(pl.program_id(1) == 0)
    def _():
        lg = -jnp.log(1.0 + jnp.exp(-lg_ref[...]))
        lg_f = lg[0]
        lg_b = lg[1]
        ii = lax.broadcasted_iota(jnp.int32, (C, C), 0)
        jj = lax.broadcasted_iota(jnp.int32, (C, C), 1)
        diff = (ii - jj).astype(F32)
        dm_scr[...] = (jnp.where(diff >= 0, jnp.exp(lg_f * jnp.maximum(diff, 0.0)), 0.0)
                       + jnp.where(diff <= 0, jnp.exp(lg_b * jnp.maximum(-diff, 0.0)), 0.0))
        pos = lax.broadcasted_iota(jnp.int32, (C, RET_DIM), 0).astype(F32)
        w_scr[0] = jnp.exp(lg_f * (pos + 1.0))
        w_scr[1] = jnp.exp(lg_f * (C - 1.0 - pos))
        w_scr[2] = jnp.exp(lg_b * (C - pos))
        w_scr[3] = jnp.exp(lg_b * pos)
        w_scr[4] = jnp.exp(lg_f * C) + jnp.zeros((C, RET_DIM), F32)
        w_scr[5] = jnp.exp(lg_b * C) + jnp.zeros((C, RET_DIM), F32)

    qw_f, kw_f, qw_b, kw_b = w_scr[0], w_scr[1], w_scr[2], w_scr[3]
    cd_f = w_scr[4, :RET_DIM, :]
    cd_b = w_scr[5, :RET_DIM, :]
    ld = lambda ref, rows: ref[rows, :].astype(F32)
    gn = gn_ref[...]
    dm = dm_scr[...]

    for b in range(n_blk):
        rows_of = lambda n: slice(b * seq_len + n * C, b * seq_len + (n + 1) * C)
        sf = lambda n: sf_scr.at[b * (nc + 1) + n]
        sb = lambda n: sb_scr.at[b * (nc + 1) + n]
        sf(0)[...] = s0(b, 0)
        for n in range(nc):
            kv = _dot_tn((ld(k_ref, rows_of(n)) * kw_f).astype(BF16), v_ref[rows_of(n), :])
            sf(n + 1)[...] = cd_f * sf(n)[...] + kv
        sb(nc)[...] = s0(b, 1)
        for n in reversed(range(nc)):
            kv = _dot_tn((ld(k_ref, rows_of(n)) * kw_b).astype(BF16), v_ref[rows_of(n), :])
            sb(n)[...] = cd_b * sb(n + 1)[...] + kv
        if emit_state:
            for jp in range(n_prev):
                st_ref[b, jp] = prev_refs[0][b, jp]
            st_ref[b, n_prev, 0] = sf(nc)[...]
            st_ref[b, n_prev, 1] = sb(0)[...]
        for n in range(nc):
            rows = rows_of(n)
            q = ld(q_ref, rows) * (RET_DIM ** -0.5)
            s = _dot_nt(q.astype(BF16), k_ref[rows, :]) * dm
            o = (_dot(s.astype(BF16), v_ref[rows, :])
                 + _dot((q * qw_f).astype(BF16), sf(n)[...].astype(BF16))
                 + _dot((q * qw_b).astype(BF16), sb(n + 1)[...].astype(BF16)))
            mu = jnp.mean(o, axis=-1, keepdims=True)
            oc = o - mu
            var = jnp.mean(oc * oc, axis=-1, keepdims=True)
            on = oc * lax.rsqrt(var + EPS) * gn
            y_ref[rows, :] = (_silu(ld(g_ref, rows)) * on).astype(BF16)


def _retention(qkvg, logit, gn, j, s0, *, seq_len, n_seq, n_blk, row0, emit_state, prev_states=None):
    n_prev = 0 if prev_states is None else prev_states.shape[1]
    nc = seq_len // RET_CHUNK
    rows = n_blk * seq_len
    blk0 = row0 // rows
    col = lambda part: pl.BlockSpec((rows, RET_DIM), lambda h, s: (blk0 + s, part * RET_HEADS + h))
    in_specs = [
        pl.BlockSpec((None, 2, 1, 1), lambda h, s: (h, 0, 0, 0)),
        col(0), col(1), col(2), col(3),
        pl.BlockSpec((None, 1, RET_DIM), lambda h, s: (j, 0, h)),
    ]
    out_specs = [pl.BlockSpec((rows, RET_DIM), lambda h, s: (s, h))]
    out_shape = [jax.ShapeDtypeStruct((n_seq * seq_len, RET_W), BF16)]
    args = [logit, qkvg, qkvg, qkvg, qkvg, gn]
    if s0 is not None:
        in_specs.append(pl.BlockSpec((n_blk, None, 2, None, RET_DIM, RET_DIM), lambda h, s: (s, j, 0, h, 0, 0)))
        args.append(s0)
    stacked = lambda n: pl.BlockSpec((n_blk, n, 2, None, RET_DIM, RET_DIM), lambda h, s: (s, 0, 0, h, 0, 0))
    if n_prev:
        in_specs.append(stacked(n_prev))
        args.append(prev_states)
    if emit_state:
        out_specs.append(stacked(n_prev + 1))
        out_shape.append(jax.ShapeDtypeStruct((n_seq, n_prev + 1, 2, RET_HEADS, RET_DIM, RET_DIM), F32))
    n_st = n_blk * (nc + 1)
    return pl.pallas_call(
        functools.partial(_ret_kernel, seq_len=seq_len, n_blk=n_blk, emit_state=emit_state, n_prev=n_prev,
                          zero_init=s0 is None),
        grid=(RET_HEADS, n_seq // n_blk),
        in_specs=in_specs,
        out_specs=out_specs,
        out_shape=out_shape,
        scratch_shapes=[pltpu.VMEM((n_st, RET_DIM, RET_DIM), F32), pltpu.VMEM((n_st, RET_DIM, RET_DIM), F32),
                        pltpu.VMEM((RET_CHUNK, RET_CHUNK), F32), pltpu.VMEM((6, RET_CHUNK, RET_DIM), F32)],
        compiler_params=_cp("parallel", "arbitrary"),
        name="retention_%d" % seq_len,
    )(*args)


def _softmax_pv(score_blocks, value_blocks):
    m = functools.reduce(jnp.maximum, [jnp.max(s, axis=-1, keepdims=True) for s in score_blocks])
    ps = [jnp.exp2(s - m) for s in score_blocks]
    l = functools.reduce(lambda a, b: a + b, [jnp.sum(p, axis=-1, keepdims=True) for p in ps])
    o = functools.reduce(lambda a, b: a + b, [_dot(p.astype(BF16), v) for p, v in zip(ps, value_blocks)])
    return o / l


def _low_half(shape):
    return lax.broadcasted_iota(jnp.int32, shape, 1) < 64


def _heads_pipelined(n_heads, scores, attend, y_ref, rows, row_slices=(slice(None),)):
    low = _low_half((rows, 128))
    units = [(hh, rs) for hh in range(n_heads) for rs in row_slices]
    nxt = scores(*units[0])
    outs = {}
    for n, (hh, rs) in enumerate(units):
        cur = nxt
        if n + 1 < len(units):
            nxt = scores(*units[n + 1])
        outs[hh % 2] = outs.get(hh % 2, {})
        outs[hh % 2][n % len(row_slices)] = attend(hh, rs, cur)
        if hh % 2 == 1:
            hp = hh // 2
            k = n % len(row_slices)
            y_ref[rs, hp * 128:(hp + 1) * 128] = jnp.where(low, outs[0][k], outs[1][k]).astype(BF16)


CTX_SEQS = 2
CTX_ROWS = [slice(b * SEQ, (b + 1) * SEQ) for b in range(CTX_SEQS)]


def _mla_ctx_kernel(q_ref, k_ref, v_ref, y_ref):
    slab = lambda hh: slice(hh * HEAD_SLAB, (hh + 1) * HEAD_SLAB)
    pair = lambda hh: slice(hh // 2 * 128, (hh // 2 + 1) * 128)
    scores = lambda hh, rs: [_dot_nt(q_ref[rs, slab(hh)], k_ref[rs, slab(hh)])]
    attend = lambda hh, rs, s: _softmax_pv(s, [v_ref[rs, pair(hh)]])
    _heads_pipelined(MLA_HEADS, scores, attend, y_ref, SEQ, CTX_ROWS)


def _mla_ctx(qp, kp, v):
    rows = CTX_SEQS * SEQ
    return pl.pallas_call(
        _mla_ctx_kernel,
        grid=(BATCH // CTX_SEQS,),
        in_specs=[pl.BlockSpec((rows, MLA_QW), lambda b: (b, 0)), pl.BlockSpec((rows, MLA_QW), lambda b: (b, 0)),
                  pl.BlockSpec((rows, MLA_VW), lambda b: (b, 0))],
        out_specs=pl.BlockSpec((rows, MLA_VW), lambda b: (b, 0)),
        out_shape=jax.ShapeDtypeStruct((N_TOK_P, MLA_VW), BF16),
        compiler_params=_cp("parallel"),
        name="mla_ctx",
    )(qp, kp, v)


def _mla_lat_kernel(q_ref, k_ref, v_ref, kc_ref, vc_ref, wu32_ref, wd32_ref, y_ref, wu_ref, wd_ref):
    _cast_ffn_slab(wu32_ref, wd32_ref, wu_ref, wd_ref)
    slab = lambda hh: slice(hh * HEAD_SLAB, (hh + 1) * HEAD_SLAB)
    pair = lambda hh: slice(hh // 2 * 128, (hh // 2 + 1) * 128)
    scores = lambda hh, rs: [_dot_nt(q_ref[:, slab(hh)], k_ref[:, slab(hh)]),
                             _dot_nt(q_ref[:, slab(hh)], kc_ref[:, slab(hh)])]
    attend = lambda hh, rs, s: _softmax_pv(s, [v_ref[:, pair(hh)], vc_ref[:, pair(hh)]])
    _heads_pipelined(MLA_HEADS, scores, attend, y_ref, TQ)


def _mla_lat(qp, kp, v, kp_c, v_c, layer, w_up, w_down):
    seq_blk = N_TOK_P // DEC_SEQ
    nq = DEC_SEQ // TQ
    cast_in, cast_out, cast_shape = _cast_ffn_specs(layer, nq)
    return pl.pallas_call(
        _mla_lat_kernel,
        grid=(DEC_BATCH, nq),
        in_specs=[
            pl.BlockSpec((TQ, MLA_QW), lambda b, t: (N_TOK_P // TQ + b * nq + t, 0)),
            pl.BlockSpec((DEC_SEQ, MLA_QW), lambda b, t: (seq_blk + b, 0)),
            pl.BlockSpec((DEC_SEQ, MLA_VW), lambda b, t: (seq_blk + b, 0)),
            pl.BlockSpec((PAST_LEN, MLA_QW), lambda b, t: (b, 0)),
            pl.BlockSpec((PAST_LEN, MLA_VW), lambda b, t: (b, 0)),
        ] + cast_in,
        out_specs=[pl.BlockSpec((TQ, MLA_VW), lambda b, t: (b * nq + t, 0))] + cast_out,
        out_shape=[jax.ShapeDtypeStruct((N_TOK_S, MLA_VW), BF16)] + cast_shape,
        compiler_params=_cp("parallel", "parallel"),
        name="mla_lat",
    )(qp, kp, v, kp_c, v_c, w_up, w_down)


def _gelu_tanh(x):
    return 0.5 * x * (1.0 + jnp.tanh(np.sqrt(2.0 / np.pi).astype(np.float32) * (x + 0.044715 * (x * x * x))))


def _ffn_kernel(*refs, odd):
    xc_ref, xl_ref, xp_ref, xn_ref = refs[:4]
    if odd:
        (yr_ref, yrp_ref, yrn_ref, u_ref, up_ref, un_ref, bc_ref, bl_ref, bp_ref, bn_ref, mod_ref, gmix_ref, gpre_ref,
         gpost_ref, wo_ref, d_ref, gw_ref, gb_ref, wu_ref, cw_ref, cb_ref, wd_ref, oc_ref, ol_ref, h_scr,
         act_scr) = refs[4:]
    else:
        (ac_ref, al_ref, ap_ref, an_ref, bc_ref, bl_ref, bp_ref, bn_ref, mod_ref, gmix_ref, gpre_ref, gpost_ref,
         wo_ref, wu_ref, cw_ref, cb_ref, wd_ref, oc_ref, ol_ref, h_scr, act_scr) = refs[4:]
    i = pl.program_id(0)
    is_lat = i >= NT_P
    t = (i - NT_P) % TILES_PER_DEC
    has_prev = jnp.logical_and(is_lat, t != 0)
    has_next = jnp.logical_and(is_lat, t != TILES_PER_DEC - 1)
    mod = mod_ref[...]
    cat = lambda main, prev_ref, next_ref: jnp.concatenate([main, prev_ref[...], next_ref[...]], axis=0)

    if odd:
        y = _gelu_tanh(cat(yr_ref[...], yrp_ref, yrn_ref) + d_ref[...] * cat(u_ref[...], up_ref, un_ref))
        ya = (y * _sigmoid(_dot(y.astype(BF16), gw_ref[...]) + gb_ref[...])).astype(BF16)
    else:
        ya = cat(_pick(ac_ref, al_ref), ap_ref, an_ref)
    yb = cat(_pick(bc_ref, bl_ref), bp_ref, bn_ref)
    half = ya.shape[1]
    r = _dot(ya, wo_ref[:half, :]) + _dot(yb, wo_ref[half:, :])
    x1 = cat(_pick(xc_ref, xl_ref), xp_ref, xn_ref) + mod[:, 2 * D:3 * D] * _rms(r, gmix_ref[...])

    shift = mod[:, 3 * D:4 * D]
    scale = mod[:, 4 * D:5 * D]
    gate = mod[:, 5 * D:6 * D]
    hall = _rms(x1, gpre_ref[...]) * (1.0 + scale) + shift
    x = x1[:TM]
    h = hall[:TM]
    blk = FF_SUB + FF_EXT
    rows = FF_NSUB * blk
    for k in range(FF_NSUB):
        h_scr[k * blk:k * blk + FF_SUB, :] = h[k * FF_SUB:(k + 1) * FF_SUB].astype(BF16)
        if k + 1 < FF_NSUB:
            after = jnp.where(is_lat, h[(k + 1) * FF_SUB:(k + 1) * FF_SUB + 8], 0.0)
            before = jnp.where(is_lat, h[(k + 1) * FF_SUB - 8:(k + 1) * FF_SUB], 0.0)
        else:
            after = jnp.where(has_next, hall[TM + HALO:TM + HALO + 8], 0.0)
            before = jnp.where(has_prev, hall[TM + HALO - 8:TM + HALO], 0.0)
        h_scr[k * blk + FF_SUB:(k + 1) * blk, :] = jnp.concatenate([after, before], axis=0).astype(BF16)
    hb = h_scr[...]

    def up(j):
        ca = slice(j * FF_CHUNK, (j + 1) * FF_CHUNK)
        cg = slice(D_FF + j * FF_CHUNK, D_FF + (j + 1) * FF_CHUNK)
        return (_dot(hb, wu_ref[:, ca]), ca), (_dot(hb, wu_ref[:, cg]), cg)

    def conv(part):
        u, cols = part
        cw = cw_ref[:, cols]
        return (cw[0:1, :] * pltpu.roll(u, 1, axis=0) + cw[1:2, :] * u + cw[2:3, :] * pltpu.roll(u, rows - 1, axis=0)
                + cb_ref[:, cols])

    nxt = up(0)
    for j in range(FF_NCHUNK):
        cur = nxt
        if j + 1 < FF_NCHUNK:
            nxt = up(j + 1)
        act = (_silu(conv(cur[1])) * conv(cur[0])).astype(BF16)
        for k in range(FF_NSUB):
            act_scr[k * FF_SUB:(k + 1) * FF_SUB, j * FF_CHUNK:(j + 1) * FF_CHUNK] = act[k * blk:k * blk + FF_SUB]
    out = x + gate * _rms(_dot(act_scr[...], wd_ref[...]), gpost_ref[...])

    @pl.when(i < NT_P)
    def _():
        oc_ref[...] = out

    @pl.when(i >= NT_P)
    def _():
        ol_ref[...] = out


def _halo_specs(width, n_rows, tile0):
    per = TM // HALO
    last = n_rows // HALO - 1
    prev = pl.BlockSpec((HALO, width), lambda i: (jnp.clip((i - tile0) * per - 1, 0, last), 0))
    nxt = pl.BlockSpec((HALO, width), lambda i: (jnp.clip((i - tile0 + 1) * per, 0, last), 0))
    return [prev, nxt]


def _ffn(xs, mix, mods, gmix, gpre, gpost, layer, j, w_out, wu, cw, cb, wd, s5=None):
    odd = s5 is not None
    pair = lambda w: [_ctx_row(w), _lat_row(w)]
    lat_halo = lambda w: _halo_specs(w, N_TOK_S, NT_P)
    all_halo = lambda w: _halo_specs(w, N_TOK, 0)
    resident = lambda shape: pl.BlockSpec(shape, lambda i: (0, 0), pipeline_mode=pl.Buffered(1))
    x_specs = pair(D) + lat_halo(D)
    x_args = [xs[0], xs[1], xs[1], xs[1]]
    if odd:
        yr, u, b_c, b_l = mix
        mix_specs = [_row(S5_W)] + all_halo(S5_W) + [_row(S5_W)] + all_halo(S5_W) + pair(NA_W) + lat_halo(NA_W)
        mix_args = [yr, yr, yr, u, u, u, b_c, b_l, b_l, b_l]
        s5_specs = [_layer(j, (1, S5_W)), _flat_layer(j, S5_W, S5_W), _layer(j, (1, S5_W))]
        s5_args = list(s5)
    else:
        a_c, a_l, b_c, b_l = mix
        mix_specs = pair(RET_W) + lat_halo(RET_W) + pair(MLA_VW) + lat_halo(MLA_VW)
        mix_args = [a_c, a_l, a_l, a_l, b_c, b_l, b_l, b_l]
        s5_specs, s5_args = [], []
    return pl.pallas_call(
        functools.partial(_ffn_kernel, odd=odd),
        grid=(NT,),
        in_specs=x_specs + mix_specs + [_mod_spec(layer), _layer(layer, (1, D)), _layer(layer, (1, D)),
                                        _layer(layer, (1, D)), _flat_layer(j, D, D)] + s5_specs
        + [resident((D, 2 * D_FF)), _layer(layer, (3, 2 * D_FF)), _layer(layer, (1, 2 * D_FF)), resident((D_FF, D))],
        out_specs=[_ctx_row(D), _lat_row(D)],
        out_shape=[jax.ShapeDtypeStruct((N_TOK_P, D), F32), jax.ShapeDtypeStruct((N_TOK_S, D), F32)],
        scratch_shapes=[pltpu.VMEM((FF_NSUB * (FF_SUB + FF_EXT), D), BF16), pltpu.VMEM((TM, D_FF), BF16)],
        compiler_params=_cp("arbitrary"),
        name="ffn_odd" if odd else "ffn_even",
    )(*x_args, *mix_args, mods, gmix, gpre, gpost, w_out, *s5_args, wu, cw, cb, wd)


def _in_odd_kernel(xc_ref, xl_ref, mod_ref, g_ref, w_ref, *rest, n_prev):
    np2 = 2 * min(n_prev, 1)
    prev_refs, (u_ref, qkv_ref, kc_ref, vc_ref) = rest[:np2], rest[np2:]
    mod = mod_ref[...]
    h = _rms(_pick(xc_ref, xl_ref), g_ref[...]) * (1.0 + mod[:, D:2 * D]) + mod[:, :D]
    r = _dot(h.astype(BF16), w_ref[...])
    u_ref[...] = r[:, :S5_W]
    qkv_ref[:, :NA_W] = (r[:, S5_W:S5_W + NA_W] * (NA_SCALE * LOG2E)).astype(BF16)
    qkv_ref[:, NA_W:] = r[:, S5_W + NA_W:].astype(BF16)

    @pl.when(pl.program_id(0) < NT_P)
    def _():
        for part, (out_ref, col0) in enumerate([(kc_ref, S5_W + NA_W), (vc_ref, S5_W + 2 * NA_W)]):
            for b in range(TM // SEQ):
                for jp in range(n_prev):
                    out_ref[b, jp] = prev_refs[part][b, jp]
                out_ref[b, n_prev] = r[b * SEQ:(b + 1) * SEQ, col0:col0 + NA_W]


def _in_odd(xs, mods, gains, layer, j, w, prev_kv=None):
    n_prev = 0 if prev_kv is None else prev_kv[0].shape[1]
    nb = TM // SEQ
    stacked = lambda n: pl.BlockSpec((nb, n, SEQ, NA_W), lambda i: (jnp.minimum(i, NT_P - 1), 0, 0, 0))
    leaf = jax.ShapeDtypeStruct((BATCH, n_prev + 1, SEQ, NA_W), F32)
    prev_specs = [stacked(n_prev)] * 2 if n_prev else []
    return pl.pallas_call(
        functools.partial(_in_odd_kernel, n_prev=n_prev),
        grid=(NT,),
        in_specs=[_ctx_row(D), _lat_row(D), _mod_spec(layer), _layer(layer, (1, D)), _flat_layer(j, D, w.shape[1])]
        + prev_specs,
        out_specs=[_row(S5_W), _row(3 * NA_W), stacked(n_prev + 1), stacked(n_prev + 1)],
        out_shape=[jax.ShapeDtypeStruct((N_TOK, S5_W), F32), jax.ShapeDtypeStruct((N_TOK, 3 * NA_W), BF16), leaf, leaf],
        compiler_params=_cp("arbitrary"),
        name="in_odd",
    )(*xs, mods, gains, w, *(prev_kv or ()))


S5_TS = S5_CHUNK * S5_GROUP
S5_PL = 2 * S5_P
S5_PREP_PAIRS = 2


def _s5_prep_kernel(*refs):
    ins, outs = refs[:7], refs[7:]
    for q in range(S5_PREP_PAIRS):
        _s5_prep_pair(*[r.at[:, q] for r in ins], *[r.at[q] for r in outs])


def _s5_prep_pair(lre_ref, lim_ref, ls_ref, btr_ref, bti_ref, cr_ref, ci_ref, m_ref, n_ref, p_ref, a_ref, ct_scr):
    T = S5_CHUNK
    S = S5_GROUP
    hi = lax.Precision.HIGHEST
    low = lax.broadcasted_iota(jnp.int32, (S, S5_PL), 1) < S5_P
    half = [low, jnp.logical_not(low)]
    pick = lambda e, v: jnp.where(half[e], v, 0.0)
    nt = (((1,), (1,)), ((), ()))
    kps = [[None, None], [None, None]]
    for d in range(2):
        lre = lre_ref[d]
        lim = lim_ref[d]
        step = jnp.exp(ls_ref[d])
        mag = jnp.exp(lre * step)
        are = mag * jnp.cos(lim * step)
        aim = mag * jnp.sin(lim * step)
        den = lre * lre + lim * lim
        zr, zi = _cmul(are - 1.0, aim, lre / den, -lim / den)
        bbr, bbi = _cmul(zr, zi, btr_ref[d], bti_ref[d])
        cr = cr_ref[d]
        ci = ci_ref[d]
        pr = jnp.ones_like(are)
        pi = jnp.zeros_like(are)
        for k in range(T + 1):
            er, ei = _cmul(cr, ci, pr, pi)
            if k < T:
                jn = T - 1 - k if d == 0 else k
                wr, wi = _cmul(pr, pi, bbr, bbi)
                for e in range(2):
                    rows = slice(e * S5_TS + jn * S, e * S5_TS + (jn + 1) * S)
                    n_ref[d, rows, 0:S5_PL] = pick(e, wr).astype(BF16)
                    n_ref[d, rows, S5_PL:2 * S5_PL] = pick(e, wi).astype(BF16)
                jc = k if d == 0 else T - 1 - k
                ct_scr[0, jc * S:(jc + 1) * S, :] = er
                ct_scr[1, jc * S:(jc + 1) * S, :] = ei
            if k >= 1:
                t = k - 1 if d == 0 else T - k
                for e in range(2):
                    rows = slice(e * S5_TS + t * S, e * S5_TS + (t + 1) * S)
                    p_ref[rows, 2 * d * S5_PL:(2 * d + 1) * S5_PL] = pick(e, er).astype(BF16)
                    p_ref[rows, (2 * d + 1) * S5_PL:(2 * d + 2) * S5_PL] = pick(e, -ei).astype(BF16)
            if k == T:
                a_ref[d, 0] = pr
                a_ref[d, 1] = pi
            pr, pi = _cmul(pr, pi, are, aim)
        for e in range(2):
            kd = (lax.dot_general(pick(e, bbr), ct_scr[0], nt, precision=hi, preferred_element_type=F32)
                  - lax.dot_general(pick(e, bbi), ct_scr[1], nt, precision=hi, preferred_element_type=F32))
            kps[d][e] = jnp.concatenate([kd, jnp.zeros_like(kd)], axis=1)
    for e in range(2):
        for t in range(T):
            fwd = pltpu.roll(kps[0][e], t * S, axis=1)[:, :S5_TS]
            bwd = pltpu.roll(kps[1][e], (2 * S5_TS - (T - 1 - t) * S) % (2 * S5_TS), axis=1)[:, :S5_TS]
            m_ref[e, t * S:(t + 1) * S, :] = (fwd + bwd).astype(BF16)


def _s5_prep(lre, lim, ls, b_re, b_im, c_re, c_im):
    NP, S, P = S5_PAIRS, S5_GROUP, S5_P
    vec = lambda a: a.reshape(2, NP, 1, S5_PL)
    b_lay = lambda a: jnp.transpose(a.reshape(2, NP, 2, P, S), (0, 1, 4, 2, 3)).reshape(2, NP, S, S5_PL)
    c_lay = lambda a: jnp.transpose(a.reshape(2, NP, 2, S, P), (0, 1, 3, 2, 4)).reshape(2, NP, S, S5_PL)
    args = (vec(lre), vec(lim), vec(jnp.repeat(ls, P, axis=-1)), b_lay(b_re), b_lay(b_im), c_lay(c_re), c_lay(c_im))
    PB = S5_PREP_PAIRS
    vspec = pl.BlockSpec((2, PB, 1, S5_PL), lambda g: (0, g, 0, 0))
    mspec = pl.BlockSpec((2, PB, S, S5_PL), lambda g: (0, g, 0, 0))
    lead = lambda shape: pl.BlockSpec((PB,) + shape, lambda g: (g,) + (0,) * len(shape))
    return pl.pallas_call(
        _s5_prep_kernel,
        grid=(NP // PB,),
        in_specs=[vspec, vspec, vspec, mspec, mspec, mspec, mspec],
        out_specs=[lead((2, S5_TS, S5_TS)), lead((2, 2 * S5_TS, 2 * S5_PL)), lead((2 * S5_TS, 4 * S5_PL)),
                   lead((2, 2, 1, S5_PL))],
        out_shape=[jax.ShapeDtypeStruct((NP, 2, S5_TS, S5_TS), BF16),
                   jax.ShapeDtypeStruct((NP, 2, 2 * S5_TS, 2 * S5_PL), BF16),
                   jax.ShapeDtypeStruct((NP, 2 * S5_TS, 4 * S5_PL), BF16),
                   jax.ShapeDtypeStruct((NP, 2, 2, 1, S5_PL), F32)],
        scratch_shapes=[pltpu.VMEM((S5_PREP_PAIRS, 2, S5_TS, S5_PL), F32)],
        compiler_params=_cp("parallel"),
        name="s5_prep",
    )(*args)


S5_NCH = N_TOK // S5_CHUNK
S5_ROWS_P = N_TOK_P // S5_CHUNK
S5_NC_P = SEQ // S5_CHUNK
S5_NC_S = DEC_SEQ // S5_CHUNK
S5_GPB = 8
S5_PPB = S5_GPB // 2
S5_XL = 8 * 128


def _s5_perm():
    r = np.arange(S5_XL)
    dst = (r // S5_GROUP % S5_GPB) * 128 + (r // 128) * S5_GROUP + r % S5_GROUP
    perm = np.zeros((S5_XL, S5_XL), np.float32)
    perm[r, dst] = 1.0
    return jnp.asarray(perm, BF16)


def _s5_kernel(u_ref, perm_ref, m_ref, n_ref, p_ref, a_ref, h0c_ref, h0l_ref, y_ref, fin_ref, z_scr, up_scr, e_scr,
               hin_scr, yc_scr):
    T = S5_CHUNK
    W = S5_PL
    for t in range(T):
        z_scr[t // 8, :, (t % 8) * 128:(t % 8 + 1) * 128] = u_ref[pl.ds(t, S5_NCH, stride=T), :].astype(BF16)
    perm = perm_ref[...]
    for j in range(2):
        up_scr[j] = _dot(z_scr[j], perm).astype(BF16)

    def scan(pp, d, h0_ref, n_chunks, n_seq, row0):
        are = a_ref[pp, d, 0]
        aim = a_ref[pp, d, 1]
        hr = h0_ref[pp, d, 0]
        hi = h0_ref[pp, d, 1]
        order = range(n_chunks) if d == 0 else reversed(range(n_chunks))
        for c in order:
            rows = pl.ds(row0 + c, n_seq, stride=n_chunks)
            hin_scr[2 * d, rows, :] = hr
            hin_scr[2 * d + 1, rows, :] = hi
            er = e_scr[2 * d, rows, :]
            ei = e_scr[2 * d + 1, rows, :]
            hr, hi = are * hr - aim * hi + er, are * hi + aim * hr + ei
        return hr, hi

    for pp in range(S5_PPB):
        us = []
        for e in range(2):
            sl = slice((2 * pp + e) * 128, (2 * pp + e + 1) * 128)
            us.append(jnp.concatenate([up_scr[0, :, sl], up_scr[1, :, sl]], axis=1))
        u2 = jnp.concatenate(us, axis=1)
        for d in range(2):
            ed = _dot(u2, n_ref[pp, d])
            e_scr[2 * d] = ed[:, :W]
            e_scr[2 * d + 1] = ed[:, W:]
        for d in range(2):
            hr, hi = scan(pp, d, h0c_ref, S5_NC_P, BATCH, 0)
            fin_ref[pp, d, 0] = hr
            fin_ref[pp, d, 1] = hi
            scan(pp, d, h0l_ref, S5_NC_S, DEC_BATCH, S5_ROWS_P)
        hin = jnp.concatenate([hin_scr[k] for k in range(4)], axis=1).astype(BF16)
        for e in range(2):
            y = _dot(us[e], m_ref[pp, e]) + _dot_nt(hin, p_ref[pp, e * S5_TS:(e + 1) * S5_TS, :])
            sl = slice((2 * pp + e) * 128, (2 * pp + e + 1) * 128)
            for j in range(2):
                yc_scr[j, :, sl] = y[:, j * 128:(j + 1) * 128]

    for j in range(2):
        r = _dot_nt(yc_scr[j].astype(BF16), perm)
        for k in range(8):
            y_ref[pl.ds(8 * j + k, S5_NCH, stride=T), :] = r[:, k * 128:(k + 1) * 128]


def _s5_scan(u, perm, m, n2, p2, a, h0c, h0l):
    nb = S5_GROUPS // S5_GPB
    lead = lambda shape: pl.BlockSpec((S5_PPB,) + shape, lambda w: (w,) + (0,) * len(shape))
    col = pl.BlockSpec((N_TOK, 128), lambda w: (0, w))
    return pl.pallas_call(
        _s5_kernel,
        grid=(nb,),
        in_specs=[col, _full(perm.shape), lead((2, S5_TS, S5_TS)), lead((2, 2 * S5_TS, 2 * S5_PL)),
                  lead((2 * S5_TS, 4 * S5_PL)), lead((2, 2, 1, S5_PL)), lead((2, 2, BATCH, S5_PL)),
                  lead((2, 2, DEC_BATCH, S5_PL))],
        out_specs=[col, lead((2, 2, BATCH, S5_PL))],
        out_shape=[jax.ShapeDtypeStruct((N_TOK, S5_W), F32),
                   jax.ShapeDtypeStruct((S5_PAIRS, 2, 2, BATCH, S5_PL), F32)],
        scratch_shapes=[pltpu.VMEM((2, S5_NCH, S5_XL), BF16), pltpu.VMEM((2, S5_NCH, S5_XL), BF16),
                        pltpu.VMEM((4, S5_NCH, S5_PL), F32), pltpu.VMEM((4, S5_NCH, S5_PL), F32),
                        pltpu.VMEM((2, S5_NCH, S5_XL), F32)],
        compiler_params=_cp("parallel"),
        name="s5_scan",
    )(u, perm, m, n2, p2, a, h0c, h0l)


def _na_heads(q_ref, keys, values, y_ref, rows, bias=None, row_slices=(slice(None),)):
    low = _low_half((rows, 128))
    pair = lambda hh: slice(hh // 2 * 128, (hh // 2 + 1) * 128)

    def scores(hh, rs):
        q = q_ref[rs, pair(hh)]
        qm = jnp.where(low == (hh % 2 == 0), q, jnp.zeros_like(q))
        s = [_dot_nt(qm, k(pair(hh), rs)) for k in keys]
        if bias is not None:
            s[0] = s[0] + bias(hh)
        return s

    attend = lambda hh, rs, s: _softmax_pv(s, [v(pair(hh), rs) for v in values])
    _heads_pipelined(NA_HEADS, scores, attend, y_ref, rows, row_slices)


def _na_ctx_kernel(q_ref, k_ref, v_ref, y_ref):
    _na_heads(q_ref, [lambda sl, rs: k_ref[rs, sl]], [lambda sl, rs: v_ref[rs, sl]], y_ref, SEQ, None, CTX_ROWS)


def _na_ctx(qkv):
    rows = CTX_SEQS * SEQ
    col = lambda part: pl.BlockSpec((rows, NA_W), lambda b: (b, part))
    return pl.pallas_call(
        _na_ctx_kernel,
        grid=(BATCH // CTX_SEQS,),
        in_specs=[col(0), col(1), col(2)],
        out_specs=pl.BlockSpec((rows, NA_W), lambda b: (b, 0)),
        out_shape=jax.ShapeDtypeStruct((N_TOK_P, NA_W), BF16),
        compiler_params=_cp("parallel"),
        name="na_ctx",
    )(qkv, qkv, qkv)


def _na_key_row0(rb):
    return jnp.clip(NA_QROWS * rb - NA_WIN_R // 2, 0, GRID_H - NA_KROWS)


def _na_lat_kernel(q_ref, ks_ref, vs_ref, kc_ref, vc_ref, tab_ref, wu32_ref, wd32_ref, y_ref, wu_ref, wd_ref):
    _cast_ffn_slab(wu32_ref, wd32_ref, wu_ref, wd_ref)
    rb = pl.program_id(1)
    u0 = _na_key_row0(rb)
    start = pl.multiple_of(u0 * GRID_W, GRID_W)
    nk = NA_KROWS * GRID_W
    low_t = _low_half((GRID_W, 128))

    def table_row(i, w):
        qr = NA_QROWS * rb + i
        kr = u0 + w
        rs = jnp.clip(qr - NA_WIN_R // 2, 0, GRID_H - NA_WIN_R)
        inside = jnp.logical_and(kr >= rs, kr < rs + NA_WIN_R)
        return jnp.where(inside, kr - qr + NA_WIN_R - 1, NA_NDR)

    idx = [[table_row(i, w) for w in range(NA_KROWS)] for i in range(NA_QROWS)]

    def bias(h):
        rows = [jnp.concatenate([jnp.where(low_t, tab_ref[h, idx[i][w]], tab_ref[h, idx[i][w + 1]])
                                 for w in range(0, NA_KROWS, 2)], axis=1) for i in range(NA_QROWS)]
        return jnp.concatenate(rows, axis=0)

    keys = [lambda sl, rs: ks_ref[pl.ds(start, nk), sl], lambda sl, rs: kc_ref[:, sl].astype(BF16)]
    values = [lambda sl, rs: vs_ref[pl.ds(start, nk), sl], lambda sl, rs: vc_ref[:, sl].astype(BF16)]
    _na_heads(q_ref, keys, values, y_ref, TQ, bias)


def _na_lat(qkv, k_c, v_c, j, table, layer, w_up, w_down):
    seq_blk = N_TOK_P // DEC_SEQ
    nb = GRID_H // NA_QROWS
    cast_in, cast_out, cast_shape = _cast_ffn_specs(layer, nb)
    return pl.pallas_call(
        _na_lat_kernel,
        grid=(DEC_BATCH, nb),
        in_specs=[
            pl.BlockSpec((TQ, NA_W), lambda b, r: (N_TOK_P // TQ + b * nb + r, 0)),
            pl.BlockSpec((DEC_SEQ, NA_W), lambda b, r: (seq_blk + b, 1)),
            pl.BlockSpec((DEC_SEQ, NA_W), lambda b, r: (seq_blk + b, 2)),
            pl.BlockSpec((None, None, PAST_LEN, NA_W), lambda b, r: (b, j, 0, 0)),
            pl.BlockSpec((None, None, PAST_LEN, NA_W), lambda b, r: (b, j, 0, 0)),
            pl.BlockSpec(table.shape, lambda b, r: (0, 0, 0, 0)),
        ] + cast_in,
        out_specs=[pl.BlockSpec((TQ, NA_W), lambda b, r: (b * nb + r, 0))] + cast_out,
        out_shape=[jax.ShapeDtypeStruct((N_TOK_S, NA_W), BF16)] + cast_shape,
        compiler_params=_cp("parallel", "arbitrary"),
        name="na_lat",
    )(qkv, qkv, qkv, k_c, v_c, table, w_up, w_down)


def _na_table_kernel(rpb_ref, t_ref):
    qc = lax.broadcasted_iota(jnp.int32, (GRID_W, 128), 0)
    kc = lax.broadcasted_iota(jnp.int32, (GRID_W, 128), 1) % GRID_W
    cs = jnp.clip(qc - NA_WIN_C // 2, 0, GRID_W - NA_WIN_C)
    in_band = jnp.logical_and(kc >= cs, kc < cs + NA_WIN_C)
    neg = jnp.full((GRID_W, 128), -jnp.inf, F32)

    def body(n, carry):
        x = jnp.broadcast_to(rpb_ref[n], (GRID_W, 128))
        t = pltpu.roll(x, 128 - (NA_WIN_C - 1), axis=1, stride=1, stride_axis=0)
        t_ref[n // NA_NDR, n % NA_NDR] = jnp.where(in_band, t * LOG2E, neg)
        return carry

    lax.fori_loop(0, NA_HEADS * NA_NDR, body, 0, unroll=8)
    for h in range(NA_HEADS):
        t_ref[h, NA_NDR] = neg


def _na_table(rpb):
    rows = jnp.pad(rpb.reshape(NA_HEADS * NA_NDR, 1, NA_NDC), ((0, 0), (0, 0), (0, GRID_W - NA_NDC)))
    rows = jnp.concatenate([rows, rows], axis=-1)
    return pl.pallas_call(
        _na_table_kernel,
        out_shape=jax.ShapeDtypeStruct((NA_HEADS, NA_NDR + 1, GRID_W, 128), F32),
        name="na_table",
    )(rows)


def _rope_tables():
    n_freq = MLA_ROPE // 4
    inv = ROPE_BASE ** (-jnp.arange(n_freq, dtype=F32) / n_freq)
    t = jnp.arange(DEC_SEQ)
    row = (t // GRID_W).astype(F32)
    colp = (t % GRID_W).astype(F32)
    ang = jnp.concatenate([row[:, None] * inv, colp[:, None] * inv], axis=-1)
    cos, sin = jnp.cos(ang), jnp.sin(ang)
    one = jnp.ones((DEC_SEQ, MLA_NOPE), F32)
    zero = jnp.zeros((DEC_SEQ, MLA_NOPE), F32)
    cos_s = jnp.concatenate([one, cos, cos, one[:, :32]], axis=-1)
    sin_s = jnp.concatenate([zero, -sin, sin, zero[:, :32]], axis=-1)
    cos_t = jnp.concatenate([jnp.ones((TM, HEAD_SLAB), F32), cos_s], axis=0)
    sin_t = jnp.concatenate([jnp.zeros((TM, HEAD_SLAB), F32), sin_s], axis=0)
    return cos_t, sin_t


def _mla_weights(w_in, w_uq, w_ukv):
    half = MLA_ROPE // 2
    wkr = w_in[:, 4 * RET_W + MLA_Q_RANK + MLA_KV_RANK:]
    z64 = jnp.zeros((D, MLA_NOPE), F32)
    z32 = jnp.zeros((D, HEAD_SLAB - MLA_NOPE - MLA_ROPE), F32)
    wkr2 = jnp.concatenate([z64, wkr, z32, z64, wkr[:, half:], wkr[:, :half], z32], axis=1).astype(BF16)
    wq = w_uq.reshape(MLA_Q_RANK, MLA_HEADS, MLA_NOPE + MLA_ROPE)
    nope, rope = wq[..., :MLA_NOPE], wq[..., MLA_NOPE:]
    zq64 = jnp.zeros_like(nope)
    zq32 = jnp.zeros_like(rope)
    q_slab = jnp.concatenate([nope, rope, zq32], axis=-1).reshape(MLA_Q_RANK, MLA_QW)
    q_sw = jnp.concatenate([zq64, rope[..., half:], rope[..., :half], zq32], axis=-1).reshape(MLA_Q_RANK, MLA_QW)
    wuq2 = jnp.concatenate([q_slab, q_sw], axis=1).astype(BF16)
    wkv = w_ukv.reshape(MLA_KV_RANK, MLA_HEADS, MLA_NOPE + MLA_V)
    wk = jnp.concatenate([wkv[..., :MLA_NOPE], jnp.zeros_like(wkv[..., :MLA_NOPE])], axis=-1)
    wk = wk.reshape(MLA_KV_RANK, MLA_QW).astype(BF16)
    wv = wkv[..., MLA_NOPE:].reshape(MLA_KV_RANK, MLA_VW).astype(BF16)
    return wkr2, wuq2, wk, wv


def kernel(x_prompt, x_sample, c, state_ret, cache_mla_ckv, cache_mla_krope, state_s5_re, state_s5_im, cache_na_k, cache_na_v, c_ctx, ada_w, ada_b, mix_pre_g, mix_post_g, ffn_pre_g, ffn_post_g, ffn_w_up, ffn_conv_w, ffn_conv_b, ffn_w_down, even_w_in, even_w_out, ret_logit, ret_gn, mla_q_norm, mla_w_uq, mla_kv_norm, mla_w_ukv, odd_w_in, odd_w_out, s5_lambda_re, s5_lambda_im, s5_log_step, s5_b_re, s5_b_im, s5_c_re, s5_c_im, s5_d, s5_glu_w, s5_glu_b, na_rpb):
    cvec = jnp.concatenate([c_ctx[None, :], c, jnp.zeros((8 - 1 - DEC_BATCH, D), F32)], axis=0)
    mods, (e_in, e_out, o_in, o_out, glu_w) = _ada_mods(cvec, ada_w, ada_b,
                                                        [even_w_in, even_w_out, odd_w_in, odd_w_out, s5_glu_w])
    row3 = lambda a: a.reshape(a.shape[0], 1, a.shape[1])
    mix_pre, mix_post, ffn_pre, ffn_post = row3(mix_pre_g), row3(mix_post_g), row3(ffn_pre_g), row3(ffn_post_g)
    conv_b = row3(ffn_conv_b)
    cos_t, sin_t = _rope_tables()
    kr_cache = jnp.pad(cache_mla_krope, ((0, 0), (0, 0), (0, 0), (MLA_NOPE, HEAD_SLAB - MLA_NOPE - MLA_ROPE)))
    na_k_cache = cache_na_k.reshape(DEC_BATCH, -1, PAST_LEN, NA_W)
    na_v_cache = cache_na_v.reshape(DEC_BATCH, -1, PAST_LEN, NA_W)
    perm = _s5_perm()
    xs = (x_prompt.reshape(N_TOK_P, D), x_sample.reshape(N_TOK_S, D))
    ret_states, na_kv = None, None
    new_ckv, new_kr, new_s5_re, new_s5_im = [], [], [], []
    for layer in range(DEPTH):
        j = layer // 2
        if layer % 2 == 0:
            wkr2, wuq2, wk, wv = _mla_weights(even_w_in[j], mla_w_uq[j], mla_w_ukv[j])
            qkvg, qp, kp, v, ckvn, kr = _in_even(xs, mods, mix_pre, layer, j, e_in, wkr2, row3(mla_q_norm),
                                                 row3(mla_kv_norm), wuq2, wk, wv, cos_t, sin_t)
            logit = jnp.transpose(ret_logit[j]).reshape(RET_HEADS, 2, 1, 1)
            gn = row3(ret_gn)
            yr_c, ret_states = _retention(qkvg, logit, gn, j, None, seq_len=SEQ, n_seq=BATCH, n_blk=4, row0=0,
                                          emit_state=True, prev_states=ret_states)
            (yr_l,) = _retention(qkvg, logit, gn, j, state_ret, seq_len=DEC_SEQ, n_seq=DEC_BATCH, n_blk=1,
                                 row0=N_TOK_P, emit_state=False)
            ym_c = _mla_ctx(qp, kp, v)
            kp_c, v_c = _mla_cache(cache_mla_ckv, kr_cache, j, wk, wv)
            ym_l, w_up, w_down = _mla_lat(qp, kp, v, kp_c, v_c, layer, ffn_w_up, ffn_w_down)
            mix, w_out, s5 = (yr_c, yr_l, ym_c, ym_l), e_out, None
            new_ckv.append(ckvn.reshape(BATCH, SEQ, MLA_KV_RANK))
            new_kr.append(kr[:, MLA_NOPE:MLA_NOPE + MLA_ROPE].reshape(BATCH, SEQ, MLA_ROPE))
        else:
            u, qkv, *na_kv = _in_odd(xs, mods, mix_pre, layer, j, o_in, prev_kv=na_kv)
            m, n2, p2, a = _s5_prep(s5_lambda_re[j], s5_lambda_im[j], s5_log_step[j], s5_b_re[j], s5_b_im[j],
                                    s5_c_re[j], s5_c_im[j])
            h0c = jnp.zeros((S5_PAIRS, 2, 2, BATCH, S5_PL), F32)
            h0 = jnp.stack([state_s5_re[:, j], state_s5_im[:, j]], axis=0)
            h0l = jnp.transpose(h0.reshape(2, DEC_BATCH, 2, S5_PAIRS, S5_PL), (3, 2, 0, 1, 4))
            y_raw, fin = _s5_scan(u, perm, m, n2, p2, a, h0c, h0l)
            yn_c = _na_ctx(qkv)
            yn_l, w_up, w_down = _na_lat(qkv, na_k_cache, na_v_cache, j, _na_table(na_rpb[j]), layer, ffn_w_up,
                                         ffn_w_down)
            mix, w_out, s5 = (y_raw, u, yn_c, yn_l), o_out, (row3(s5_d), glu_w, row3(s5_glu_b))
            st = jnp.transpose(fin.reshape(S5_PAIRS, 2, 2, BATCH, 2, S5_P), (2, 3, 1, 0, 4, 5))
            st = st.reshape(2, BATCH, 2, S5_GROUPS, S5_P)
            new_s5_re.append(st[0])
            new_s5_im.append(st[1])
        xs = _ffn(xs, mix, mods, mix_post, ffn_pre, ffn_post, layer, j, w_out, w_up, ffn_conv_w, conv_b, w_down, s5)
    stack = lambda a: jnp.stack(a, axis=1)
    heads = lambda a: a.reshape(BATCH, a.shape[1], SEQ, NA_HEADS, NA_DIM)
    return (xs[0].reshape(BATCH, SEQ, D), xs[1].reshape(DEC_BATCH, DEC_SEQ, D), ret_states, stack(new_ckv),
            stack(new_kr), stack(new_s5_re), stack(new_s5_im), heads(na_kv[0]), heads(na_kv[1]))
```
